```python
import math
import jax
import jax.numpy as jnp
from jax import lax
import numpy as np

D_MODEL = 1024
BATCH = 2
SEQ = 8192
DEPTH = 2
DEC_BATCH = 128
DEC_SEQ = 8
PAST_LEN = 2048
PAGE_SIZE = 128

POOL_WIDTH = D_MODEL // 4
SGU_WIDTH = D_MODEL // 4
ATT_WIDTH = D_MODEL // 2
MIX_WIDTH = POOL_WIDTH + SGU_WIDTH + ATT_WIDTH

POOL_WINDOWS = (2, 4, 8, 16)
POOL_GROUPS = len(POOL_WINDOWS)
POOL_GROUP_DIM = POOL_WIDTH // POOL_GROUPS
POOL_WMAX = max(POOL_WINDOWS)
POOL_BUF = POOL_WMAX - 1

SGU_GROUPS = 4
SGU_GROUP_DIM = SGU_WIDTH // SGU_GROUPS
SGU_CHUNK = 128

N_HEADS = 8
HEAD_DIM = ATT_WIDTH // N_HEADS
N_KV = 2
REP = N_HEADS // N_KV
N_BRANCH = 3
CMP_BLOCK = 32
CMP_HIDDEN = 2 * HEAD_DIM
SEL_BLOCK = 64
CMP_PER_SEL = SEL_BLOCK // CMP_BLOCK
N_SEL = 16
WINDOW = 512
Q_BLOCK = 128
FORCED_SCORE = 1e4

IN_WIDTH = POOL_WIDTH + 2 * SGU_WIDTH + ATT_WIDTH + N_BRANCH * 2 * N_KV * HEAD_DIM + N_HEADS * N_BRANCH

N_BUCKETS = 32
REL_MAX_DIST = 128

N_EXPERTS = 16
N_GROUPS = 4
EXPERTS_PER_GROUP = N_EXPERTS // N_GROUPS
TOP_K = 2
D_EXPERT = 512
MOE_BLOCK = 128

ALPHA = (2 * DEPTH) ** 0.25
BETA = (8 * DEPTH) ** -0.25
LN_EPS = 1e-5
NEG = -1e30

kernel_name = 'hybrid_pool_sgu_nsa_moe_decoder_step'


def layer_norm(x, g, b):
    xf = x.astype(jnp.float32)
    mu = xf.mean(-1, keepdims=True)
    var = jnp.mean(jnp.square(xf - mu), -1, keepdims=True)
    return ((xf - mu) * lax.rsqrt(var + LN_EPS) * g.astype(jnp.float32) + b.astype(jnp.float32)).astype(x.dtype)


def rel_bucket(dist):
    n = jnp.maximum(dist, 0)
    max_exact = N_BUCKETS // 2
    nf = jnp.maximum(n, 1).astype(jnp.float32)
    large = max_exact + (jnp.log(nf / max_exact) / math.log(REL_MAX_DIST / max_exact) * (N_BUCKETS - max_exact)).astype(jnp.int32)
    large = jnp.minimum(large, N_BUCKETS - 1)
    return jnp.where(n < max_exact, n, large)


def masked_softmax(s, mask):
    s = jnp.where(mask, s.astype(jnp.float32), NEG)
    p = jnp.where(mask, jnp.exp(s - s.max(-1, keepdims=True)), 0.0)
    return p / jnp.maximum(p.sum(-1, keepdims=True), 1e-30)


def split_proj(z):
    B, T = z.shape[:2]
    cuts = [POOL_WIDTH, POOL_WIDTH + SGU_WIDTH, POOL_WIDTH + 2 * SGU_WIDTH,
            POOL_WIDTH + 2 * SGU_WIDTH + ATT_WIDTH,
            POOL_WIDTH + 2 * SGU_WIDTH + ATT_WIDTH + N_BRANCH * 2 * N_KV * HEAD_DIM]
    p, u, v, q, kv, g = jnp.split(z, cuts, axis=-1)
    q = q.reshape(B, T, N_KV, REP, HEAD_DIM)
    kv = kv.reshape(B, T, N_BRANCH, 2, N_KV, HEAD_DIM)
    gates = jax.nn.sigmoid(g.astype(jnp.float32)).reshape(B, T, N_KV, REP, N_BRANCH).astype(z.dtype)
    return p, u, v, q, kv, gates


def pool_mix(p_ext, n_prev, pool_w, pool_scale):
    B, L, C = p_ext.shape
    T = L - n_prev
    pf = p_ext.astype(jnp.float32)
    cs = jnp.concatenate([jnp.zeros((B, POOL_WMAX + 1, C), jnp.float32), jnp.cumsum(pf, axis=1)], axis=1)
    hi0 = n_prev + 1 + POOL_WMAX
    row = n_prev + 1 + jnp.arange(T, dtype=jnp.int32)
    means = []
    for gi, w in enumerate(POOL_WINDOWS):
        c0, c1 = gi * POOL_GROUP_DIM, (gi + 1) * POOL_GROUP_DIM
        hi = cs[:, hi0:hi0 + T, c0:c1]
        lo = cs[:, hi0 - w:hi0 - w + T, c0:c1]
        cnt = jnp.minimum(w, row).astype(jnp.float32)
        means.append((hi - lo) / cnt[None, :, None])
    diff = jnp.concatenate(means, axis=-1) - pf[:, n_prev:]
    out = jnp.einsum('btgc,gcd->btgd', diff.reshape(B, T, POOL_GROUPS, POOL_GROUP_DIM), pool_w.astype(jnp.float32))
    return (out.reshape(B, T, POOL_WIDTH) * pool_scale.astype(jnp.float32)).astype(p_ext.dtype)


def sgu_mix(u, vn, sgu_w, sgu_b):
    B, T, _ = vn.shape
    C = min(T, SGU_CHUNK)
    n = T // C
    mask = jnp.tril(jnp.ones((C, C), bool))
    w = jnp.where(mask, sgu_w[:, :C, :C], 0.0).astype(vn.dtype)
    vc = vn.reshape(B, n, C, SGU_GROUPS, SGU_GROUP_DIM)
    mixed = jnp.einsum('gij,bnjgd->bnigd', w, vc) + sgu_b[:, :C].T[None, None, :, :, None]
    return u * mixed.reshape(B, T, SGU_WIDTH).astype(u.dtype)


def compress(raw, pe, w1, w2):
    B, S = raw.shape[:2]
    n = S // CMP_BLOCK
    x = raw.reshape(B, n, CMP_BLOCK, 2, N_KV, HEAD_DIM) + jnp.transpose(pe, (1, 0, 2))[:, :, None, :]
    x = jnp.transpose(x, (0, 1, 3, 4, 2, 5)).reshape(B, n, 2, N_KV, CMP_BLOCK * HEAD_DIM)
    hdn = jax.nn.gelu(jnp.einsum('bnsgf,sfh->bnsgh', x, w1))
    return jnp.einsum('bnsgh,shd->bnsgd', hdn, w2)


def to_sel_blocks(raw):
    B, S = raw.shape[:2]
    r = raw.reshape(B, S // SEL_BLOCK, SEL_BLOCK, 2, N_KV, HEAD_DIM)
    r = jnp.transpose(r, (3, 0, 4, 1, 2, 5))
    return r[0], r[1]


def nsa_core(q, q_pos, gates, kc, vc, ks_blk, vs_blk, kw, vw, kw_pos, rel_bias):
    B, Q = q.shape[:2]
    dt = q.dtype
    scale = HEAD_DIM ** -0.5
    tb = rel_bias.astype(jnp.float32).reshape(N_BUCKETS, N_KV, REP)
    n_cmp = kc.shape[1]
    c_end = (jnp.arange(n_cmp, dtype=jnp.int32) + 1) * CMP_BLOCK - 1
    dist_c = q_pos[:, None] - c_end[None, :]
    bias_c = jnp.transpose(tb[rel_bucket(dist_c)], (2, 3, 0, 1))
    s_c = jnp.einsum('bqgrd,bngd->bgrqn', q, kc).astype(jnp.float32) * scale + bias_c
    p_c = masked_softmax(s_c, dist_c >= 0)
    o_c = jnp.einsum('bgrqn,bngd->bqgrd', p_c.astype(dt), vc)
    n_blk = ks_blk.shape[2]
    imp = p_c.sum(axis=2).reshape(B, N_KV, Q, n_blk, CMP_PER_SEL).sum(-1)
    blk = jnp.arange(n_blk, dtype=jnp.int32)[None, :]
    cur = (q_pos // SEL_BLOCK)[:, None]
    forced = (blk == 0) | (blk == cur) | (blk == cur - 1)
    valid = blk * SEL_BLOCK <= q_pos[:, None]
    imp = jnp.where(forced, FORCED_SCORE, jnp.where(valid, imp, -1.0))
    _, sel = lax.top_k(imp, min(N_SEL, n_blk))
    b_i = jnp.arange(B)[:, None, None, None]
    g_i = jnp.arange(N_KV)[None, :, None, None]
    k_s = ks_blk[b_i, g_i, sel].reshape(B, N_KV, Q, -1, HEAD_DIM)
    v_s = vs_blk[b_i, g_i, sel].reshape(B, N_KV, Q, -1, HEAD_DIM)
    k_pos = (sel[..., None] * SEL_BLOCK + jnp.arange(SEL_BLOCK, dtype=jnp.int32)).reshape(B, N_KV, Q, -1)
    dist_s = q_pos[None, None, :, None] - k_pos
    bias_s = jnp.moveaxis(tb[rel_bucket(dist_s), g_i], -1, 2)
    s_s = jnp.einsum('bqgrd,bgqkd->bgrqk', q, k_s).astype(jnp.float32) * scale + bias_s
    p_s = masked_softmax(s_s, (dist_s >= 0)[:, :, None])
    o_s = jnp.einsum('bgrqk,bgqkd->bqgrd', p_s.astype(dt), v_s)
    dist_w = q_pos[:, None] - kw_pos[None, :]
    mask_w = (dist_w >= 0) & (dist_w < WINDOW) & (kw_pos[None, :] >= 0)
    bias_w = jnp.transpose(tb[rel_bucket(dist_w)], (2, 3, 0, 1))
    s_w = jnp.einsum('bqgrd,bkgd->bgrqk', q, kw).astype(jnp.float32) * scale + bias_w
    p_w = masked_softmax(s_w, mask_w)
    o_w = jnp.einsum('bgrqk,bkgd->bqgrd', p_w.astype(dt), vw)
    return gates[..., 0:1] * o_c + gates[..., 1:2] * o_s + gates[..., 2:3] * o_w


def nsa_prompt(q, kv, gates, lp, rel_bias):
    B, T = q.shape[:2]
    cmp_raw, sel_raw, win_raw = kv[:, :, 0], kv[:, :, 1], kv[:, :, 2]
    kvc = compress(cmp_raw, lp['cmp_pe'], lp['cmp_w1'], lp['cmp_w2'])
    kc, vc = kvc[:, :, 0], kvc[:, :, 1]
    ks_blk, vs_blk = to_sel_blocks(sel_raw)
    win_pad = jnp.pad(win_raw, ((0, 0), (WINDOW, 0), (0, 0), (0, 0), (0, 0)))

    def one_block(i):
        s = i * Q_BLOCK
        qb = lax.dynamic_slice_in_dim(q, s, Q_BLOCK, 1)
        gb = lax.dynamic_slice_in_dim(gates, s, Q_BLOCK, 1)
        wb = lax.dynamic_slice_in_dim(win_pad, s, WINDOW + Q_BLOCK, 1)
        q_pos = s + jnp.arange(Q_BLOCK, dtype=jnp.int32)
        kw_pos = s - WINDOW + jnp.arange(WINDOW + Q_BLOCK, dtype=jnp.int32)
        return nsa_core(qb, q_pos, gb, kc, vc, ks_blk, vs_blk, wb[:, :, 0], wb[:, :, 1], kw_pos, rel_bias)

    o = lax.map(one_block, jnp.arange(T // Q_BLOCK, dtype=jnp.int32))
    o = jnp.moveaxis(o, 0, 1).reshape(B, T, ATT_WIDTH)
    return o, cmp_raw, sel_raw, win_raw[:, T - min(WINDOW, T):]


def nsa_sample(q, kv, gates, cmp_past, sel_past, win_buf, past_len, lp, rel_bias):
    B, T = q.shape[:2]
    cmp_new, sel_new, win_new = kv[:, :, 0], kv[:, :, 1], kv[:, :, 2]
    s_tot = past_len + T
    s_pad = -(-s_tot // SEL_BLOCK) * SEL_BLOCK
    pad = ((0, 0), (0, s_pad - s_tot), (0, 0), (0, 0), (0, 0))
    cmp_full = jnp.pad(jnp.concatenate([cmp_past.astype(q.dtype), cmp_new], 1), pad)
    sel_full = jnp.pad(jnp.concatenate([sel_past.astype(q.dtype), sel_new], 1), pad)
    kvc = compress(cmp_full, lp['cmp_pe'], lp['cmp_w1'], lp['cmp_w2'])
    kc, vc = kvc[:, :, 0], kvc[:, :, 1]
    ks_blk, vs_blk = to_sel_blocks(sel_full)
    win_keys = jnp.concatenate([win_buf.astype(q.dtype), win_new], 1)
    n_buf = win_buf.shape[1]
    kw_pos = past_len - n_buf + jnp.arange(n_buf + T, dtype=jnp.int32)

    def one_tok(j):
        qj = lax.dynamic_slice_in_dim(q, j, 1, 1)
        gj = lax.dynamic_slice_in_dim(gates, j, 1, 1)
        q_pos = jnp.reshape(past_len + j, (1,))
        return nsa_core(qj, q_pos, gj, kc, vc, ks_blk, vs_blk, win_keys[:, :, 0], win_keys[:, :, 1], kw_pos, rel_bias)

    o = lax.map(one_tok, jnp.arange(T, dtype=jnp.int32))
    o = jnp.moveaxis(o, 0, 1).reshape(B, T, ATT_WIDTH)
    return o, cmp_new, sel_new, win_keys[:, T:]


def mix_prompt(h, lp, rel_bias):
    p, u, v, q, kv, gates = split_proj(h @ lp['w_in'])
    pool_out = pool_mix(p, 0, lp['pool_w'], lp['pool_scale'])
    vn = layer_norm(v, lp['sgu_ln_g'], lp['sgu_ln_b'])
    sgu_out = sgu_mix(u, vn, lp['sgu_w'], lp['sgu_b'])
    att_out, new_cmp, new_sel, new_win = nsa_prompt(q, kv, gates, lp, rel_bias)
    out = jnp.concatenate([pool_out, sgu_out, att_out], -1) @ lp['w_out']
    return out, (new_cmp, new_sel, new_win, p[:, p.shape[1] - POOL_BUF:])


def mix_sample(h, lp, rel_bias, cmp_past, sel_past, win_buf, pool_buf, past_len):
    p, u, v, q, kv, gates = split_proj(h @ lp['w_in'])
    p_ext = jnp.concatenate([pool_buf.astype(p.dtype), p], 1)
    pool_out = pool_mix(p_ext, pool_buf.shape[1], lp['pool_w'], lp['pool_scale'])
    vn = layer_norm(v, lp['sgu_ln_g'], lp['sgu_ln_b'])
    sgu_out = sgu_mix(u, vn, lp['sgu_w'], lp['sgu_b'])
    att_out, new_cmp, new_sel, new_win = nsa_sample(q, kv, gates, cmp_past, sel_past, win_buf, past_len, lp, rel_bias)
    out = jnp.concatenate([pool_out, sgu_out, att_out], -1) @ lp['w_out']
    return out, (new_cmp, new_sel, new_win, p_ext[:, p_ext.shape[1] - POOL_BUF:], vn)


def moe(h, router_w, router_b, w_gate, w_up, w_down):
    N, D = h.shape
    s = jax.nn.sigmoid(h.astype(jnp.float32) @ router_w.astype(jnp.float32))
    grp = (s + router_b.astype(jnp.float32)).reshape(N, N_GROUPS, EXPERTS_PER_GROUP)
    g_sel = jnp.argmax(lax.top_k(grp, 2)[0].sum(-1), axis=-1)
    in_grp = jnp.take_along_axis(grp, g_sel[:, None, None], axis=1)[:, 0]
    _, loc = lax.top_k(in_grp, TOP_K)
    e_idx = (g_sel[:, None] * EXPERTS_PER_GROUP + loc).astype(jnp.int32)
    w = jnp.take_along_axis(s, e_idx, axis=1)
    w = w / w.sum(-1, keepdims=True)
    A = N * TOP_K
    flat_e = e_idx.reshape(A)
    flat_tok = jnp.arange(A, dtype=jnp.int32) // TOP_K
    flat_w = w.reshape(A)
    order = jnp.argsort(flat_e)
    e_sorted = flat_e[order]
    counts = jnp.bincount(flat_e, length=N_EXPERTS)
    padded = (counts + MOE_BLOCK - 1) // MOE_BLOCK * MOE_BLOCK
    pad_end = jnp.cumsum(padded)
    pad_start = pad_end - padded
    start = jnp.cumsum(counts) - counts
    dest = pad_start[e_sorted] + jnp.arange(A, dtype=jnp.int32) - start[e_sorted]
    n_blocks = -(-(A + N_EXPERTS * (MOE_BLOCK - 1)) // MOE_BLOCK)
    P = n_blocks * MOE_BLOCK
    buf_tok = jnp.full((P,), N, jnp.int32).at[dest].set(flat_tok[order])
    buf_w = jnp.zeros((P,), jnp.float32).at[dest].set(flat_w[order])
    block_e = jnp.minimum(jnp.searchsorted(pad_end, jnp.arange(n_blocks) * MOE_BLOCK, side='right'), N_EXPERTS - 1)
    h_pad = jnp.concatenate([h, jnp.zeros((1, D), h.dtype)], 0)
    xb = h_pad[buf_tok].reshape(n_blocks, MOE_BLOCK, D)

    def expert_block(args):
        x_blk, e = args
        return (jax.nn.silu(x_blk @ w_gate[e]) * (x_blk @ w_up[e])) @ w_down[e]

    y = lax.map(expert_block, (xb, block_e)).reshape(P, D)
    out = jnp.zeros((N + 1, D), jnp.float32).at[buf_tok].add(y.astype(jnp.float32) * buf_w[:, None])
    return out[:N].astype(h.dtype)


def post_norm_layer(x, c, w_mod_l, b_mod_l, ln_g_l, ln_b_l, mix_fn, router_w, router_b, wg, wu, wd):
    B, T, D = x.shape
    m = (jax.nn.silu(c) @ w_mod_l + b_mod_l).reshape(B, 6, 1, D)
    sh1, sc1, g1, sh2, sc2, g2 = [m[:, i] for i in range(6)]
    mix_out, new_state = mix_fn(x * (1 + sc1) + sh1)
    x = layer_norm(ALPHA * x + (1 + g1) * mix_out, ln_g_l[0], ln_b_l[0])
    h2 = x * (1 + sc2) + sh2
    ff = moe(h2.reshape(B * T, D), router_w, router_b, wg, wu, wd).reshape(B, T, D)
    x = layer_norm(ALPHA * x + (1 + g2) * ff, ln_g_l[1], ln_b_l[1])
    return x, new_state


def setup_inputs(seed: int = 0) -> dict:
    key = jax.random.key(seed)
    k = jax.random.split(key, 32)
    f32 = jnp.float32
    n_pages = PAST_LEN // PAGE_SIZE
    n_used = DEC_BATCH * n_pages
    n_phys = n_used + max(1, n_used // 4)
    win_buf = min(WINDOW, PAST_LEN)

    def nrm(kk, shape, s=1.0):
        return s * jax.random.normal(kk, shape, f32)

    page_table = jax.random.permutation(k[0], n_phys)[:n_used].reshape(DEC_BATCH, n_pages).astype(jnp.int32)
    return {
        'x_prompt': nrm(k[1], (BATCH, SEQ, D_MODEL)),
        'x_sample': nrm(k[2], (DEC_BATCH, DEC_SEQ, D_MODEL)),
        'cache_cmp_kv': nrm(k[3], (DEPTH, n_phys, PAGE_SIZE, 2, N_KV, HEAD_DIM)),
        'cache_sel_kv': nrm(k[4], (DEPTH, n_phys, PAGE_SIZE, 2, N_KV, HEAD_DIM)),
        'state_win_kv': nrm(k[5], (DEPTH, DEC_BATCH, win_buf, 2, N_KV, HEAD_DIM)),
        'state_pool': nrm(k[6], (DEPTH, DEC_BATCH, POOL_BUF, POOL_WIDTH)),
        'page_table': page_table,
        'c_prompt': nrm(k[7], (BATCH, D_MODEL)),
        'c_sample': nrm(k[8], (DEC_BATCH, D_MODEL)),
        'w_in': nrm(k[9], (DEPTH, D_MODEL, IN_WIDTH), D_MODEL ** -0.5),
        'w_out': nrm(k[10], (DEPTH, MIX_WIDTH, D_MODEL), BETA * MIX_WIDTH ** -0.5),
        'pool_w': nrm(k[11], (DEPTH, POOL_GROUPS, POOL_GROUP_DIM, POOL_GROUP_DIM), POOL_GROUP_DIM ** -0.5),
        'pool_scale': 1.0 + nrm(k[12], (DEPTH, POOL_WIDTH), 0.1),
        'sgu_ln_g': 1.0 + nrm(k[13], (DEPTH, SGU_WIDTH), 0.1),
        'sgu_ln_b': nrm(k[14], (DEPTH, SGU_WIDTH), 0.02),
        'sgu_w': nrm(k[15], (DEPTH, SGU_GROUPS, SGU_CHUNK, SGU_CHUNK), SGU_CHUNK ** -0.5),
        'sgu_b': 1.0 + nrm(k[16], (DEPTH, SGU_GROUPS, SGU_CHUNK), 0.1),
        'cmp_pe': nrm(k[17], (DEPTH, 2, CMP_BLOCK, HEAD_DIM), 0.1),
        'cmp_w1': nrm(k[18], (DEPTH, 2, CMP_BLOCK * HEAD_DIM, CMP_HIDDEN), (CMP_BLOCK * HEAD_DIM) ** -0.5),
        'cmp_w2': nrm(k[19], (DEPTH, 2, CMP_HIDDEN, HEAD_DIM), CMP_HIDDEN ** -0.5),
        'rel_bias': nrm(k[20], (N_BUCKETS, N_HEADS), 0.5),
        'w_mod': nrm(k[21], (DEPTH, D_MODEL, 6 * D_MODEL), 0.2 * D_MODEL ** -0.5),
        'b_mod': nrm(k[22], (DEPTH, 6 * D_MODEL), 0.02),
        'ln_g': 1.0 + nrm(k[23], (DEPTH, 2, D_MODEL), 0.1),
        'ln_b': nrm(k[24], (DEPTH, 2, D_MODEL), 0.02),
        'router_w': nrm(k[25], (D_MODEL, N_EXPERTS), D_MODEL ** -0.5),
        'router_b': nrm(k[26], (N_EXPERTS,), 0.01),
        'moe_w_gate': nrm(k[27], (DEPTH, N_EXPERTS, D_MODEL, D_EXPERT), D_MODEL ** -0.5),
        'moe_w_up': nrm(k[28], (DEPTH, N_EXPERTS, D_MODEL, D_EXPERT), D_MODEL ** -0.5),
        'moe_w_down': nrm(k[29], (DEPTH, N_EXPERTS, D_EXPERT, D_MODEL), BETA * D_EXPERT ** -0.5),
    }


def reference(x_prompt, x_sample, cache_cmp_kv, cache_sel_kv, state_win_kv, state_pool, page_table,
              c_prompt, c_sample, w_in, w_out, pool_w, pool_scale, sgu_ln_g, sgu_ln_b, sgu_w, sgu_b,
              cmp_pe, cmp_w1, cmp_w2, rel_bias, w_mod, b_mod, ln_g, ln_b, router_w, router_b,
              moe_w_gate, moe_w_up, moe_w_down):
    n_dec, n_pages = page_table.shape
    past_len = n_pages * PAGE_SIZE
    xp, xs = x_prompt, x_sample
    cmp_p, cmp_s, sel_p, sel_s, win_p, win_s, pool_p, pool_s, sgu_s = [], [], [], [], [], [], [], [], []
    for l in range(DEPTH):
        lp = {'w_in': w_in[l], 'w_out': w_out[l], 'pool_w': pool_w[l], 'pool_scale': pool_scale[l],
              'sgu_ln_g': sgu_ln_g[l], 'sgu_ln_b': sgu_ln_b[l], 'sgu_w': sgu_w[l], 'sgu_b': sgu_b[l],
              'cmp_pe': cmp_pe[l], 'cmp_w1': cmp_w1[l], 'cmp_w2': cmp_w2[l]}
        xp, (nc, ns, nw, npool) = post_norm_layer(
            xp, c_prompt, w_mod[l], b_mod[l], ln_g[l], ln_b[l],
            lambda h: mix_prompt(h, lp, rel_bias),
            router_w, router_b, moe_w_gate[l], moe_w_up[l], moe_w_down[l])
        cmp_p.append(nc); sel_p.append(ns); win_p.append(nw); pool_p.append(npool)
        cmp_past = cache_cmp_kv[l][page_table].reshape(n_dec, past_len, 2, N_KV, HEAD_DIM)
        sel_past = cache_sel_kv[l][page_table].reshape(n_dec, past_len, 2, N_KV, HEAD_DIM)
        xs, (nc, ns, nw, npool, nv) = post_norm_layer(
            xs, c_sample, w_mod[l], b_mod[l], ln_g[l], ln_b[l],
            lambda h: mix_sample(h, lp, rel_bias, cmp_past, sel_past, state_win_kv[l], state_pool[l], past_len),
            router_w, router_b, moe_w_gate[l], moe_w_up[l], moe_w_down[l])
        cmp_s.append(nc); sel_s.append(ns); win_s.append(nw); pool_s.append(npool); sgu_s.append(nv)
    return (xp, xs,
            jnp.stack(cmp_p), jnp.stack(cmp_s),
            jnp.stack(sel_p), jnp.stack(sel_s),
            jnp.stack(win_p), jnp.stack(win_s),
            jnp.stack(pool_p), jnp.stack(pool_s),
            jnp.stack(sgu_s))
```

```python
import functools
import math

import numpy as np
import jax
import jax.numpy as jnp
from jax import lax
from jax.experimental import pallas as pl
from jax.experimental.pallas import tpu as pltpu

F32 = jnp.float32
BF16 = jnp.bfloat16

D_MODEL = 1024
POOL_WIDTH = 256
SGU_WIDTH = 256
ATT_WIDTH = 512
POOL_WINDOWS = (2, 4, 8, 16)
POOL_GROUP_DIM = 64
POOL_BUF = 15
SGU_GROUPS = 4
SGU_CHUNK = 128
N_HEADS = 8
HEAD_DIM = 64
N_KV = 2
REP = 4
CMP_BLOCK = 32
CMP_HIDDEN = 128
SEL_BLOCK = 64
N_SEL = 16
WINDOW = 512
N_BUCKETS = 32
REL_MAX_DIST = 128
N_EXPERTS = 16
N_GROUPS = 4
EXPERTS_PER_GROUP = 4
D_EXPERT = 512
DEPTH = 2
ALPHA = (2 * DEPTH) ** 0.25
LN_EPS = 1e-5
FORCED_SCORE = 1e4
NEG = -1e30
PAGE_SIZE = 128

KV_WIDTH = 2 * N_KV * HEAD_DIM
LANES = 128
VMEM_LIMIT = 56 * 1024 * 1024

C_P, C_U, C_V, C_CMP, C_SEL, C_WIN, C_GATE, C_Q = 0, 256, 512, 768, 1024, 1280, 1536, 1664
W_PROJ = C_Q + N_HEADS * LANES

TQ = 256
TK = 256
NEAR_BLOCKS = 12
FAR_BUCKET_DIST = 113


def _cparams(sem):
    return pltpu.CompilerParams(dimension_semantics=sem, vmem_limit_bytes=VMEM_LIMIT)


def _layer_norm(x, g, b):
    mu = jnp.mean(x, axis=-1, keepdims=True)
    xc = x - mu
    var = jnp.mean(xc * xc, axis=-1, keepdims=True)
    return xc * lax.rsqrt(var + LN_EPS) * g + b


def _mod_kernel(c_ref, w_ref, b_ref, o_ref):
    c = c_ref[...]
    a = (c * jax.nn.sigmoid(c)).astype(BF16)
    o_ref[0] = jnp.dot(a, w_ref[0].astype(BF16), preferred_element_type=F32) + b_ref[0]


def _modulation(c_all, w_mod, b_mod):
    n, d = c_all.shape
    depth, _, w = w_mod.shape
    tn = 1536
    return pl.pallas_call(
        _mod_kernel,
        grid=(depth, w // tn),
        in_specs=[pl.BlockSpec((n, d), lambda l, j: (0, 0)),
                  pl.BlockSpec((1, d, tn), lambda l, j: (l, 0, j)),
                  pl.BlockSpec((1, 1, tn), lambda l, j: (l, 0, j))],
        out_specs=pl.BlockSpec((1, n, tn), lambda l, j: (l, 0, j)),
        out_shape=jax.ShapeDtypeStruct((depth, n, w), F32),
        compiler_params=_cparams(("arbitrary", "arbitrary")),
        name="adaln_mod",
    )(c_all, w_mod, b_mod.reshape(depth, 1, w))


def _mod_spec(mod, tm, tiles_per_batch):
    if mod.ndim == 4:
        return pl.BlockSpec((None, 6, 1, D_MODEL), lambda i, *_: (i // tiles_per_batch, 0, 0, 0))
    return pl.BlockSpec((6, tm, D_MODEL), lambda i, *_: (0, i, 0))


def _inproj_kernel(x_ref, mod_ref, w_ref, lng_ref, lnb_ref,
                   p_ref, u_ref, vn_ref, cmp_ref, sel_ref, win_ref, gate_ref, qq_ref,
                   ksel_ref, vsel_ref, kwin_ref, vwin_ref):
    h = (x_ref[...] * (1.0 + mod_ref[1]) + mod_ref[0]).astype(BF16)

    def seg(a, b):
        return jnp.dot(h, w_ref[:, a:b], preferred_element_type=F32)

    p_ref[...] = seg(C_P, C_U)
    u_ref[...] = seg(C_U, C_V)
    vn_ref[...] = _layer_norm(seg(C_V, C_CMP), lng_ref[...], lnb_ref[...])
    cmp_ref[...] = seg(C_CMP, C_SEL)
    sel = seg(C_SEL, C_WIN)
    sel_ref[...] = sel
    ksel_ref[...] = sel[:, :LANES].astype(BF16)
    vsel_ref[...] = sel[:, LANES:].astype(BF16)
    win = seg(C_WIN, C_GATE)
    win_ref[...] = win
    kwin_ref[...] = win[:, :LANES].astype(BF16)
    vwin_ref[...] = win[:, LANES:].astype(BF16)
    gate_ref[...] = jax.nn.sigmoid(seg(C_GATE, C_Q))
    for hd in range(N_HEADS):
        qq_ref[hd] = seg(C_Q + hd * LANES, C_Q + (hd + 1) * LANES).astype(BF16)


def _in_proj(x, mod, w_proj, ln_g, ln_b, tm, tiles_per_batch):
    n = x.shape[0]
    row = lambda wd: pl.BlockSpec((tm, wd), lambda i: (i, 0))
    full = lambda a: pl.BlockSpec(a.shape, lambda i: (0,) * a.ndim)
    f32o = lambda wd: jax.ShapeDtypeStruct((n, wd), F32)
    b16o = lambda wd: jax.ShapeDtypeStruct((n, wd), BF16)
    return pl.pallas_call(
        _inproj_kernel,
        grid=(n // tm,),
        in_specs=[row(D_MODEL), _mod_spec(mod, tm, tiles_per_batch), full(w_proj), full(ln_g), full(ln_b)],
        out_specs=[row(256), row(256), row(256), row(256), row(256), row(256), row(LANES),
                   pl.BlockSpec((N_HEADS, tm, LANES), lambda i: (0, i, 0)),
                   row(LANES), row(LANES), row(LANES), row(LANES)],
        out_shape=[f32o(256), f32o(256), f32o(256), f32o(256), f32o(256), f32o(256), f32o(LANES),
                   jax.ShapeDtypeStruct((N_HEADS, n, LANES), BF16),
                   b16o(LANES), b16o(LANES), b16o(LANES), b16o(LANES)],
        compiler_params=_cparams(("arbitrary",)),
        name="in_proj",
    )(x, mod, w_proj, ln_g, ln_b)


def _prep_w_proj(w_in):
    d = w_in.shape[0]
    q = w_in[:, 768:1280].reshape(d, N_KV, REP, HEAD_DIM) * (HEAD_DIM ** -0.5)
    qq = jnp.zeros((d, N_KV, REP, N_KV, HEAD_DIM), w_in.dtype)
    for g in range(N_KV):
        qq = qq.at[:, g, :, g, :].set(q[:, g])
    gate = jnp.pad(w_in[:, 2048:2072], ((0, 0), (0, LANES - 24)))
    cols = [w_in[:, 0:768], w_in[:, 1280:2048], gate, qq.reshape(d, N_HEADS * LANES)]
    return jnp.concatenate(cols, axis=1).astype(BF16)


def _window_sums(shifted):
    acc = shifted(0)
    sums = {}
    for k in range(1, 16):
        acc = acc + shifted(k)
        if k + 1 in POOL_WINDOWS:
            sums[k + 1] = acc
    lane = lax.broadcasted_iota(jnp.int32, acc.shape, 1)
    return jnp.where(lane < 64, sums[2], jnp.where(lane < 128, sums[4], jnp.where(lane < 192, sums[8], sums[16])))


def _mixers_tail(sums, cnt, cur, u, vn, wpool_ref, pscale_ref, wcat_ref, sbias_ref, o_ref):
    diff = (sums / cnt - cur).astype(BF16)
    pool = jnp.dot(diff, wpool_ref[...], preferred_element_type=F32) * pscale_ref[...]
    lane = lax.broadcasted_iota(jnp.int32, vn.shape, 1)
    vb = vn.astype(BF16)
    zero = jnp.zeros_like(vb)
    stacked = jnp.concatenate([jnp.where((lane // 64) == g, vb, zero) for g in range(SGU_GROUPS)], axis=0)
    mixed = jnp.dot(wcat_ref[...], stacked, preferred_element_type=F32) + sbias_ref[...]
    o_ref[:, :POOL_WIDTH] = pool.astype(BF16)
    o_ref[:, POOL_WIDTH:] = (u * mixed).astype(BF16)


def _lane_window():
    lane = lax.broadcasted_iota(jnp.int32, (SGU_CHUNK, POOL_WIDTH), 1)
    return jnp.where(lane < 64, 2, jnp.where(lane < 128, 4, jnp.where(lane < 192, 8, 16)))


def _mix_prompt_kernel(p_ref, u_ref, vn_ref, wpool_ref, pscale_ref, wcat_ref, sbias_ref, o_ref, prev_ref,
                       *, tiles_per_batch):
    t = pl.program_id(0) % tiles_per_batch

    @pl.when(t == 0)
    def _():
        prev_ref[...] = jnp.zeros_like(prev_ref)

    cur = p_ref[...]
    prev = prev_ref[...]
    row = lax.broadcasted_iota(jnp.int32, cur.shape, 0)

    def shifted(k):
        if k == 0:
            return cur
        return jnp.where(row >= k, pltpu.roll(cur, k, 0), pltpu.roll(prev, k, 0))

    sums = _window_sums(shifted)
    prev_ref[...] = cur
    cnt = jnp.minimum(_lane_window(), t * SGU_CHUNK + row + 1).astype(F32)
    _mixers_tail(sums, cnt, cur, u_ref[...], vn_ref[...], wpool_ref, pscale_ref, wcat_ref, sbias_ref, o_ref)


def _mix_sample_kernel(pa_ref, pb_ref, pc_ref, u_ref, vn_ref, wpool_ref, pscale_ref, wcat_ref, sbias_ref, o_ref,
                       *, t_new):
    a, b, c = pa_ref[...], pb_ref[...], pc_ref[...]
    rows = c.shape[0]
    t = lax.broadcasted_iota(jnp.int32, c.shape, 0) % t_new

    def shifted(k):
        if k == 0:
            return c
        hi, lo = (c, b) if k < t_new else (b, a)
        kk = k % t_new
        if kk == 0:
            return hi
        return jnp.where(t >= kk, pltpu.roll(hi, kk, 0), pltpu.roll(lo, rows - t_new + kk, 0))

    sums = _window_sums(shifted)
    cnt = _lane_window().astype(F32)
    _mixers_tail(sums, cnt, c, u_ref[...], vn_ref[...], wpool_ref, pscale_ref, wcat_ref, sbias_ref, o_ref)


def _mixers(p_in, u, vn, wpool, pscale, wcat, sbias, tiles_per_batch=None, t_new=None):
    n = u.shape[0]
    tm = SGU_CHUNK
    row = lambda wd: pl.BlockSpec((tm, wd), lambda i: (i, 0))
    full = lambda a: pl.BlockSpec(a.shape, lambda i: (0,) * a.ndim)
    weights = [wpool, pscale, wcat, sbias]
    if t_new is None:
        kern = functools.partial(_mix_prompt_kernel, tiles_per_batch=tiles_per_batch)
        ins, scratch = [p_in], [pltpu.VMEM((tm, POOL_WIDTH), F32)]
    else:
        kern = functools.partial(_mix_sample_kernel, t_new=t_new)
        ins, scratch = list(p_in), []
    return pl.pallas_call(
        kern,
        grid=(n // tm,),
        in_specs=[row(256)] * (len(ins) + 2) + [full(a) for a in weights],
        out_specs=row(512),
        out_shape=jax.ShapeDtypeStruct((n, 512), BF16),
        scratch_shapes=scratch,
        compiler_params=_cparams(("arbitrary",)),
        name="mixers",
    )(*ins, u, vn, *weights)


def _prep_mixer_weights(pool_w, pool_scale, sgu_w, sgu_b, chunk):
    wpool = jax.scipy.linalg.block_diag(*[pool_w[g] for g in range(len(POOL_WINDOWS))]).astype(BF16)
    tri = jnp.tril(jnp.ones((chunk, chunk), bool))
    w = jnp.where(tri, sgu_w[:, :chunk, :chunk], 0.0)
    reps = SGU_CHUNK // chunk
    eye = jnp.eye(reps, dtype=w.dtype)
    wt = jnp.concatenate([jnp.kron(eye, w[g]) for g in range(SGU_GROUPS)], axis=1).astype(BF16)
    b = jnp.tile(sgu_b[:, :chunk], (1, reps))
    sbias = jnp.repeat(b.T, SGU_WIDTH // SGU_GROUPS, axis=1)
    return wpool, pool_scale.reshape(1, POOL_WIDTH), wt, sbias


def _compress_kernel(raw_ref, pe_ref, w1_ref, w2_ref, o_ref, *, nblk):
    for s in range(2):
        acc = jnp.zeros((nblk, N_KV * CMP_HIDDEN), F32)
        for j in range(CMP_BLOCK):
            xj = raw_ref[0, pl.ds(2 * j + s, nblk, stride=2 * CMP_BLOCK), :] + pe_ref[0, s, j:j + 1, :]
            acc = acc + jnp.dot(xj.astype(BF16), w1_ref[0, s, j], preferred_element_type=F32)
        hdn = jax.nn.gelu(acc).astype(BF16)
        o_ref[0, :, s * LANES:(s + 1) * LANES] = jnp.dot(hdn, w2_ref[0, s], preferred_element_type=F32)


def _compress(raw, pe_rows, w1bd, w2bd, nblk, layer0=0):
    nl, r, _ = raw.shape
    assert r % (nblk * CMP_BLOCK) == 0
    steps = r // (nblk * CMP_BLOCK)
    wspec = lambda a: pl.BlockSpec((1,) + a.shape[1:], lambda l, i: (l + layer0,) + (0,) * (a.ndim - 1))
    return pl.pallas_call(
        functools.partial(_compress_kernel, nblk=nblk),
        grid=(nl, steps),
        in_specs=[pl.BlockSpec((1, 2 * nblk * CMP_BLOCK, LANES), lambda l, i: (l, i, 0)),
                  wspec(pe_rows), wspec(w1bd), wspec(w2bd)],
        out_specs=pl.BlockSpec((1, nblk, KV_WIDTH), lambda l, i: (l, i, 0)),
        out_shape=jax.ShapeDtypeStruct((nl, r // CMP_BLOCK, KV_WIDTH), F32),
        compiler_params=_cparams(("arbitrary", "arbitrary")),
        name="compress",
    )(raw.reshape(nl, 2 * r, LANES), pe_rows, w1bd, w2bd)


def _prep_compress_weights(cmp_pe, cmp_w1, cmp_w2):
    nl = cmp_pe.shape[0]
    pe_rows = jnp.concatenate([cmp_pe] * N_KV, axis=-1)
    w1 = cmp_w1.reshape(nl, 2, CMP_BLOCK, HEAD_DIM, CMP_HIDDEN)
    w1bd = jnp.zeros((nl, 2, CMP_BLOCK, N_KV, HEAD_DIM, N_KV, CMP_HIDDEN), F32)
    w2bd = jnp.zeros((nl, 2, N_KV, CMP_HIDDEN, N_KV, HEAD_DIM), F32)
    for g in range(N_KV):
        w1bd = w1bd.at[:, :, :, g, :, g, :].set(w1)
        w2bd = w2bd.at[:, :, g, :, g, :].set(cmp_w2)
    return (pe_rows, w1bd.reshape(nl, 2, CMP_BLOCK, LANES, N_KV * CMP_HIDDEN).astype(BF16),
            w2bd.reshape(nl, 2, N_KV * CMP_HIDDEN, LANES).astype(BF16))


def _np_bucket(dist):
    n = np.maximum(dist, 0)
    nf = np.maximum(n, 1).astype(np.float32)
    large = 16 + (np.log(nf / np.float32(16)) / np.float32(math.log(REL_MAX_DIST / 16)) * np.float32(16)).astype(np.int32)
    return np.where(n < 16, n, np.minimum(large, N_BUCKETS - 1)).astype(np.int32)


def _bias_minus_far(rel_bias, dist):
    tbt = rel_bias.astype(F32).T
    val = tbt[:, _np_bucket(dist)] - tbt[:, N_BUCKETS - 1].reshape((-1,) + (1,) * dist.ndim)
    return jnp.where(jnp.asarray(dist >= 0), val, NEG)


def _split3(x):
    hi = x.astype(BF16)
    r1 = x - hi.astype(F32)
    mid = r1.astype(BF16)
    lo = (r1 - mid.astype(F32)).astype(BF16)
    return hi, mid, lo


def _prompt_bias_tables(rel_bias):
    a = np.arange(TQ)[:, None]
    near = _bias_minus_far(rel_bias, a + TK - np.arange(2 * TK)[None, :])
    dist_c = a + (4 * CMP_BLOCK - CMP_BLOCK + 1) - CMP_BLOCK * np.arange(NEAR_BLOCKS)[None, :]
    dc = _bias_minus_far(rel_bias, dist_c)
    hi, mid, lo = _split3(dc)
    cols = jnp.stack([hi, mid, lo], axis=-1).reshape(N_HEADS, TQ, 3 * NEAR_BLOCKS)
    future = jnp.full((N_HEADS, TQ, 1), NEG, F32).astype(BF16)
    pad = jnp.zeros((N_HEADS, TQ, LANES - 3 * NEAR_BLOCKS - 1), BF16)
    dtab = jnp.concatenate([cols, future, pad], axis=-1).reshape(N_HEADS * TQ, LANES)
    lane = np.arange(LANES)
    lane_m = np.where(lane < 3 * NEAR_BLOCKS, lane // 3, -1000).astype(np.int32)
    return near, dtab, jnp.asarray(np.tile(lane_m[None, :], (8, 1)))


_NT = (((1,), (1,)), ((), ()))


def _softmax_step(s, v, m_ref, l_ref, acc_ref, h):
    m_prev = m_ref[h]
    m_new = jnp.maximum(m_prev, jnp.max(s, axis=1, keepdims=True))
    alpha = jnp.exp(m_prev - m_new)
    p = jnp.exp(s - m_new)
    l_ref[h] = alpha * l_ref[h] + jnp.sum(p, axis=1, keepdims=True)
    acc_ref[h] = alpha * acc_ref[h] + jnp.dot(p.astype(BF16), v, preferred_element_type=F32)
    m_ref[h] = m_new


def _select_blocks(imp_t, cur, unroll=False):
    blk = lax.broadcasted_iota(jnp.int32, imp_t.shape, 0)
    forced = (blk == 0) | (blk == cur) | (blk == cur - 1)
    vals = jnp.where(forced, FORCED_SCORE, jnp.where(blk <= cur, imp_t, -1.0))

    def body(_, carry):
        vals, neg = carry
        mx = jnp.max(vals, axis=0, keepdims=True)
        first = jnp.min(jnp.where(vals == mx, blk, 1 << 20), axis=0, keepdims=True)
        pick = blk == first
        return jnp.where(pick, -3e38, vals), jnp.where(pick, 0.0, neg)

    _, neg = lax.fori_loop(0, N_SEL, body, (vals, jnp.full(imp_t.shape, NEG, F32)), unroll=unroll)
    return neg


def _attn_kernel(qq_ref, kc_ref, vc_ref, dtab_ref, lanem_ref, ksel_ref, vsel_ref, kwin_ref, vwin_ref,
                 xt_ref, near_ref, gate_ref, o_ref, qs_ref, m_ref, l_ref, acc_ref, out_ref, *, tiles_per_batch):
    u = pl.program_id(0) % tiles_per_batch
    nc = kc_ref.shape[0]
    ns = nc // 2
    def gate(h, br):
        c = h * 3 + br
        return gate_ref[:, c:c + 1]

    c = lax.broadcasted_iota(jnp.int32, (nc, LANES), 0)
    lane = lax.broadcasted_iota(jnp.int32, (nc, LANES), 1)
    rel = jnp.where(c < ns, 2 * c, 2 * c - (nc - 1)) - (8 * u - 4)
    near_hit = jnp.where(rel == lanem_ref[0:1, :], 1.0, 0.0)
    future_hit = jnp.where(rel >= NEAR_BLOCKS, 1.0, 0.0)
    onehot = jnp.where(lane < 3 * NEAR_BLOCKS, near_hit, jnp.where(lane == 3 * NEAR_BLOCKS, future_hit, 0.0))
    kk_c = jnp.concatenate([kc_ref[...], onehot.astype(BF16)], axis=1)
    vc = vc_ref[...]
    imp = [jnp.zeros((TQ, ns), F32) for _ in range(N_KV)]
    for h in range(N_HEADS):
        qh = jnp.concatenate([qq_ref[h], dtab_ref[h * TQ:(h + 1) * TQ, :]], axis=1)
        s = lax.dot_general(qh, kk_c, _NT, preferred_element_type=F32)
        mx = jnp.max(s, axis=1, keepdims=True)
        p = jnp.where(s > 0.1 * NEG, jnp.exp(s - mx), 0.0)
        pn = p / jnp.maximum(jnp.sum(p, axis=1, keepdims=True), 1e-30)
        imp[h // REP] = imp[h // REP] + (pn[:, :ns] + pn[:, ns:])
        out_ref[h] = gate(h, 0) * jnp.dot(pn.astype(BF16), vc, preferred_element_type=F32)

    imp_t = jnp.concatenate(imp, axis=0).T
    a = lax.broadcasted_iota(jnp.int32, imp_t.shape, 1) & (TQ - 1)
    cur = (TQ // SEL_BLOCK) * u + (a >> 6)
    neg = _select_blocks(imp_t, cur).T.astype(BF16)
    for h in range(N_HEADS):
        g = h // REP
        qs_ref[h] = jnp.concatenate([qq_ref[h], neg[g * TQ:(g + 1) * TQ, :]], axis=1)

    def reset():
        m_ref[...] = jnp.full(m_ref.shape, -1e38, F32)
        l_ref[...] = jnp.zeros(l_ref.shape, F32)
        acc_ref[...] = jnp.zeros(acc_ref.shape, F32)

    def finish(br):
        for h in range(N_HEADS):
            out_ref[h] = out_ref[h] + gate(h, br) * (acc_ref[h] / l_ref[h])

    reset()

    def sel_tile(kt, col):
        rows = pl.ds(pl.multiple_of(kt * TK, TK), TK)
        kk = jnp.concatenate([ksel_ref[rows, :], xt_ref[rows, :]], axis=1)
        v = vsel_ref[rows, :]
        for h in range(N_HEADS):
            s = lax.dot_general(qs_ref[h], kk, _NT, preferred_element_type=F32)
            if col is not None:
                s = s + near_ref[h, :, col:col + TK]
            _softmax_step(s, v, m_ref, l_ref, acc_ref, h)

    def far_body(kt, carry):
        sel_tile(kt, None)
        return carry

    lax.fori_loop(0, u - 1, far_body, 0)

    @pl.when(u >= 1)
    def _():
        sel_tile(u - 1, 0)

    sel_tile(u, TK)
    finish(1)

    reset()

    def win_tile(kt, col):
        rows = pl.ds(pl.multiple_of(kt * TK, TK), TK)
        kk = kwin_ref[rows, :]
        v = vwin_ref[rows, :]
        for h in range(N_HEADS):
            s = lax.dot_general(qq_ref[h], kk, _NT, preferred_element_type=F32)
            if col is None:
                qa = lax.broadcasted_iota(jnp.int32, s.shape, 0)
                kj = lax.broadcasted_iota(jnp.int32, s.shape, 1)
                s = jnp.where(kj > qa, s, NEG)
            else:
                s = s + near_ref[h, :, col:col + TK]
            _softmax_step(s, v, m_ref, l_ref, acc_ref, h)

    @pl.when(u >= 2)
    def _():
        win_tile(u - 2, None)

    @pl.when(u >= 1)
    def _():
        win_tile(u - 1, 0)

    win_tile(u, TK)
    finish(2)

    lane_o = lax.broadcasted_iota(jnp.int32, (TQ, LANES), 1)
    for i in range(N_HEADS // 2):
        left, right = out_ref[2 * i], out_ref[2 * i + 1]
        if (2 * i) // REP == 0:
            right = pltpu.roll(right, HEAD_DIM, 1)
        else:
            left = pltpu.roll(left, HEAD_DIM, 1)
        o_ref[:, i * LANES:(i + 1) * LANES] = jnp.where(lane_o < HEAD_DIM, left, right).astype(BF16)


def _attention_prompt(qq, kc, vc, ksel, vsel, kwin, vwin, gates, tables, batch, seq):
    near, dtab, lane_m, xt = tables
    n = batch * seq
    tpb = seq // TQ
    nc = seq // CMP_BLOCK
    per_batch = lambda rows: pl.BlockSpec((rows, LANES), lambda i: (i // tpb, 0))
    full = lambda a: pl.BlockSpec(a.shape, lambda i: (0,) * a.ndim)
    return pl.pallas_call(
        functools.partial(_attn_kernel, tiles_per_batch=tpb),
        grid=(n // TQ,),
        in_specs=[pl.BlockSpec((N_HEADS, TQ, LANES), lambda i: (0, i, 0)),
                  per_batch(nc), per_batch(nc), full(dtab), full(lane_m),
                  per_batch(seq), per_batch(seq), per_batch(seq), per_batch(seq),
                  full(xt), full(near), pl.BlockSpec((TQ, LANES), lambda i: (i, 0))],
        out_specs=pl.BlockSpec((TQ, ATT_WIDTH), lambda i: (i, 0)),
        out_shape=jax.ShapeDtypeStruct((n, ATT_WIDTH), BF16),
        scratch_shapes=[pltpu.VMEM((N_HEADS, TQ, 2 * LANES), BF16),
                        pltpu.VMEM((N_HEADS, TQ, 1), F32), pltpu.VMEM((N_HEADS, TQ, 1), F32),
                        pltpu.VMEM((N_HEADS, TQ, LANES), F32), pltpu.VMEM((N_HEADS, TQ, LANES), F32)],
        compiler_params=_cparams(("arbitrary",)),
        name="nsa_prompt",
    )(qq, kc, vc, dtab, lane_m, ksel, vsel, kwin, vwin, xt, near, gates)


def _block_onehot_rows(seq):
    j = np.arange(seq)[:, None] // SEL_BLOCK
    return jnp.asarray((j == np.arange(LANES)[None, :]).astype(np.float32)).astype(BF16)


def _even_odd(x, nc):
    x = x.reshape(-1, nc // 2, 2, x.shape[-1])
    return jnp.concatenate([x[:, :, 0], x[:, :, 1]], axis=1).reshape(-1, x.shape[-1])


def _sample_bias_tables(rel_bias, past_len, t_new, n_win):
    t = np.arange(t_new)[:, None]
    nc = past_len // CMP_BLOCK
    tb_full = lambda dist, ok: jnp.where(jnp.asarray(ok), rel_bias.astype(F32).T[:, _np_bucket(dist)], NEG)
    dist_c = past_len + t - (CMP_BLOCK * np.arange(nc)[None, :] + CMP_BLOCK - 1)
    dist_s = past_len + t - np.arange(past_len)[None, :]
    jn = np.arange(LANES)[None, :]
    dist_n = t - jn
    dist_w = n_win + t - np.arange(n_win)[None, :]
    rows = lambda x: x.reshape(N_HEADS * t_new, x.shape[-1])
    return (rows(tb_full(dist_c, dist_c >= 0)), rows(tb_full(dist_s, dist_s >= 0)),
            rows(tb_full(dist_n, (dist_n >= 0) & (jn < t_new))),
            rows(tb_full(dist_w, (dist_w >= 0) & (dist_w < WINDOW))))


def _attn_sample_kernel(pt_ref, *refs, n_pages, t_new):
    del pt_ref
    kvc_refs = refs[:n_pages]
    sel_refs = refs[n_pages:2 * n_pages]
    (qq_ref, seln_ref, winn_ref, winb_ref, gate_ref, xt_ref, bc_ref, bs_ref, bn_ref, bw_ref,
     rsum_ref, pair_ref, o_ref, qs_ref, out_ref) = refs[2 * n_pages:]

    def gate_col(br):
        return jnp.concatenate([gate_ref[:, h * 3 + br:h * 3 + br + 1] for h in range(N_HEADS)], axis=0)

    def softmax_tiles(q, tiles):
        m = l = acc = None
        for tile in tiles:
            k, v, bias = tile()
            s = lax.dot_general(q, k, _NT, preferred_element_type=F32) + bias
            mt = jnp.max(s, axis=1, keepdims=True)
            if m is None:
                m_new = mt
                p = jnp.exp(s - m_new)
                l = jnp.sum(p, axis=1, keepdims=True)
                acc = jnp.dot(p.astype(BF16), v, preferred_element_type=F32)
            else:
                m_new = jnp.maximum(m, mt)
                alpha = jnp.exp(m - m_new)
                p = jnp.exp(s - m_new)
                l = alpha * l + jnp.sum(p, axis=1, keepdims=True)
                acc = alpha * acc + jnp.dot(p.astype(BF16), v, preferred_element_type=F32)
            m = m_new
        return acc / l

    qs_ref[:, :LANES] = qq_ref[...]

    kvc = jnp.concatenate([r[...] for r in kvc_refs], axis=0)
    s = lax.dot_general(qq_ref[...], kvc[:, :LANES].astype(BF16), _NT, preferred_element_type=F32) + bc_ref[...]
    mx = jnp.max(s, axis=1, keepdims=True)
    p = jnp.where(s > 0.1 * NEG, jnp.exp(s - mx), 0.0)
    pn = p / jnp.maximum(jnp.sum(p, axis=1, keepdims=True), 1e-30)
    out_ref[...] = gate_col(0) * jnp.dot(pn.astype(BF16), kvc[:, LANES:].astype(BF16), preferred_element_type=F32)

    hp = lax.Precision.HIGHEST
    imp = jnp.dot(jnp.dot(rsum_ref[...], pn, precision=hp, preferred_element_type=F32), pair_ref[...],
                  precision=hp, preferred_element_type=F32)
    cur = (n_pages * PAGE_SIZE) // SEL_BLOCK
    neg = _select_blocks(imp.T, cur, unroll=True).T
    qs_ref[:, LANES:] = jnp.concatenate(
        [neg[(h // REP) * t_new:(h // REP + 1) * t_new, :] for h in range(N_HEADS)], axis=0).astype(BF16)

    def sel_tile(ref, j, bias_ref, col):
        onehot = xt_ref[j * PAGE_SIZE:(j + 1) * PAGE_SIZE, :]
        return (jnp.concatenate([ref[:, :LANES].astype(BF16), onehot], axis=1), ref[:, LANES:].astype(BF16),
                bias_ref[:, col:col + PAGE_SIZE])

    tiles = [functools.partial(sel_tile, sel_refs[j], j, bs_ref, j * PAGE_SIZE) for j in range(n_pages)]
    tiles.append(functools.partial(sel_tile, seln_ref, n_pages, bn_ref, 0))
    out_ref[...] += gate_col(1) * softmax_tiles(qs_ref[...], tiles)

    def win_tile(ref, bias_ref):
        return ref[:, :LANES].astype(BF16), ref[:, LANES:].astype(BF16), bias_ref[...]

    tiles = [functools.partial(win_tile, winb_ref, bw_ref), functools.partial(win_tile, winn_ref, bn_ref)]
    out_ref[...] += gate_col(2) * softmax_tiles(qq_ref[...], tiles)

    lane_o = lax.broadcasted_iota(jnp.int32, (t_new, LANES), 1)
    for i in range(N_HEADS // 2):
        left = out_ref[2 * i * t_new:(2 * i + 1) * t_new, :]
        right = out_ref[(2 * i + 1) * t_new:(2 * i + 2) * t_new, :]
        if (2 * i) // REP == 0:
            right = pltpu.roll(right, HEAD_DIM, 1)
        else:
            left = pltpu.roll(left, HEAD_DIM, 1)
        o_ref[:, i * LANES:(i + 1) * LANES] = jnp.where(lane_o < HEAD_DIM, left, right)


def _attention_sample(page_table, kvc_pages, sel_cache, layer, qq, sel_new, win_new, win_buf, gates, tables, t_new):
    n_seq, n_pages = page_table.shape
    xt, bc, bs, bn, bw, rsum, pair = tables
    n_win = win_buf.shape[2]
    nq = N_HEADS * t_new
    page_spec = lambda rows, j: pl.BlockSpec((None, None, rows, KV_WIDTH), lambda b, pt, j=j: (layer, pt[b, j], 0, 0))
    full = lambda a: pl.BlockSpec(a.shape, lambda b, pt: (0,) * a.ndim)
    new = pl.BlockSpec((None, PAGE_SIZE, KV_WIDTH), lambda b, pt: (b, 0, 0))
    in_specs = ([page_spec(PAGE_SIZE // CMP_BLOCK, j) for j in range(n_pages)]
                + [page_spec(PAGE_SIZE, j) for j in range(n_pages)]
                + [pl.BlockSpec((None, nq, LANES), lambda b, pt: (b, 0, 0)), new, new,
                   pl.BlockSpec((None, None, n_win, KV_WIDTH), lambda b, pt: (layer, b, 0, 0)),
                   pl.BlockSpec((t_new, LANES), lambda b, pt: (b, 0))]
                + [full(a) for a in (xt, bc, bs, bn, bw, rsum, pair)])
    return pl.pallas_call(
        functools.partial(_attn_sample_kernel, n_pages=n_pages, t_new=t_new),
        grid_spec=pltpu.PrefetchScalarGridSpec(
            num_scalar_prefetch=1, grid=(n_seq,), in_specs=in_specs,
            out_specs=pl.BlockSpec((t_new, ATT_WIDTH), lambda b, pt: (b, 0)),
            scratch_shapes=[pltpu.VMEM((nq, 2 * LANES), BF16), pltpu.VMEM((nq, LANES), F32)]),
        out_shape=jax.ShapeDtypeStruct((n_seq * t_new, ATT_WIDTH), F32),
        compiler_params=_cparams(("arbitrary",)),
        name="nsa_sample",
    )(page_table, *([kvc_pages] * n_pages), *([sel_cache] * n_pages), qq, sel_new, win_new, win_buf, gates,
      xt, bc, bs, bn, bw, rsum, pair)


def _sample_sum_matrices(t_new, nc):
    rsum = np.zeros((LANES, N_HEADS * t_new), np.float32)
    for h in range(N_HEADS):
        for t in range(t_new):
            rsum[(h // REP) * t_new + t, h * t_new + t] = 1.0
    pair = np.zeros((nc, LANES), np.float32)
    pair[np.arange(nc), np.arange(nc) // 2] = 1.0
    return jnp.asarray(rsum), jnp.asarray(pair)


def _rank_before(vals, k):
    r = jnp.zeros(vals[k].shape, jnp.int32)
    for j, vj in enumerate(vals):
        if j < k:
            r = r + jnp.where(vj >= vals[k], 1, 0)
        elif j > k:
            r = r + jnp.where(vj > vals[k], 1, 0)
    return r


def _route(s_rows, b_rows):
    scores = []
    for g in range(N_GROUPS):
        b0, b1, b2, b3 = b_rows[4 * g:4 * g + 4]
        hi01, lo01, hi23, lo23 = jnp.maximum(b0, b1), jnp.minimum(b0, b1), jnp.maximum(b2, b3), jnp.minimum(b2, b3)
        top1 = jnp.maximum(hi01, hi23)
        top2 = jnp.maximum(jnp.maximum(lo01, lo23), jnp.minimum(hi01, hi23))
        scores.append(top1 + top2)
    in_group = [_rank_before(scores, g) == 0 for g in range(N_GROUPS)]

    def pick(rows, k):
        out = rows[4 * (N_GROUPS - 1) + k]
        for g in range(N_GROUPS - 2, -1, -1):
            out = jnp.where(in_group[g], rows[4 * g + k], out)
        return out

    bv = [pick(b_rows, k) for k in range(EXPERTS_PER_GROUP)]
    sv = [pick(s_rows, k) for k in range(EXPERTS_PER_GROUP)]
    w = [jnp.where(_rank_before(bv, k) < 2, sv[k], 0.0) for k in range(EXPERTS_PER_GROUP)]
    den = (w[0] + w[1]) + (w[2] + w[3])
    return [jnp.where(in_group[e // 4], w[e % 4] / den, 0.0) for e in range(N_EXPERTS)]


def _outproj_kernel(ps_ref, att_ref, x_ref, mod_ref, wout_ref, lng_ref, lnb_ref, rwt_ref, rb_ref,
                    x1_ref, h2_ref, g_ref):
    half = ps_ref.shape[1]
    mix = (jnp.dot(ps_ref[...], wout_ref[:half, :], preferred_element_type=F32)
           + jnp.dot(att_ref[...], wout_ref[half:, :], preferred_element_type=F32))
    x1 = _layer_norm(ALPHA * x_ref[...] + (1.0 + mod_ref[2]) * mix, lng_ref[...], lnb_ref[...])
    x1_ref[...] = x1
    h2 = x1 * (1.0 + mod_ref[4]) + mod_ref[3]
    h2_ref[...] = h2.astype(BF16)
    st = jax.nn.sigmoid(lax.dot_general(rwt_ref[...], h2, _NT, precision=lax.Precision.HIGHEST,
                                        preferred_element_type=F32))
    s_rows = [st[e:e + 1, :] for e in range(N_EXPERTS)]
    b_rows = [s_rows[e] + rb_ref[e:e + 1, :] for e in range(N_EXPERTS)]
    gt = jnp.concatenate(_route(s_rows, b_rows) + [jnp.zeros((LANES - N_EXPERTS, st.shape[1]), F32)], axis=0)
    g_ref[...] = gt.T


def _out_proj(ps, att, x, mod, wout, ln_g, ln_b, rwt, rb, tm, tiles_per_batch):
    n = x.shape[0]
    row = lambda wd: pl.BlockSpec((tm, wd), lambda i: (i, 0))
    full = lambda a: pl.BlockSpec(a.shape, lambda i: (0,) * a.ndim)
    return pl.pallas_call(
        _outproj_kernel,
        grid=(n // tm,),
        in_specs=[row(512), row(512), row(D_MODEL), _mod_spec(mod, tm, tiles_per_batch),
                  full(wout), full(ln_g), full(ln_b), full(rwt), full(rb)],
        out_specs=[row(D_MODEL), row(D_MODEL), row(LANES)],
        out_shape=[jax.ShapeDtypeStruct((n, D_MODEL), F32), jax.ShapeDtypeStruct((n, D_MODEL), BF16),
                   jax.ShapeDtypeStruct((n, LANES), F32)],
        compiler_params=_cparams(("arbitrary",)),
        name="out_proj",
    )(ps, att, x, mod, wout, ln_g, ln_b, rwt, rb)


def _moe_kernel(h2_ref, g_ref, wg_ref, wu_ref, wd_ref, x1_ref, mod_ref, lng_ref, lnb_ref, o_ref, acc_ref):
    e = pl.program_id(1)

    @pl.when(e == 0)
    def _():
        acc_ref[...] = jnp.zeros_like(acc_ref)

    x = h2_ref[...]
    gate = jnp.dot(x, wg_ref[0, 0], preferred_element_type=F32)
    up = jnp.dot(x, wu_ref[0, 0], preferred_element_type=F32)
    act = (gate * jax.nn.sigmoid(gate) * up).astype(BF16)
    y = jnp.dot(act, wd_ref[0, 0], preferred_element_type=F32)
    lane = lax.broadcasted_iota(jnp.int32, g_ref.shape, 1)
    w = jnp.sum(jnp.where(lane == e, g_ref[...], 0.0), axis=1, keepdims=True)
    acc_ref[...] += y * w

    @pl.when(e == pl.num_programs(1) - 1)
    def _():
        o_ref[...] = _layer_norm(ALPHA * x1_ref[...] + (1.0 + mod_ref[5]) * acc_ref[...], lng_ref[...], lnb_ref[...])


def _moe(h2, gates, wg, wu, wd, layer, x1, mod, ln_g, ln_b, tm, tiles_per_batch):
    n = h2.shape[0]
    row = lambda wd_: pl.BlockSpec((tm, wd_), lambda i, e: (i, 0))
    full = lambda a: pl.BlockSpec(a.shape, lambda i, e: (0,) * a.ndim)
    wspec = lambda a: pl.BlockSpec((1, 1) + a.shape[2:], lambda i, e: (layer, e, 0, 0))
    return pl.pallas_call(
        _moe_kernel,
        grid=(n // tm, N_EXPERTS),
        in_specs=[row(D_MODEL), row(LANES), wspec(wg), wspec(wu), wspec(wd), row(D_MODEL),
                  _mod_spec(mod, tm, tiles_per_batch), full(ln_g), full(ln_b)],
        out_specs=row(D_MODEL),
        out_shape=jax.ShapeDtypeStruct((n, D_MODEL), F32),
        scratch_shapes=[pltpu.VMEM((tm, D_MODEL), F32)],
        compiler_params=_cparams(("arbitrary", "arbitrary")),
        name="moe",
    )(h2, gates, wg, wu, wd, x1, mod, ln_g, ln_b)


TM_PROJ = 256
TM_MOE = 512


def _channel_mix(ps, att, x, mod, lw, sw, layer, tiles_per_seq):
    x1, h2, gts = _out_proj(ps, att, x, mod, lw["wout"], lw["ln_g0"], lw["ln_b0"], sw["rwt"], sw["rb"],
                            TM_PROJ, tiles_per_seq(TM_PROJ))
    return _moe(h2, gts, sw["wg"], sw["wu"], sw["wd"], layer, x1, mod, lw["ln_g1"], lw["ln_b1"],
                TM_MOE, tiles_per_seq(TM_MOE))


def _prompt_layer(x, mod, lw, sw, layer, batch, seq):
    n = batch * seq
    tiles = lambda tm: seq // tm
    (p, u, vn, cmp_raw, sel_raw, win_raw, gates, qq, ksel, vsel, kwin, vwin) = _in_proj(
        x, mod, lw["w_proj"], lw["sgu_ln_g"], lw["sgu_ln_b"], TM_PROJ, tiles(TM_PROJ))
    ps = _mixers(p, u, vn, *lw["mix_prompt"], tiles_per_batch=tiles(SGU_CHUNK))
    nc = seq // CMP_BLOCK
    kvc = _compress(cmp_raw.reshape(1, n, KV_WIDTH), sw["pe_rows"], sw["w1bd"], sw["w2bd"], nc, layer0=layer)[0]
    kc = _even_odd(kvc[:, :LANES].astype(BF16), nc)
    vc = _even_odd(kvc[:, LANES:].astype(BF16), nc)
    att = _attention_prompt(qq, kc, vc, ksel, vsel, kwin, vwin, gates, sw["prompt_tables"], batch, seq)
    x2 = _channel_mix(ps, att, x, mod, lw, sw, layer, tiles)
    return x2, (cmp_raw, sel_raw, win_raw, p)


def _sample_layer(x, mod, lw, sw, layer, page_table, kvc_pages, sel_cache, win_state, pool_state, n_seq, t_new):
    tiles = lambda tm: 1
    (p, u, vn, cmp_raw, sel_raw, win_raw, gates, qq, _, _, _, _) = _in_proj(
        x, mod, lw["w_proj"], lw["sgu_ln_g"], lw["sgu_ln_b"], TM_PROJ, 1)
    p_ext = jnp.concatenate([pool_state[layer], p.reshape(n_seq, t_new, POOL_WIDTH)], axis=1)
    hist = jnp.pad(p_ext, ((0, 0), (3 * t_new - p_ext.shape[1], 0), (0, 0)))
    chunks = [hist[:, k * t_new:(k + 1) * t_new].reshape(n_seq * t_new, POOL_WIDTH) for k in range(3)]
    ps = _mixers(chunks, u, vn, *lw["mix_sample"], t_new=t_new)
    q_rows = jnp.transpose(qq.reshape(N_HEADS, n_seq, t_new, LANES), (1, 0, 2, 3)).reshape(n_seq, N_HEADS * t_new, LANES)
    new_page = lambda raw: jnp.pad(raw.reshape(n_seq, t_new, KV_WIDTH), ((0, 0), (0, PAGE_SIZE - t_new), (0, 0)))
    att = _attention_sample(page_table, kvc_pages, sel_cache, layer, q_rows, new_page(sel_raw), new_page(win_raw),
                            win_state, gates, sw["sample_tables"], t_new)
    x2 = _channel_mix(ps, att.astype(BF16), x, mod, lw, sw, layer, tiles)
    win_all = jnp.concatenate([win_state[layer], win_raw.reshape(n_seq, t_new, KV_WIDTH)], axis=1)
    return x2, (cmp_raw, sel_raw, win_all[:, t_new:], p_ext[:, p_ext.shape[1] - POOL_BUF:], vn)


def kernel(x_prompt, x_sample, cache_cmp_kv, cache_sel_kv, state_win_kv, state_pool, page_table, c_prompt, c_sample,
           w_in, w_out, pool_w, pool_scale, sgu_ln_g, sgu_ln_b, sgu_w, sgu_b, cmp_pe, cmp_w1, cmp_w2, rel_bias,
           w_mod, b_mod, ln_g, ln_b, router_w, router_b, moe_w_gate, moe_w_up, moe_w_down):
    batch, seq, d = x_prompt.shape
    n_seq, t_new, _ = x_sample.shape
    depth = w_in.shape[0]
    n_pages = page_table.shape[1]
    past_len = n_pages * PAGE_SIZE
    n_phys = cache_cmp_kv.shape[1]
    n_win = state_win_kv.shape[2]
    assert seq // SEL_BLOCK == LANES and seq % TM_MOE == 0 and (n_seq * t_new) % TM_MOE == 0
    assert POOL_BUF + 1 == 2 * t_new and past_len % TK == 0

    n_c = batch + n_seq
    c_all = jnp.pad(jnp.concatenate([c_prompt, c_sample], axis=0), ((0, -n_c % 8), (0, 0)))
    m_all = _modulation(c_all, w_mod, b_mod)
    mod_p = m_all[:, :batch].reshape(depth, batch, 6, 1, d)
    mod_s = jnp.transpose(jnp.repeat(m_all[:, batch:n_c].reshape(depth, n_seq, 6, d), t_new, axis=1), (0, 2, 1, 3))

    pe_rows, w1bd, w2bd = _prep_compress_weights(cmp_pe, cmp_w1, cmp_w2)
    near, dtab, lane_m = _prompt_bias_tables(rel_bias)
    nc_past = past_len // CMP_BLOCK
    shared = {
        "pe_rows": pe_rows, "w1bd": w1bd, "w2bd": w2bd,
        "prompt_tables": (near, dtab, lane_m, _block_onehot_rows(seq)),
        "sample_tables": (_block_onehot_rows(past_len + PAGE_SIZE),
                          *_sample_bias_tables(rel_bias, past_len, t_new, n_win),
                          *_sample_sum_matrices(t_new, nc_past)),
        "rwt": router_w.T, "rb": router_b.reshape(N_EXPERTS, 1),
        "wg": moe_w_gate.astype(BF16), "wu": moe_w_up.astype(BF16), "wd": moe_w_down.astype(BF16),
    }
    layers = []
    for l in range(depth):
        layers.append({
            "w_proj": _prep_w_proj(w_in[l]), "wout": w_out[l].astype(BF16),
            "sgu_ln_g": sgu_ln_g[l].reshape(1, -1), "sgu_ln_b": sgu_ln_b[l].reshape(1, -1),
            "mix_prompt": _prep_mixer_weights(pool_w[l], pool_scale[l], sgu_w[l], sgu_b[l], SGU_CHUNK),
            "mix_sample": _prep_mixer_weights(pool_w[l], pool_scale[l], sgu_w[l], sgu_b[l], t_new),
            "ln_g0": ln_g[l, 0].reshape(1, d), "ln_b0": ln_b[l, 0].reshape(1, d),
            "ln_g1": ln_g[l, 1].reshape(1, d), "ln_b1": ln_b[l, 1].reshape(1, d),
        })

    kvc_pages = _compress(cache_cmp_kv.reshape(depth, n_phys * PAGE_SIZE, KV_WIDTH), pe_rows, w1bd, w2bd, 256)
    kvc_pages = kvc_pages.reshape(depth, n_phys, PAGE_SIZE // CMP_BLOCK, KV_WIDTH)
    sel_cache = cache_sel_kv.reshape(depth, n_phys, PAGE_SIZE, KV_WIDTH)
    win_state = state_win_kv.reshape(depth, n_seq, n_win, KV_WIDTH)

    xp = x_prompt.reshape(batch * seq, d)
    xs = x_sample.reshape(n_seq * t_new, d)
    outs_p, outs_s = [], []
    for l in range(depth):
        xp, st = _prompt_layer(xp, mod_p[l], layers[l], shared, l, batch, seq)
        outs_p.append(st)
        xs, st = _sample_layer(xs, mod_s[l], layers[l], shared, l, page_table, kvc_pages, sel_cache, win_state,
                               state_pool, n_seq, t_new)
        outs_s.append(st)

    kv5 = lambda x, b: x.reshape(b, -1, 2, N_KV, HEAD_DIM)
    stack = lambda xs_: jnp.stack(xs_)
    w_keep = min(WINDOW, seq)
    return (xp.reshape(batch, seq, d), xs.reshape(n_seq, t_new, d),
            stack([kv5(o[0], batch) for o in outs_p]), stack([kv5(o[0], n_seq) for o in outs_s]),
            stack([kv5(o[1], batch) for o in outs_p]), stack([kv5(o[1], n_seq) for o in outs_s]),
            stack([kv5(o[2], batch)[:, seq - w_keep:] for o in outs_p]), stack([kv5(o[2], n_seq) for o in outs_s]),
            stack([o[3].reshape(batch, seq, POOL_WIDTH)[:, seq - POOL_BUF:] for o in outs_p]),
            stack([o[3] for o in outs_s]),
            stack([o[4].reshape(n_seq, t_new, SGU_WIDTH) for o in outs_s]))
```

```python
import functools
import math

import numpy as np
import jax
import jax.numpy as jnp
from jax import lax
from jax.experimental import pallas as pl
from jax.experimental.pallas import tpu as pltpu

F32 = jnp.float32
BF16 = jnp.bfloat16

D_MODEL = 1024
POOL_WIDTH = 256
SGU_WIDTH = 256
ATT_WIDTH = 512
POOL_WINDOWS = (2, 4, 8, 16)
POOL_GROUP_DIM = 64
POOL_BUF = 15
SGU_GROUPS = 4
SGU_CHUNK = 128
N_HEADS = 8
HEAD_DIM = 64
N_KV = 2
REP = 4
CMP_BLOCK = 32
CMP_HIDDEN = 128
SEL_BLOCK = 64
N_SEL = 16
WINDOW = 512
N_BUCKETS = 32
REL_MAX_DIST = 128
N_EXPERTS = 16
N_GROUPS = 4
EXPERTS_PER_GROUP = 4
D_EXPERT = 512
DEPTH = 2
ALPHA = (2 * DEPTH) ** 0.25
LN_EPS = 1e-5
FORCED_SCORE = 1e4
NEG = -1e30
PAGE_SIZE = 128

KV_WIDTH = 2 * N_KV * HEAD_DIM
LANES = 128
VMEM_LIMIT = 56 * 1024 * 1024

C_P, C_U, C_V, C_CMP, C_SEL, C_WIN, C_GATE, C_Q = 0, 256, 512, 768, 1024, 1280, 1536, 1664
W_PROJ = C_Q + N_HEADS * LANES

TQ = 256
TK = 256
NEAR_BLOCKS = 12
FAR_BUCKET_DIST = 113


def _cparams(sem):
    return pltpu.CompilerParams(dimension_semantics=sem, vmem_limit_bytes=VMEM_LIMIT)


def _layer_norm(x, g, b):
    mu = jnp.mean(x, axis=-1, keepdims=True)
    xc = x - mu
    var = jnp.mean(xc * xc, axis=-1, keepdims=True)
    return xc * lax.rsqrt(var + LN_EPS) * g + b


def _mod_kernel(c_ref, w_ref, b_ref, o_ref):
    c = c_ref[...]
    a = (c * jax.nn.sigmoid(c)).astype(BF16)
    o_ref[0] = jnp.dot(a, w_ref[0].astype(BF16), preferred_element_type=F32) + b_ref[0]


def _modulation(c_all, w_mod, b_mod):
    n, d = c_all.shape
    depth, _, w = w_mod.shape
    tn = 1536
    return pl.pallas_call(
        _mod_kernel,
        grid=(depth, w // tn),
        in_specs=[pl.BlockSpec((n, d), lambda l, j: (0, 0)),
                  pl.BlockSpec((1, d, tn), lambda l, j: (l, 0, j)),
                  pl.BlockSpec((1, 1, tn), lambda l, j: (l, 0, j))],
        out_specs=pl.BlockSpec((1, n, tn), lambda l, j: (l, 0, j)),
        out_shape=jax.ShapeDtypeStruct((depth, n, w), F32),
        compiler_params=_cparams(("arbitrary", "arbitrary")),
        name="adaln_mod",
    )(c_all, w_mod, b_mod.reshape(depth, 1, w))


def _mod_spec(mod, tm, tiles_per_batch):
    if mod.ndim == 4:
        return pl.BlockSpec((None, 6, 1, D_MODEL), lambda i, *_: (i // tiles_per_batch, 0, 0, 0))
    return pl.BlockSpec((6, tm, D_MODEL), lambda i, *_: (0, i, 0))


def _inproj_kernel(x_ref, mod_ref, w_ref, lng_ref, lnb_ref, p_ref, u_ref, vn_ref, qq_ref, *rest, channel_major):
    h = (x_ref[...] * (1.0 + mod_ref[1]) + mod_ref[0]).astype(BF16)

    def seg(a, b):
        return jnp.dot(h, w_ref[:, a:b], preferred_element_type=F32)

    p_ref[...] = seg(C_P, C_U)
    u_ref[...] = seg(C_U, C_V)
    vn_ref[...] = _layer_norm(seg(C_V, C_CMP), lng_ref[...], lnb_ref[...])
    for hd in range(N_HEADS):
        qq_ref[hd] = seg(C_Q + hd * LANES, C_Q + (hd + 1) * LANES).astype(BF16)
    cmp = seg(C_CMP, C_SEL)
    sel = seg(C_SEL, C_WIN)
    win = seg(C_WIN, C_GATE)
    gates = jax.nn.sigmoid(seg(C_GATE, C_Q))
    if not channel_major:
        cmp_ref, sel_ref, win_ref, gate_ref = rest
        cmp_ref[...] = cmp
        sel_ref[...] = sel
        win_ref[...] = win
        gate_ref[...] = gates
        return
    cmpk_ref, cmpv_ref, cmpt_ref, selt_ref, wint_ref, gatet_ref, ksel_ref, vselt_ref, kwin_ref, vwint_ref = rest
    cmpk_ref[...] = cmp[:, :LANES]
    cmpv_ref[...] = cmp[:, LANES:]
    cmpt_ref[...] = cmp.T
    sel_t = sel.T
    selt_ref[...] = sel_t
    ksel_ref[...] = sel[:, :LANES].astype(BF16)
    vselt_ref[...] = sel_t[LANES:, :].astype(BF16)
    win_t = win.T
    wint_ref[...] = win_t
    kwin_ref[...] = win[:, :LANES].astype(BF16)
    vwint_ref[...] = win_t[LANES:, :].astype(BF16)
    gatet_ref[...] = gates.T


def _in_proj(x, mod, w_proj, ln_g, ln_b, tm, tiles_per_batch, batch=None):
    n = x.shape[0]
    row = lambda wd: pl.BlockSpec((tm, wd), lambda i: (i, 0))
    full = lambda a: pl.BlockSpec(a.shape, lambda i: (0,) * a.ndim)
    f32o = lambda wd: jax.ShapeDtypeStruct((n, wd), F32)
    out_specs = [row(256), row(256), row(256), pl.BlockSpec((N_HEADS, tm, LANES), lambda i: (0, i, 0))]
    out_shape = [f32o(256), f32o(256), f32o(256), jax.ShapeDtypeStruct((N_HEADS, n, LANES), BF16)]
    if batch is None:
        out_specs += [row(256), row(256), row(256), row(LANES)]
        out_shape += [f32o(256), f32o(256), f32o(256), f32o(LANES)]
    else:
        seq = n // batch
        chan = lambda c: pl.BlockSpec((None, c, tm), lambda i: (i // tiles_per_batch, 0, i % tiles_per_batch))
        chan_o = lambda c, dt: jax.ShapeDtypeStruct((batch, c, seq), dt)
        out_specs += [row(LANES), row(LANES), chan(256), chan(256), chan(256), pl.BlockSpec((LANES, tm), lambda i: (0, i)),
                      row(LANES), chan(LANES), row(LANES), chan(LANES)]
        out_shape += [f32o(LANES), f32o(LANES),
                      chan_o(256, F32), chan_o(256, F32), chan_o(256, F32), jax.ShapeDtypeStruct((LANES, n), F32),
                      jax.ShapeDtypeStruct((n, LANES), BF16), chan_o(LANES, BF16),
                      jax.ShapeDtypeStruct((n, LANES), BF16), chan_o(LANES, BF16)]
    return pl.pallas_call(
        functools.partial(_inproj_kernel, channel_major=batch is not None),
        grid=(n // tm,),
        in_specs=[row(D_MODEL), _mod_spec(mod, tm, tiles_per_batch), full(w_proj), full(ln_g), full(ln_b)],
        out_specs=out_specs,
        out_shape=out_shape,
        compiler_params=_cparams(("arbitrary",)),
        name="in_proj",
    )(x, mod, w_proj, ln_g, ln_b)


def _prep_w_proj(w_in):
    d = w_in.shape[0]
    q = w_in[:, 768:1280].reshape(d, N_KV, REP, HEAD_DIM) * (HEAD_DIM ** -0.5)
    qq = jnp.zeros((d, N_KV, REP, N_KV, HEAD_DIM), w_in.dtype)
    for g in range(N_KV):
        qq = qq.at[:, g, :, g, :].set(q[:, g])
    gate = jnp.pad(w_in[:, 2048:2072], ((0, 0), (0, LANES - 24)))
    cols = [w_in[:, 0:768], w_in[:, 1280:2048], gate, qq.reshape(d, N_HEADS * LANES)]
    return jnp.concatenate(cols, axis=1).astype(BF16)


def _window_sums(shifted):
    acc = shifted(0)
    sums = {}
    for k in range(1, 16):
        acc = acc + shifted(k)
        if k + 1 in POOL_WINDOWS:
            sums[k + 1] = acc
    lane = lax.broadcasted_iota(jnp.int32, acc.shape, 1)
    return jnp.where(lane < 64, sums[2], jnp.where(lane < 128, sums[4], jnp.where(lane < 192, sums[8], sums[16])))


def _mixers_tail(sums, cnt, cur, u, vn, wpool_ref, pscale_ref, wcat_ref, sbias_ref, o_ref):
    diff = (sums / cnt - cur).astype(BF16)
    pool = jnp.dot(diff, wpool_ref[...], preferred_element_type=F32) * pscale_ref[...]
    lane = lax.broadcasted_iota(jnp.int32, vn.shape, 1)
    vb = vn.astype(BF16)
    zero = jnp.zeros_like(vb)
    stacked = jnp.concatenate([jnp.where((lane // 64) == g, vb, zero) for g in range(SGU_GROUPS)], axis=0)
    mixed = jnp.dot(wcat_ref[...], stacked, preferred_element_type=F32) + sbias_ref[...]
    o_ref[:, :POOL_WIDTH] = pool.astype(BF16)
    o_ref[:, POOL_WIDTH:] = (u * mixed).astype(BF16)


def _lane_window():
    lane = lax.broadcasted_iota(jnp.int32, (SGU_CHUNK, POOL_WIDTH), 1)
    return jnp.where(lane < 64, 2, jnp.where(lane < 128, 4, jnp.where(lane < 192, 8, 16)))


def _mix_prompt_kernel(p_ref, u_ref, vn_ref, wpool_ref, pscale_ref, wcat_ref, sbias_ref, o_ref, prev_ref,
                       *, tiles_per_batch):
    t = pl.program_id(0) % tiles_per_batch

    @pl.when(t == 0)
    def _():
        prev_ref[...] = jnp.zeros_like(prev_ref)

    cur = p_ref[...]
    prev = prev_ref[...]
    row = lax.broadcasted_iota(jnp.int32, cur.shape, 0)

    def shifted(k):
        if k == 0:
            return cur
        return jnp.where(row >= k, pltpu.roll(cur, k, 0), pltpu.roll(prev, k, 0))

    sums = _window_sums(shifted)
    prev_ref[...] = cur
    cnt = jnp.minimum(_lane_window(), t * SGU_CHUNK + row + 1).astype(F32)
    _mixers_tail(sums, cnt, cur, u_ref[...], vn_ref[...], wpool_ref, pscale_ref, wcat_ref, sbias_ref, o_ref)


def _mix_sample_kernel(pa_ref, pb_ref, pc_ref, u_ref, vn_ref, wpool_ref, pscale_ref, wcat_ref, sbias_ref, o_ref,
                       *, t_new):
    a, b, c = pa_ref[...], pb_ref[...], pc_ref[...]
    rows = c.shape[0]
    t = lax.broadcasted_iota(jnp.int32, c.shape, 0) % t_new

    def shifted(k):
        if k == 0:
            return c
        hi, lo = (c, b) if k < t_new else (b, a)
        kk = k % t_new
        if kk == 0:
            return hi
        return jnp.where(t >= kk, pltpu.roll(hi, kk, 0), pltpu.roll(lo, rows - t_new + kk, 0))

    sums = _window_sums(shifted)
    cnt = _lane_window().astype(F32)
    _mixers_tail(sums, cnt, c, u_ref[...], vn_ref[...], wpool_ref, pscale_ref, wcat_ref, sbias_ref, o_ref)


def _mixers(p_in, u, vn, wpool, pscale, wcat, sbias, tiles_per_batch=None, t_new=None):
    n = u.shape[0]
    tm = SGU_CHUNK
    row = lambda wd: pl.BlockSpec((tm, wd), lambda i: (i, 0))
    full = lambda a: pl.BlockSpec(a.shape, lambda i: (0,) * a.ndim)
    weights = [wpool, pscale, wcat, sbias]
    if t_new is None:
        kern = functools.partial(_mix_prompt_kernel, tiles_per_batch=tiles_per_batch)
        ins, scratch = [p_in], [pltpu.VMEM((tm, POOL_WIDTH), F32)]
    else:
        kern = functools.partial(_mix_sample_kernel, t_new=t_new)
        ins, scratch = list(p_in), []
    return pl.pallas_call(
        kern,
        grid=(n // tm,),
        in_specs=[row(256)] * (len(ins) + 2) + [full(a) for a in weights],
        out_specs=row(512),
        out_shape=jax.ShapeDtypeStruct((n, 512), BF16),
        scratch_shapes=scratch,
        compiler_params=_cparams(("arbitrary",)),
        name="mixers",
    )(*ins, u, vn, *weights)


def _prep_mixer_weights(pool_w, pool_scale, sgu_w, sgu_b, chunk):
    wpool = jax.scipy.linalg.block_diag(*[pool_w[g] for g in range(len(POOL_WINDOWS))]).astype(BF16)
    tri = jnp.tril(jnp.ones((chunk, chunk), bool))
    w = jnp.where(tri, sgu_w[:, :chunk, :chunk], 0.0)
    reps = SGU_CHUNK // chunk
    eye = jnp.eye(reps, dtype=w.dtype)
    wt = jnp.concatenate([jnp.kron(eye, w[g]) for g in range(SGU_GROUPS)], axis=1).astype(BF16)
    b = jnp.tile(sgu_b[:, :chunk], (1, reps))
    sbias = jnp.repeat(b.T, SGU_WIDTH // SGU_GROUPS, axis=1)
    return wpool, pool_scale.reshape(1, POOL_WIDTH), wt, sbias


def _compress_mlp(token_rows, pe_ref, w1_ref, w2_ref, o_ref, nblk):
    for s in range(2):
        acc = jnp.zeros((nblk, N_KV * CMP_HIDDEN), F32)
        for j in range(CMP_BLOCK):
            xj = token_rows(s, j) + pe_ref[0, s, j:j + 1, :]
            acc = acc + jnp.dot(xj.astype(BF16), w1_ref[0, s, j], preferred_element_type=F32)
        hdn = jax.nn.gelu(acc).astype(BF16)
        o_ref[0, :, s * LANES:(s + 1) * LANES] = jnp.dot(hdn, w2_ref[0, s], preferred_element_type=F32)


def _compress_kernel(k_ref, v_ref, pe_ref, w1_ref, w2_ref, o_ref, *, nblk):
    bufs = (k_ref, v_ref)
    rows = lambda s, j: bufs[s][pl.ds(j, nblk, stride=CMP_BLOCK), :]
    _compress_mlp(rows, pe_ref, w1_ref, w2_ref, o_ref, nblk)


def _compress_pages_kernel(raw_ref, pe_ref, w1_ref, w2_ref, o_ref, k_ref, v_ref, *, nblk):
    n_pages = raw_ref.shape[1]

    def body(pg, carry):
        t = raw_ref[0, pg].T
        rows = pl.ds(pl.multiple_of(pg * PAGE_SIZE, PAGE_SIZE), PAGE_SIZE)
        k_ref[rows, :] = t[:, :LANES]
        v_ref[rows, :] = t[:, LANES:]
        return carry

    lax.fori_loop(0, n_pages, body, 0)
    bufs = (k_ref, v_ref)
    rows = lambda s, j: bufs[s][pl.ds(j, nblk, stride=CMP_BLOCK), :]
    _compress_mlp(rows, pe_ref, w1_ref, w2_ref, o_ref, nblk)


def _compress_pages(pages, pe_rows, w1bd, w2bd, pages_per_step):
    nl, n_pages, _, _ = pages.shape
    assert n_pages % pages_per_step == 0
    nblk = pages_per_step * (PAGE_SIZE // CMP_BLOCK)
    wspec = lambda a: pl.BlockSpec((1,) + a.shape[1:], lambda l, i: (l,) + (0,) * (a.ndim - 1))
    return pl.pallas_call(
        functools.partial(_compress_pages_kernel, nblk=nblk),
        grid=(nl, n_pages // pages_per_step),
        in_specs=[pl.BlockSpec((1, pages_per_step, KV_WIDTH, PAGE_SIZE), lambda l, i: (l, i, 0, 0)),
                  wspec(pe_rows), wspec(w1bd), wspec(w2bd)],
        out_specs=pl.BlockSpec((1, nblk, KV_WIDTH), lambda l, i: (l, i, 0)),
        out_shape=jax.ShapeDtypeStruct((nl, n_pages * (PAGE_SIZE // CMP_BLOCK), KV_WIDTH), F32),
        scratch_shapes=[pltpu.VMEM((pages_per_step * PAGE_SIZE, LANES), F32)] * 2,
        compiler_params=_cparams(("arbitrary", "arbitrary")),
        name="compress_pages",
    )(pages, pe_rows, w1bd, w2bd)


def _compress(k_raw, v_raw, pe_rows, w1bd, w2bd, nblk, layer):
    r = k_raw.shape[0]
    assert r % (nblk * CMP_BLOCK) == 0
    wspec = lambda a: pl.BlockSpec((1,) + a.shape[1:], lambda i: (layer,) + (0,) * (a.ndim - 1))
    rows = pl.BlockSpec((nblk * CMP_BLOCK, LANES), lambda i: (i, 0))
    return pl.pallas_call(
        functools.partial(_compress_kernel, nblk=nblk),
        grid=(r // (nblk * CMP_BLOCK),),
        in_specs=[rows, rows, wspec(pe_rows), wspec(w1bd), wspec(w2bd)],
        out_specs=pl.BlockSpec((1, nblk, KV_WIDTH), lambda i: (0, i, 0)),
        out_shape=jax.ShapeDtypeStruct((1, r // CMP_BLOCK, KV_WIDTH), F32),
        compiler_params=_cparams(("arbitrary",)),
        name="compress",
    )(k_raw, v_raw, pe_rows, w1bd, w2bd)[0]


def _prep_compress_weights(cmp_pe, cmp_w1, cmp_w2):
    nl = cmp_pe.shape[0]
    pe_rows = jnp.concatenate([cmp_pe] * N_KV, axis=-1)
    w1 = cmp_w1.reshape(nl, 2, CMP_BLOCK, HEAD_DIM, CMP_HIDDEN)
    w1bd = jnp.zeros((nl, 2, CMP_BLOCK, N_KV, HEAD_DIM, N_KV, CMP_HIDDEN), F32)
    w2bd = jnp.zeros((nl, 2, N_KV, CMP_HIDDEN, N_KV, HEAD_DIM), F32)
    for g in range(N_KV):
        w1bd = w1bd.at[:, :, :, g, :, g, :].set(w1)
        w2bd = w2bd.at[:, :, g, :, g, :].set(cmp_w2)
    return (pe_rows, w1bd.reshape(nl, 2, CMP_BLOCK, LANES, N_KV * CMP_HIDDEN).astype(BF16),
            w2bd.reshape(nl, 2, N_KV * CMP_HIDDEN, LANES).astype(BF16))


def _np_bucket(dist):
    n = np.maximum(dist, 0)
    nf = np.maximum(n, 1).astype(np.float32)
    large = 16 + (np.log(nf / np.float32(16)) / np.float32(math.log(REL_MAX_DIST / 16)) * np.float32(16)).astype(np.int32)
    return np.where(n < 16, n, np.minimum(large, N_BUCKETS - 1)).astype(np.int32)


def _bucket_values(tbt, dist):
    hit = jnp.asarray(_np_bucket(dist))[None, ..., None] == jnp.arange(N_BUCKETS, dtype=jnp.int32)
    return jnp.sum(jnp.where(hit, tbt.reshape((tbt.shape[0],) + (1,) * dist.ndim + (N_BUCKETS,)), 0.0), axis=-1)


def _bias_minus_far(rel_bias, dist):
    tbt = rel_bias.astype(F32).T
    val = _bucket_values(tbt, dist) - tbt[:, N_BUCKETS - 1].reshape((-1,) + (1,) * dist.ndim)
    return jnp.where(jnp.asarray(dist >= 0), val, NEG)


def _split3(x):
    hi = x.astype(BF16)
    r1 = x - hi.astype(F32)
    mid = r1.astype(BF16)
    lo = (r1 - mid.astype(F32)).astype(BF16)
    return hi, mid, lo


def _prompt_bias_tables(rel_bias):
    a = np.arange(TQ)[:, None]
    near = _bias_minus_far(rel_bias, np.arange(TQ)[None, :] + TK - np.arange(2 * TK)[:, None])
    dist_c = a + (4 * CMP_BLOCK - CMP_BLOCK + 1) - CMP_BLOCK * np.arange(NEAR_BLOCKS)[None, :]
    dc = _bias_minus_far(rel_bias, dist_c)
    hi, mid, lo = _split3(dc)
    cols = jnp.stack([hi, mid, lo], axis=-1).reshape(N_HEADS, TQ, 3 * NEAR_BLOCKS)
    future = jnp.full((N_HEADS, TQ, 1), NEG, F32).astype(BF16)
    pad = jnp.zeros((N_HEADS, TQ, LANES - 3 * NEAR_BLOCKS - 1), BF16)
    dtab = jnp.concatenate([cols, future, pad], axis=-1).reshape(N_HEADS * TQ, LANES)
    lane = np.arange(LANES)
    lane_m = np.where(lane < 3 * NEAR_BLOCKS, lane // 3, -1000).astype(np.int32)
    return near, dtab, jnp.asarray(np.tile(lane_m[None, :], (8, 1)))


_NT = (((1,), (1,)), ((), ()))


def _softmax_step(s, vt, m_ref, l_ref, acc_ref, h):
    m_prev = m_ref[h]
    m_new = jnp.maximum(m_prev, jnp.max(s, axis=0, keepdims=True))
    alpha = jnp.exp(m_prev - m_new)
    p = jnp.exp(s - m_new)
    l_ref[h] = alpha * l_ref[h] + jnp.sum(p, axis=0, keepdims=True)
    acc_ref[h] = alpha * acc_ref[h] + jnp.dot(vt, p.astype(BF16), preferred_element_type=F32)
    m_ref[h] = m_new


def _select_blocks(imp_t, cur, unroll=False):
    blk = lax.broadcasted_iota(jnp.int32, imp_t.shape, 0)
    forced = (blk == 0) | (blk == cur) | (blk == cur - 1)
    vals = jnp.where(forced, FORCED_SCORE, jnp.where(blk <= cur, imp_t, -1.0))

    def body(_, carry):
        vals, neg = carry
        mx = jnp.max(vals, axis=0, keepdims=True)
        first = jnp.min(jnp.where(vals == mx, blk, 1 << 20), axis=0, keepdims=True)
        pick = blk == first
        return jnp.where(pick, -3e38, vals), jnp.where(pick, 0.0, neg)

    _, neg = lax.fori_loop(0, N_SEL, body, (vals, jnp.full(imp_t.shape, NEG, F32)), unroll=unroll)
    return neg


def _attn_kernel(qq_ref, kc_ref, vct_ref, dtab_ref, lanem_ref, ksel_ref, vselt_ref, kwin_ref, vwint_ref,
                 xt_ref, near_ref, gatet_ref, o_ref, qs_ref, s_ref, p_ref, m_ref, l_ref, alpha_ref, acc_ref, out_ref,
                 *, tiles_per_batch):
    u = pl.program_id(0) % tiles_per_batch
    nc = kc_ref.shape[0]
    ns = nc // 2
    gcols = REP * TQ

    def gate(h, br):
        c = h * 3 + br
        return gatet_ref[c:c + 1, :]

    def gate_row(g, br):
        return jnp.concatenate([gate(g * REP + r, br) for r in range(REP)], axis=1)

    def group_rows(h):
        g = h // REP
        return slice(g * HEAD_DIM, (g + 1) * HEAD_DIM)

    def head_cols(h):
        return slice((h % REP) * TQ, (h % REP + 1) * TQ)

    c = lax.broadcasted_iota(jnp.int32, (nc, LANES), 0)
    lane = lax.broadcasted_iota(jnp.int32, (nc, LANES), 1)
    rel = jnp.where(c < ns, 2 * c, 2 * c - (nc - 1)) - (8 * u - 4)
    near_hit = jnp.where(rel == lanem_ref[0:1, :], 1.0, 0.0)
    future_hit = jnp.where(rel >= NEAR_BLOCKS, 1.0, 0.0)
    onehot = jnp.where(lane < 3 * NEAR_BLOCKS, near_hit, jnp.where(lane == 3 * NEAR_BLOCKS, future_hit, 0.0))
    kk_c = jnp.concatenate([kc_ref[...], onehot.astype(BF16)], axis=1)
    imp = [jnp.zeros((ns, TQ), F32) for _ in range(N_KV)]
    for h in range(N_HEADS):
        qh = jnp.concatenate([qq_ref[h], dtab_ref[h * TQ:(h + 1) * TQ, :]], axis=1)
        s = lax.dot_general(kk_c, qh, _NT, preferred_element_type=F32)
        mx = jnp.max(s, axis=0, keepdims=True)
        p = jnp.where(s > 0.1 * NEG, jnp.exp(s - mx), 0.0)
        pn = p / jnp.maximum(jnp.sum(p, axis=0, keepdims=True), 1e-30)
        imp[h // REP] = imp[h // REP] + (pn[:ns, :] + pn[ns:, :])
        out_ref[h // REP, :, head_cols(h)] = gate(h, 0) * jnp.dot(vct_ref[group_rows(h), :], pn.astype(BF16),
                                                                  preferred_element_type=F32)

    imp_t = jnp.concatenate(imp, axis=1)
    a = lax.broadcasted_iota(jnp.int32, imp_t.shape, 1) & (TQ - 1)
    cur = (TQ // SEL_BLOCK) * u + (a >> 6)
    neg = _select_blocks(imp_t, cur).T.astype(BF16)
    for h in range(N_HEADS):
        g = h // REP
        qs_ref[h * TQ:(h + 1) * TQ, :] = jnp.concatenate([qq_ref[h], neg[g * TQ:(g + 1) * TQ, :]], axis=1)

    def reset():
        m_ref[...] = jnp.full(m_ref.shape, -1e38, F32)
        l_ref[...] = jnp.zeros(l_ref.shape, F32)
        acc_ref[...] = jnp.zeros(acc_ref.shape, F32)

    def finish(br):
        for g in range(N_KV):
            cols = slice(g * gcols, (g + 1) * gcols)
            out_ref[g] = out_ref[g] + gate_row(g, br) * (acc_ref[g] / l_ref[:, cols])

    def softmax_tile(vt_ref, keys, bias):
        for h in range(N_HEADS):
            cols = slice(h * TQ, (h + 1) * TQ)
            s = bias(h, s_ref[:, cols])
            m_prev = m_ref[:, cols]
            m_new = jnp.maximum(m_prev, jnp.max(s, axis=0, keepdims=True))
            alpha = jnp.exp(m_prev - m_new)
            p = jnp.exp(s - m_new)
            l_ref[:, cols] = alpha * l_ref[:, cols] + jnp.sum(p, axis=0, keepdims=True)
            p_ref[:, cols] = p.astype(BF16)
            alpha_ref[:, cols] = alpha
            m_ref[:, cols] = m_new
        for g in range(N_KV):
            cols = slice(g * gcols, (g + 1) * gcols)
            pv = jnp.dot(vt_ref[g * HEAD_DIM:(g + 1) * HEAD_DIM, keys], p_ref[:, cols], preferred_element_type=F32)
            acc_ref[g] = alpha_ref[:, cols] * acc_ref[g] + pv

    def near_bias(row):
        if row is None:
            return lambda h, s: s
        return lambda h, s: s + near_ref[h, row:row + TK, :]

    reset()

    def sel_tile(kt, row):
        keys = pl.ds(pl.multiple_of(kt * TK, TK), TK)
        kk = jnp.concatenate([ksel_ref[keys, :], xt_ref[keys, :]], axis=1)
        s_ref[...] = lax.dot_general(kk, qs_ref[...], _NT, preferred_element_type=F32)
        softmax_tile(vselt_ref, keys, near_bias(row))

    def far_body(kt, carry):
        sel_tile(kt, None)
        return carry

    lax.fori_loop(0, u - 1, far_body, 0)

    @pl.when(u >= 1)
    def _():
        sel_tile(u - 1, 0)

    sel_tile(u, TK)
    finish(1)

    reset()

    def later_keys_only(h, s):
        kj = lax.broadcasted_iota(jnp.int32, s.shape, 0)
        qa = lax.broadcasted_iota(jnp.int32, s.shape, 1)
        return jnp.where(kj > qa, s, NEG)

    def win_tile(kt, row):
        keys = pl.ds(pl.multiple_of(kt * TK, TK), TK)
        s_ref[...] = lax.dot_general(kwin_ref[keys, :], qs_ref[:, :LANES], _NT, preferred_element_type=F32)
        softmax_tile(vwint_ref, keys, later_keys_only if row is None else near_bias(row))

    @pl.when(u >= 2)
    def _():
        win_tile(u - 2, None)

    @pl.when(u >= 1)
    def _():
        win_tile(u - 1, 0)

    win_tile(u, TK)
    finish(2)

    for i in range(N_HEADS // 2):
        pair = jnp.concatenate([out_ref[(2 * i) // REP, :, head_cols(2 * i)],
                                out_ref[(2 * i + 1) // REP, :, head_cols(2 * i + 1)]], axis=0)
        o_ref[:, i * LANES:(i + 1) * LANES] = pair.T.astype(BF16)


def _attention_prompt(qq, kc, vct, ksel, vselt, kwin, vwint, gatet, tables, batch, seq):
    near, dtab, lane_m, xt = tables
    n = batch * seq
    tpb = seq // TQ
    nc = seq // CMP_BLOCK
    per_batch = lambda rows: pl.BlockSpec((rows, LANES), lambda i: (i // tpb, 0))
    chan = lambda cols: pl.BlockSpec((None, LANES, cols), lambda i: (i // tpb, 0, 0))
    full = lambda a: pl.BlockSpec(a.shape, lambda i: (0,) * a.ndim)
    return pl.pallas_call(
        functools.partial(_attn_kernel, tiles_per_batch=tpb),
        grid=(n // TQ,),
        in_specs=[pl.BlockSpec((N_HEADS, TQ, LANES), lambda i: (0, i, 0)),
                  per_batch(nc), chan(nc), full(dtab), full(lane_m),
                  per_batch(seq), chan(seq), per_batch(seq), chan(seq),
                  full(xt), full(near), pl.BlockSpec((LANES, TQ), lambda i: (0, i))],
        out_specs=pl.BlockSpec((TQ, ATT_WIDTH), lambda i: (i, 0)),
        out_shape=jax.ShapeDtypeStruct((n, ATT_WIDTH), BF16),
        scratch_shapes=[pltpu.VMEM((N_HEADS * TQ, 2 * LANES), BF16),
                        pltpu.VMEM((TK, N_HEADS * TQ), F32), pltpu.VMEM((TK, N_HEADS * TQ), BF16),
                        pltpu.VMEM((1, N_HEADS * TQ), F32), pltpu.VMEM((1, N_HEADS * TQ), F32),
                        pltpu.VMEM((1, N_HEADS * TQ), F32),
                        pltpu.VMEM((N_KV, HEAD_DIM, REP * TQ), F32), pltpu.VMEM((N_KV, HEAD_DIM, REP * TQ), F32)],
        compiler_params=_cparams(("arbitrary",)),
        name="nsa_prompt",
    )(qq, kc, vct, dtab, lane_m, ksel, vselt, kwin, vwint, xt, near, gatet)


def _block_onehot_rows(seq):
    j = np.arange(seq)[:, None] // SEL_BLOCK
    return jnp.asarray((j == np.arange(LANES)[None, :]).astype(np.float32)).astype(BF16)


def _even_odd(x, nc):
    x = x.reshape(-1, nc // 2, 2, x.shape[-1])
    return jnp.concatenate([x[:, :, 0], x[:, :, 1]], axis=1).reshape(-1, x.shape[-1])


def _sample_bias_tables(rel_bias, past_len, t_new, n_win):
    t = np.arange(t_new)[:, None]
    nc = past_len // CMP_BLOCK
    tb_full = lambda dist, ok: jnp.where(jnp.asarray(ok), _bucket_values(rel_bias.astype(F32).T, dist), NEG)
    dist_c = past_len + t - (CMP_BLOCK * np.arange(nc)[None, :] + CMP_BLOCK - 1)
    dist_s = past_len + t - np.arange(past_len)[None, :]
    jn = np.arange(LANES)[None, :]
    dist_n = t - jn
    dist_w = n_win + t - np.arange(n_win)[None, :]
    rows = lambda x: x.reshape(N_HEADS * t_new, x.shape[-1])
    return (rows(tb_full(dist_c, dist_c >= 0)), rows(tb_full(dist_s, dist_s >= 0)),
            rows(tb_full(dist_n, (dist_n >= 0) & (jn < t_new))),
            rows(tb_full(dist_w, (dist_w >= 0) & (dist_w < WINDOW))))


def _attn_sample_kernel(pt_ref, *refs, n_pages, t_new):
    del pt_ref
    kvc_refs = refs[:n_pages]
    sel_refs = refs[n_pages:2 * n_pages]
    (qq_ref, seln_ref, winn_ref, winb_ref, gate_ref, xt_ref, xtn_ref, bc_ref, bs_ref, bn_ref, bw_ref,
     rsum_ref, pair_ref, o_ref, qs_ref, out_ref) = refs[2 * n_pages:]

    def gate_col(br):
        return jnp.concatenate([gate_ref[:, h * 3 + br:h * 3 + br + 1] for h in range(N_HEADS)], axis=0)

    def softmax_tiles(q, tiles):
        m = l = acc = None
        for tile in tiles:
            k, v, bias, channel_major = tile()
            if channel_major:
                s = jnp.dot(q, k, preferred_element_type=F32) + bias
                pv = lambda p, v=v: lax.dot_general(p, v, _NT, preferred_element_type=F32)
            else:
                s = lax.dot_general(q, k, _NT, preferred_element_type=F32) + bias
                pv = lambda p, v=v: jnp.dot(p, v, preferred_element_type=F32)
            mt = jnp.max(s, axis=1, keepdims=True)
            if m is None:
                m_new = mt
                p = jnp.exp(s - m_new)
                l = jnp.sum(p, axis=1, keepdims=True)
                acc = pv(p.astype(BF16))
            else:
                m_new = jnp.maximum(m, mt)
                alpha = jnp.exp(m - m_new)
                p = jnp.exp(s - m_new)
                l = alpha * l + jnp.sum(p, axis=1, keepdims=True)
                acc = alpha * acc + pv(p.astype(BF16))
            m = m_new
        return acc / l

    qs_ref[:, :LANES] = qq_ref[...]

    kvc = jnp.concatenate([r[...] for r in kvc_refs], axis=0)
    s = lax.dot_general(qq_ref[...], kvc[:, :LANES].astype(BF16), _NT, preferred_element_type=F32) + bc_ref[...]
    mx = jnp.max(s, axis=1, keepdims=True)
    p = jnp.where(s > 0.1 * NEG, jnp.exp(s - mx), 0.0)
    pn = p / jnp.maximum(jnp.sum(p, axis=1, keepdims=True), 1e-30)
    out_ref[...] = gate_col(0) * jnp.dot(pn.astype(BF16), kvc[:, LANES:].astype(BF16), preferred_element_type=F32)

    hp = lax.Precision.HIGHEST
    imp = jnp.dot(jnp.dot(rsum_ref[...], pn, precision=hp, preferred_element_type=F32), pair_ref[...],
                  precision=hp, preferred_element_type=F32)
    cur = (n_pages * PAGE_SIZE) // SEL_BLOCK
    neg = _select_blocks(imp.T, cur, unroll=True).T
    qs_ref[:, LANES:] = jnp.concatenate(
        [neg[(h // REP) * t_new:(h // REP + 1) * t_new, :] for h in range(N_HEADS)], axis=0).astype(BF16)

    def past_tile(ref, j):
        cols = slice(j * PAGE_SIZE, (j + 1) * PAGE_SIZE)
        return (jnp.concatenate([ref[:LANES, :].astype(BF16), xt_ref[:, cols]], axis=0), ref[LANES:, :].astype(BF16),
                bs_ref[:, cols], True)

    def new_sel_tile():
        return (jnp.concatenate([seln_ref[:, :LANES].astype(BF16), xtn_ref[...]], axis=1),
                seln_ref[:, LANES:].astype(BF16), bn_ref[...], False)

    tiles = [functools.partial(past_tile, sel_refs[j], j) for j in range(n_pages)] + [new_sel_tile]
    out_ref[...] += gate_col(1) * softmax_tiles(qs_ref[...], tiles)

    tiles = [lambda: (winb_ref[:LANES, :].astype(BF16), winb_ref[LANES:, :].astype(BF16), bw_ref[...], True),
             lambda: (winn_ref[:, :LANES].astype(BF16), winn_ref[:, LANES:].astype(BF16), bn_ref[...], False)]
    out_ref[...] += gate_col(2) * softmax_tiles(qq_ref[...], tiles)

    lane_o = lax.broadcasted_iota(jnp.int32, (t_new, LANES), 1)
    for i in range(N_HEADS // 2):
        left = out_ref[2 * i * t_new:(2 * i + 1) * t_new, :]
        right = out_ref[(2 * i + 1) * t_new:(2 * i + 2) * t_new, :]
        if (2 * i) // REP == 0:
            right = pltpu.roll(right, HEAD_DIM, 1)
        else:
            left = pltpu.roll(left, HEAD_DIM, 1)
        o_ref[:, i * LANES:(i + 1) * LANES] = jnp.where(lane_o < HEAD_DIM, left, right)


def _attention_sample(page_table, kvc_pages, sel_cache, layer, qq, sel_new, win_new, win_buf, gates, tables, t_new):
    n_seq, n_pages = page_table.shape
    xt, xtn, bc, bs, bn, bw, rsum, pair = tables
    n_win = win_buf.shape[3]
    nq = N_HEADS * t_new
    page_spec = lambda shape, j: pl.BlockSpec((None, None) + shape, lambda b, pt, j=j: (layer, pt[b, j], 0, 0))
    full = lambda a: pl.BlockSpec(a.shape, lambda b, pt: (0,) * a.ndim)
    new = pl.BlockSpec((None, PAGE_SIZE, KV_WIDTH), lambda b, pt: (b, 0, 0))
    in_specs = ([page_spec((PAGE_SIZE // CMP_BLOCK, KV_WIDTH), j) for j in range(n_pages)]
                + [page_spec((KV_WIDTH, PAGE_SIZE), j) for j in range(n_pages)]
                + [pl.BlockSpec((None, nq, LANES), lambda b, pt: (b, 0, 0)), new, new,
                   pl.BlockSpec((None, None, KV_WIDTH, n_win), lambda b, pt: (layer, b, 0, 0)),
                   pl.BlockSpec((t_new, LANES), lambda b, pt: (b, 0))]
                + [full(a) for a in (xt, xtn, bc, bs, bn, bw, rsum, pair)])
    return pl.pallas_call(
        functools.partial(_attn_sample_kernel, n_pages=n_pages, t_new=t_new),
        grid_spec=pltpu.PrefetchScalarGridSpec(
            num_scalar_prefetch=1, grid=(n_seq,), in_specs=in_specs,
            out_specs=pl.BlockSpec((t_new, ATT_WIDTH), lambda b, pt: (b, 0)),
            scratch_shapes=[pltpu.VMEM((nq, 2 * LANES), BF16), pltpu.VMEM((nq, LANES), F32)]),
        out_shape=jax.ShapeDtypeStruct((n_seq * t_new, ATT_WIDTH), F32),
        compiler_params=_cparams(("arbitrary",)),
        name="nsa_sample",
    )(page_table, *([kvc_pages] * n_pages), *([sel_cache] * n_pages), qq, sel_new, win_new, win_buf, gates,
      xt, xtn, bc, bs, bn, bw, rsum, pair)


def _sample_sum_matrices(t_new, nc):
    rsum = np.zeros((LANES, N_HEADS * t_new), np.float32)
    for h in range(N_HEADS):
        for t in range(t_new):
            rsum[(h // REP) * t_new + t, h * t_new + t] = 1.0
    pair = np.zeros((nc, LANES), np.float32)
    pair[np.arange(nc), np.arange(nc) // 2] = 1.0
    return jnp.asarray(rsum), jnp.asarray(pair)


def _rank_before(vals, k):
    r = jnp.zeros(vals[k].shape, jnp.int32)
    for j, vj in enumerate(vals):
        if j < k:
            r = r + jnp.where(vj >= vals[k], 1, 0)
        elif j > k:
            r = r + jnp.where(vj > vals[k], 1, 0)
    return r


def _route(s_rows, b_rows):
    scores = []
    for g in range(N_GROUPS):
        b0, b1, b2, b3 = b_rows[4 * g:4 * g + 4]
        hi01, lo01, hi23, lo23 = jnp.maximum(b0, b1), jnp.minimum(b0, b1), jnp.maximum(b2, b3), jnp.minimum(b2, b3)
        top1 = jnp.maximum(hi01, hi23)
        top2 = jnp.maximum(jnp.maximum(lo01, lo23), jnp.minimum(hi01, hi23))
        scores.append(top1 + top2)
    in_group = [_rank_before(scores, g) == 0 for g in range(N_GROUPS)]

    def pick(rows, k):
        out = rows[4 * (N_GROUPS - 1) + k]
        for g in range(N_GROUPS - 2, -1, -1):
            out = jnp.where(in_group[g], rows[4 * g + k], out)
        return out

    bv = [pick(b_rows, k) for k in range(EXPERTS_PER_GROUP)]
    sv = [pick(s_rows, k) for k in range(EXPERTS_PER_GROUP)]
    w = [jnp.where(_rank_before(bv, k) < 2, sv[k], 0.0) for k in range(EXPERTS_PER_GROUP)]
    den = (w[0] + w[1]) + (w[2] + w[3])
    return [jnp.where(in_group[e // 4], w[e % 4] / den, 0.0) for e in range(N_EXPERTS)]


def _outproj_kernel(ps_ref, att_ref, x_ref, mod_ref, wout_ref, lng_ref, lnb_ref, rwt_ref, rb_ref,
                    x1_ref, h2_ref, g_ref):
    half = ps_ref.shape[1]
    mix = (jnp.dot(ps_ref[...], wout_ref[:half, :], preferred_element_type=F32)
           + jnp.dot(att_ref[...], wout_ref[half:, :], preferred_element_type=F32))
    x1 = _layer_norm(ALPHA * x_ref[...] + (1.0 + mod_ref[2]) * mix, lng_ref[...], lnb_ref[...])
    x1_ref[...] = x1
    h2 = x1 * (1.0 + mod_ref[4]) + mod_ref[3]
    h2_ref[...] = h2.astype(BF16)
    st = jax.nn.sigmoid(lax.dot_general(rwt_ref[...], h2, _NT, precision=lax.Precision.HIGHEST,
                                        preferred_element_type=F32))
    s_rows = [st[e:e + 1, :] for e in range(N_EXPERTS)]
    b_rows = [s_rows[e] + rb_ref[e:e + 1, :] for e in range(N_EXPERTS)]
    gt = jnp.concatenate(_route(s_rows, b_rows) + [jnp.zeros((LANES - N_EXPERTS, st.shape[1]), F32)], axis=0)
    g_ref[...] = gt.T


def _out_proj(ps, att, x, mod, wout, ln_g, ln_b, rwt, rb, tm, tiles_per_batch):
    n = x.shape[0]
    row = lambda wd: pl.BlockSpec((tm, wd), lambda i: (i, 0))
    full = lambda a: pl.BlockSpec(a.shape, lambda i: (0,) * a.ndim)
    return pl.pallas_call(
        _outproj_kernel,
        grid=(n // tm,),
        in_specs=[row(512), row(512), row(D_MODEL), _mod_spec(mod, tm, tiles_per_batch),
                  full(wout), full(ln_g), full(ln_b), full(rwt), full(rb)],
        out_specs=[row(D_MODEL), row(D_MODEL), row(LANES)],
        out_shape=[jax.ShapeDtypeStruct((n, D_MODEL), F32), jax.ShapeDtypeStruct((n, D_MODEL), BF16),
                   jax.ShapeDtypeStruct((n, LANES), F32)],
        compiler_params=_cparams(("arbitrary",)),
        name="out_proj",
    )(ps, att, x, mod, wout, ln_g, ln_b, rwt, rb)


def _moe_kernel(h2_ref, g_ref, wg_ref, wu_ref, wd_ref, x1_ref, mod_ref, lng_ref, lnb_ref, o_ref, acc_ref):
    e = pl.program_id(1)

    @pl.when(e == 0)
    def _():
        acc_ref[...] = jnp.zeros_like(acc_ref)

    x = h2_ref[...]
    gate = jnp.dot(x, wg_ref[0, 0], preferred_element_type=F32)
    up = jnp.dot(x, wu_ref[0, 0], preferred_element_type=F32)
    act = (gate * jax.nn.sigmoid(gate) * up).astype(BF16)
    y = jnp.dot(act, wd_ref[0, 0], preferred_element_type=F32)
    lane = lax.broadcasted_iota(jnp.int32, g_ref.shape, 1)
    w = jnp.sum(jnp.where(lane == e, g_ref[...], 0.0), axis=1, keepdims=True)
    acc_ref[...] += y * w

    @pl.when(e == pl.num_programs(1) - 1)
    def _():
        o_ref[...] = _layer_norm(ALPHA * x1_ref[...] + (1.0 + mod_ref[5]) * acc_ref[...], lng_ref[...], lnb_ref[...])


def _moe(h2, gates, wg, wu, wd, layer, x1, mod, ln_g, ln_b, tm, tiles_per_batch):
    n = h2.shape[0]
    row = lambda wd_: pl.BlockSpec((tm, wd_), lambda i, e: (i, 0))
    full = lambda a: pl.BlockSpec(a.shape, lambda i, e: (0,) * a.ndim)
    wspec = lambda a: pl.BlockSpec((1, 1) + a.shape[2:], lambda i, e: (layer, e, 0, 0))
    return pl.pallas_call(
        _moe_kernel,
        grid=(n // tm, N_EXPERTS),
        in_specs=[row(D_MODEL), row(LANES), wspec(wg), wspec(wu), wspec(wd), row(D_MODEL),
                  _mod_spec(mod, tm, tiles_per_batch), full(ln_g), full(ln_b)],
        out_specs=row(D_MODEL),
        out_shape=jax.ShapeDtypeStruct((n, D_MODEL), F32),
        scratch_shapes=[pltpu.VMEM((tm, D_MODEL), F32)],
        compiler_params=_cparams(("arbitrary", "arbitrary")),
        name="moe",
    )(h2, gates, wg, wu, wd, x1, mod, ln_g, ln_b)


TM_PROJ = 256
TM_MOE = 512


def _channel_mix(ps, att, x, mod, lw, sw, layer, tiles_per_seq):
    x1, h2, gts = _out_proj(ps, att, x, mod, lw["wout"], lw["ln_g0"], lw["ln_b0"], sw["rwt"], sw["rb"],
                            TM_PROJ, tiles_per_seq(TM_PROJ))
    return _moe(h2, gts, sw["wg"], sw["wu"], sw["wd"], layer, x1, mod, lw["ln_g1"], lw["ln_b1"],
                TM_MOE, tiles_per_seq(TM_MOE))


def _prompt_layer(x, mod, lw, sw, layer, batch, seq):
    n = batch * seq
    tiles = lambda tm: seq // tm
    (p, u, vn, qq, cmp_k, cmp_v, cmp_t, sel_t, win_t, gate_t, ksel, vsel_t, kwin, vwin_t) = _in_proj(
        x, mod, lw["w_proj"], lw["sgu_ln_g"], lw["sgu_ln_b"], TM_PROJ, tiles(TM_PROJ), batch=batch)
    ps = _mixers(p, u, vn, *lw["mix_prompt"], tiles_per_batch=tiles(SGU_CHUNK))
    nc = seq // CMP_BLOCK
    kvc = _compress(cmp_k, cmp_v, sw["pe_rows"], sw["w1bd"], sw["w2bd"], nc, layer)
    kc = _even_odd(kvc[:, :LANES].astype(BF16), nc)
    vc_t = jnp.transpose(_even_odd(kvc[:, LANES:].astype(BF16), nc).reshape(batch, nc, LANES), (0, 2, 1))
    att = _attention_prompt(qq, kc, vc_t, ksel, vsel_t, kwin, vwin_t, gate_t, sw["prompt_tables"], batch, seq)
    x2 = _channel_mix(ps, att, x, mod, lw, sw, layer, tiles)
    return x2, (cmp_t, sel_t, win_t, p)


def _sample_layer(x, mod, lw, sw, layer, page_table, kvc_pages, sel_cache, win_state, pool_state, n_seq, t_new):
    tiles = lambda tm: 1
    (p, u, vn, qq, cmp_raw, sel_raw, win_raw, gates) = _in_proj(
        x, mod, lw["w_proj"], lw["sgu_ln_g"], lw["sgu_ln_b"], TM_PROJ, 1)
    p_ext = jnp.concatenate([pool_state[layer], p.reshape(n_seq, t_new, POOL_WIDTH)], axis=1)
    hist = jnp.pad(p_ext, ((0, 0), (3 * t_new - p_ext.shape[1], 0), (0, 0)))
    chunks = [hist[:, k * t_new:(k + 1) * t_new].reshape(n_seq * t_new, POOL_WIDTH) for k in range(3)]
    ps = _mixers(chunks, u, vn, *lw["mix_sample"], t_new=t_new)
    q_rows = jnp.transpose(qq.reshape(N_HEADS, n_seq, t_new, LANES), (1, 0, 2, 3)).reshape(n_seq, N_HEADS * t_new, LANES)
    new_page = lambda raw: jnp.pad(raw.reshape(n_seq, t_new, KV_WIDTH), ((0, 0), (0, PAGE_SIZE - t_new), (0, 0)))
    att = _attention_sample(page_table, kvc_pages, sel_cache, layer, q_rows, new_page(sel_raw), new_page(win_raw),
                            win_state, gates, sw["sample_tables"], t_new)
    x2 = _channel_mix(ps, att.astype(BF16), x, mod, lw, sw, layer, tiles)
    win_new_t = jnp.transpose(win_raw.reshape(n_seq, t_new, KV_WIDTH), (0, 2, 1))
    win_all_t = jnp.concatenate([win_state[layer][:, :, t_new:], win_new_t], axis=2)
    return x2, (cmp_raw, sel_raw, win_all_t, p_ext[:, p_ext.shape[1] - POOL_BUF:], vn)


def kernel(x_prompt, x_sample, cache_cmp_kv, cache_sel_kv, state_win_kv, state_pool, page_table, c_prompt, c_sample,
           w_in, w_out, pool_w, pool_scale, sgu_ln_g, sgu_ln_b, sgu_w, sgu_b, cmp_pe, cmp_w1, cmp_w2, rel_bias,
           w_mod, b_mod, ln_g, ln_b, router_w, router_b, moe_w_gate, moe_w_up, moe_w_down):
    batch, seq, d = x_prompt.shape
    n_seq, t_new, _ = x_sample.shape
    depth = w_in.shape[0]
    n_pages = page_table.shape[1]
    past_len = n_pages * PAGE_SIZE
    n_phys = cache_cmp_kv.shape[1]
    n_win = state_win_kv.shape[2]
    assert seq // SEL_BLOCK == LANES and seq % TM_MOE == 0 and (n_seq * t_new) % TM_MOE == 0
    assert POOL_BUF + 1 == 2 * t_new and past_len % TK == 0

    n_c = batch + n_seq
    c_all = jnp.pad(jnp.concatenate([c_prompt, c_sample], axis=0), ((0, -n_c % 8), (0, 0)))
    m_all = _modulation(c_all, w_mod, b_mod)
    mod_p = m_all[:, :batch].reshape(depth, batch, 6, 1, d)
    mod_s = jnp.transpose(jnp.repeat(m_all[:, batch:n_c].reshape(depth, n_seq, 6, d), t_new, axis=1), (0, 2, 1, 3))

    pe_rows, w1bd, w2bd = _prep_compress_weights(cmp_pe, cmp_w1, cmp_w2)
    near, dtab, lane_m = _prompt_bias_tables(rel_bias)
    nc_past = past_len // CMP_BLOCK
    shared = {
        "pe_rows": pe_rows, "w1bd": w1bd, "w2bd": w2bd,
        "prompt_tables": (near, dtab, lane_m, _block_onehot_rows(seq)),
        "sample_tables": (_block_onehot_rows(past_len + PAGE_SIZE).T, _block_onehot_rows(past_len + PAGE_SIZE)[past_len:],
                          *_sample_bias_tables(rel_bias, past_len, t_new, n_win),
                          *_sample_sum_matrices(t_new, nc_past)),
        "rwt": router_w.T, "rb": router_b.reshape(N_EXPERTS, 1),
        "wg": moe_w_gate.astype(BF16), "wu": moe_w_up.astype(BF16), "wd": moe_w_down.astype(BF16),
    }
    layers = []
    for l in range(depth):
        layers.append({
            "w_proj": _prep_w_proj(w_in[l]), "wout": w_out[l].astype(BF16),
            "sgu_ln_g": sgu_ln_g[l].reshape(1, -1), "sgu_ln_b": sgu_ln_b[l].reshape(1, -1),
            "mix_prompt": _prep_mixer_weights(pool_w[l], pool_scale[l], sgu_w[l], sgu_b[l], SGU_CHUNK),
            "mix_sample": _prep_mixer_weights(pool_w[l], pool_scale[l], sgu_w[l], sgu_b[l], t_new),
            "ln_g0": ln_g[l, 0].reshape(1, d), "ln_b0": ln_b[l, 0].reshape(1, d),
            "ln_g1": ln_g[l, 1].reshape(1, d), "ln_b1": ln_b[l, 1].reshape(1, d),
        })

    chan_major = lambda x: jnp.transpose(x, (0, 1, 3, 4, 5, 2)).reshape(x.shape[0], x.shape[1], KV_WIDTH, x.shape[2])
    kvc_pages = _compress_pages(chan_major(cache_cmp_kv), pe_rows, w1bd, w2bd, 64)
    kvc_pages = kvc_pages.reshape(depth, n_phys, PAGE_SIZE // CMP_BLOCK, KV_WIDTH)
    sel_cache = chan_major(cache_sel_kv)
    win_state = chan_major(state_win_kv)

    xp = x_prompt.reshape(batch * seq, d)
    xs = x_sample.reshape(n_seq * t_new, d)
    outs_p, outs_s = [], []
    for l in range(depth):
        xp, st = _prompt_layer(xp, mod_p[l], layers[l], shared, l, batch, seq)
        outs_p.append(st)
        xs, st = _sample_layer(xs, mod_s[l], layers[l], shared, l, page_table, kvc_pages, sel_cache, win_state,
                               state_pool, n_seq, t_new)
        outs_s.append(st)

    kv5 = lambda x, b: x.reshape(b, -1, 2, N_KV, HEAD_DIM)
    kv5_t = lambda x: jnp.transpose(x.reshape(x.shape[0], 2, N_KV, HEAD_DIM, x.shape[2]), (0, 4, 1, 2, 3))
    stack = lambda xs_: jnp.stack(xs_)
    w_keep = min(WINDOW, seq)
    return (xp.reshape(batch, seq, d), xs.reshape(n_seq, t_new, d),
            stack([kv5_t(o[0]) for o in outs_p]), stack([kv5(o[0], n_seq) for o in outs_s]),
            stack([kv5_t(o[1]) for o in outs_p]), stack([kv5(o[1], n_seq) for o in outs_s]),
            stack([kv5_t(o[2][:, :, seq - w_keep:]) for o in outs_p]), stack([kv5_t(o[2]) for o in outs_s]),
            stack([o[3].reshape(batch, seq, POOL_WIDTH)[:, seq - POOL_BUF:] for o in outs_p]),
            stack([o[3] for o in outs_s]),
            stack([o[4].reshape(n_seq, t_new, SGU_WIDTH) for o in outs_s]))
```

```python
import functools
import math

import numpy as np
import jax
import jax.numpy as jnp
from jax import lax
from jax.experimental import pallas as pl
from jax.experimental.pallas import tpu as pltpu

F32 = jnp.float32
BF16 = jnp.bfloat16

D_MODEL = 1024
POOL_WIDTH = 256
SGU_WIDTH = 256
ATT_WIDTH = 512
POOL_WINDOWS = (2, 4, 8, 16)
POOL_GROUP_DIM = 64
POOL_BUF = 15
SGU_GROUPS = 4
SGU_CHUNK = 128
N_HEADS = 8
HEAD_DIM = 64
N_KV = 2
REP = 4
CMP_BLOCK = 32
CMP_HIDDEN = 128
SEL_BLOCK = 64
N_SEL = 16
WINDOW = 512
N_BUCKETS = 32
REL_MAX_DIST = 128
N_EXPERTS = 16
N_GROUPS = 4
EXPERTS_PER_GROUP = 4
D_EXPERT = 512
DEPTH = 2
ALPHA = (2 * DEPTH) ** 0.25
LN_EPS = 1e-5
FORCED_SCORE = 1e4
NEG = -1e30
PAGE_SIZE = 128

KV_WIDTH = 2 * N_KV * HEAD_DIM
LANES = 128
VMEM_LIMIT = 56 * 1024 * 1024

C_P, C_U, C_V, C_CMP, C_SEL, C_WIN, C_GATE, C_Q = 0, 256, 512, 768, 1024, 1280, 1536, 1664
W_PROJ = C_Q + N_HEADS * LANES

TQ = 256
TK = 256
NEAR_BLOCKS = 12
FAR_BUCKET_DIST = 113


def _cparams(sem):
    return pltpu.CompilerParams(dimension_semantics=sem, vmem_limit_bytes=VMEM_LIMIT)


def _layer_norm(x, g, b):
    mu = jnp.mean(x, axis=-1, keepdims=True)
    xc = x - mu
    var = jnp.mean(xc * xc, axis=-1, keepdims=True)
    return xc * lax.rsqrt(var + LN_EPS) * g + b


def _mod_kernel(c_ref, w_ref, b_ref, o_ref):
    c = c_ref[...]
    a = (c * jax.nn.sigmoid(c)).astype(BF16)
    o_ref[0] = jnp.dot(a, w_ref[0].astype(BF16), preferred_element_type=F32) + b_ref[0]


def _modulation(c_all, w_mod, b_mod):
    n, d = c_all.shape
    depth, _, w = w_mod.shape
    tn = 1536
    return pl.pallas_call(
        _mod_kernel,
        grid=(depth, w // tn),
        in_specs=[pl.BlockSpec((n, d), lambda l, j: (0, 0)),
                  pl.BlockSpec((1, d, tn), lambda l, j: (l, 0, j)),
                  pl.BlockSpec((1, 1, tn), lambda l, j: (l, 0, j))],
        out_specs=pl.BlockSpec((1, n, tn), lambda l, j: (l, 0, j)),
        out_shape=jax.ShapeDtypeStruct((depth, n, w), F32),
        compiler_params=_cparams(("arbitrary", "arbitrary")),
        name="adaln_mod",
    )(c_all, w_mod, b_mod.reshape(depth, 1, w))


def _mod_spec(mod, tm, tiles_per_batch):
    if mod.ndim == 4:
        return pl.BlockSpec((None, 6, 1, D_MODEL), lambda i, *_: (i // tiles_per_batch, 0, 0, 0))
    return pl.BlockSpec((6, tm, D_MODEL), lambda i, *_: (0, i, 0))


def _inproj_kernel(x_ref, mod_ref, w_ref, lng_ref, lnb_ref, p_ref, u_ref, vn_ref, qq_ref, *rest, channel_major):
    h = (x_ref[...] * (1.0 + mod_ref[1]) + mod_ref[0]).astype(BF16)

    def seg(a, b):
        return jnp.dot(h, w_ref[:, a:b], preferred_element_type=F32)

    p_ref[...] = seg(C_P, C_U)
    u_ref[...] = seg(C_U, C_V)
    vn_ref[...] = _layer_norm(seg(C_V, C_CMP), lng_ref[...], lnb_ref[...])
    for hd in range(N_HEADS):
        qq_ref[hd] = seg(C_Q + hd * LANES, C_Q + (hd + 1) * LANES).astype(BF16)
    cmp = seg(C_CMP, C_SEL)
    sel = seg(C_SEL, C_WIN)
    win = seg(C_WIN, C_GATE)
    gates = jax.nn.sigmoid(seg(C_GATE, C_Q))
    if not channel_major:
        cmp_ref, sel_ref, win_ref, gate_ref = rest
        cmp_ref[...] = cmp
        sel_ref[...] = sel
        win_ref[...] = win
        gate_ref[...] = gates
        return
    cmpk_ref, cmpv_ref, cmpt_ref, selt_ref, wint_ref, gatet_ref, ksel_ref, vselt_ref, kwin_ref, vwint_ref = rest
    cmpk_ref[...] = cmp[:, :LANES]
    cmpv_ref[...] = cmp[:, LANES:]
    cmpt_ref[...] = cmp.T
    sel_t = sel.T
    selt_ref[...] = sel_t
    ksel_ref[...] = sel[:, :LANES].astype(BF16)
    vselt_ref[...] = sel_t[LANES:, :].astype(BF16)
    win_t = win.T
    wint_ref[...] = win_t
    kwin_ref[...] = win[:, :LANES].astype(BF16)
    vwint_ref[...] = win_t[LANES:, :].astype(BF16)
    gatet_ref[...] = gates.T


def _in_proj(x, mod, w_proj, ln_g, ln_b, tm, tiles_per_batch, batch=None):
    n = x.shape[0]
    row = lambda wd: pl.BlockSpec((tm, wd), lambda i: (i, 0))
    full = lambda a: pl.BlockSpec(a.shape, lambda i: (0,) * a.ndim)
    f32o = lambda wd: jax.ShapeDtypeStruct((n, wd), F32)
    out_specs = [row(256), row(256), row(256), pl.BlockSpec((N_HEADS, tm, LANES), lambda i: (0, i, 0))]
    out_shape = [f32o(256), f32o(256), f32o(256), jax.ShapeDtypeStruct((N_HEADS, n, LANES), BF16)]
    if batch is None:
        out_specs += [row(256), row(256), row(256), row(LANES)]
        out_shape += [f32o(256), f32o(256), f32o(256), f32o(LANES)]
    else:
        seq = n // batch
        chan = lambda c: pl.BlockSpec((None, c, tm), lambda i: (i // tiles_per_batch, 0, i % tiles_per_batch))
        chan_o = lambda c, dt: jax.ShapeDtypeStruct((batch, c, seq), dt)
        out_specs += [row(LANES), row(LANES), chan(256), chan(256), chan(256), pl.BlockSpec((LANES, tm), lambda i: (0, i)),
                      row(LANES), chan(LANES), row(LANES), chan(LANES)]
        out_shape += [f32o(LANES), f32o(LANES),
                      chan_o(256, F32), chan_o(256, F32), chan_o(256, F32), jax.ShapeDtypeStruct((LANES, n), F32),
                      jax.ShapeDtypeStruct((n, LANES), BF16), chan_o(LANES, BF16),
                      jax.ShapeDtypeStruct((n, LANES), BF16), chan_o(LANES, BF16)]
    return pl.pallas_call(
        functools.partial(_inproj_kernel, channel_major=batch is not None),
        grid=(n // tm,),
        in_specs=[row(D_MODEL), _mod_spec(mod, tm, tiles_per_batch), full(w_proj), full(ln_g), full(ln_b)],
        out_specs=out_specs,
        out_shape=out_shape,
        compiler_params=_cparams(("arbitrary",)),
        name="in_proj",
    )(x, mod, w_proj, ln_g, ln_b)


def _prep_w_proj(w_in):
    d = w_in.shape[0]
    q = w_in[:, 768:1280].reshape(d, N_KV, REP, HEAD_DIM) * (HEAD_DIM ** -0.5)
    qq = jnp.zeros((d, N_KV, REP, N_KV, HEAD_DIM), w_in.dtype)
    for g in range(N_KV):
        qq = qq.at[:, g, :, g, :].set(q[:, g])
    gate = jnp.pad(w_in[:, 2048:2072], ((0, 0), (0, LANES - 24)))
    cols = [w_in[:, 0:768], w_in[:, 1280:2048], gate, qq.reshape(d, N_HEADS * LANES)]
    return jnp.concatenate(cols, axis=1).astype(BF16)


def _window_sums(shifted):
    acc = shifted(0)
    sums = {}
    for k in range(1, 16):
        acc = acc + shifted(k)
        if k + 1 in POOL_WINDOWS:
            sums[k + 1] = acc
    lane = lax.broadcasted_iota(jnp.int32, acc.shape, 1)
    return jnp.where(lane < 64, sums[2], jnp.where(lane < 128, sums[4], jnp.where(lane < 192, sums[8], sums[16])))


def _mixers_tail(sums, cnt, cur, u, vn, wpool_ref, pscale_ref, wcat_ref, sbias_ref, o_ref):
    diff = (sums / cnt - cur).astype(BF16)
    pool = jnp.dot(diff, wpool_ref[...], preferred_element_type=F32) * pscale_ref[...]
    lane = lax.broadcasted_iota(jnp.int32, vn.shape, 1)
    vb = vn.astype(BF16)
    zero = jnp.zeros_like(vb)
    stacked = jnp.concatenate([jnp.where((lane // 64) == g, vb, zero) for g in range(SGU_GROUPS)], axis=0)
    mixed = jnp.dot(wcat_ref[...], stacked, preferred_element_type=F32) + sbias_ref[...]
    o_ref[:, :POOL_WIDTH] = pool.astype(BF16)
    o_ref[:, POOL_WIDTH:] = (u * mixed).astype(BF16)


def _lane_window():
    lane = lax.broadcasted_iota(jnp.int32, (SGU_CHUNK, POOL_WIDTH), 1)
    return jnp.where(lane < 64, 2, jnp.where(lane < 128, 4, jnp.where(lane < 192, 8, 16)))


def _mix_prompt_kernel(p_ref, u_ref, vn_ref, wpool_ref, pscale_ref, wcat_ref, sbias_ref, o_ref, prev_ref,
                       *, tiles_per_batch):
    t = pl.program_id(0) % tiles_per_batch

    @pl.when(t == 0)
    def _():
        prev_ref[...] = jnp.zeros_like(prev_ref)

    cur = p_ref[...]
    prev = prev_ref[...]
    row = lax.broadcasted_iota(jnp.int32, cur.shape, 0)

    def shifted(k):
        if k == 0:
            return cur
        return jnp.where(row >= k, pltpu.roll(cur, k, 0), pltpu.roll(prev, k, 0))

    sums = _window_sums(shifted)
    prev_ref[...] = cur
    cnt = jnp.minimum(_lane_window(), t * SGU_CHUNK + row + 1).astype(F32)
    _mixers_tail(sums, cnt, cur, u_ref[...], vn_ref[...], wpool_ref, pscale_ref, wcat_ref, sbias_ref, o_ref)


def _mix_sample_kernel(pa_ref, pb_ref, pc_ref, u_ref, vn_ref, wpool_ref, pscale_ref, wcat_ref, sbias_ref, o_ref,
                       *, t_new):
    a, b, c = pa_ref[...], pb_ref[...], pc_ref[...]
    rows = c.shape[0]
    t = lax.broadcasted_iota(jnp.int32, c.shape, 0) % t_new

    def shifted(k):
        if k == 0:
            return c
        hi, lo = (c, b) if k < t_new else (b, a)
        kk = k % t_new
        if kk == 0:
            return hi
        return jnp.where(t >= kk, pltpu.roll(hi, kk, 0), pltpu.roll(lo, rows - t_new + kk, 0))

    sums = _window_sums(shifted)
    cnt = _lane_window().astype(F32)
    _mixers_tail(sums, cnt, c, u_ref[...], vn_ref[...], wpool_ref, pscale_ref, wcat_ref, sbias_ref, o_ref)


def _mixers(p_in, u, vn, wpool, pscale, wcat, sbias, tiles_per_batch=None, t_new=None):
    n = u.shape[0]
    tm = SGU_CHUNK
    row = lambda wd: pl.BlockSpec((tm, wd), lambda i: (i, 0))
    full = lambda a: pl.BlockSpec(a.shape, lambda i: (0,) * a.ndim)
    weights = [wpool, pscale, wcat, sbias]
    if t_new is None:
        kern = functools.partial(_mix_prompt_kernel, tiles_per_batch=tiles_per_batch)
        ins, scratch = [p_in], [pltpu.VMEM((tm, POOL_WIDTH), F32)]
    else:
        kern = functools.partial(_mix_sample_kernel, t_new=t_new)
        ins, scratch = list(p_in), []
    return pl.pallas_call(
        kern,
        grid=(n // tm,),
        in_specs=[row(256)] * (len(ins) + 2) + [full(a) for a in weights],
        out_specs=row(512),
        out_shape=jax.ShapeDtypeStruct((n, 512), BF16),
        scratch_shapes=scratch,
        compiler_params=_cparams(("arbitrary",)),
        name="mixers",
    )(*ins, u, vn, *weights)


def _prep_mixer_weights(pool_w, pool_scale, sgu_w, sgu_b, chunk):
    wpool = jax.scipy.linalg.block_diag(*[pool_w[g] for g in range(len(POOL_WINDOWS))]).astype(BF16)
    tri = jnp.tril(jnp.ones((chunk, chunk), bool))
    w = jnp.where(tri, sgu_w[:, :chunk, :chunk], 0.0)
    reps = SGU_CHUNK // chunk
    eye = jnp.eye(reps, dtype=w.dtype)
    wt = jnp.concatenate([jnp.kron(eye, w[g]) for g in range(SGU_GROUPS)], axis=1).astype(BF16)
    b = jnp.tile(sgu_b[:, :chunk], (1, reps))
    sbias = jnp.repeat(b.T, SGU_WIDTH // SGU_GROUPS, axis=1)
    return wpool, pool_scale.reshape(1, POOL_WIDTH), wt, sbias


def _compress_mlp(token_rows, w1_ref, w2_ref, o_ref, nblk):
    for s in range(2):
        acc = jnp.zeros((nblk, N_KV * CMP_HIDDEN), F32)
        for j in range(0, CMP_BLOCK, 2):
            pair = jnp.concatenate([token_rows(s, j), token_rows(s, j + 1)], axis=1)
            acc = acc + jnp.dot(pair, w1_ref[0, s, j // 2], preferred_element_type=F32)
        hdn = jax.nn.gelu(acc).astype(BF16)
        o_ref[0, :, s * LANES:(s + 1) * LANES] = jnp.dot(hdn, w2_ref[0, s], preferred_element_type=F32)


def _compress_kernel(k_ref, v_ref, pe_ref, w1_ref, w2_ref, o_ref, *, nblk):
    bufs = (k_ref, v_ref)
    rows = lambda s, j: (bufs[s][pl.ds(j, nblk, stride=CMP_BLOCK), :] + pe_ref[0, s, j:j + 1, :]).astype(BF16)
    _compress_mlp(rows, w1_ref, w2_ref, o_ref, nblk)


PAIR_TOKENS = 2 * PAGE_SIZE
PAIR_BLOCKS = PAIR_TOKENS // CMP_BLOCK


def _compress_pages_kernel(raw_ref, pe_ref, perm_ref, w1_ref, w2_ref, o_ref, k_ref, v_ref, *, nblk):
    bufs = (k_ref, v_ref)

    def body(q, carry):
        for s in range(2):
            ch = slice(s * LANES, (s + 1) * LANES)
            x = jnp.concatenate([raw_ref[0, 2 * q, ch, :] + pe_ref[0, ch, :],
                                 raw_ref[0, 2 * q + 1, ch, :] + pe_ref[0, ch, :]], axis=1).astype(BF16)
            t = lax.dot_general(perm_ref[...], x, _NT, preferred_element_type=F32)
            rows = pl.ds(pl.multiple_of(q * PAIR_BLOCKS, PAIR_BLOCKS), PAIR_BLOCKS)
            bufs[s][:, rows, :] = t.reshape(CMP_BLOCK, PAIR_BLOCKS, LANES)
        return carry

    lax.fori_loop(0, raw_ref.shape[1] // 2, body, 0, unroll=4)
    _compress_mlp(lambda s, j: bufs[s][j].astype(BF16), w1_ref, w2_ref, o_ref, nblk)


def _compress_pages(pages, pe_pages, w1bd, w2bd, pages_per_step):
    nl, n_pages, _, _ = pages.shape
    assert n_pages % pages_per_step == 0 and pages_per_step % 2 == 0
    nblk = pages_per_step * (PAGE_SIZE // CMP_BLOCK)
    tok = np.arange(PAIR_TOKENS)
    dest = (tok % CMP_BLOCK) * PAIR_BLOCKS + tok // CMP_BLOCK
    perm = jnp.asarray((np.arange(PAIR_TOKENS)[:, None] == dest[None, :]).astype(np.float32)).astype(BF16)
    wspec = lambda a: pl.BlockSpec((1,) + a.shape[1:], lambda l, i: (l,) + (0,) * (a.ndim - 1))
    return pl.pallas_call(
        functools.partial(_compress_pages_kernel, nblk=nblk),
        grid=(nl, n_pages // pages_per_step),
        in_specs=[pl.BlockSpec((1, pages_per_step, KV_WIDTH, PAGE_SIZE), lambda l, i: (l, i, 0, 0)),
                  wspec(pe_pages), pl.BlockSpec(perm.shape, lambda l, i: (0, 0)), wspec(w1bd), wspec(w2bd)],
        out_specs=pl.BlockSpec((1, nblk, KV_WIDTH), lambda l, i: (l, i, 0)),
        out_shape=jax.ShapeDtypeStruct((nl, n_pages * (PAGE_SIZE // CMP_BLOCK), KV_WIDTH), F32),
        scratch_shapes=[pltpu.VMEM((CMP_BLOCK, nblk, LANES), F32)] * 2,
        compiler_params=_cparams(("arbitrary", "arbitrary")),
        name="compress_pages",
    )(pages, pe_pages, perm, w1bd, w2bd)


def _compress(k_raw, v_raw, pe_rows, w1bd, w2bd, nblk, layer):
    r = k_raw.shape[0]
    assert r % (nblk * CMP_BLOCK) == 0
    wspec = lambda a: pl.BlockSpec((1,) + a.shape[1:], lambda i: (layer,) + (0,) * (a.ndim - 1))
    rows = pl.BlockSpec((nblk * CMP_BLOCK, LANES), lambda i: (i, 0))
    return pl.pallas_call(
        functools.partial(_compress_kernel, nblk=nblk),
        grid=(r // (nblk * CMP_BLOCK),),
        in_specs=[rows, rows, wspec(pe_rows), wspec(w1bd), wspec(w2bd)],
        out_specs=pl.BlockSpec((1, nblk, KV_WIDTH), lambda i: (0, i, 0)),
        out_shape=jax.ShapeDtypeStruct((1, r // CMP_BLOCK, KV_WIDTH), F32),
        compiler_params=_cparams(("arbitrary",)),
        name="compress",
    )(k_raw, v_raw, pe_rows, w1bd, w2bd)[0]


def _prep_compress_weights(cmp_pe, cmp_w1, cmp_w2):
    nl = cmp_pe.shape[0]
    pe_rows = jnp.concatenate([cmp_pe] * N_KV, axis=-1)
    w1 = cmp_w1.reshape(nl, 2, CMP_BLOCK, HEAD_DIM, CMP_HIDDEN)
    eye = jnp.eye(N_KV, dtype=F32)
    w1bd = w1[:, :, :, None, :, None, :] * eye[None, None, None, :, None, :, None]
    w2bd = cmp_w2[:, :, None, :, None, :] * eye[None, None, :, None, :, None]
    return (pe_rows, w1bd.reshape(nl, 2, CMP_BLOCK // 2, 2 * LANES, N_KV * CMP_HIDDEN).astype(BF16),
            w2bd.reshape(nl, 2, N_KV * CMP_HIDDEN, LANES).astype(BF16))


def _np_bucket(dist):
    n = np.maximum(dist, 0)
    nf = np.maximum(n, 1).astype(np.float32)
    large = 16 + (np.log(nf / np.float32(16)) / np.float32(math.log(REL_MAX_DIST / 16)) * np.float32(16)).astype(np.int32)
    return np.where(n < 16, n, np.minimum(large, N_BUCKETS - 1)).astype(np.int32)


def _bucket_values(tbt, dist):
    hit = jnp.asarray(_np_bucket(dist))[None, ..., None] == jnp.arange(N_BUCKETS, dtype=jnp.int32)
    return jnp.sum(jnp.where(hit, tbt.reshape((tbt.shape[0],) + (1,) * dist.ndim + (N_BUCKETS,)), 0.0), axis=-1)


def _bias_minus_far(rel_bias, dist):
    tbt = rel_bias.astype(F32).T
    val = _bucket_values(tbt, dist) - tbt[:, N_BUCKETS - 1].reshape((-1,) + (1,) * dist.ndim)
    return jnp.where(jnp.asarray(dist >= 0), val, NEG)


def _split3(x):
    hi = x.astype(BF16)
    r1 = x - hi.astype(F32)
    mid = r1.astype(BF16)
    lo = (r1 - mid.astype(F32)).astype(BF16)
    return hi, mid, lo


def _prompt_bias_tables(rel_bias):
    a = np.arange(TQ)[:, None]
    near = _bias_minus_far(rel_bias, np.arange(TQ)[None, :] + TK - np.arange(2 * TK)[:, None])
    dist_c = a + (4 * CMP_BLOCK - CMP_BLOCK + 1) - CMP_BLOCK * np.arange(NEAR_BLOCKS)[None, :]
    dc = _bias_minus_far(rel_bias, dist_c)
    hi, mid, lo = _split3(dc)
    cols = jnp.stack([hi, mid, lo], axis=-1).reshape(N_HEADS, TQ, 3 * NEAR_BLOCKS)
    future = jnp.full((N_HEADS, TQ, 1), NEG, F32).astype(BF16)
    pad = jnp.zeros((N_HEADS, TQ, LANES - 3 * NEAR_BLOCKS - 1), BF16)
    dtab = jnp.concatenate([cols, future, pad], axis=-1).reshape(N_HEADS * TQ, LANES)
    lane = np.arange(LANES)
    lane_m = np.where(lane < 3 * NEAR_BLOCKS, lane // 3, -1000).astype(np.int32)
    return near, dtab, jnp.asarray(np.tile(lane_m[None, :], (8, 1)))


_NT = (((1,), (1,)), ((), ()))


def _softmax_step(s, vt, m_ref, l_ref, acc_ref, h):
    m_prev = m_ref[h]
    m_new = jnp.maximum(m_prev, jnp.max(s, axis=0, keepdims=True))
    alpha = jnp.exp(m_prev - m_new)
    p = jnp.exp(s - m_new)
    l_ref[h] = alpha * l_ref[h] + jnp.sum(p, axis=0, keepdims=True)
    acc_ref[h] = alpha * acc_ref[h] + jnp.dot(vt, p.astype(BF16), preferred_element_type=F32)
    m_ref[h] = m_new


def _select_blocks(imp_t, cur, unroll=False):
    blk = lax.broadcasted_iota(jnp.int32, imp_t.shape, 0)
    forced = (blk == 0) | (blk == cur) | (blk == cur - 1)
    vals = jnp.where(forced, FORCED_SCORE, jnp.where(blk <= cur, imp_t, -1.0))

    def body(_, carry):
        vals, neg = carry
        mx = jnp.max(vals, axis=0, keepdims=True)
        first = jnp.min(jnp.where(vals == mx, blk, 1 << 20), axis=0, keepdims=True)
        pick = blk == first
        return jnp.where(pick, -3e38, vals), jnp.where(pick, 0.0, neg)

    _, neg = lax.fori_loop(0, N_SEL, body, (vals, jnp.full(imp_t.shape, NEG, F32)), unroll=unroll)
    return neg


def _attn_kernel(qq_ref, kc_ref, vct_ref, dtab_ref, lanem_ref, ksel_ref, vselt_ref, kwin_ref, vwint_ref,
                 xt_ref, near_ref, gatet_ref, o_ref, qs_ref, s_ref, p_ref, m_ref, l_ref, alpha_ref, acc_ref, out_ref,
                 *, tiles_per_batch):
    u = pl.program_id(0) % tiles_per_batch
    nc = kc_ref.shape[0]
    ns = nc // 2
    gcols = REP * TQ

    def gate(h, br):
        c = h * 3 + br
        return gatet_ref[c:c + 1, :]

    def gate_row(g, br):
        return jnp.concatenate([gate(g * REP + r, br) for r in range(REP)], axis=1)

    def group_rows(h):
        g = h // REP
        return slice(g * HEAD_DIM, (g + 1) * HEAD_DIM)

    def head_cols(h):
        return slice((h % REP) * TQ, (h % REP + 1) * TQ)

    c = lax.broadcasted_iota(jnp.int32, (nc, LANES), 0)
    lane = lax.broadcasted_iota(jnp.int32, (nc, LANES), 1)
    rel = jnp.where(c < ns, 2 * c, 2 * c - (nc - 1)) - (8 * u - 4)
    near_hit = jnp.where(rel == lanem_ref[0:1, :], 1.0, 0.0)
    future_hit = jnp.where(rel >= NEAR_BLOCKS, 1.0, 0.0)
    onehot = jnp.where(lane < 3 * NEAR_BLOCKS, near_hit, jnp.where(lane == 3 * NEAR_BLOCKS, future_hit, 0.0))
    kk_c = jnp.concatenate([kc_ref[...], onehot.astype(BF16)], axis=1)
    imp = [jnp.zeros((ns, TQ), F32) for _ in range(N_KV)]
    for h in range(N_HEADS):
        qh = jnp.concatenate([qq_ref[h], dtab_ref[h * TQ:(h + 1) * TQ, :]], axis=1)
        s = lax.dot_general(kk_c, qh, _NT, preferred_element_type=F32)
        mx = jnp.max(s, axis=0, keepdims=True)
        p = jnp.where(s > 0.1 * NEG, jnp.exp(s - mx), 0.0)
        pn = p / jnp.maximum(jnp.sum(p, axis=0, keepdims=True), 1e-30)
        imp[h // REP] = imp[h // REP] + (pn[:ns, :] + pn[ns:, :])
        out_ref[h // REP, :, head_cols(h)] = gate(h, 0) * jnp.dot(vct_ref[group_rows(h), :], pn.astype(BF16),
                                                                  preferred_element_type=F32)

    imp_t = jnp.concatenate(imp, axis=1)
    a = lax.broadcasted_iota(jnp.int32, imp_t.shape, 1) & (TQ - 1)
    cur = (TQ // SEL_BLOCK) * u + (a >> 6)
    neg = _select_blocks(imp_t, cur).T.astype(BF16)
    for h in range(N_HEADS):
        g = h // REP
        qs_ref[h * TQ:(h + 1) * TQ, :] = jnp.concatenate([qq_ref[h], neg[g * TQ:(g + 1) * TQ, :]], axis=1)

    def reset():
        m_ref[...] = jnp.full(m_ref.shape, -1e38, F32)
        l_ref[...] = jnp.zeros(l_ref.shape, F32)
        acc_ref[...] = jnp.zeros(acc_ref.shape, F32)

    def finish(br):
        for g in range(N_KV):
            cols = slice(g * gcols, (g + 1) * gcols)
            out_ref[g] = out_ref[g] + gate_row(g, br) * (acc_ref[g] / l_ref[:, cols])

    def softmax_tile(vt_ref, keys, bias):
        for h in range(N_HEADS):
            cols = slice(h * TQ, (h + 1) * TQ)
            s = bias(h, s_ref[:, cols])
            m_prev = m_ref[:, cols]
            m_new = jnp.maximum(m_prev, jnp.max(s, axis=0, keepdims=True))
            alpha = jnp.exp(m_prev - m_new)
            p = jnp.exp(s - m_new)
            l_ref[:, cols] = alpha * l_ref[:, cols] + jnp.sum(p, axis=0, keepdims=True)
            p_ref[:, cols] = p.astype(BF16)
            alpha_ref[:, cols] = alpha
            m_ref[:, cols] = m_new
        for g in range(N_KV):
            cols = slice(g * gcols, (g + 1) * gcols)
            pv = jnp.dot(vt_ref[g * HEAD_DIM:(g + 1) * HEAD_DIM, keys], p_ref[:, cols], preferred_element_type=F32)
            acc_ref[g] = alpha_ref[:, cols] * acc_ref[g] + pv

    def near_bias(row):
        if row is None:
            return lambda h, s: s
        return lambda h, s: s + near_ref[h, row:row + TK, :]

    reset()

    def sel_tile(kt, row):
        keys = pl.ds(pl.multiple_of(kt * TK, TK), TK)
        kk = jnp.concatenate([ksel_ref[keys, :], xt_ref[keys, :]], axis=1)
        s_ref[...] = lax.dot_general(kk, qs_ref[...], _NT, preferred_element_type=F32)
        softmax_tile(vselt_ref, keys, near_bias(row))

    def far_body(kt, carry):
        sel_tile(kt, None)
        return carry

    lax.fori_loop(0, u - 1, far_body, 0)

    @pl.when(u >= 1)
    def _():
        sel_tile(u - 1, 0)

    sel_tile(u, TK)
    finish(1)

    reset()

    def later_keys_only(h, s):
        kj = lax.broadcasted_iota(jnp.int32, s.shape, 0)
        qa = lax.broadcasted_iota(jnp.int32, s.shape, 1)
        return jnp.where(kj > qa, s, NEG)

    def win_tile(kt, row):
        keys = pl.ds(pl.multiple_of(kt * TK, TK), TK)
        s_ref[...] = lax.dot_general(kwin_ref[keys, :], qs_ref[:, :LANES], _NT, preferred_element_type=F32)
        softmax_tile(vwint_ref, keys, later_keys_only if row is None else near_bias(row))

    @pl.when(u >= 2)
    def _():
        win_tile(u - 2, None)

    @pl.when(u >= 1)
    def _():
        win_tile(u - 1, 0)

    win_tile(u, TK)
    finish(2)

    for i in range(N_HEADS // 2):
        pair = jnp.concatenate([out_ref[(2 * i) // REP, :, head_cols(2 * i)],
                                out_ref[(2 * i + 1) // REP, :, head_cols(2 * i + 1)]], axis=0)
        o_ref[:, i * LANES:(i + 1) * LANES] = pair.T.astype(BF16)


def _attention_prompt(qq, kc, vct, ksel, vselt, kwin, vwint, gatet, tables, batch, seq):
    near, dtab, lane_m, xt = tables
    n = batch * seq
    tpb = seq // TQ
    nc = seq // CMP_BLOCK
    per_batch = lambda rows: pl.BlockSpec((rows, LANES), lambda i: (i // tpb, 0))
    chan = lambda cols: pl.BlockSpec((None, LANES, cols), lambda i: (i // tpb, 0, 0))
    full = lambda a: pl.BlockSpec(a.shape, lambda i: (0,) * a.ndim)
    return pl.pallas_call(
        functools.partial(_attn_kernel, tiles_per_batch=tpb),
        grid=(n // TQ,),
        in_specs=[pl.BlockSpec((N_HEADS, TQ, LANES), lambda i: (0, i, 0)),
                  per_batch(nc), chan(nc), full(dtab), full(lane_m),
                  per_batch(seq), chan(seq), per_batch(seq), chan(seq),
                  full(xt), full(near), pl.BlockSpec((LANES, TQ), lambda i: (0, i))],
        out_specs=pl.BlockSpec((TQ, ATT_WIDTH), lambda i: (i, 0)),
        out_shape=jax.ShapeDtypeStruct((n, ATT_WIDTH), BF16),
        scratch_shapes=[pltpu.VMEM((N_HEADS * TQ, 2 * LANES), BF16),
                        pltpu.VMEM((TK, N_HEADS * TQ), F32), pltpu.VMEM((TK, N_HEADS * TQ), BF16),
                        pltpu.VMEM((1, N_HEADS * TQ), F32), pltpu.VMEM((1, N_HEADS * TQ), F32),
                        pltpu.VMEM((1, N_HEADS * TQ), F32),
                        pltpu.VMEM((N_KV, HEAD_DIM, REP * TQ), F32), pltpu.VMEM((N_KV, HEAD_DIM, REP * TQ), F32)],
        compiler_params=_cparams(("arbitrary",)),
        name="nsa_prompt",
    )(qq, kc, vct, dtab, lane_m, ksel, vselt, kwin, vwint, xt, near, gatet)


def _block_onehot_rows(seq):
    j = np.arange(seq)[:, None] // SEL_BLOCK
    return jnp.asarray((j == np.arange(LANES)[None, :]).astype(np.float32)).astype(BF16)


def _even_odd(x, nc):
    x = x.reshape(-1, nc // 2, 2, x.shape[-1])
    return jnp.concatenate([x[:, :, 0], x[:, :, 1]], axis=1).reshape(-1, x.shape[-1])


def _sample_bias_tables(rel_bias, past_len, t_new, n_win):
    t = np.arange(t_new)[:, None]
    nc = past_len // CMP_BLOCK
    tb_full = lambda dist, ok: jnp.where(jnp.asarray(ok), _bucket_values(rel_bias.astype(F32).T, dist), NEG)
    dist_c = past_len + t - (CMP_BLOCK * np.arange(nc)[None, :] + CMP_BLOCK - 1)
    dist_s = past_len + t - np.arange(past_len)[None, :]
    jn = np.arange(LANES)[None, :]
    dist_n = t - jn
    dist_w = n_win + t - np.arange(n_win)[None, :]
    rows = lambda x: x.reshape(N_HEADS * t_new, x.shape[-1])
    return (rows(tb_full(dist_c, dist_c >= 0)), rows(tb_full(dist_s, dist_s >= 0)),
            rows(tb_full(dist_n, (dist_n >= 0) & (jn < t_new))),
            rows(tb_full(dist_w, (dist_w >= 0) & (dist_w < WINDOW))))


def _attn_sample_kernel(pt_ref, *refs, n_pages, t_new):
    del pt_ref
    kvc_refs = refs[:n_pages]
    sel_refs = refs[n_pages:2 * n_pages]
    (qq_ref, seln_ref, winn_ref, winb_ref, gate_ref, xt_ref, xtn_ref, bc_ref, bs_ref, bn_ref, bw_ref,
     rsum_ref, pair_ref, o_ref, qs_ref, out_ref, s_ref, p_ref) = refs[2 * n_pages:]

    def gate_col(br):
        return jnp.concatenate([gate_ref[:, h * 3 + br:h * 3 + br + 1] for h in range(N_HEADS)], axis=0)

    def softmax_tiles(q, tiles):
        off = 0
        for k, _, bias, channel_major, width in tiles:
            if channel_major:
                s = jnp.dot(q, k(), preferred_element_type=F32)
            else:
                s = lax.dot_general(q, k(), _NT, preferred_element_type=F32)
            s_ref[:, off:off + width] = s + bias()
            off += width
        s = s_ref[:, :off]
        p = jnp.exp(s - jnp.max(s, axis=1, keepdims=True))
        den = jnp.sum(p, axis=1, keepdims=True)
        p_ref[:, :off] = p.astype(BF16)
        acc, off = None, 0
        for _, v, _, channel_major, width in tiles:
            p_t = p_ref[:, off:off + width]
            if channel_major:
                pv = lax.dot_general(p_t, v(), _NT, preferred_element_type=F32)
            else:
                pv = jnp.dot(p_t, v(), preferred_element_type=F32)
            acc = pv if acc is None else acc + pv
            off += width
        return acc / den

    qs_ref[:, :LANES] = qq_ref[...]

    kvc = jnp.concatenate([r[...] for r in kvc_refs], axis=0)
    s = lax.dot_general(qq_ref[...], kvc[:, :LANES].astype(BF16), _NT, preferred_element_type=F32) + bc_ref[...]
    mx = jnp.max(s, axis=1, keepdims=True)
    p = jnp.where(s > 0.1 * NEG, jnp.exp(s - mx), 0.0)
    pn = p / jnp.maximum(jnp.sum(p, axis=1, keepdims=True), 1e-30)
    out_ref[...] = gate_col(0) * jnp.dot(pn.astype(BF16), kvc[:, LANES:].astype(BF16), preferred_element_type=F32)

    hp = lax.Precision.HIGHEST
    imp = jnp.dot(jnp.dot(rsum_ref[...], pn, precision=hp, preferred_element_type=F32), pair_ref[...],
                  precision=hp, preferred_element_type=F32)
    cur = (n_pages * PAGE_SIZE) // SEL_BLOCK
    neg = _select_blocks(imp.T, cur, unroll=True).T
    qs_ref[:, LANES:] = jnp.concatenate(
        [neg[(h // REP) * t_new:(h // REP + 1) * t_new, :] for h in range(N_HEADS)], axis=0).astype(BF16)

    def past_tile(ref, j):
        cols = slice(j * PAGE_SIZE, (j + 1) * PAGE_SIZE)
        return (lambda: jnp.concatenate([ref[:LANES, :].astype(BF16), xt_ref[:, cols]], axis=0),
                lambda: ref[LANES:, :].astype(BF16), lambda: bs_ref[:, cols], True, PAGE_SIZE)

    new_sel_tile = (lambda: jnp.concatenate([seln_ref[:, :LANES].astype(BF16), xtn_ref[...]], axis=1),
                    lambda: seln_ref[:, LANES:].astype(BF16), lambda: bn_ref[...], False, PAGE_SIZE)
    tiles = [past_tile(sel_refs[j], j) for j in range(n_pages)] + [new_sel_tile]
    out_ref[...] += gate_col(1) * softmax_tiles(qs_ref[...], tiles)

    tiles = [(lambda: winb_ref[:LANES, :].astype(BF16), lambda: winb_ref[LANES:, :].astype(BF16), lambda: bw_ref[...],
              True, winb_ref.shape[1]),
             (lambda: winn_ref[:, :LANES].astype(BF16), lambda: winn_ref[:, LANES:].astype(BF16), lambda: bn_ref[...],
              False, PAGE_SIZE)]
    out_ref[...] += gate_col(2) * softmax_tiles(qq_ref[...], tiles)

    lane_o = lax.broadcasted_iota(jnp.int32, (t_new, LANES), 1)
    for i in range(N_HEADS // 2):
        left = out_ref[2 * i * t_new:(2 * i + 1) * t_new, :]
        right = out_ref[(2 * i + 1) * t_new:(2 * i + 2) * t_new, :]
        if (2 * i) // REP == 0:
            right = pltpu.roll(right, HEAD_DIM, 1)
        else:
            left = pltpu.roll(left, HEAD_DIM, 1)
        o_ref[:, i * LANES:(i + 1) * LANES] = jnp.where(lane_o < HEAD_DIM, left, right)


def _attention_sample(page_table, kvc_pages, sel_cache, layer, qq, sel_new, win_new, win_buf, gates, tables, t_new):
    n_seq, n_pages = page_table.shape
    xt, xtn, bc, bs, bn, bw, rsum, pair = tables
    n_win = win_buf.shape[3]
    nq = N_HEADS * t_new
    page_spec = lambda shape, j: pl.BlockSpec((None, None) + shape, lambda b, pt, j=j: (layer, pt[b, j], 0, 0))
    full = lambda a: pl.BlockSpec(a.shape, lambda b, pt: (0,) * a.ndim)
    new = pl.BlockSpec((None, PAGE_SIZE, KV_WIDTH), lambda b, pt: (b, 0, 0))
    in_specs = ([page_spec((PAGE_SIZE // CMP_BLOCK, KV_WIDTH), j) for j in range(n_pages)]
                + [page_spec((KV_WIDTH, PAGE_SIZE), j) for j in range(n_pages)]
                + [pl.BlockSpec((None, nq, LANES), lambda b, pt: (b, 0, 0)), new, new,
                   pl.BlockSpec((None, None, KV_WIDTH, n_win), lambda b, pt: (layer, b, 0, 0)),
                   pl.BlockSpec((t_new, LANES), lambda b, pt: (b, 0))]
                + [full(a) for a in (xt, xtn, bc, bs, bn, bw, rsum, pair)])
    return pl.pallas_call(
        functools.partial(_attn_sample_kernel, n_pages=n_pages, t_new=t_new),
        grid_spec=pltpu.PrefetchScalarGridSpec(
            num_scalar_prefetch=1, grid=(n_seq,), in_specs=in_specs,
            out_specs=pl.BlockSpec((t_new, ATT_WIDTH), lambda b, pt: (b, 0)),
            scratch_shapes=[pltpu.VMEM((nq, 2 * LANES), BF16), pltpu.VMEM((nq, LANES), F32),
                            pltpu.VMEM((nq, (n_pages + 1) * PAGE_SIZE), F32),
                            pltpu.VMEM((nq, (n_pages + 1) * PAGE_SIZE), BF16)]),
        out_shape=jax.ShapeDtypeStruct((n_seq * t_new, ATT_WIDTH), F32),
        compiler_params=_cparams(("arbitrary",)),
        name="nsa_sample",
    )(page_table, *([kvc_pages] * n_pages), *([sel_cache] * n_pages), qq, sel_new, win_new, win_buf, gates,
      xt, xtn, bc, bs, bn, bw, rsum, pair)


def _sample_sum_matrices(t_new, nc):
    rsum = np.zeros((LANES, N_HEADS * t_new), np.float32)
    for h in range(N_HEADS):
        for t in range(t_new):
            rsum[(h // REP) * t_new + t, h * t_new + t] = 1.0
    pair = np.zeros((nc, LANES), np.float32)
    pair[np.arange(nc), np.arange(nc) // 2] = 1.0
    return jnp.asarray(rsum), jnp.asarray(pair)


def _rank_before(vals, k):
    r = jnp.zeros(vals[k].shape, jnp.int32)
    for j, vj in enumerate(vals):
        if j < k:
            r = r + jnp.where(vj >= vals[k], 1, 0)
        elif j > k:
            r = r + jnp.where(vj > vals[k], 1, 0)
    return r


def _route(s_rows, b_rows):
    scores = []
    for g in range(N_GROUPS):
        b0, b1, b2, b3 = b_rows[4 * g:4 * g + 4]
        hi01, lo01, hi23, lo23 = jnp.maximum(b0, b1), jnp.minimum(b0, b1), jnp.maximum(b2, b3), jnp.minimum(b2, b3)
        top1 = jnp.maximum(hi01, hi23)
        top2 = jnp.maximum(jnp.maximum(lo01, lo23), jnp.minimum(hi01, hi23))
        scores.append(top1 + top2)
    in_group = [_rank_before(scores, g) == 0 for g in range(N_GROUPS)]

    def pick(rows, k):
        out = rows[4 * (N_GROUPS - 1) + k]
        for g in range(N_GROUPS - 2, -1, -1):
            out = jnp.where(in_group[g], rows[4 * g + k], out)
        return out

    bv = [pick(b_rows, k) for k in range(EXPERTS_PER_GROUP)]
    sv = [pick(s_rows, k) for k in range(EXPERTS_PER_GROUP)]
    w = [jnp.where(_rank_before(bv, k) < 2, sv[k], 0.0) for k in range(EXPERTS_PER_GROUP)]
    den = (w[0] + w[1]) + (w[2] + w[3])
    return [jnp.where(in_group[e // 4], w[e % 4] / den, 0.0) for e in range(N_EXPERTS)]


def _outproj_kernel(ps_ref, att_ref, x_ref, mod_ref, wout_ref, lng_ref, lnb_ref, rwt_ref, rb_ref,
                    x1_ref, h2_ref, g_ref):
    half = ps_ref.shape[1]
    mix = (jnp.dot(ps_ref[...], wout_ref[:half, :], preferred_element_type=F32)
           + jnp.dot(att_ref[...], wout_ref[half:, :], preferred_element_type=F32))
    x1 = _layer_norm(ALPHA * x_ref[...] + (1.0 + mod_ref[2]) * mix, lng_ref[...], lnb_ref[...])
    x1_ref[...] = x1
    h2 = x1 * (1.0 + mod_ref[4]) + mod_ref[3]
    h2_ref[...] = h2.astype(BF16)
    st = jax.nn.sigmoid(lax.dot_general(rwt_ref[...], h2, _NT, precision=lax.Precision.HIGHEST,
                                        preferred_element_type=F32))
    s_rows = [st[e:e + 1, :] for e in range(N_EXPERTS)]
    b_rows = [s_rows[e] + rb_ref[e:e + 1, :] for e in range(N_EXPERTS)]
    gt = jnp.concatenate(_route(s_rows, b_rows) + [jnp.zeros((LANES - N_EXPERTS, st.shape[1]), F32)], axis=0)
    g_ref[...] = gt.T


def _out_proj(ps, att, x, mod, wout, ln_g, ln_b, rwt, rb, tm, tiles_per_batch):
    n = x.shape[0]
    row = lambda wd: pl.BlockSpec((tm, wd), lambda i: (i, 0))
    full = lambda a: pl.BlockSpec(a.shape, lambda i: (0,) * a.ndim)
    return pl.pallas_call(
        _outproj_kernel,
        grid=(n // tm,),
        in_specs=[row(512), row(512), row(D_MODEL), _mod_spec(mod, tm, tiles_per_batch),
                  full(wout), full(ln_g), full(ln_b), full(rwt), full(rb)],
        out_specs=[row(D_MODEL), row(D_MODEL), row(LANES)],
        out_shape=[jax.ShapeDtypeStruct((n, D_MODEL), F32), jax.ShapeDtypeStruct((n, D_MODEL), BF16),
                   jax.ShapeDtypeStruct((n, LANES), F32)],
        compiler_params=_cparams(("arbitrary",)),
        name="out_proj",
    )(ps, att, x, mod, wout, ln_g, ln_b, rwt, rb)


def _moe_kernel(h2_ref, g_ref, wg_ref, wu_ref, wd_ref, x1_ref, mod_ref, lng_ref, lnb_ref, o_ref, acc_ref):
    e = pl.program_id(1)

    @pl.when(e == 0)
    def _():
        acc_ref[...] = jnp.zeros_like(acc_ref)

    x = h2_ref[...]
    gate = jnp.dot(x, wg_ref[0, 0], preferred_element_type=F32)
    up = jnp.dot(x, wu_ref[0, 0], preferred_element_type=F32)
    act = (gate * jax.nn.sigmoid(gate) * up).astype(BF16)
    y = jnp.dot(act, wd_ref[0, 0], preferred_element_type=F32)
    lane = lax.broadcasted_iota(jnp.int32, g_ref.shape, 1)
    w = jnp.sum(jnp.where(lane == e, g_ref[...], 0.0), axis=1, keepdims=True)
    acc_ref[...] += y * w

    @pl.when(e == pl.num_programs(1) - 1)
    def _():
        o_ref[...] = _layer_norm(ALPHA * x1_ref[...] + (1.0 + mod_ref[5]) * acc_ref[...], lng_ref[...], lnb_ref[...])


def _moe(h2, gates, wg, wu, wd, layer, x1, mod, ln_g, ln_b, tm, tiles_per_batch):
    n = h2.shape[0]
    row = lambda wd_: pl.BlockSpec((tm, wd_), lambda i, e: (i, 0))
    full = lambda a: pl.BlockSpec(a.shape, lambda i, e: (0,) * a.ndim)
    wspec = lambda a: pl.BlockSpec((1, 1) + a.shape[2:], lambda i, e: (layer, e, 0, 0))
    return pl.pallas_call(
        _moe_kernel,
        grid=(n // tm, N_EXPERTS),
        in_specs=[row(D_MODEL), row(LANES), wspec(wg), wspec(wu), wspec(wd), row(D_MODEL),
                  _mod_spec(mod, tm, tiles_per_batch), full(ln_g), full(ln_b)],
        out_specs=row(D_MODEL),
        out_shape=jax.ShapeDtypeStruct((n, D_MODEL), F32),
        scratch_shapes=[pltpu.VMEM((tm, D_MODEL), F32)],
        compiler_params=_cparams(("arbitrary", "arbitrary")),
        name="moe",
    )(h2, gates, wg, wu, wd, x1, mod, ln_g, ln_b)


TM_PROJ = 256
TM_MOE = 512


def _channel_mix(ps, att, x, mod, lw, sw, layer, tiles_per_seq):
    x1, h2, gts = _out_proj(ps, att, x, mod, lw["wout"], lw["ln_g0"], lw["ln_b0"], sw["rwt"], sw["rb"],
                            TM_PROJ, tiles_per_seq(TM_PROJ))
    return _moe(h2, gts, sw["wg"], sw["wu"], sw["wd"], layer, x1, mod, lw["ln_g1"], lw["ln_b1"],
                TM_MOE, tiles_per_seq(TM_MOE))


def _prompt_layer(x, mod, lw, sw, layer, batch, seq):
    n = batch * seq
    tiles = lambda tm: seq // tm
    (p, u, vn, qq, cmp_k, cmp_v, cmp_t, sel_t, win_t, gate_t, ksel, vsel_t, kwin, vwin_t) = _in_proj(
        x, mod, lw["w_proj"], lw["sgu_ln_g"], lw["sgu_ln_b"], TM_PROJ, tiles(TM_PROJ), batch=batch)
    ps = _mixers(p, u, vn, *lw["mix_prompt"], tiles_per_batch=tiles(SGU_CHUNK))
    nc = seq // CMP_BLOCK
    kvc = _compress(cmp_k, cmp_v, sw["pe_rows"], sw["w1bd"], sw["w2bd"], nc, layer)
    kc = _even_odd(kvc[:, :LANES].astype(BF16), nc)
    vc_t = jnp.transpose(_even_odd(kvc[:, LANES:].astype(BF16), nc).reshape(batch, nc, LANES), (0, 2, 1))
    att = _attention_prompt(qq, kc, vc_t, ksel, vsel_t, kwin, vwin_t, gate_t, sw["prompt_tables"], batch, seq)
    x2 = _channel_mix(ps, att, x, mod, lw, sw, layer, tiles)
    return x2, (cmp_t, sel_t, win_t, p)


def _sample_layer(x, mod, lw, sw, layer, page_table, kvc_pages, sel_cache, win_state, pool_state, n_seq, t_new):
    tiles = lambda tm: 1
    (p, u, vn, qq, cmp_raw, sel_raw, win_raw, gates) = _in_proj(
        x, mod, lw["w_proj"], lw["sgu_ln_g"], lw["sgu_ln_b"], TM_PROJ, 1)
    p_ext = jnp.concatenate([pool_state[layer], p.reshape(n_seq, t_new, POOL_WIDTH)], axis=1)
    hist = jnp.pad(p_ext, ((0, 0), (3 * t_new - p_ext.shape[1], 0), (0, 0)))
    chunks = [hist[:, k * t_new:(k + 1) * t_new].reshape(n_seq * t_new, POOL_WIDTH) for k in range(3)]
    ps = _mixers(chunks, u, vn, *lw["mix_sample"], t_new=t_new)
    q_rows = jnp.transpose(qq.reshape(N_HEADS, n_seq, t_new, LANES), (1, 0, 2, 3)).reshape(n_seq, N_HEADS * t_new, LANES)
    new_page = lambda raw: jnp.pad(raw.reshape(n_seq, t_new, KV_WIDTH), ((0, 0), (0, PAGE_SIZE - t_new), (0, 0)))
    att = _attention_sample(page_table, kvc_pages, sel_cache, layer, q_rows, new_page(sel_raw), new_page(win_raw),
                            win_state, gates, sw["sample_tables"], t_new)
    x2 = _channel_mix(ps, att.astype(BF16), x, mod, lw, sw, layer, tiles)
    win_new_t = jnp.transpose(win_raw.reshape(n_seq, t_new, KV_WIDTH), (0, 2, 1))
    win_all_t = jnp.concatenate([win_state[layer][:, :, t_new:], win_new_t], axis=2)
    return x2, (cmp_raw, sel_raw, win_all_t, p_ext[:, p_ext.shape[1] - POOL_BUF:], vn)


def kernel(x_prompt, x_sample, cache_cmp_kv, cache_sel_kv, state_win_kv, state_pool, page_table, c_prompt, c_sample,
           w_in, w_out, pool_w, pool_scale, sgu_ln_g, sgu_ln_b, sgu_w, sgu_b, cmp_pe, cmp_w1, cmp_w2, rel_bias,
           w_mod, b_mod, ln_g, ln_b, router_w, router_b, moe_w_gate, moe_w_up, moe_w_down):
    batch, seq, d = x_prompt.shape
    n_seq, t_new, _ = x_sample.shape
    depth = w_in.shape[0]
    n_pages = page_table.shape[1]
    past_len = n_pages * PAGE_SIZE
    n_phys = cache_cmp_kv.shape[1]
    n_win = state_win_kv.shape[2]
    assert seq // SEL_BLOCK == LANES and seq % TM_MOE == 0 and (n_seq * t_new) % TM_MOE == 0
    assert POOL_BUF + 1 == 2 * t_new and past_len % TK == 0

    n_c = batch + n_seq
    c_all = jnp.pad(jnp.concatenate([c_prompt, c_sample], axis=0), ((0, -n_c % 8), (0, 0)))
    m_all = _modulation(c_all, w_mod, b_mod)
    mod_p = m_all[:, :batch].reshape(depth, batch, 6, 1, d)
    mod_s = jnp.transpose(jnp.repeat(m_all[:, batch:n_c].reshape(depth, n_seq, 6, d), t_new, axis=1), (0, 2, 1, 3))

    pe_rows, w1bd, w2bd = _prep_compress_weights(cmp_pe, cmp_w1, cmp_w2)
    near, dtab, lane_m = _prompt_bias_tables(rel_bias)
    nc_past = past_len // CMP_BLOCK
    shared = {
        "pe_rows": pe_rows, "w1bd": w1bd, "w2bd": w2bd,
        "prompt_tables": (near, dtab, lane_m, _block_onehot_rows(seq)),
        "sample_tables": (_block_onehot_rows(past_len + PAGE_SIZE).T, _block_onehot_rows(past_len + PAGE_SIZE)[past_len:],
                          *_sample_bias_tables(rel_bias, past_len, t_new, n_win),
                          *_sample_sum_matrices(t_new, nc_past)),
        "rwt": router_w.T, "rb": router_b.reshape(N_EXPERTS, 1),
        "wg": moe_w_gate.astype(BF16), "wu": moe_w_up.astype(BF16), "wd": moe_w_down.astype(BF16),
    }
    layers = []
    for l in range(depth):
        layers.append({
            "w_proj": _prep_w_proj(w_in[l]), "wout": w_out[l].astype(BF16),
            "sgu_ln_g": sgu_ln_g[l].reshape(1, -1), "sgu_ln_b": sgu_ln_b[l].reshape(1, -1),
            "mix_prompt": _prep_mixer_weights(pool_w[l], pool_scale[l], sgu_w[l], sgu_b[l], SGU_CHUNK),
            "mix_sample": _prep_mixer_weights(pool_w[l], pool_scale[l], sgu_w[l], sgu_b[l], t_new),
            "ln_g0": ln_g[l, 0].reshape(1, d), "ln_b0": ln_b[l, 0].reshape(1, d),
            "ln_g1": ln_g[l, 1].reshape(1, d), "ln_b1": ln_b[l, 1].reshape(1, d),
        })

    chan_major = lambda x: jnp.transpose(x, (0, 1, 3, 4, 5, 2)).reshape(x.shape[0], x.shape[1], KV_WIDTH, x.shape[2])
    pe_pages = jnp.broadcast_to(jnp.transpose(cmp_pe, (0, 1, 3, 2))[:, :, None, :, None, :],
                                (depth, 2, N_KV, HEAD_DIM, PAGE_SIZE // CMP_BLOCK, CMP_BLOCK))
    kvc_pages = _compress_pages(chan_major(cache_cmp_kv), pe_pages.reshape(depth, KV_WIDTH, PAGE_SIZE), w1bd, w2bd, 64)
    kvc_pages = kvc_pages.reshape(depth, n_phys, PAGE_SIZE // CMP_BLOCK, KV_WIDTH)
    sel_cache = chan_major(cache_sel_kv)
    win_state = chan_major(state_win_kv)

    xp = x_prompt.reshape(batch * seq, d)
    xs = x_sample.reshape(n_seq * t_new, d)
    outs_p, outs_s = [], []
    for l in range(depth):
        xp, st = _prompt_layer(xp, mod_p[l], layers[l], shared, l, batch, seq)
        outs_p.append(st)
        xs, st = _sample_layer(xs, mod_s[l], layers[l], shared, l, page_table, kvc_pages, sel_cache, win_state,
                               state_pool, n_seq, t_new)
        outs_s.append(st)

    kv5 = lambda x, b: x.reshape(b, -1, 2, N_KV, HEAD_DIM)
    kv5_t = lambda x: jnp.transpose(x.reshape(x.shape[0], 2, N_KV, HEAD_DIM, x.shape[2]), (0, 4, 1, 2, 3))
    stack = lambda xs_: jnp.stack(xs_)
    w_keep = min(WINDOW, seq)
    return (xp.reshape(batch, seq, d), xs.reshape(n_seq, t_new, d),
            stack([kv5_t(o[0]) for o in outs_p]), stack([kv5(o[0], n_seq) for o in outs_s]),
            stack([kv5_t(o[1]) for o in outs_p]), stack([kv5(o[1], n_seq) for o in outs_s]),
            stack([kv5_t(o[2][:, :, seq - w_keep:]) for o in outs_p]), stack([kv5_t(o[2]) for o in outs_s]),
            stack([o[3].reshape(batch, seq, POOL_WIDTH)[:, seq - POOL_BUF:] for o in outs_p]),
            stack([o[3] for o in outs_s]),
            stack([o[4].reshape(n_seq, t_new, SGU_WIDTH) for o in outs_s]))
```

```python
import functools
import math

import numpy as np
import jax
import jax.numpy as jnp
from jax import lax
from jax.experimental import pallas as pl
from jax.experimental.pallas import tpu as pltpu

F32 = jnp.float32
BF16 = jnp.bfloat16

D_MODEL = 1024
POOL_WIDTH = 256
SGU_WIDTH = 256
ATT_WIDTH = 512
POOL_WINDOWS = (2, 4, 8, 16)
POOL_GROUP_DIM = 64
POOL_BUF = 15
SGU_GROUPS = 4
SGU_CHUNK = 128
N_HEADS = 8
HEAD_DIM = 64
N_KV = 2
REP = 4
CMP_BLOCK = 32
CMP_HIDDEN = 128
SEL_BLOCK = 64
N_SEL = 16
WINDOW = 512
N_BUCKETS = 32
REL_MAX_DIST = 128
N_EXPERTS = 16
N_GROUPS = 4
EXPERTS_PER_GROUP = 4
D_EXPERT = 512
DEPTH = 2
ALPHA = (2 * DEPTH) ** 0.25
LN_EPS = 1e-5
FORCED_SCORE = 1e4
NEG = -1e30
PAGE_SIZE = 128

KV_WIDTH = 2 * N_KV * HEAD_DIM
LANES = 128
VMEM_LIMIT = 56 * 1024 * 1024

C_P, C_U, C_V, C_CMP, C_SEL, C_WIN, C_GATE, C_Q = 0, 256, 512, 768, 1024, 1280, 1536, 1664
W_PROJ = C_Q + N_HEADS * LANES

TQ = 256
TK = 256
NEAR_BLOCKS = 12
FAR_BUCKET_DIST = 113


def _cparams(sem):
    return pltpu.CompilerParams(dimension_semantics=sem, vmem_limit_bytes=VMEM_LIMIT)


def _layer_norm(x, g, b):
    mu = jnp.mean(x, axis=-1, keepdims=True)
    xc = x - mu
    var = jnp.mean(xc * xc, axis=-1, keepdims=True)
    return xc * lax.rsqrt(var + LN_EPS) * g + b


def _mod_kernel(c_ref, w_ref, b_ref, o_ref):
    c = c_ref[...]
    a = (c * jax.nn.sigmoid(c)).astype(BF16)
    o_ref[0] = jnp.dot(a, w_ref[0].astype(BF16), preferred_element_type=F32) + b_ref[0]


def _modulation(c_all, w_mod, b_mod):
    n, d = c_all.shape
    depth, _, w = w_mod.shape
    tn = 1536
    return pl.pallas_call(
        _mod_kernel,
        grid=(depth, w // tn),
        in_specs=[pl.BlockSpec((n, d), lambda l, j: (0, 0)),
                  pl.BlockSpec((1, d, tn), lambda l, j: (l, 0, j)),
                  pl.BlockSpec((1, 1, tn), lambda l, j: (l, 0, j))],
        out_specs=pl.BlockSpec((1, n, tn), lambda l, j: (l, 0, j)),
        out_shape=jax.ShapeDtypeStruct((depth, n, w), F32),
        compiler_params=_cparams(("arbitrary", "arbitrary")),
        name="adaln_mod",
    )(c_all, w_mod, b_mod.reshape(depth, 1, w))


def _mod_spec(mod, tm, tiles_per_batch):
    if mod.ndim == 4:
        return pl.BlockSpec((None, 6, 1, D_MODEL), lambda i, *_: (i // tiles_per_batch, 0, 0, 0))
    return pl.BlockSpec((6, tm, D_MODEL), lambda i, *_: (0, i, 0))


def _inproj_kernel(x_ref, mod_ref, w_ref, lng_ref, lnb_ref, p_ref, u_ref, vn_ref, qq_ref, *rest, channel_major):
    h = (x_ref[...] * (1.0 + mod_ref[1]) + mod_ref[0]).astype(BF16)

    def seg(a, b):
        return jnp.dot(h, w_ref[:, a:b], preferred_element_type=F32)

    p_ref[...] = seg(C_P, C_U)
    u_ref[...] = seg(C_U, C_V)
    vn_ref[...] = _layer_norm(seg(C_V, C_CMP), lng_ref[...], lnb_ref[...])
    for hd in range(N_HEADS):
        qq_ref[hd] = seg(C_Q + hd * LANES, C_Q + (hd + 1) * LANES).astype(BF16)
    cmp = seg(C_CMP, C_SEL)
    sel = seg(C_SEL, C_WIN)
    win = seg(C_WIN, C_GATE)
    gates = jax.nn.sigmoid(seg(C_GATE, C_Q))
    if not channel_major:
        cmp_ref, sel_ref, win_ref, gate_ref = rest
        cmp_ref[...] = cmp
        sel_ref[...] = sel
        win_ref[...] = win
        gate_ref[...] = gates
        return
    cmpk_ref, cmpv_ref, cmpt_ref, selt_ref, wint_ref, gatet_ref, ksel_ref, vselt_ref, kwin_ref, vwint_ref = rest
    cmpk_ref[...] = cmp[:, :LANES]
    cmpv_ref[...] = cmp[:, LANES:]
    cmpt_ref[...] = cmp.T
    sel_t = sel.T
    selt_ref[...] = sel_t
    ksel_ref[...] = sel[:, :LANES].astype(BF16)
    vselt_ref[...] = sel_t[LANES:, :].astype(BF16)
    win_t = win.T
    wint_ref[...] = win_t
    kwin_ref[...] = win[:, :LANES].astype(BF16)
    vwint_ref[...] = win_t[LANES:, :].astype(BF16)
    gatet_ref[...] = gates.T


def _in_proj(x, mod, w_proj, ln_g, ln_b, tm, tiles_per_batch, batch=None):
    n = x.shape[0]
    row = lambda wd: pl.BlockSpec((tm, wd), lambda i: (i, 0))
    full = lambda a: pl.BlockSpec(a.shape, lambda i: (0,) * a.ndim)
    f32o = lambda wd: jax.ShapeDtypeStruct((n, wd), F32)
    out_specs = [row(256), row(256), row(256), pl.BlockSpec((N_HEADS, tm, LANES), lambda i: (0, i, 0))]
    out_shape = [f32o(256), f32o(256), f32o(256), jax.ShapeDtypeStruct((N_HEADS, n, LANES), BF16)]
    if batch is None:
        out_specs += [row(256), row(256), row(256), row(LANES)]
        out_shape += [f32o(256), f32o(256), f32o(256), f32o(LANES)]
    else:
        seq = n // batch
        chan = lambda c: pl.BlockSpec((None, c, tm), lambda i: (i // tiles_per_batch, 0, i % tiles_per_batch))
        chan_o = lambda c, dt: jax.ShapeDtypeStruct((batch, c, seq), dt)
        out_specs += [row(LANES), row(LANES), chan(256), chan(256), chan(256), pl.BlockSpec((LANES, tm), lambda i: (0, i)),
                      row(LANES), chan(LANES), row(LANES), chan(LANES)]
        out_shape += [f32o(LANES), f32o(LANES),
                      chan_o(256, F32), chan_o(256, F32), chan_o(256, F32), jax.ShapeDtypeStruct((LANES, n), F32),
                      jax.ShapeDtypeStruct((n, LANES), BF16), chan_o(LANES, BF16),
                      jax.ShapeDtypeStruct((n, LANES), BF16), chan_o(LANES, BF16)]
    return pl.pallas_call(
        functools.partial(_inproj_kernel, channel_major=batch is not None),
        grid=(n // tm,),
        in_specs=[row(D_MODEL), _mod_spec(mod, tm, tiles_per_batch), full(w_proj), full(ln_g), full(ln_b)],
        out_specs=out_specs,
        out_shape=out_shape,
        compiler_params=_cparams(("arbitrary",)),
        name="in_proj",
    )(x, mod, w_proj, ln_g, ln_b)


def _prep_w_proj(w_in):
    d = w_in.shape[0]
    q = w_in[:, 768:1280].reshape(d, N_KV, REP, HEAD_DIM) * (HEAD_DIM ** -0.5)
    qq = jnp.zeros((d, N_KV, REP, N_KV, HEAD_DIM), w_in.dtype)
    for g in range(N_KV):
        qq = qq.at[:, g, :, g, :].set(q[:, g])
    gate = jnp.pad(w_in[:, 2048:2072], ((0, 0), (0, LANES - 24)))
    cols = [w_in[:, 0:768], w_in[:, 1280:2048], gate, qq.reshape(d, N_HEADS * LANES)]
    return jnp.concatenate(cols, axis=1).astype(BF16)


def _window_sums(shifted):
    acc = shifted(0)
    sums = {}
    for k in range(1, 16):
        acc = acc + shifted(k)
        if k + 1 in POOL_WINDOWS:
            sums[k + 1] = acc
    lane = lax.broadcasted_iota(jnp.int32, acc.shape, 1)
    return jnp.where(lane < 64, sums[2], jnp.where(lane < 128, sums[4], jnp.where(lane < 192, sums[8], sums[16])))


def _mixers_tail(sums, cnt, cur, u, vn, wpool_ref, pscale_ref, wcat_ref, sbias_ref, o_ref):
    diff = (sums / cnt - cur).astype(BF16)
    pool = jnp.dot(diff, wpool_ref[...], preferred_element_type=F32) * pscale_ref[...]
    lane = lax.broadcasted_iota(jnp.int32, vn.shape, 1)
    vb = vn.astype(BF16)
    zero = jnp.zeros_like(vb)
    stacked = jnp.concatenate([jnp.where((lane // 64) == g, vb, zero) for g in range(SGU_GROUPS)], axis=0)
    mixed = jnp.dot(wcat_ref[...], stacked, preferred_element_type=F32) + sbias_ref[...]
    o_ref[:, :POOL_WIDTH] = pool.astype(BF16)
    o_ref[:, POOL_WIDTH:] = (u * mixed).astype(BF16)


def _lane_window():
    lane = lax.broadcasted_iota(jnp.int32, (SGU_CHUNK, POOL_WIDTH), 1)
    return jnp.where(lane < 64, 2, jnp.where(lane < 128, 4, jnp.where(lane < 192, 8, 16)))


def _mix_prompt_kernel(p_ref, u_ref, vn_ref, wpool_ref, pscale_ref, wcat_ref, sbias_ref, o_ref, prev_ref,
                       *, tiles_per_batch):
    t = pl.program_id(0) % tiles_per_batch

    @pl.when(t == 0)
    def _():
        prev_ref[...] = jnp.zeros_like(prev_ref)

    cur = p_ref[...]
    prev = prev_ref[...]
    row = lax.broadcasted_iota(jnp.int32, cur.shape, 0)

    def shifted(k):
        if k == 0:
            return cur
        return jnp.where(row >= k, pltpu.roll(cur, k, 0), pltpu.roll(prev, k, 0))

    sums = _window_sums(shifted)
    prev_ref[...] = cur
    cnt = jnp.minimum(_lane_window(), t * SGU_CHUNK + row + 1).astype(F32)
    _mixers_tail(sums, cnt, cur, u_ref[...], vn_ref[...], wpool_ref, pscale_ref, wcat_ref, sbias_ref, o_ref)


def _mix_sample_kernel(pa_ref, pb_ref, pc_ref, u_ref, vn_ref, wpool_ref, pscale_ref, wcat_ref, sbias_ref, o_ref,
                       *, t_new):
    a, b, c = pa_ref[...], pb_ref[...], pc_ref[...]
    rows = c.shape[0]
    t = lax.broadcasted_iota(jnp.int32, c.shape, 0) % t_new

    def shifted(k):
        if k == 0:
            return c
        hi, lo = (c, b) if k < t_new else (b, a)
        kk = k % t_new
        if kk == 0:
            return hi
        return jnp.where(t >= kk, pltpu.roll(hi, kk, 0), pltpu.roll(lo, rows - t_new + kk, 0))

    sums = _window_sums(shifted)
    cnt = _lane_window().astype(F32)
    _mixers_tail(sums, cnt, c, u_ref[...], vn_ref[...], wpool_ref, pscale_ref, wcat_ref, sbias_ref, o_ref)


def _mixers(p_in, u, vn, wpool, pscale, wcat, sbias, tiles_per_batch=None, t_new=None):
    n = u.shape[0]
    tm = SGU_CHUNK
    row = lambda wd: pl.BlockSpec((tm, wd), lambda i: (i, 0))
    full = lambda a: pl.BlockSpec(a.shape, lambda i: (0,) * a.ndim)
    weights = [wpool, pscale, wcat, sbias]
    if t_new is None:
        kern = functools.partial(_mix_prompt_kernel, tiles_per_batch=tiles_per_batch)
        ins, scratch = [p_in], [pltpu.VMEM((tm, POOL_WIDTH), F32)]
    else:
        kern = functools.partial(_mix_sample_kernel, t_new=t_new)
        ins, scratch = list(p_in), []
    return pl.pallas_call(
        kern,
        grid=(n // tm,),
        in_specs=[row(256)] * (len(ins) + 2) + [full(a) for a in weights],
        out_specs=row(512),
        out_shape=jax.ShapeDtypeStruct((n, 512), BF16),
        scratch_shapes=scratch,
        compiler_params=_cparams(("arbitrary",)),
        name="mixers",
    )(*ins, u, vn, *weights)


def _prep_mixer_weights(pool_w, pool_scale, sgu_w, sgu_b, chunk):
    wpool = jax.scipy.linalg.block_diag(*[pool_w[g] for g in range(len(POOL_WINDOWS))]).astype(BF16)
    tri = jnp.tril(jnp.ones((chunk, chunk), bool))
    w = jnp.where(tri, sgu_w[:, :chunk, :chunk], 0.0)
    reps = SGU_CHUNK // chunk
    eye = jnp.eye(reps, dtype=w.dtype)
    wt = jnp.concatenate([jnp.kron(eye, w[g]) for g in range(SGU_GROUPS)], axis=1).astype(BF16)
    b = jnp.tile(sgu_b[:, :chunk], (1, reps))
    sbias = jnp.repeat(b.T, SGU_WIDTH // SGU_GROUPS, axis=1)
    return wpool, pool_scale.reshape(1, POOL_WIDTH), wt, sbias


def _compress_mlp(token_rows, w1_ref, w2_ref, o_ref, nblk):
    for s in range(2):
        acc = jnp.zeros((nblk, N_KV * CMP_HIDDEN), F32)
        for j in range(0, CMP_BLOCK, 2):
            pair = jnp.concatenate([token_rows(s, j), token_rows(s, j + 1)], axis=1)
            acc = acc + jnp.dot(pair, w1_ref[0, s, j // 2], preferred_element_type=F32)
        hdn = jax.nn.gelu(acc).astype(BF16)
        o_ref[0, :, s * LANES:(s + 1) * LANES] = jnp.dot(hdn, w2_ref[0, s], preferred_element_type=F32)


def _compress_kernel(k_ref, v_ref, pe_ref, w1_ref, w2_ref, o_ref, *, nblk):
    bufs = (k_ref, v_ref)
    rows = lambda s, j: (bufs[s][pl.ds(j, nblk, stride=CMP_BLOCK), :] + pe_ref[0, s, j:j + 1, :]).astype(BF16)
    _compress_mlp(rows, w1_ref, w2_ref, o_ref, nblk)


PAIR_TOKENS = 2 * PAGE_SIZE
PAIR_BLOCKS = PAIR_TOKENS // CMP_BLOCK


def _compress_pages_kernel(raw_ref, pe_ref, perm_ref, w1_ref, w2_ref, o_ref, k_ref, v_ref, *, nblk):
    bufs = (k_ref, v_ref)

    def body(q, carry):
        for s in range(2):
            ch = slice(s * LANES, (s + 1) * LANES)
            x = jnp.concatenate([raw_ref[0, 2 * q, ch, :] + pe_ref[0, ch, :],
                                 raw_ref[0, 2 * q + 1, ch, :] + pe_ref[0, ch, :]], axis=1).astype(BF16)
            t = lax.dot_general(perm_ref[...], x, _NT, preferred_element_type=F32)
            rows = pl.ds(pl.multiple_of(q * PAIR_BLOCKS, PAIR_BLOCKS), PAIR_BLOCKS)
            bufs[s][:, rows, :] = t.reshape(CMP_BLOCK, PAIR_BLOCKS, LANES)
        return carry

    lax.fori_loop(0, raw_ref.shape[1] // 2, body, 0, unroll=4)
    _compress_mlp(lambda s, j: bufs[s][j].astype(BF16), w1_ref, w2_ref, o_ref, nblk)


def _compress_pages(pages, pe_pages, w1bd, w2bd, pages_per_step):
    nl, n_pages, _, _ = pages.shape
    assert n_pages % pages_per_step == 0 and pages_per_step % 2 == 0
    nblk = pages_per_step * (PAGE_SIZE // CMP_BLOCK)
    tok = np.arange(PAIR_TOKENS)
    dest = (tok % CMP_BLOCK) * PAIR_BLOCKS + tok // CMP_BLOCK
    perm = jnp.asarray((np.arange(PAIR_TOKENS)[:, None] == dest[None, :]).astype(np.float32)).astype(BF16)
    wspec = lambda a: pl.BlockSpec((1,) + a.shape[1:], lambda l, i: (l,) + (0,) * (a.ndim - 1))
    return pl.pallas_call(
        functools.partial(_compress_pages_kernel, nblk=nblk),
        grid=(nl, n_pages // pages_per_step),
        in_specs=[pl.BlockSpec((1, pages_per_step, KV_WIDTH, PAGE_SIZE), lambda l, i: (l, i, 0, 0)),
                  wspec(pe_pages), pl.BlockSpec(perm.shape, lambda l, i: (0, 0)), wspec(w1bd), wspec(w2bd)],
        out_specs=pl.BlockSpec((1, nblk, KV_WIDTH), lambda l, i: (l, i, 0)),
        out_shape=jax.ShapeDtypeStruct((nl, n_pages * (PAGE_SIZE // CMP_BLOCK), KV_WIDTH), F32),
        scratch_shapes=[pltpu.VMEM((CMP_BLOCK, nblk, LANES), F32)] * 2,
        compiler_params=_cparams(("arbitrary", "arbitrary")),
        name="compress_pages",
    )(pages, pe_pages, perm, w1bd, w2bd)


def _compress(k_raw, v_raw, pe_rows, w1bd, w2bd, nblk, layer):
    r = k_raw.shape[0]
    assert r % (nblk * CMP_BLOCK) == 0
    wspec = lambda a: pl.BlockSpec((1,) + a.shape[1:], lambda i: (layer,) + (0,) * (a.ndim - 1))
    rows = pl.BlockSpec((nblk * CMP_BLOCK, LANES), lambda i: (i, 0))
    return pl.pallas_call(
        functools.partial(_compress_kernel, nblk=nblk),
        grid=(r // (nblk * CMP_BLOCK),),
        in_specs=[rows, rows, wspec(pe_rows), wspec(w1bd), wspec(w2bd)],
        out_specs=pl.BlockSpec((1, nblk, KV_WIDTH), lambda i: (0, i, 0)),
        out_shape=jax.ShapeDtypeStruct((1, r // CMP_BLOCK, KV_WIDTH), F32),
        compiler_params=_cparams(("arbitrary",)),
        name="compress",
    )(k_raw, v_raw, pe_rows, w1bd, w2bd)[0]


def _prep_compress_weights(cmp_pe, cmp_w1, cmp_w2):
    nl = cmp_pe.shape[0]
    pe_rows = jnp.concatenate([cmp_pe] * N_KV, axis=-1)
    w1 = cmp_w1.reshape(nl, 2, CMP_BLOCK, HEAD_DIM, CMP_HIDDEN)
    eye = jnp.eye(N_KV, dtype=F32)
    w1bd = w1[:, :, :, None, :, None, :] * eye[None, None, None, :, None, :, None]
    w2bd = cmp_w2[:, :, None, :, None, :] * eye[None, None, :, None, :, None]
    return (pe_rows, w1bd.reshape(nl, 2, CMP_BLOCK // 2, 2 * LANES, N_KV * CMP_HIDDEN).astype(BF16),
            w2bd.reshape(nl, 2, N_KV * CMP_HIDDEN, LANES).astype(BF16))


def _np_bucket(dist):
    n = np.maximum(dist, 0)
    nf = np.maximum(n, 1).astype(np.float32)
    large = 16 + (np.log(nf / np.float32(16)) / np.float32(math.log(REL_MAX_DIST / 16)) * np.float32(16)).astype(np.int32)
    return np.where(n < 16, n, np.minimum(large, N_BUCKETS - 1)).astype(np.int32)


def _bucket_values(tbt, dist):
    hit = jnp.asarray(_np_bucket(dist))[None, ..., None] == jnp.arange(N_BUCKETS, dtype=jnp.int32)
    return jnp.sum(jnp.where(hit, tbt.reshape((tbt.shape[0],) + (1,) * dist.ndim + (N_BUCKETS,)), 0.0), axis=-1)


def _bias_minus_far(rel_bias, dist):
    tbt = rel_bias.astype(F32).T
    val = _bucket_values(tbt, dist) - tbt[:, N_BUCKETS - 1].reshape((-1,) + (1,) * dist.ndim)
    return jnp.where(jnp.asarray(dist >= 0), val, NEG)


def _split3(x):
    hi = x.astype(BF16)
    r1 = x - hi.astype(F32)
    mid = r1.astype(BF16)
    lo = (r1 - mid.astype(F32)).astype(BF16)
    return hi, mid, lo


def _prompt_bias_tables(rel_bias):
    a = np.arange(TQ)[:, None]
    near = _bias_minus_far(rel_bias, np.arange(TQ)[None, :] + TK - np.arange(2 * TK)[:, None])
    dist_c = a + (4 * CMP_BLOCK - CMP_BLOCK + 1) - CMP_BLOCK * np.arange(NEAR_BLOCKS)[None, :]
    dc = _bias_minus_far(rel_bias, dist_c)
    hi, mid, lo = _split3(dc)
    cols = jnp.stack([hi, mid, lo], axis=-1).reshape(N_HEADS, TQ, 3 * NEAR_BLOCKS)
    future = jnp.full((N_HEADS, TQ, 1), NEG, F32).astype(BF16)
    pad = jnp.zeros((N_HEADS, TQ, LANES - 3 * NEAR_BLOCKS - 1), BF16)
    dtab = jnp.concatenate([cols, future, pad], axis=-1).reshape(N_HEADS * TQ, LANES)
    lane = np.arange(LANES)
    lane_m = np.where(lane < 3 * NEAR_BLOCKS, lane // 3, -1000).astype(np.int32)
    return near, dtab, jnp.asarray(np.tile(lane_m[None, :], (8, 1)))


_NT = (((1,), (1,)), ((), ()))


def _select_blocks(imp_t, cur, unroll=False):
    blk = lax.broadcasted_iota(jnp.int32, imp_t.shape, 0)
    forced = (blk == 0) | (blk == cur) | (blk == cur - 1)
    vals = jnp.where(forced, FORCED_SCORE, jnp.where(blk <= cur, imp_t, -1.0))

    def body(_, carry):
        vals, neg = carry
        mx = jnp.max(vals, axis=0, keepdims=True)
        first = jnp.min(jnp.where(vals == mx, blk, 1 << 20), axis=0, keepdims=True)
        pick = blk == first
        return jnp.where(pick, -3e38, vals), jnp.where(pick, 0.0, neg)

    _, neg = lax.fori_loop(0, N_SEL, body, (vals, jnp.full(imp_t.shape, NEG, F32)), unroll=unroll)
    return neg


def _attn_kernel(qq_ref, kc_ref, vct_ref, dtab_ref, lanem_ref, ksel_ref, vselt_ref, kwin_ref, vwint_ref,
                 xt_ref, near_ref, gatet_ref, o_ref, qs_ref, s_ref, s2_ref, p_ref, m_ref, l_ref, alpha_ref, acc_ref, out_ref,
                 *, tiles_per_batch):
    u = pl.program_id(0) % tiles_per_batch
    nc = kc_ref.shape[0]
    ns = nc // 2
    gcols = REP * TQ

    def gate(h, br):
        c = h * 3 + br
        return gatet_ref[c:c + 1, :]

    def gate_row(g, br):
        return jnp.concatenate([gate(g * REP + r, br) for r in range(REP)], axis=1)

    def group_rows(h):
        g = h // REP
        return slice(g * HEAD_DIM, (g + 1) * HEAD_DIM)

    def head_cols(h):
        return slice((h % REP) * TQ, (h % REP + 1) * TQ)

    c = lax.broadcasted_iota(jnp.int32, (nc, LANES), 0)
    lane = lax.broadcasted_iota(jnp.int32, (nc, LANES), 1)
    rel = jnp.where(c < ns, 2 * c, 2 * c - (nc - 1)) - (8 * u - 4)
    near_hit = jnp.where(rel == lanem_ref[0:1, :], 1.0, 0.0)
    future_hit = jnp.where(rel >= NEAR_BLOCKS, 1.0, 0.0)
    onehot = jnp.where(lane < 3 * NEAR_BLOCKS, near_hit, jnp.where(lane == 3 * NEAR_BLOCKS, future_hit, 0.0))
    kk_c = jnp.concatenate([kc_ref[...], onehot.astype(BF16)], axis=1)
    imp = [jnp.zeros((ns, TQ), F32) for _ in range(N_KV)]
    for h in range(N_HEADS):
        qh = jnp.concatenate([qq_ref[h], dtab_ref[h * TQ:(h + 1) * TQ, :]], axis=1)
        s = lax.dot_general(kk_c, qh, _NT, preferred_element_type=F32)
        mx = jnp.max(s, axis=0, keepdims=True)
        p = jnp.where(s > 0.1 * NEG, jnp.exp(s - mx), 0.0)
        pn = p / jnp.maximum(jnp.sum(p, axis=0, keepdims=True), 1e-30)
        imp[h // REP] = imp[h // REP] + (pn[:ns, :] + pn[ns:, :])
        out_ref[h // REP, :, head_cols(h)] = gate(h, 0) * jnp.dot(vct_ref[group_rows(h), :], pn.astype(BF16),
                                                                  preferred_element_type=F32)

    imp_t = jnp.concatenate(imp, axis=1)
    a = lax.broadcasted_iota(jnp.int32, imp_t.shape, 1) & (TQ - 1)
    cur = (TQ // SEL_BLOCK) * u + (a >> 6)
    neg = _select_blocks(imp_t, cur).T.astype(BF16)
    for h in range(N_HEADS):
        g = h // REP
        qs_ref[h * TQ:(h + 1) * TQ, :] = jnp.concatenate([qq_ref[h], neg[g * TQ:(g + 1) * TQ, :]], axis=1)

    def reset():
        m_ref[...] = jnp.full(m_ref.shape, -1e38, F32)
        l_ref[...] = jnp.zeros(l_ref.shape, F32)
        acc_ref[...] = jnp.zeros(acc_ref.shape, F32)

    def finish(br):
        for g in range(N_KV):
            cols = slice(g * gcols, (g + 1) * gcols)
            out_ref[g] = out_ref[g] + gate_row(g, br) * (acc_ref[g] / l_ref[:, cols])

    def tile_keys(kt):
        return pl.ds(pl.multiple_of(kt * TK, TK), TK)

    def softmax_tile(buf, vt_ref, kt, bias):
        keys = tile_keys(kt)
        for g in range(N_KV):
            for c0 in range(g * gcols, (g + 1) * gcols, LANES):
                cols = slice(c0, c0 + LANES)
                s = bias(c0 // TQ, c0 % TQ, buf[:, cols])
                m_prev = m_ref[:, cols]
                m_new = jnp.maximum(m_prev, jnp.max(s, axis=0, keepdims=True))
                alpha = jnp.exp(m_prev - m_new)
                p = jnp.exp(s - m_new)
                l_ref[:, cols] = alpha * l_ref[:, cols] + jnp.sum(p, axis=0, keepdims=True)
                p_ref[:, cols] = p.astype(BF16)
                alpha_ref[:, cols] = alpha
                m_ref[:, cols] = m_new
            cols = slice(g * gcols, (g + 1) * gcols)
            pv = jnp.dot(vt_ref[g * HEAD_DIM:(g + 1) * HEAD_DIM, keys], p_ref[:, cols], preferred_element_type=F32)
            acc_ref[g] = alpha_ref[:, cols] * acc_ref[g] + pv

    def near_bias(row):
        if row is None:
            return lambda h, q0, s: s
        return lambda h, q0, s: s + near_ref[h, row:row + TK, q0:q0 + LANES]

    reset()

    def sel_scores(kt, buf):
        keys = tile_keys(kt)
        kk = jnp.concatenate([ksel_ref[keys, :], xt_ref[keys, :]], axis=1)
        buf[...] = lax.dot_general(kk, qs_ref[...], _NT, preferred_element_type=F32)

    def stage(pred, cur_buf, cur_tile, bias, next_buf=None, next_tile=None):
        @pl.when(pred)
        def _():
            if next_buf is not None:
                sel_scores(next_tile, next_buf)
            softmax_tile(cur_buf, vselt_ref, cur_tile, bias)

    far, near, diag = near_bias(None), near_bias(0), near_bias(TK)
    sel_scores(0, s_ref)

    def far_pair(i, carry):
        in_range = 2 * i + 1 < u
        stage(in_range, s_ref, 2 * i, far, s2_ref, 2 * i + 1)
        stage(in_range, s2_ref, 2 * i + 1, far, s_ref, 2 * i + 2)
        return carry

    lax.fori_loop(0, (u - 1) // 2, far_pair, 0)

    stage(u == 0, s_ref, 0, diag)
    odd = u % 2 == 1
    stage(odd, s_ref, u - 1, near, s2_ref, u)
    stage(odd, s2_ref, u, diag)
    even = (u >= 2) & (u % 2 == 0)
    stage(even, s_ref, u - 2, far, s2_ref, u - 1)
    stage(even, s2_ref, u - 1, near, s_ref, u)
    stage(even, s_ref, u, diag)
    finish(1)

    reset()

    def later_keys_only(h, q0, s):
        kj = lax.broadcasted_iota(jnp.int32, s.shape, 0)
        qa = lax.broadcasted_iota(jnp.int32, s.shape, 1) + q0
        return jnp.where(kj > qa, s, NEG)

    def win_tile(kt, bias):
        s_ref[...] = lax.dot_general(kwin_ref[tile_keys(kt), :], qs_ref[:, :LANES], _NT, preferred_element_type=F32)
        softmax_tile(s_ref, vwint_ref, kt, bias)

    @pl.when(u >= 2)
    def _():
        win_tile(u - 2, later_keys_only)

    @pl.when(u >= 1)
    def _():
        win_tile(u - 1, near)

    win_tile(u, diag)
    finish(2)

    for i in range(N_HEADS // 2):
        pair = jnp.concatenate([out_ref[(2 * i) // REP, :, head_cols(2 * i)],
                                out_ref[(2 * i + 1) // REP, :, head_cols(2 * i + 1)]], axis=0)
        o_ref[:, i * LANES:(i + 1) * LANES] = pair.T.astype(BF16)


def _attention_prompt(qq, kc, vct, ksel, vselt, kwin, vwint, gatet, tables, batch, seq):
    near, dtab, lane_m, xt = tables
    n = batch * seq
    tpb = seq // TQ
    nc = seq // CMP_BLOCK
    per_batch = lambda rows: pl.BlockSpec((rows, LANES), lambda i: (i // tpb, 0))
    chan = lambda cols: pl.BlockSpec((None, LANES, cols), lambda i: (i // tpb, 0, 0))
    full = lambda a: pl.BlockSpec(a.shape, lambda i: (0,) * a.ndim)
    return pl.pallas_call(
        functools.partial(_attn_kernel, tiles_per_batch=tpb),
        grid=(n // TQ,),
        in_specs=[pl.BlockSpec((N_HEADS, TQ, LANES), lambda i: (0, i, 0)),
                  per_batch(nc), chan(nc), full(dtab), full(lane_m),
                  per_batch(seq), chan(seq), per_batch(seq), chan(seq),
                  full(xt), full(near), pl.BlockSpec((LANES, TQ), lambda i: (0, i))],
        out_specs=pl.BlockSpec((TQ, ATT_WIDTH), lambda i: (i, 0)),
        out_shape=jax.ShapeDtypeStruct((n, ATT_WIDTH), BF16),
        scratch_shapes=[pltpu.VMEM((N_HEADS * TQ, 2 * LANES), BF16),
                        pltpu.VMEM((TK, N_HEADS * TQ), F32), pltpu.VMEM((TK, N_HEADS * TQ), F32),
                        pltpu.VMEM((TK, N_HEADS * TQ), BF16),
                        pltpu.VMEM((1, N_HEADS * TQ), F32), pltpu.VMEM((1, N_HEADS * TQ), F32),
                        pltpu.VMEM((1, N_HEADS * TQ), F32),
                        pltpu.VMEM((N_KV, HEAD_DIM, REP * TQ), F32), pltpu.VMEM((N_KV, HEAD_DIM, REP * TQ), F32)],
        compiler_params=_cparams(("arbitrary",)),
        name="nsa_prompt",
    )(qq, kc, vct, dtab, lane_m, ksel, vselt, kwin, vwint, xt, near, gatet)


def _block_onehot_rows(seq):
    j = np.arange(seq)[:, None] // SEL_BLOCK
    return jnp.asarray((j == np.arange(LANES)[None, :]).astype(np.float32)).astype(BF16)


def _even_odd(x, nc):
    x = x.reshape(-1, nc // 2, 2, x.shape[-1])
    return jnp.concatenate([x[:, :, 0], x[:, :, 1]], axis=1).reshape(-1, x.shape[-1])


def _sample_bias_tables(rel_bias, past_len, t_new, n_win):
    t = np.arange(t_new)[:, None]
    nc = past_len // CMP_BLOCK
    tb_full = lambda dist, ok: jnp.where(jnp.asarray(ok), _bucket_values(rel_bias.astype(F32).T, dist), NEG)
    dist_c = past_len + t - (CMP_BLOCK * np.arange(nc)[None, :] + CMP_BLOCK - 1)
    dist_s = past_len + t - np.arange(past_len)[None, :]
    jn = np.arange(LANES)[None, :]
    dist_n = t - jn
    dist_w = n_win + t - np.arange(n_win)[None, :]
    rows = lambda x: x.reshape(N_HEADS * t_new, x.shape[-1])
    return (rows(tb_full(dist_c, dist_c >= 0)), rows(tb_full(dist_s, dist_s >= 0)),
            rows(tb_full(dist_n, (dist_n >= 0) & (jn < t_new))),
            rows(tb_full(dist_w, (dist_w >= 0) & (dist_w < WINDOW))))


def _attn_sample_kernel(pt_ref, *refs, n_pages, t_new):
    del pt_ref
    kvc_refs = refs[:n_pages]
    sel_refs = refs[n_pages:2 * n_pages]
    (qq_ref, seln_ref, winn_ref, winb_ref, gate_ref, xt_ref, xtn_ref, bc_ref, bs_ref, bn_ref, bw_ref,
     rsum_ref, pair_ref, o_ref, qs_ref, out_ref, s_ref, p_ref) = refs[2 * n_pages:]

    def gate_col(br):
        return jnp.concatenate([gate_ref[:, h * 3 + br:h * 3 + br + 1] for h in range(N_HEADS)], axis=0)

    def softmax_tiles(q, tiles):
        off = 0
        for k, _, bias, channel_major, width in tiles:
            if channel_major:
                s = jnp.dot(q, k(), preferred_element_type=F32)
            else:
                s = lax.dot_general(q, k(), _NT, preferred_element_type=F32)
            s_ref[:, off:off + width] = s + bias()
            off += width
        s = s_ref[:, :off]
        p = jnp.exp(s - jnp.max(s, axis=1, keepdims=True))
        den = jnp.sum(p, axis=1, keepdims=True)
        p_ref[:, :off] = p.astype(BF16)
        acc, off = None, 0
        for _, v, _, channel_major, width in tiles:
            p_t = p_ref[:, off:off + width]
            if channel_major:
                pv = lax.dot_general(p_t, v(), _NT, preferred_element_type=F32)
            else:
                pv = jnp.dot(p_t, v(), preferred_element_type=F32)
            acc = pv if acc is None else acc + pv
            off += width
        return acc / den

    qs_ref[:, :LANES] = qq_ref[...]

    kvc = jnp.concatenate([r[...] for r in kvc_refs], axis=0)
    s = lax.dot_general(qq_ref[...], kvc[:, :LANES].astype(BF16), _NT, preferred_element_type=F32) + bc_ref[...]
    mx = jnp.max(s, axis=1, keepdims=True)
    p = jnp.where(s > 0.1 * NEG, jnp.exp(s - mx), 0.0)
    pn = p / jnp.maximum(jnp.sum(p, axis=1, keepdims=True), 1e-30)
    out_ref[...] = gate_col(0) * jnp.dot(pn.astype(BF16), kvc[:, LANES:].astype(BF16), preferred_element_type=F32)

    hp = lax.Precision.HIGHEST
    imp = jnp.dot(jnp.dot(rsum_ref[...], pn, precision=hp, preferred_element_type=F32), pair_ref[...],
                  precision=hp, preferred_element_type=F32)
    cur = (n_pages * PAGE_SIZE) // SEL_BLOCK
    neg = _select_blocks(imp.T, cur, unroll=True).T
    qs_ref[:, LANES:] = jnp.concatenate(
        [neg[(h // REP) * t_new:(h // REP + 1) * t_new, :] for h in range(N_HEADS)], axis=0).astype(BF16)

    def past_tile(ref, j):
        cols = slice(j * PAGE_SIZE, (j + 1) * PAGE_SIZE)
        return (lambda: jnp.concatenate([ref[:LANES, :].astype(BF16), xt_ref[:, cols]], axis=0),
                lambda: ref[LANES:, :].astype(BF16), lambda: bs_ref[:, cols], True, PAGE_SIZE)

    new_sel_tile = (lambda: jnp.concatenate([seln_ref[:, :LANES].astype(BF16), xtn_ref[...]], axis=1),
                    lambda: seln_ref[:, LANES:].astype(BF16), lambda: bn_ref[...], False, PAGE_SIZE)
    tiles = [past_tile(sel_refs[j], j) for j in range(n_pages)] + [new_sel_tile]
    out_ref[...] += gate_col(1) * softmax_tiles(qs_ref[...], tiles)

    tiles = [(lambda: winb_ref[:LANES, :].astype(BF16), lambda: winb_ref[LANES:, :].astype(BF16), lambda: bw_ref[...],
              True, winb_ref.shape[1]),
             (lambda: winn_ref[:, :LANES].astype(BF16), lambda: winn_ref[:, LANES:].astype(BF16), lambda: bn_ref[...],
              False, PAGE_SIZE)]
    out_ref[...] += gate_col(2) * softmax_tiles(qq_ref[...], tiles)

    lane_o = lax.broadcasted_iota(jnp.int32, (t_new, LANES), 1)
    for i in range(N_HEADS // 2):
        left = out_ref[2 * i * t_new:(2 * i + 1) * t_new, :]
        right = out_ref[(2 * i + 1) * t_new:(2 * i + 2) * t_new, :]
        if (2 * i) // REP == 0:
            right = pltpu.roll(right, HEAD_DIM, 1)
        else:
            left = pltpu.roll(left, HEAD_DIM, 1)
        o_ref[:, i * LANES:(i + 1) * LANES] = jnp.where(lane_o < HEAD_DIM, left, right)


def _attention_sample(page_table, kvc_pages, sel_cache, layer, qq, sel_new, win_new, win_buf, gates, tables, t_new):
    n_seq, n_pages = page_table.shape
    xt, xtn, bc, bs, bn, bw, rsum, pair = tables
    n_win = win_buf.shape[3]
    nq = N_HEADS * t_new
    page_spec = lambda shape, j: pl.BlockSpec((None, None) + shape, lambda b, pt, j=j: (layer, pt[b, j], 0, 0))
    full = lambda a: pl.BlockSpec(a.shape, lambda b, pt: (0,) * a.ndim)
    new = pl.BlockSpec((None, PAGE_SIZE, KV_WIDTH), lambda b, pt: (b, 0, 0))
    in_specs = ([page_spec((PAGE_SIZE // CMP_BLOCK, KV_WIDTH), j) for j in range(n_pages)]
                + [page_spec((KV_WIDTH, PAGE_SIZE), j) for j in range(n_pages)]
                + [pl.BlockSpec((None, nq, LANES), lambda b, pt: (b, 0, 0)), new, new,
                   pl.BlockSpec((None, None, KV_WIDTH, n_win), lambda b, pt: (layer, b, 0, 0)),
                   pl.BlockSpec((t_new, LANES), lambda b, pt: (b, 0))]
                + [full(a) for a in (xt, xtn, bc, bs, bn, bw, rsum, pair)])
    return pl.pallas_call(
        functools.partial(_attn_sample_kernel, n_pages=n_pages, t_new=t_new),
        grid_spec=pltpu.PrefetchScalarGridSpec(
            num_scalar_prefetch=1, grid=(n_seq,), in_specs=in_specs,
            out_specs=pl.BlockSpec((t_new, ATT_WIDTH), lambda b, pt: (b, 0)),
            scratch_shapes=[pltpu.VMEM((nq, 2 * LANES), BF16), pltpu.VMEM((nq, LANES), F32),
                            pltpu.VMEM((nq, (n_pages + 1) * PAGE_SIZE), F32),
                            pltpu.VMEM((nq, (n_pages + 1) * PAGE_SIZE), BF16)]),
        out_shape=jax.ShapeDtypeStruct((n_seq * t_new, ATT_WIDTH), F32),
        compiler_params=_cparams(("arbitrary",)),
        name="nsa_sample",
    )(page_table, *([kvc_pages] * n_pages), *([sel_cache] * n_pages), qq, sel_new, win_new, win_buf, gates,
      xt, xtn, bc, bs, bn, bw, rsum, pair)


def _sample_sum_matrices(t_new, nc):
    rsum = np.zeros((LANES, N_HEADS * t_new), np.float32)
    for h in range(N_HEADS):
        for t in range(t_new):
            rsum[(h // REP) * t_new + t, h * t_new + t] = 1.0
    pair = np.zeros((nc, LANES), np.float32)
    pair[np.arange(nc), np.arange(nc) // 2] = 1.0
    return jnp.asarray(rsum), jnp.asarray(pair)


def _rank_before(vals, k):
    r = jnp.zeros(vals[k].shape, jnp.int32)
    for j, vj in enumerate(vals):
        if j < k:
            r = r + jnp.where(vj >= vals[k], 1, 0)
        elif j > k:
            r = r + jnp.where(vj > vals[k], 1, 0)
    return r


def _route(s_rows, b_rows):
    scores = []
    for g in range(N_GROUPS):
        b0, b1, b2, b3 = b_rows[4 * g:4 * g + 4]
        hi01, lo01, hi23, lo23 = jnp.maximum(b0, b1), jnp.minimum(b0, b1), jnp.maximum(b2, b3), jnp.minimum(b2, b3)
        top1 = jnp.maximum(hi01, hi23)
        top2 = jnp.maximum(jnp.maximum(lo01, lo23), jnp.minimum(hi01, hi23))
        scores.append(top1 + top2)
    in_group = [_rank_before(scores, g) == 0 for g in range(N_GROUPS)]

    def pick(rows, k):
        out = rows[4 * (N_GROUPS - 1) + k]
        for g in range(N_GROUPS - 2, -1, -1):
            out = jnp.where(in_group[g], rows[4 * g + k], out)
        return out

    bv = [pick(b_rows, k) for k in range(EXPERTS_PER_GROUP)]
    sv = [pick(s_rows, k) for k in range(EXPERTS_PER_GROUP)]
    w = [jnp.where(_rank_before(bv, k) < 2, sv[k], 0.0) for k in range(EXPERTS_PER_GROUP)]
    den = (w[0] + w[1]) + (w[2] + w[3])
    return [jnp.where(in_group[e // 4], w[e % 4] / den, 0.0) for e in range(N_EXPERTS)]


def _outproj_kernel(ps_ref, att_ref, x_ref, mod_ref, wout_ref, lng_ref, lnb_ref, rwt_ref, rb_ref,
                    x1_ref, h2_ref, g_ref):
    half = ps_ref.shape[1]
    mix = (jnp.dot(ps_ref[...], wout_ref[:half, :], preferred_element_type=F32)
           + jnp.dot(att_ref[...], wout_ref[half:, :], preferred_element_type=F32))
    x1 = _layer_norm(ALPHA * x_ref[...] + (1.0 + mod_ref[2]) * mix, lng_ref[...], lnb_ref[...])
    x1_ref[...] = x1
    h2 = x1 * (1.0 + mod_ref[4]) + mod_ref[3]
    h2_ref[...] = h2.astype(BF16)
    st = jax.nn.sigmoid(lax.dot_general(rwt_ref[...], h2, _NT, precision=lax.Precision.HIGHEST,
                                        preferred_element_type=F32))
    s_rows = [st[e:e + 1, :] for e in range(N_EXPERTS)]
    b_rows = [s_rows[e] + rb_ref[e:e + 1, :] for e in range(N_EXPERTS)]
    gt = jnp.concatenate(_route(s_rows, b_rows) + [jnp.zeros((LANES - N_EXPERTS, st.shape[1]), F32)], axis=0)
    g_ref[...] = gt.T


def _out_proj(ps, att, x, mod, wout, ln_g, ln_b, rwt, rb, tm, tiles_per_batch):
    n = x.shape[0]
    row = lambda wd: pl.BlockSpec((tm, wd), lambda i: (i, 0))
    full = lambda a: pl.BlockSpec(a.shape, lambda i: (0,) * a.ndim)
    return pl.pallas_call(
        _outproj_kernel,
        grid=(n // tm,),
        in_specs=[row(512), row(512), row(D_MODEL), _mod_spec(mod, tm, tiles_per_batch),
                  full(wout), full(ln_g), full(ln_b), full(rwt), full(rb)],
        out_specs=[row(D_MODEL), row(D_MODEL), row(LANES)],
        out_shape=[jax.ShapeDtypeStruct((n, D_MODEL), F32), jax.ShapeDtypeStruct((n, D_MODEL), BF16),
                   jax.ShapeDtypeStruct((n, LANES), F32)],
        compiler_params=_cparams(("arbitrary",)),
        name="out_proj",
    )(ps, att, x, mod, wout, ln_g, ln_b, rwt, rb)


def _moe_kernel(h2_ref, g_ref, wg_ref, wu_ref, wd_ref, x1_ref, mod_ref, lng_ref, lnb_ref, o_ref, acc_ref):
    e = pl.program_id(1)

    @pl.when(e == 0)
    def _():
        acc_ref[...] = jnp.zeros_like(acc_ref)

    x = h2_ref[...]
    gate = jnp.dot(x, wg_ref[0, 0], preferred_element_type=F32)
    up = jnp.dot(x, wu_ref[0, 0], preferred_element_type=F32)
    act = (gate * jax.nn.sigmoid(gate) * up).astype(BF16)
    y = jnp.dot(act, wd_ref[0, 0], preferred_element_type=F32)
    lane = lax.broadcasted_iota(jnp.int32, g_ref.shape, 1)
    w = jnp.sum(jnp.where(lane == e, g_ref[...], 0.0), axis=1, keepdims=True)
    acc_ref[...] += y * w

    @pl.when(e == pl.num_programs(1) - 1)
    def _():
        o_ref[...] = _layer_norm(ALPHA * x1_ref[...] + (1.0 + mod_ref[5]) * acc_ref[...], lng_ref[...], lnb_ref[...])


def _moe(h2, gates, wg, wu, wd, layer, x1, mod, ln_g, ln_b, tm, tiles_per_batch):
    n = h2.shape[0]
    row = lambda wd_: pl.BlockSpec((tm, wd_), lambda i, e: (i, 0))
    full = lambda a: pl.BlockSpec(a.shape, lambda i, e: (0,) * a.ndim)
    wspec = lambda a: pl.BlockSpec((1, 1) + a.shape[2:], lambda i, e: (layer, e, 0, 0))
    return pl.pallas_call(
        _moe_kernel,
        grid=(n // tm, N_EXPERTS),
        in_specs=[row(D_MODEL), row(LANES), wspec(wg), wspec(wu), wspec(wd), row(D_MODEL),
                  _mod_spec(mod, tm, tiles_per_batch), full(ln_g), full(ln_b)],
        out_specs=row(D_MODEL),
        out_shape=jax.ShapeDtypeStruct((n, D_MODEL), F32),
        scratch_shapes=[pltpu.VMEM((tm, D_MODEL), F32)],
        compiler_params=_cparams(("arbitrary", "arbitrary")),
        name="moe",
    )(h2, gates, wg, wu, wd, x1, mod, ln_g, ln_b)


TM_PROJ = 256
TM_MOE = 512


def _channel_mix(ps, att, x, mod, lw, sw, layer, tiles_per_seq):
    x1, h2, gts = _out_proj(ps, att, x, mod, lw["wout"], lw["ln_g0"], lw["ln_b0"], sw["rwt"], sw["rb"],
                            TM_PROJ, tiles_per_seq(TM_PROJ))
    return _moe(h2, gts, sw["wg"], sw["wu"], sw["wd"], layer, x1, mod, lw["ln_g1"], lw["ln_b1"],
                TM_MOE, tiles_per_seq(TM_MOE))


def _prompt_layer(x, mod, lw, sw, layer, batch, seq):
    n = batch * seq
    tiles = lambda tm: seq // tm
    (p, u, vn, qq, cmp_k, cmp_v, cmp_t, sel_t, win_t, gate_t, ksel, vsel_t, kwin, vwin_t) = _in_proj(
        x, mod, lw["w_proj"], lw["sgu_ln_g"], lw["sgu_ln_b"], TM_PROJ, tiles(TM_PROJ), batch=batch)
    ps = _mixers(p, u, vn, *lw["mix_prompt"], tiles_per_batch=tiles(SGU_CHUNK))
    nc = seq // CMP_BLOCK
    kvc = _compress(cmp_k, cmp_v, sw["pe_rows"], sw["w1bd"], sw["w2bd"], nc, layer)
    kc = _even_odd(kvc[:, :LANES].astype(BF16), nc)
    vc_t = jnp.transpose(_even_odd(kvc[:, LANES:].astype(BF16), nc).reshape(batch, nc, LANES), (0, 2, 1))
    att = _attention_prompt(qq, kc, vc_t, ksel, vsel_t, kwin, vwin_t, gate_t, sw["prompt_tables"], batch, seq)
    x2 = _channel_mix(ps, att, x, mod, lw, sw, layer, tiles)
    return x2, (cmp_t, sel_t, win_t, p)


def _sample_layer(x, mod, lw, sw, layer, page_table, kvc_pages, sel_cache, win_state, pool_state, n_seq, t_new):
    tiles = lambda tm: 1
    (p, u, vn, qq, cmp_raw, sel_raw, win_raw, gates) = _in_proj(
        x, mod, lw["w_proj"], lw["sgu_ln_g"], lw["sgu_ln_b"], TM_PROJ, 1)
    p_ext = jnp.concatenate([pool_state[layer], p.reshape(n_seq, t_new, POOL_WIDTH)], axis=1)
    hist = jnp.pad(p_ext, ((0, 0), (3 * t_new - p_ext.shape[1], 0), (0, 0)))
    chunks = [hist[:, k * t_new:(k + 1) * t_new].reshape(n_seq * t_new, POOL_WIDTH) for k in range(3)]
    ps = _mixers(chunks, u, vn, *lw["mix_sample"], t_new=t_new)
    q_rows = jnp.transpose(qq.reshape(N_HEADS, n_seq, t_new, LANES), (1, 0, 2, 3)).reshape(n_seq, N_HEADS * t_new, LANES)
    new_page = lambda raw: jnp.pad(raw.reshape(n_seq, t_new, KV_WIDTH), ((0, 0), (0, PAGE_SIZE - t_new), (0, 0)))
    att = _attention_sample(page_table, kvc_pages, sel_cache, layer, q_rows, new_page(sel_raw), new_page(win_raw),
                            win_state, gates, sw["sample_tables"], t_new)
    x2 = _channel_mix(ps, att.astype(BF16), x, mod, lw, sw, layer, tiles)
    win_new_t = jnp.transpose(win_raw.reshape(n_seq, t_new, KV_WIDTH), (0, 2, 1))
    win_all_t = jnp.concatenate([win_state[layer][:, :, t_new:], win_new_t], axis=2)
    return x2, (cmp_raw, sel_raw, win_all_t, p_ext[:, p_ext.shape[1] - POOL_BUF:], vn)


def kernel(x_prompt, x_sample, cache_cmp_kv, cache_sel_kv, state_win_kv, state_pool, page_table, c_prompt, c_sample,
           w_in, w_out, pool_w, pool_scale, sgu_ln_g, sgu_ln_b, sgu_w, sgu_b, cmp_pe, cmp_w1, cmp_w2, rel_bias,
           w_mod, b_mod, ln_g, ln_b, router_w, router_b, moe_w_gate, moe_w_up, moe_w_down):
    batch, seq, d = x_prompt.shape
    n_seq, t_new, _ = x_sample.shape
    depth = w_in.shape[0]
    n_pages = page_table.shape[1]
    past_len = n_pages * PAGE_SIZE
    n_phys = cache_cmp_kv.shape[1]
    n_win = state_win_kv.shape[2]
    assert seq // SEL_BLOCK == LANES and seq % TM_MOE == 0 and (n_seq * t_new) % TM_MOE == 0
    assert POOL_BUF + 1 == 2 * t_new and past_len % TK == 0

    n_c = batch + n_seq
    c_all = jnp.pad(jnp.concatenate([c_prompt, c_sample], axis=0), ((0, -n_c % 8), (0, 0)))
    m_all = _modulation(c_all, w_mod, b_mod)
    mod_p = m_all[:, :batch].reshape(depth, batch, 6, 1, d)
    mod_s = jnp.transpose(jnp.repeat(m_all[:, batch:n_c].reshape(depth, n_seq, 6, d), t_new, axis=1), (0, 2, 1, 3))

    pe_rows, w1bd, w2bd = _prep_compress_weights(cmp_pe, cmp_w1, cmp_w2)
    near, dtab, lane_m = _prompt_bias_tables(rel_bias)
    nc_past = past_len // CMP_BLOCK
    shared = {
        "pe_rows": pe_rows, "w1bd": w1bd, "w2bd": w2bd,
        "prompt_tables": (near, dtab, lane_m, _block_onehot_rows(seq)),
        "sample_tables": (_block_onehot_rows(past_len + PAGE_SIZE).T, _block_onehot_rows(past_len + PAGE_SIZE)[past_len:],
                          *_sample_bias_tables(rel_bias, past_len, t_new, n_win),
                          *_sample_sum_matrices(t_new, nc_past)),
        "rwt": router_w.T, "rb": router_b.reshape(N_EXPERTS, 1),
        "wg": moe_w_gate.astype(BF16), "wu": moe_w_up.astype(BF16), "wd": moe_w_down.astype(BF16),
    }
    layers = []
    for l in range(depth):
        layers.append({
            "w_proj": _prep_w_proj(w_in[l]), "wout": w_out[l].astype(BF16),
            "sgu_ln_g": sgu_ln_g[l].reshape(1, -1), "sgu_ln_b": sgu_ln_b[l].reshape(1, -1),
            "mix_prompt": _prep_mixer_weights(pool_w[l], pool_scale[l], sgu_w[l], sgu_b[l], SGU_CHUNK),
            "mix_sample": _prep_mixer_weights(pool_w[l], pool_scale[l], sgu_w[l], sgu_b[l], t_new),
            "ln_g0": ln_g[l, 0].reshape(1, d), "ln_b0": ln_b[l, 0].reshape(1, d),
            "ln_g1": ln_g[l, 1].reshape(1, d), "ln_b1": ln_b[l, 1].reshape(1, d),
        })

    chan_major = lambda x: jnp.transpose(x, (0, 1, 3, 4, 5, 2)).reshape(x.shape[0], x.shape[1], KV_WIDTH, x.shape[2])
    pe_pages = jnp.broadcast_to(jnp.transpose(cmp_pe, (0, 1, 3, 2))[:, :, None, :, None, :],
                                (depth, 2, N_KV, HEAD_DIM, PAGE_SIZE // CMP_BLOCK, CMP_BLOCK))
    kvc_pages = _compress_pages(chan_major(cache_cmp_kv), pe_pages.reshape(depth, KV_WIDTH, PAGE_SIZE), w1bd, w2bd, 64)
    kvc_pages = kvc_pages.reshape(depth, n_phys, PAGE_SIZE // CMP_BLOCK, KV_WIDTH)
    sel_cache = chan_major(cache_sel_kv)
    win_state = chan_major(state_win_kv)

    xp = x_prompt.reshape(batch * seq, d)
    xs = x_sample.reshape(n_seq * t_new, d)
    outs_p, outs_s = [], []
    for l in range(depth):
        xp, st = _prompt_layer(xp, mod_p[l], layers[l], shared, l, batch, seq)
        outs_p.append(st)
        xs, st = _sample_layer(xs, mod_s[l], layers[l], shared, l, page_table, kvc_pages, sel_cache, win_state,
                               state_pool, n_seq, t_new)
        outs_s.append(st)

    kv5 = lambda x, b: x.reshape(b, -1, 2, N_KV, HEAD_DIM)
    kv5_t = lambda x: jnp.transpose(x.reshape(x.shape[0], 2, N_KV, HEAD_DIM, x.shape[2]), (0, 4, 1, 2, 3))
    stack = lambda xs_: jnp.stack(xs_)
    w_keep = min(WINDOW, seq)
    return (xp.reshape(batch, seq, d), xs.reshape(n_seq, t_new, d),
            stack([kv5_t(o[0]) for o in outs_p]), stack([kv5(o[0], n_seq) for o in outs_s]),
            stack([kv5_t(o[1]) for o in outs_p]), stack([kv5(o[1], n_seq) for o in outs_s]),
            stack([kv5_t(o[2][:, :, seq - w_keep:]) for o in outs_p]), stack([kv5_t(o[2]) for o in outs_s]),
            stack([o[3].reshape(batch, seq, POOL_WIDTH)[:, seq - POOL_BUF:] for o in outs_p]),
            stack([o[3] for o in outs_s]),
            stack([o[4].reshape(n_seq, t_new, SGU_WIDTH) for o in outs_s]))
```

```python
import functools
import math

import numpy as np
import jax
import jax.numpy as jnp
from jax import lax
from jax.experimental import pallas as pl
from jax.experimental.pallas import tpu as pltpu

F32 = jnp.float32
BF16 = jnp.bfloat16

D_MODEL = 1024
POOL_WIDTH = 256
SGU_WIDTH = 256
ATT_WIDTH = 512
POOL_WINDOWS = (2, 4, 8, 16)
POOL_GROUP_DIM = 64
POOL_BUF = 15
SGU_GROUPS = 4
SGU_CHUNK = 128
N_HEADS = 8
HEAD_DIM = 64
N_KV = 2
REP = 4
CMP_BLOCK = 32
CMP_HIDDEN = 128
SEL_BLOCK = 64
N_SEL = 16
WINDOW = 512
N_BUCKETS = 32
REL_MAX_DIST = 128
N_EXPERTS = 16
N_GROUPS = 4
EXPERTS_PER_GROUP = 4
D_EXPERT = 512
DEPTH = 2
ALPHA = (2 * DEPTH) ** 0.25
LN_EPS = 1e-5
FORCED_SCORE = 1e4
NEG = -1e30
PAGE_SIZE = 128

KV_WIDTH = 2 * N_KV * HEAD_DIM
LANES = 128
VMEM_LIMIT = 56 * 1024 * 1024

C_P, C_U, C_V, C_CMP, C_SEL, C_WIN, C_GATE, C_Q = 0, 256, 512, 768, 1024, 1280, 1536, 1664
W_PROJ = C_Q + N_HEADS * LANES

TQ = 256
TK = 256
NEAR_BLOCKS = 12
FAR_BUCKET_DIST = 113


def _cparams(sem):
    return pltpu.CompilerParams(dimension_semantics=sem, vmem_limit_bytes=VMEM_LIMIT)


def _layer_norm(x, g, b):
    mu = jnp.mean(x, axis=-1, keepdims=True)
    xc = x - mu
    var = jnp.mean(xc * xc, axis=-1, keepdims=True)
    return xc * lax.rsqrt(var + LN_EPS) * g + b


def _mod_kernel(c_ref, w_ref, b_ref, o_ref):
    c = c_ref[...]
    a = (c * jax.nn.sigmoid(c)).astype(BF16)
    o_ref[0] = jnp.dot(a, w_ref[0].astype(BF16), preferred_element_type=F32) + b_ref[0]


def _modulation(c_all, w_mod, b_mod):
    n, d = c_all.shape
    depth, _, w = w_mod.shape
    tn = 1536
    return pl.pallas_call(
        _mod_kernel,
        grid=(depth, w // tn),
        in_specs=[pl.BlockSpec((n, d), lambda l, j: (0, 0)),
                  pl.BlockSpec((1, d, tn), lambda l, j: (l, 0, j)),
                  pl.BlockSpec((1, 1, tn), lambda l, j: (l, 0, j))],
        out_specs=pl.BlockSpec((1, n, tn), lambda l, j: (l, 0, j)),
        out_shape=jax.ShapeDtypeStruct((depth, n, w), F32),
        compiler_params=_cparams(("arbitrary", "arbitrary")),
        name="adaln_mod",
    )(c_all, w_mod, b_mod.reshape(depth, 1, w))


def _mod_spec(mod, tm, tiles_per_batch):
    if mod.ndim == 4:
        return pl.BlockSpec((None, 6, 1, D_MODEL), lambda i, *_: (i // tiles_per_batch, 0, 0, 0))
    return pl.BlockSpec((6, tm, D_MODEL), lambda i, *_: (0, i, 0))


def _inproj_kernel(x_ref, mod_ref, w_ref, lng_ref, lnb_ref, *rest, channel_major):
    h = (x_ref[...] * (1.0 + mod_ref[1]) + mod_ref[0]).astype(BF16)

    def seg(a, b):
        return jnp.dot(h, w_ref[:, a:b], preferred_element_type=F32)

    if channel_major:
        wqt_ref, p_ref, u_ref, vn_ref, qt_ref = rest[:5]
        rest = rest[5:]
        qt_ref[...] = lax.dot_general(wqt_ref[...], h, _NT, preferred_element_type=F32).astype(BF16)
    else:
        p_ref, u_ref, vn_ref, qq_ref = rest[:4]
        rest = rest[4:]
        for hd in range(N_HEADS):
            qq_ref[hd] = seg(C_Q + hd * LANES, C_Q + (hd + 1) * LANES).astype(BF16)
    p_ref[...] = seg(C_P, C_U)
    u_ref[...] = seg(C_U, C_V)
    vn_ref[...] = _layer_norm(seg(C_V, C_CMP), lng_ref[...], lnb_ref[...])
    cmp = seg(C_CMP, C_SEL)
    sel = seg(C_SEL, C_WIN)
    win = seg(C_WIN, C_GATE)
    gates = jax.nn.sigmoid(seg(C_GATE, C_Q))
    if not channel_major:
        cmp_ref, sel_ref, win_ref, gate_ref = rest
        cmp_ref[...] = cmp
        sel_ref[...] = sel
        win_ref[...] = win
        gate_ref[...] = gates
        return
    cmpk_ref, cmpv_ref, cmpt_ref, selt_ref, wint_ref, gatet_ref, ksel_ref, vselt_ref, kwin_ref, vwint_ref = rest
    cmpk_ref[...] = cmp[:, :LANES]
    cmpv_ref[...] = cmp[:, LANES:]
    cmpt_ref[...] = cmp.T
    sel_t = sel.T
    selt_ref[...] = sel_t
    ksel_ref[...] = sel[:, :LANES].astype(BF16)
    vselt_ref[...] = sel_t[LANES:, :].astype(BF16)
    win_t = win.T
    wint_ref[...] = win_t
    kwin_ref[...] = win[:, :LANES].astype(BF16)
    vwint_ref[...] = win_t[LANES:, :].astype(BF16)
    gatet_ref[...] = gates.T


def _in_proj(x, mod, w_proj, ln_g, ln_b, tm, tiles_per_batch, batch=None, wq_t=None):
    n = x.shape[0]
    row = lambda wd: pl.BlockSpec((tm, wd), lambda i: (i, 0))
    full = lambda a: pl.BlockSpec(a.shape, lambda i: (0,) * a.ndim)
    f32o = lambda wd: jax.ShapeDtypeStruct((n, wd), F32)
    out_specs = [row(256), row(256), row(256)]
    out_shape = [f32o(256), f32o(256), f32o(256)]
    ins, in_specs = [x, mod, w_proj, ln_g, ln_b], [row(D_MODEL), _mod_spec(mod, tm, tiles_per_batch), full(w_proj),
                                                    full(ln_g), full(ln_b)]
    if batch is None:
        out_specs += [pl.BlockSpec((N_HEADS, tm, LANES), lambda i: (0, i, 0)), row(256), row(256), row(256), row(LANES)]
        out_shape += [jax.ShapeDtypeStruct((N_HEADS, n, LANES), BF16), f32o(256), f32o(256), f32o(256), f32o(LANES)]
    else:
        ins.append(wq_t)
        in_specs.append(full(wq_t))
        out_specs.append(pl.BlockSpec((N_HEADS * LANES, tm), lambda i: (0, i)))
        out_shape.append(jax.ShapeDtypeStruct((N_HEADS * LANES, n), BF16))
        seq = n // batch
        chan = lambda c: pl.BlockSpec((None, c, tm), lambda i: (i // tiles_per_batch, 0, i % tiles_per_batch))
        chan_o = lambda c, dt: jax.ShapeDtypeStruct((batch, c, seq), dt)
        out_specs += [row(LANES), row(LANES), chan(256), chan(256), chan(256), pl.BlockSpec((LANES, tm), lambda i: (0, i)),
                      row(LANES), chan(LANES), row(LANES), chan(LANES)]
        out_shape += [f32o(LANES), f32o(LANES),
                      chan_o(256, F32), chan_o(256, F32), chan_o(256, F32), jax.ShapeDtypeStruct((LANES, n), F32),
                      jax.ShapeDtypeStruct((n, LANES), BF16), chan_o(LANES, BF16),
                      jax.ShapeDtypeStruct((n, LANES), BF16), chan_o(LANES, BF16)]
    return pl.pallas_call(
        functools.partial(_inproj_kernel, channel_major=batch is not None),
        grid=(n // tm,),
        in_specs=in_specs,
        out_specs=out_specs,
        out_shape=out_shape,
        compiler_params=_cparams(("arbitrary",)),
        name="in_proj",
    )(*ins)


def _prep_w_proj(w_in):
    d = w_in.shape[0]
    q = w_in[:, 768:1280].reshape(d, N_KV, REP, HEAD_DIM) * (HEAD_DIM ** -0.5)
    qq = jnp.zeros((d, N_KV, REP, N_KV, HEAD_DIM), w_in.dtype)
    for g in range(N_KV):
        qq = qq.at[:, g, :, g, :].set(q[:, g])
    gate = jnp.pad(w_in[:, 2048:2072], ((0, 0), (0, LANES - 24)))
    cols = [w_in[:, 0:768], w_in[:, 1280:2048], gate, qq.reshape(d, N_HEADS * LANES)]
    return jnp.concatenate(cols, axis=1).astype(BF16)


def _window_sums(shifted):
    acc = shifted(0)
    sums = {}
    for k in range(1, 16):
        acc = acc + shifted(k)
        if k + 1 in POOL_WINDOWS:
            sums[k + 1] = acc
    lane = lax.broadcasted_iota(jnp.int32, acc.shape, 1)
    return jnp.where(lane < 64, sums[2], jnp.where(lane < 128, sums[4], jnp.where(lane < 192, sums[8], sums[16])))


def _mixers_tail(sums, cnt, cur, u, vn, wpool_ref, pscale_ref, wcat_ref, sbias_ref, o_ref):
    diff = (sums / cnt - cur).astype(BF16)
    pool = jnp.dot(diff, wpool_ref[...], preferred_element_type=F32) * pscale_ref[...]
    lane = lax.broadcasted_iota(jnp.int32, vn.shape, 1)
    vb = vn.astype(BF16)
    zero = jnp.zeros_like(vb)
    stacked = jnp.concatenate([jnp.where((lane // 64) == g, vb, zero) for g in range(SGU_GROUPS)], axis=0)
    mixed = jnp.dot(wcat_ref[...], stacked, preferred_element_type=F32) + sbias_ref[...]
    o_ref[:, :POOL_WIDTH] = pool.astype(BF16)
    o_ref[:, POOL_WIDTH:] = (u * mixed).astype(BF16)


def _lane_window():
    lane = lax.broadcasted_iota(jnp.int32, (SGU_CHUNK, POOL_WIDTH), 1)
    return jnp.where(lane < 64, 2, jnp.where(lane < 128, 4, jnp.where(lane < 192, 8, 16)))


def _mix_prompt_kernel(p_ref, u_ref, vn_ref, wpool_ref, pscale_ref, wcat_ref, sbias_ref, o_ref, prev_ref,
                       *, tiles_per_batch):
    t = pl.program_id(0) % tiles_per_batch

    @pl.when(t == 0)
    def _():
        prev_ref[...] = jnp.zeros_like(prev_ref)

    cur = p_ref[...]
    prev = prev_ref[...]
    row = lax.broadcasted_iota(jnp.int32, cur.shape, 0)

    def shifted(k):
        if k == 0:
            return cur
        return jnp.where(row >= k, pltpu.roll(cur, k, 0), pltpu.roll(prev, k, 0))

    sums = _window_sums(shifted)
    prev_ref[...] = cur
    cnt = jnp.minimum(_lane_window(), t * SGU_CHUNK + row + 1).astype(F32)
    _mixers_tail(sums, cnt, cur, u_ref[...], vn_ref[...], wpool_ref, pscale_ref, wcat_ref, sbias_ref, o_ref)


def _mix_sample_kernel(pa_ref, pb_ref, pc_ref, u_ref, vn_ref, wpool_ref, pscale_ref, wcat_ref, sbias_ref, o_ref,
                       *, t_new):
    a, b, c = pa_ref[...], pb_ref[...], pc_ref[...]
    rows = c.shape[0]
    t = lax.broadcasted_iota(jnp.int32, c.shape, 0) % t_new

    def shifted(k):
        if k == 0:
            return c
        hi, lo = (c, b) if k < t_new else (b, a)
        kk = k % t_new
        if kk == 0:
            return hi
        return jnp.where(t >= kk, pltpu.roll(hi, kk, 0), pltpu.roll(lo, rows - t_new + kk, 0))

    sums = _window_sums(shifted)
    cnt = _lane_window().astype(F32)
    _mixers_tail(sums, cnt, c, u_ref[...], vn_ref[...], wpool_ref, pscale_ref, wcat_ref, sbias_ref, o_ref)


def _mixers(p_in, u, vn, wpool, pscale, wcat, sbias, tiles_per_batch=None, t_new=None):
    n = u.shape[0]
    tm = SGU_CHUNK
    row = lambda wd: pl.BlockSpec((tm, wd), lambda i: (i, 0))
    full = lambda a: pl.BlockSpec(a.shape, lambda i: (0,) * a.ndim)
    weights = [wpool, pscale, wcat, sbias]
    if t_new is None:
        kern = functools.partial(_mix_prompt_kernel, tiles_per_batch=tiles_per_batch)
        ins, scratch = [p_in], [pltpu.VMEM((tm, POOL_WIDTH), F32)]
    else:
        kern = functools.partial(_mix_sample_kernel, t_new=t_new)
        ins, scratch = list(p_in), []
    return pl.pallas_call(
        kern,
        grid=(n // tm,),
        in_specs=[row(256)] * (len(ins) + 2) + [full(a) for a in weights],
        out_specs=row(512),
        out_shape=jax.ShapeDtypeStruct((n, 512), BF16),
        scratch_shapes=scratch,
        compiler_params=_cparams(("arbitrary",)),
        name="mixers",
    )(*ins, u, vn, *weights)


def _prep_mixer_weights(pool_w, pool_scale, sgu_w, sgu_b, chunk):
    wpool = jax.scipy.linalg.block_diag(*[pool_w[g] for g in range(len(POOL_WINDOWS))]).astype(BF16)
    tri = jnp.tril(jnp.ones((chunk, chunk), bool))
    w = jnp.where(tri, sgu_w[:, :chunk, :chunk], 0.0)
    reps = SGU_CHUNK // chunk
    eye = jnp.eye(reps, dtype=w.dtype)
    wt = jnp.concatenate([jnp.kron(eye, w[g]) for g in range(SGU_GROUPS)], axis=1).astype(BF16)
    b = jnp.tile(sgu_b[:, :chunk], (1, reps))
    sbias = jnp.repeat(b.T, SGU_WIDTH // SGU_GROUPS, axis=1)
    return wpool, pool_scale.reshape(1, POOL_WIDTH), wt, sbias


def _compress_mlp(token_rows, w1_ref, w2_ref, o_ref, nblk):
    for s in range(2):
        acc = jnp.zeros((nblk, N_KV * CMP_HIDDEN), F32)
        for j in range(0, CMP_BLOCK, 2):
            pair = jnp.concatenate([token_rows(s, j), token_rows(s, j + 1)], axis=1)
            acc = acc + jnp.dot(pair, w1_ref[0, s, j // 2], preferred_element_type=F32)
        hdn = jax.nn.gelu(acc).astype(BF16)
        o_ref[0, :, s * LANES:(s + 1) * LANES] = jnp.dot(hdn, w2_ref[0, s], preferred_element_type=F32)


def _compress_kernel(k_ref, v_ref, pe_ref, w1_ref, w2_ref, o_ref, *, nblk):
    bufs = (k_ref, v_ref)
    rows = lambda s, j: (bufs[s][pl.ds(j, nblk, stride=CMP_BLOCK), :] + pe_ref[0, s, j:j + 1, :]).astype(BF16)
    _compress_mlp(rows, w1_ref, w2_ref, o_ref, nblk)


PAIR_TOKENS = 2 * PAGE_SIZE
PAIR_BLOCKS = PAIR_TOKENS // CMP_BLOCK


def _compress_pages_kernel(raw_ref, pe_ref, perm_ref, w1_ref, w2_ref, o_ref, k_ref, v_ref, *, nblk):
    bufs = (k_ref, v_ref)

    def body(q, carry):
        for s in range(2):
            ch = slice(s * LANES, (s + 1) * LANES)
            x = jnp.concatenate([raw_ref[0, 2 * q, ch, :] + pe_ref[0, ch, :],
                                 raw_ref[0, 2 * q + 1, ch, :] + pe_ref[0, ch, :]], axis=1).astype(BF16)
            t = lax.dot_general(perm_ref[...], x, _NT, preferred_element_type=F32)
            rows = pl.ds(pl.multiple_of(q * PAIR_BLOCKS, PAIR_BLOCKS), PAIR_BLOCKS)
            bufs[s][:, rows, :] = t.reshape(CMP_BLOCK, PAIR_BLOCKS, LANES)
        return carry

    lax.fori_loop(0, raw_ref.shape[1] // 2, body, 0, unroll=4)
    _compress_mlp(lambda s, j: bufs[s][j].astype(BF16), w1_ref, w2_ref, o_ref, nblk)


def _compress_pages(pages, pe_pages, w1bd, w2bd, pages_per_step):
    nl, n_pages, _, _ = pages.shape
    assert n_pages % pages_per_step == 0 and pages_per_step % 2 == 0
    nblk = pages_per_step * (PAGE_SIZE // CMP_BLOCK)
    tok = np.arange(PAIR_TOKENS)
    dest = (tok % CMP_BLOCK) * PAIR_BLOCKS + tok // CMP_BLOCK
    perm = jnp.asarray((np.arange(PAIR_TOKENS)[:, None] == dest[None, :]).astype(np.float32)).astype(BF16)
    wspec = lambda a: pl.BlockSpec((1,) + a.shape[1:], lambda l, i: (l,) + (0,) * (a.ndim - 1))
    return pl.pallas_call(
        functools.partial(_compress_pages_kernel, nblk=nblk),
        grid=(nl, n_pages // pages_per_step),
        in_specs=[pl.BlockSpec((1, pages_per_step, KV_WIDTH, PAGE_SIZE), lambda l, i: (l, i, 0, 0)),
                  wspec(pe_pages), pl.BlockSpec(perm.shape, lambda l, i: (0, 0)), wspec(w1bd), wspec(w2bd)],
        out_specs=pl.BlockSpec((1, nblk, KV_WIDTH), lambda l, i: (l, i, 0)),
        out_shape=jax.ShapeDtypeStruct((nl, n_pages * (PAGE_SIZE // CMP_BLOCK), KV_WIDTH), F32),
        scratch_shapes=[pltpu.VMEM((CMP_BLOCK, nblk, LANES), F32)] * 2,
        compiler_params=_cparams(("arbitrary", "arbitrary")),
        name="compress_pages",
    )(pages, pe_pages, perm, w1bd, w2bd)


def _compress(k_raw, v_raw, pe_rows, w1bd, w2bd, nblk, layer):
    r = k_raw.shape[0]
    assert r % (nblk * CMP_BLOCK) == 0
    wspec = lambda a: pl.BlockSpec((1,) + a.shape[1:], lambda i: (layer,) + (0,) * (a.ndim - 1))
    rows = pl.BlockSpec((nblk * CMP_BLOCK, LANES), lambda i: (i, 0))
    return pl.pallas_call(
        functools.partial(_compress_kernel, nblk=nblk),
        grid=(r // (nblk * CMP_BLOCK),),
        in_specs=[rows, rows, wspec(pe_rows), wspec(w1bd), wspec(w2bd)],
        out_specs=pl.BlockSpec((1, nblk, KV_WIDTH), lambda i: (0, i, 0)),
        out_shape=jax.ShapeDtypeStruct((1, r // CMP_BLOCK, KV_WIDTH), F32),
        compiler_params=_cparams(("arbitrary",)),
        name="compress",
    )(k_raw, v_raw, pe_rows, w1bd, w2bd)[0]


def _prep_compress_weights(cmp_pe, cmp_w1, cmp_w2):
    nl = cmp_pe.shape[0]
    pe_rows = jnp.concatenate([cmp_pe] * N_KV, axis=-1)
    w1 = cmp_w1.reshape(nl, 2, CMP_BLOCK, HEAD_DIM, CMP_HIDDEN)
    eye = jnp.eye(N_KV, dtype=F32)
    w1bd = w1[:, :, :, None, :, None, :] * eye[None, None, None, :, None, :, None]
    w2bd = cmp_w2[:, :, None, :, None, :] * eye[None, None, :, None, :, None]
    return (pe_rows, w1bd.reshape(nl, 2, CMP_BLOCK // 2, 2 * LANES, N_KV * CMP_HIDDEN).astype(BF16),
            w2bd.reshape(nl, 2, N_KV * CMP_HIDDEN, LANES).astype(BF16))


def _np_bucket(dist):
    n = np.maximum(dist, 0)
    nf = np.maximum(n, 1).astype(np.float32)
    large = 16 + (np.log(nf / np.float32(16)) / np.float32(math.log(REL_MAX_DIST / 16)) * np.float32(16)).astype(np.int32)
    return np.where(n < 16, n, np.minimum(large, N_BUCKETS - 1)).astype(np.int32)


def _bucket_values(tbt, dist):
    hit = jnp.asarray(_np_bucket(dist))[None, ..., None] == jnp.arange(N_BUCKETS, dtype=jnp.int32)
    return jnp.sum(jnp.where(hit, tbt.reshape((tbt.shape[0],) + (1,) * dist.ndim + (N_BUCKETS,)), 0.0), axis=-1)


def _bias_minus_far(rel_bias, dist):
    tbt = rel_bias.astype(F32).T
    val = _bucket_values(tbt, dist) - tbt[:, N_BUCKETS - 1].reshape((-1,) + (1,) * dist.ndim)
    return jnp.where(jnp.asarray(dist >= 0), val, NEG)


def _split3(x):
    hi = x.astype(BF16)
    r1 = x - hi.astype(F32)
    mid = r1.astype(BF16)
    lo = (r1 - mid.astype(F32)).astype(BF16)
    return hi, mid, lo


def _prompt_bias_tables(rel_bias):
    a = np.arange(TQ)[:, None]
    near = _bias_minus_far(rel_bias, np.arange(TQ)[None, :] + TK - np.arange(2 * TK)[:, None])
    dist_c = a + (4 * CMP_BLOCK - CMP_BLOCK + 1) - CMP_BLOCK * np.arange(NEAR_BLOCKS)[None, :]
    dc = _bias_minus_far(rel_bias, dist_c)
    hi, mid, lo = _split3(dc)
    cols = jnp.stack([hi, mid, lo], axis=-1).reshape(N_HEADS, TQ, 3 * NEAR_BLOCKS)
    future = jnp.full((N_HEADS, TQ, 1), NEG, F32).astype(BF16)
    pad = jnp.zeros((N_HEADS, TQ, LANES - 3 * NEAR_BLOCKS - 1), BF16)
    dtab = jnp.concatenate([cols, future, pad], axis=-1).reshape(N_HEADS * TQ, LANES).T
    lane = np.arange(LANES)
    lane_m = np.where(lane < 3 * NEAR_BLOCKS, lane // 3, -1000).astype(np.int32)
    return near, dtab, jnp.asarray(np.tile(lane_m[None, :], (8, 1)))


_NT = (((1,), (1,)), ((), ()))


def _select_blocks(imp_t, cur, unroll=False):
    blk = lax.broadcasted_iota(jnp.int32, imp_t.shape, 0)
    forced = (blk == 0) | (blk == cur) | (blk == cur - 1)
    vals = jnp.where(forced, FORCED_SCORE, jnp.where(blk <= cur, imp_t, -1.0))

    def body(_, carry):
        vals, neg = carry
        mx = jnp.max(vals, axis=0, keepdims=True)
        first = jnp.min(jnp.where(vals == mx, blk, 1 << 20), axis=0, keepdims=True)
        pick = blk == first
        return jnp.where(pick, -3e38, vals), jnp.where(pick, 0.0, neg)

    _, neg = lax.fori_loop(0, N_SEL, body, (vals, jnp.full(imp_t.shape, NEG, F32)), unroll=unroll)
    return neg


def _attn_kernel(qt_ref, kc_ref, vct_ref, dtabt_ref, lanem_ref, ksel_ref, vselt_ref, kwin_ref, vwint_ref,
                 xt_ref, near_ref, gatet_ref, o_ref, qs_ref, s_ref, s2_ref, p_ref, m_ref, l_ref, alpha_ref, acc_ref, out_ref,
                 *, tiles_per_batch):
    u = pl.program_id(0) % tiles_per_batch
    nc = kc_ref.shape[0]
    ns = nc // 2
    gcols = REP * TQ

    def gate(h, br):
        c = h * 3 + br
        return gatet_ref[c:c + 1, :]

    def gate_row(g, br):
        return jnp.concatenate([gate(g * REP + r, br) for r in range(REP)], axis=1)

    def group_rows(h):
        g = h // REP
        return slice(g * HEAD_DIM, (g + 1) * HEAD_DIM)

    def head_cols(h):
        return slice((h % REP) * TQ, (h % REP + 1) * TQ)

    c = lax.broadcasted_iota(jnp.int32, (nc, LANES), 0)
    lane = lax.broadcasted_iota(jnp.int32, (nc, LANES), 1)
    rel = jnp.where(c < ns, 2 * c, 2 * c - (nc - 1)) - (8 * u - 4)
    near_hit = jnp.where(rel == lanem_ref[0:1, :], 1.0, 0.0)
    future_hit = jnp.where(rel >= NEAR_BLOCKS, 1.0, 0.0)
    onehot = jnp.where(lane < 3 * NEAR_BLOCKS, near_hit, jnp.where(lane == 3 * NEAR_BLOCKS, future_hit, 0.0))
    kk_c = jnp.concatenate([kc_ref[...], onehot.astype(BF16)], axis=1)
    imp = [jnp.zeros((ns, TQ), F32) for _ in range(N_KV)]
    for h in range(N_HEADS):
        qh = jnp.concatenate([qt_ref[h * LANES:(h + 1) * LANES, :], dtabt_ref[:, h * TQ:(h + 1) * TQ]], axis=0)
        s = jnp.dot(kk_c, qh, preferred_element_type=F32)
        mx = jnp.max(s, axis=0, keepdims=True)
        p = jnp.where(s > 0.1 * NEG, jnp.exp(s - mx), 0.0)
        pn = p / jnp.maximum(jnp.sum(p, axis=0, keepdims=True), 1e-30)
        imp[h // REP] = imp[h // REP] + (pn[:ns, :] + pn[ns:, :])
        out_ref[h // REP, :, head_cols(h)] = gate(h, 0) * jnp.dot(vct_ref[group_rows(h), :], pn.astype(BF16),
                                                                  preferred_element_type=F32)

    imp_t = jnp.concatenate(imp, axis=1)
    a = lax.broadcasted_iota(jnp.int32, imp_t.shape, 1) & (TQ - 1)
    cur = (TQ // SEL_BLOCK) * u + (a >> 6)
    neg = _select_blocks(imp_t, cur).astype(BF16)
    for h in range(N_HEADS):
        g = h // REP
        qs_ref[:LANES, h * TQ:(h + 1) * TQ] = qt_ref[h * LANES:(h + 1) * LANES, :]
        qs_ref[LANES:, h * TQ:(h + 1) * TQ] = neg[:, g * TQ:(g + 1) * TQ]

    def reset():
        m_ref[...] = jnp.full(m_ref.shape, -1e38, F32)
        l_ref[...] = jnp.zeros(l_ref.shape, F32)
        acc_ref[...] = jnp.zeros(acc_ref.shape, F32)

    def finish(br):
        for g in range(N_KV):
            cols = slice(g * gcols, (g + 1) * gcols)
            out_ref[g] = out_ref[g] + gate_row(g, br) * (acc_ref[g] / l_ref[:, cols])

    def tile_keys(kt):
        return pl.ds(pl.multiple_of(kt * TK, TK), TK)

    def softmax_tile(buf, vt_ref, kt, bias):
        keys = tile_keys(kt)
        for g in range(N_KV):
            for c0 in range(g * gcols, (g + 1) * gcols, LANES):
                cols = slice(c0, c0 + LANES)
                s = bias(c0 // TQ, c0 % TQ, buf[:, cols])
                m_prev = m_ref[:, cols]
                m_new = jnp.maximum(m_prev, jnp.max(s, axis=0, keepdims=True))
                alpha = jnp.exp(m_prev - m_new)
                p = jnp.exp(s - m_new)
                l_ref[:, cols] = alpha * l_ref[:, cols] + jnp.sum(p, axis=0, keepdims=True)
                p_ref[:, cols] = p.astype(BF16)
                alpha_ref[:, cols] = alpha
                m_ref[:, cols] = m_new
            cols = slice(g * gcols, (g + 1) * gcols)
            pv = jnp.dot(vt_ref[g * HEAD_DIM:(g + 1) * HEAD_DIM, keys], p_ref[:, cols], preferred_element_type=F32)
            acc_ref[g] = alpha_ref[:, cols] * acc_ref[g] + pv

    def near_bias(row):
        if row is None:
            return lambda h, q0, s: s
        return lambda h, q0, s: s + near_ref[h, row:row + TK, q0:q0 + LANES]

    reset()

    def sel_scores(kt, buf):
        keys = tile_keys(kt)
        kk = jnp.concatenate([ksel_ref[keys, :], xt_ref[keys, :]], axis=1)
        buf[...] = jnp.dot(kk, qs_ref[...], preferred_element_type=F32)

    def stage(pred, cur_buf, cur_tile, bias, next_buf=None, next_tile=None):
        @pl.when(pred)
        def _():
            if next_buf is not None:
                sel_scores(next_tile, next_buf)
            softmax_tile(cur_buf, vselt_ref, cur_tile, bias)

    far, near, diag = near_bias(None), near_bias(0), near_bias(TK)
    sel_scores(0, s_ref)

    def far_pair(i, carry):
        in_range = 2 * i + 1 < u
        stage(in_range, s_ref, 2 * i, far, s2_ref, 2 * i + 1)
        stage(in_range, s2_ref, 2 * i + 1, far, s_ref, 2 * i + 2)
        return carry

    lax.fori_loop(0, (u - 1) // 2, far_pair, 0)

    stage(u == 0, s_ref, 0, diag)
    odd = u % 2 == 1
    stage(odd, s_ref, u - 1, near, s2_ref, u)
    stage(odd, s2_ref, u, diag)
    even = (u >= 2) & (u % 2 == 0)
    stage(even, s_ref, u - 2, far, s2_ref, u - 1)
    stage(even, s2_ref, u - 1, near, s_ref, u)
    stage(even, s_ref, u, diag)
    finish(1)

    reset()

    def later_keys_only(h, q0, s):
        kj = lax.broadcasted_iota(jnp.int32, s.shape, 0)
        qa = lax.broadcasted_iota(jnp.int32, s.shape, 1) + q0
        return jnp.where(kj > qa, s, NEG)

    def win_tile(kt, bias):
        s_ref[...] = jnp.dot(kwin_ref[tile_keys(kt), :], qs_ref[:LANES, :], preferred_element_type=F32)
        softmax_tile(s_ref, vwint_ref, kt, bias)

    @pl.when(u >= 2)
    def _():
        win_tile(u - 2, later_keys_only)

    @pl.when(u >= 1)
    def _():
        win_tile(u - 1, near)

    win_tile(u, diag)
    finish(2)

    for i in range(N_HEADS // 2):
        pair = jnp.concatenate([out_ref[(2 * i) // REP, :, head_cols(2 * i)],
                                out_ref[(2 * i + 1) // REP, :, head_cols(2 * i + 1)]], axis=0)
        o_ref[:, i * LANES:(i + 1) * LANES] = pair.T.astype(BF16)


def _attention_prompt(qt, kc, vct, ksel, vselt, kwin, vwint, gatet, tables, batch, seq):
    near, dtab, lane_m, xt = tables
    n = batch * seq
    tpb = seq // TQ
    nc = seq // CMP_BLOCK
    per_batch = lambda rows: pl.BlockSpec((rows, LANES), lambda i: (i // tpb, 0))
    chan = lambda cols: pl.BlockSpec((None, LANES, cols), lambda i: (i // tpb, 0, 0))
    full = lambda a: pl.BlockSpec(a.shape, lambda i: (0,) * a.ndim)
    return pl.pallas_call(
        functools.partial(_attn_kernel, tiles_per_batch=tpb),
        grid=(n // TQ,),
        in_specs=[pl.BlockSpec((N_HEADS * LANES, TQ), lambda i: (0, i)),
                  per_batch(nc), chan(nc), full(dtab), full(lane_m),
                  per_batch(seq), chan(seq), per_batch(seq), chan(seq),
                  full(xt), full(near), pl.BlockSpec((LANES, TQ), lambda i: (0, i))],
        out_specs=pl.BlockSpec((TQ, ATT_WIDTH), lambda i: (i, 0)),
        out_shape=jax.ShapeDtypeStruct((n, ATT_WIDTH), BF16),
        scratch_shapes=[pltpu.VMEM((2 * LANES, N_HEADS * TQ), BF16),
                        pltpu.VMEM((TK, N_HEADS * TQ), F32), pltpu.VMEM((TK, N_HEADS * TQ), F32),
                        pltpu.VMEM((TK, N_HEADS * TQ), BF16),
                        pltpu.VMEM((1, N_HEADS * TQ), F32), pltpu.VMEM((1, N_HEADS * TQ), F32),
                        pltpu.VMEM((1, N_HEADS * TQ), F32),
                        pltpu.VMEM((N_KV, HEAD_DIM, REP * TQ), F32), pltpu.VMEM((N_KV, HEAD_DIM, REP * TQ), F32)],
        compiler_params=_cparams(("arbitrary",)),
        name="nsa_prompt",
    )(qt, kc, vct, dtab, lane_m, ksel, vselt, kwin, vwint, xt, near, gatet)


def _block_onehot_rows(seq):
    j = np.arange(seq)[:, None] // SEL_BLOCK
    return jnp.asarray((j == np.arange(LANES)[None, :]).astype(np.float32)).astype(BF16)


def _even_odd(x, nc):
    x = x.reshape(-1, nc // 2, 2, x.shape[-1])
    return jnp.concatenate([x[:, :, 0], x[:, :, 1]], axis=1).reshape(-1, x.shape[-1])


def _sample_bias_tables(rel_bias, past_len, t_new, n_win):
    t = np.arange(t_new)[:, None]
    nc = past_len // CMP_BLOCK
    tb_full = lambda dist, ok: jnp.where(jnp.asarray(ok), _bucket_values(rel_bias.astype(F32).T, dist), NEG)
    dist_c = past_len + t - (CMP_BLOCK * np.arange(nc)[None, :] + CMP_BLOCK - 1)
    dist_s = past_len + t - np.arange(past_len)[None, :]
    jn = np.arange(LANES)[None, :]
    dist_n = t - jn
    dist_w = n_win + t - np.arange(n_win)[None, :]
    rows = lambda x: x.reshape(N_HEADS * t_new, x.shape[-1])
    return (rows(tb_full(dist_c, dist_c >= 0)), rows(tb_full(dist_s, dist_s >= 0)),
            rows(tb_full(dist_n, (dist_n >= 0) & (jn < t_new))),
            rows(tb_full(dist_w, (dist_w >= 0) & (dist_w < WINDOW))))


def _attn_sample_kernel(pt_ref, *refs, n_pages, t_new):
    del pt_ref
    kvc_refs = refs[:n_pages]
    sel_refs = refs[n_pages:2 * n_pages]
    (qq_ref, seln_ref, winn_ref, winb_ref, gate_ref, xt_ref, xtn_ref, bc_ref, bs_ref, bn_ref, bw_ref,
     rsum_ref, pair_ref, o_ref, qs_ref, out_ref, s_ref, p_ref) = refs[2 * n_pages:]

    def gate_col(br):
        return jnp.concatenate([gate_ref[:, h * 3 + br:h * 3 + br + 1] for h in range(N_HEADS)], axis=0)

    def softmax_tiles(q, tiles):
        off = 0
        for k, _, bias, channel_major, width in tiles:
            if channel_major:
                s = jnp.dot(q, k(), preferred_element_type=F32)
            else:
                s = lax.dot_general(q, k(), _NT, preferred_element_type=F32)
            s_ref[:, off:off + width] = s + bias()
            off += width
        s = s_ref[:, :off]
        p = jnp.exp(s - jnp.max(s, axis=1, keepdims=True))
        den = jnp.sum(p, axis=1, keepdims=True)
        p_ref[:, :off] = p.astype(BF16)
        acc, off = None, 0
        for _, v, _, channel_major, width in tiles:
            p_t = p_ref[:, off:off + width]
            if channel_major:
                pv = lax.dot_general(p_t, v(), _NT, preferred_element_type=F32)
            else:
                pv = jnp.dot(p_t, v(), preferred_element_type=F32)
            acc = pv if acc is None else acc + pv
            off += width
        return acc / den

    qs_ref[:, :LANES] = qq_ref[...]

    kvc = jnp.concatenate([r[...] for r in kvc_refs], axis=0)
    s = lax.dot_general(qq_ref[...], kvc[:, :LANES].astype(BF16), _NT, preferred_element_type=F32) + bc_ref[...]
    mx = jnp.max(s, axis=1, keepdims=True)
    p = jnp.where(s > 0.1 * NEG, jnp.exp(s - mx), 0.0)
    pn = p / jnp.maximum(jnp.sum(p, axis=1, keepdims=True), 1e-30)
    out_ref[...] = gate_col(0) * jnp.dot(pn.astype(BF16), kvc[:, LANES:].astype(BF16), preferred_element_type=F32)

    hp = lax.Precision.HIGHEST
    imp = jnp.dot(jnp.dot(rsum_ref[...], pn, precision=hp, preferred_element_type=F32), pair_ref[...],
                  precision=hp, preferred_element_type=F32)
    cur = (n_pages * PAGE_SIZE) // SEL_BLOCK
    neg = _select_blocks(imp.T, cur, unroll=True).T
    qs_ref[:, LANES:] = jnp.concatenate(
        [neg[(h // REP) * t_new:(h // REP + 1) * t_new, :] for h in range(N_HEADS)], axis=0).astype(BF16)

    def past_tile(ref, j):
        cols = slice(j * PAGE_SIZE, (j + 1) * PAGE_SIZE)
        return (lambda: jnp.concatenate([ref[:LANES, :].astype(BF16), xt_ref[:, cols]], axis=0),
                lambda: ref[LANES:, :].astype(BF16), lambda: bs_ref[:, cols], True, PAGE_SIZE)

    new_sel_tile = (lambda: jnp.concatenate([seln_ref[:, :LANES].astype(BF16), xtn_ref[...]], axis=1),
                    lambda: seln_ref[:, LANES:].astype(BF16), lambda: bn_ref[...], False, PAGE_SIZE)
    tiles = [past_tile(sel_refs[j], j) for j in range(n_pages)] + [new_sel_tile]
    out_ref[...] += gate_col(1) * softmax_tiles(qs_ref[...], tiles)

    tiles = [(lambda: winb_ref[:LANES, :].astype(BF16), lambda: winb_ref[LANES:, :].astype(BF16), lambda: bw_ref[...],
              True, winb_ref.shape[1]),
             (lambda: winn_ref[:, :LANES].astype(BF16), lambda: winn_ref[:, LANES:].astype(BF16), lambda: bn_ref[...],
              False, PAGE_SIZE)]
    out_ref[...] += gate_col(2) * softmax_tiles(qq_ref[...], tiles)

    lane_o = lax.broadcasted_iota(jnp.int32, (t_new, LANES), 1)
    for i in range(N_HEADS // 2):
        left = out_ref[2 * i * t_new:(2 * i + 1) * t_new, :]
        right = out_ref[(2 * i + 1) * t_new:(2 * i + 2) * t_new, :]
        if (2 * i) // REP == 0:
            right = pltpu.roll(right, HEAD_DIM, 1)
        else:
            left = pltpu.roll(left, HEAD_DIM, 1)
        o_ref[:, i * LANES:(i + 1) * LANES] = jnp.where(lane_o < HEAD_DIM, left, right)


def _attention_sample(page_table, kvc_pages, sel_cache, layer, qq, sel_new, win_new, win_buf, gates, tables, t_new):
    n_seq, n_pages = page_table.shape
    xt, xtn, bc, bs, bn, bw, rsum, pair = tables
    n_win = win_buf.shape[3]
    nq = N_HEADS * t_new
    page_spec = lambda shape, j: pl.BlockSpec((None, None) + shape, lambda b, pt, j=j: (layer, pt[b, j], 0, 0))
    full = lambda a: pl.BlockSpec(a.shape, lambda b, pt: (0,) * a.ndim)
    new = pl.BlockSpec((None, PAGE_SIZE, KV_WIDTH), lambda b, pt: (b, 0, 0))
    in_specs = ([page_spec((PAGE_SIZE // CMP_BLOCK, KV_WIDTH), j) for j in range(n_pages)]
                + [page_spec((KV_WIDTH, PAGE_SIZE), j) for j in range(n_pages)]
                + [pl.BlockSpec((None, nq, LANES), lambda b, pt: (b, 0, 0)), new, new,
                   pl.BlockSpec((None, None, KV_WIDTH, n_win), lambda b, pt: (layer, b, 0, 0)),
                   pl.BlockSpec((t_new, LANES), lambda b, pt: (b, 0))]
                + [full(a) for a in (xt, xtn, bc, bs, bn, bw, rsum, pair)])
    return pl.pallas_call(
        functools.partial(_attn_sample_kernel, n_pages=n_pages, t_new=t_new),
        grid_spec=pltpu.PrefetchScalarGridSpec(
            num_scalar_prefetch=1, grid=(n_seq,), in_specs=in_specs,
            out_specs=pl.BlockSpec((t_new, ATT_WIDTH), lambda b, pt: (b, 0)),
            scratch_shapes=[pltpu.VMEM((nq, 2 * LANES), BF16), pltpu.VMEM((nq, LANES), F32),
                            pltpu.VMEM((nq, (n_pages + 1) * PAGE_SIZE), F32),
                            pltpu.VMEM((nq, (n_pages + 1) * PAGE_SIZE), BF16)]),
        out_shape=jax.ShapeDtypeStruct((n_seq * t_new, ATT_WIDTH), F32),
        compiler_params=_cparams(("arbitrary",)),
        name="nsa_sample",
    )(page_table, *([kvc_pages] * n_pages), *([sel_cache] * n_pages), qq, sel_new, win_new, win_buf, gates,
      xt, xtn, bc, bs, bn, bw, rsum, pair)


def _sample_sum_matrices(t_new, nc):
    rsum = np.zeros((LANES, N_HEADS * t_new), np.float32)
    for h in range(N_HEADS):
        for t in range(t_new):
            rsum[(h // REP) * t_new + t, h * t_new + t] = 1.0
    pair = np.zeros((nc, LANES), np.float32)
    pair[np.arange(nc), np.arange(nc) // 2] = 1.0
    return jnp.asarray(rsum), jnp.asarray(pair)


def _rank_before(vals, k):
    r = jnp.zeros(vals[k].shape, jnp.int32)
    for j, vj in enumerate(vals):
        if j < k:
            r = r + jnp.where(vj >= vals[k], 1, 0)
        elif j > k:
            r = r + jnp.where(vj > vals[k], 1, 0)
    return r


def _route(s_rows, b_rows):
    scores = []
    for g in range(N_GROUPS):
        b0, b1, b2, b3 = b_rows[4 * g:4 * g + 4]
        hi01, lo01, hi23, lo23 = jnp.maximum(b0, b1), jnp.minimum(b0, b1), jnp.maximum(b2, b3), jnp.minimum(b2, b3)
        top1 = jnp.maximum(hi01, hi23)
        top2 = jnp.maximum(jnp.maximum(lo01, lo23), jnp.minimum(hi01, hi23))
        scores.append(top1 + top2)
    in_group = [_rank_before(scores, g) == 0 for g in range(N_GROUPS)]

    def pick(rows, k):
        out = rows[4 * (N_GROUPS - 1) + k]
        for g in range(N_GROUPS - 2, -1, -1):
            out = jnp.where(in_group[g], rows[4 * g + k], out)
        return out

    bv = [pick(b_rows, k) for k in range(EXPERTS_PER_GROUP)]
    sv = [pick(s_rows, k) for k in range(EXPERTS_PER_GROUP)]
    w = [jnp.where(_rank_before(bv, k) < 2, sv[k], 0.0) for k in range(EXPERTS_PER_GROUP)]
    den = (w[0] + w[1]) + (w[2] + w[3])
    return [jnp.where(in_group[e // 4], w[e % 4] / den, 0.0) for e in range(N_EXPERTS)]


def _outproj_kernel(ps_ref, att_ref, x_ref, mod_ref, wout_ref, lng_ref, lnb_ref, rwt_ref, rb_ref,
                    x1_ref, h2_ref, g_ref):
    half = ps_ref.shape[1]
    mix = (jnp.dot(ps_ref[...], wout_ref[:half, :], preferred_element_type=F32)
           + jnp.dot(att_ref[...], wout_ref[half:, :], preferred_element_type=F32))
    x1 = _layer_norm(ALPHA * x_ref[...] + (1.0 + mod_ref[2]) * mix, lng_ref[...], lnb_ref[...])
    x1_ref[...] = x1
    h2 = x1 * (1.0 + mod_ref[4]) + mod_ref[3]
    h2b = h2.astype(BF16)
    h2_ref[...] = h2b
    st = jax.nn.sigmoid(lax.dot_general(rwt_ref[...].astype(BF16), h2b, _NT, preferred_element_type=F32))
    s_rows = [st[e:e + 1, :] for e in range(N_EXPERTS)]
    b_rows = [s_rows[e] + rb_ref[e:e + 1, :] for e in range(N_EXPERTS)]
    gt = jnp.concatenate(_route(s_rows, b_rows) + [jnp.zeros((LANES - N_EXPERTS, st.shape[1]), F32)], axis=0)
    g_ref[...] = gt.T


def _out_proj(ps, att, x, mod, wout, ln_g, ln_b, rwt, rb, tm, tiles_per_batch):
    n = x.shape[0]
    row = lambda wd: pl.BlockSpec((tm, wd), lambda i: (i, 0))
    full = lambda a: pl.BlockSpec(a.shape, lambda i: (0,) * a.ndim)
    return pl.pallas_call(
        _outproj_kernel,
        grid=(n // tm,),
        in_specs=[row(512), row(512), row(D_MODEL), _mod_spec(mod, tm, tiles_per_batch),
                  full(wout), full(ln_g), full(ln_b), full(rwt), full(rb)],
        out_specs=[row(D_MODEL), row(D_MODEL), row(LANES)],
        out_shape=[jax.ShapeDtypeStruct((n, D_MODEL), F32), jax.ShapeDtypeStruct((n, D_MODEL), BF16),
                   jax.ShapeDtypeStruct((n, LANES), F32)],
        compiler_params=_cparams(("arbitrary",)),
        name="out_proj",
    )(ps, att, x, mod, wout, ln_g, ln_b, rwt, rb)


def _moe_kernel(h2_ref, g_ref, wg_ref, wu_ref, wd_ref, x1_ref, mod_ref, lng_ref, lnb_ref, o_ref, acc_ref):
    e = pl.program_id(1)

    @pl.when(e == 0)
    def _():
        acc_ref[...] = jnp.zeros_like(acc_ref)

    x = h2_ref[...]
    gate = jnp.dot(x, wg_ref[0, 0], preferred_element_type=F32)
    up = jnp.dot(x, wu_ref[0, 0], preferred_element_type=F32)
    act = (gate * jax.nn.sigmoid(gate) * up).astype(BF16)
    y = jnp.dot(act, wd_ref[0, 0], preferred_element_type=F32)
    lane = lax.broadcasted_iota(jnp.int32, g_ref.shape, 1)
    w = jnp.sum(jnp.where(lane == e, g_ref[...], 0.0), axis=1, keepdims=True)
    acc_ref[...] += y * w

    @pl.when(e == pl.num_programs(1) - 1)
    def _():
        o_ref[...] = _layer_norm(ALPHA * x1_ref[...] + (1.0 + mod_ref[5]) * acc_ref[...], lng_ref[...], lnb_ref[...])


def _moe(h2, gates, wg, wu, wd, layer, x1, mod, ln_g, ln_b, tm, tiles_per_batch):
    n = h2.shape[0]
    row = lambda wd_: pl.BlockSpec((tm, wd_), lambda i, e: (i, 0))
    full = lambda a: pl.BlockSpec(a.shape, lambda i, e: (0,) * a.ndim)
    wspec = lambda a: pl.BlockSpec((1, 1) + a.shape[2:], lambda i, e: (layer, e, 0, 0))
    return pl.pallas_call(
        _moe_kernel,
        grid=(n // tm, N_EXPERTS),
        in_specs=[row(D_MODEL), row(LANES), wspec(wg), wspec(wu), wspec(wd), row(D_MODEL),
                  _mod_spec(mod, tm, tiles_per_batch), full(ln_g), full(ln_b)],
        out_specs=row(D_MODEL),
        out_shape=jax.ShapeDtypeStruct((n, D_MODEL), F32),
        scratch_shapes=[pltpu.VMEM((tm, D_MODEL), F32)],
        compiler_params=_cparams(("arbitrary", "arbitrary")),
        name="moe",
    )(h2, gates, wg, wu, wd, x1, mod, ln_g, ln_b)


TM_PROJ = 256
TM_MOE = 512


def _channel_mix(ps, att, x, mod, lw, sw, layer, tiles_per_seq):
    x1, h2, gts = _out_proj(ps, att, x, mod, lw["wout"], lw["ln_g0"], lw["ln_b0"], sw["rwt"], sw["rb"],
                            TM_PROJ, tiles_per_seq(TM_PROJ))
    return _moe(h2, gts, sw["wg"], sw["wu"], sw["wd"], layer, x1, mod, lw["ln_g1"], lw["ln_b1"],
                TM_MOE, tiles_per_seq(TM_MOE))


def _prompt_layer(x, mod, lw, sw, layer, batch, seq):
    n = batch * seq
    tiles = lambda tm: seq // tm
    (p, u, vn, q_t, cmp_k, cmp_v, cmp_t, sel_t, win_t, gate_t, ksel, vsel_t, kwin, vwin_t) = _in_proj(
        x, mod, lw["w_proj"], lw["sgu_ln_g"], lw["sgu_ln_b"], TM_PROJ, tiles(TM_PROJ), batch=batch, wq_t=lw["wq_t"])
    ps = _mixers(p, u, vn, *lw["mix_prompt"], tiles_per_batch=tiles(SGU_CHUNK))
    nc = seq // CMP_BLOCK
    kvc = _compress(cmp_k, cmp_v, sw["pe_rows"], sw["w1bd"], sw["w2bd"], nc, layer)
    kc = _even_odd(kvc[:, :LANES].astype(BF16), nc)
    vc_t = jnp.transpose(_even_odd(kvc[:, LANES:].astype(BF16), nc).reshape(batch, nc, LANES), (0, 2, 1))
    att = _attention_prompt(q_t, kc, vc_t, ksel, vsel_t, kwin, vwin_t, gate_t, sw["prompt_tables"], batch, seq)
    x2 = _channel_mix(ps, att, x, mod, lw, sw, layer, tiles)
    return x2, (cmp_t, sel_t, win_t, p)


def _sample_layer(x, mod, lw, sw, layer, page_table, kvc_pages, sel_cache, win_state, pool_state, n_seq, t_new):
    tiles = lambda tm: 1
    (p, u, vn, qq, cmp_raw, sel_raw, win_raw, gates) = _in_proj(
        x, mod, lw["w_proj"], lw["sgu_ln_g"], lw["sgu_ln_b"], TM_PROJ, 1)
    p_ext = jnp.concatenate([pool_state[layer], p.reshape(n_seq, t_new, POOL_WIDTH)], axis=1)
    hist = jnp.pad(p_ext, ((0, 0), (3 * t_new - p_ext.shape[1], 0), (0, 0)))
    chunks = [hist[:, k * t_new:(k + 1) * t_new].reshape(n_seq * t_new, POOL_WIDTH) for k in range(3)]
    ps = _mixers(chunks, u, vn, *lw["mix_sample"], t_new=t_new)
    q_rows = jnp.transpose(qq.reshape(N_HEADS, n_seq, t_new, LANES), (1, 0, 2, 3)).reshape(n_seq, N_HEADS * t_new, LANES)
    new_page = lambda raw: jnp.pad(raw.reshape(n_seq, t_new, KV_WIDTH), ((0, 0), (0, PAGE_SIZE - t_new), (0, 0)))
    att = _attention_sample(page_table, kvc_pages, sel_cache, layer, q_rows, new_page(sel_raw), new_page(win_raw),
                            win_state, gates, sw["sample_tables"], t_new)
    x2 = _channel_mix(ps, att.astype(BF16), x, mod, lw, sw, layer, tiles)
    win_new_t = jnp.transpose(win_raw.reshape(n_seq, t_new, KV_WIDTH), (0, 2, 1))
    win_all_t = jnp.concatenate([win_state[layer][:, :, t_new:], win_new_t], axis=2)
    return x2, (cmp_raw, sel_raw, win_all_t, p_ext[:, p_ext.shape[1] - POOL_BUF:], vn)


def kernel(x_prompt, x_sample, cache_cmp_kv, cache_sel_kv, state_win_kv, state_pool, page_table, c_prompt, c_sample,
           w_in, w_out, pool_w, pool_scale, sgu_ln_g, sgu_ln_b, sgu_w, sgu_b, cmp_pe, cmp_w1, cmp_w2, rel_bias,
           w_mod, b_mod, ln_g, ln_b, router_w, router_b, moe_w_gate, moe_w_up, moe_w_down):
    batch, seq, d = x_prompt.shape
    n_seq, t_new, _ = x_sample.shape
    depth = w_in.shape[0]
    n_pages = page_table.shape[1]
    past_len = n_pages * PAGE_SIZE
    n_phys = cache_cmp_kv.shape[1]
    n_win = state_win_kv.shape[2]
    assert seq // SEL_BLOCK == LANES and seq % TM_MOE == 0 and (n_seq * t_new) % TM_MOE == 0
    assert POOL_BUF + 1 == 2 * t_new and past_len % TK == 0

    n_c = batch + n_seq
    c_all = jnp.pad(jnp.concatenate([c_prompt, c_sample], axis=0), ((0, -n_c % 8), (0, 0)))
    m_all = _modulation(c_all, w_mod, b_mod)
    mod_p = m_all[:, :batch].reshape(depth, batch, 6, 1, d)
    mod_s = jnp.transpose(jnp.repeat(m_all[:, batch:n_c].reshape(depth, n_seq, 6, d), t_new, axis=1), (0, 2, 1, 3))

    pe_rows, w1bd, w2bd = _prep_compress_weights(cmp_pe, cmp_w1, cmp_w2)
    near, dtab, lane_m = _prompt_bias_tables(rel_bias)
    nc_past = past_len // CMP_BLOCK
    shared = {
        "pe_rows": pe_rows, "w1bd": w1bd, "w2bd": w2bd,
        "prompt_tables": (near, dtab, lane_m, _block_onehot_rows(seq)),
        "sample_tables": (_block_onehot_rows(past_len + PAGE_SIZE).T, _block_onehot_rows(past_len + PAGE_SIZE)[past_len:],
                          *_sample_bias_tables(rel_bias, past_len, t_new, n_win),
                          *_sample_sum_matrices(t_new, nc_past)),
        "rwt": router_w.T, "rb": router_b.reshape(N_EXPERTS, 1),
        "wg": moe_w_gate.astype(BF16), "wu": moe_w_up.astype(BF16), "wd": moe_w_down.astype(BF16),
    }
    layers = []
    for l in range(depth):
        layers.append({
            "w_proj": _prep_w_proj(w_in[l]), "wq_t": _prep_w_proj(w_in[l])[:, C_Q:].T, "wout": w_out[l].astype(BF16),
            "sgu_ln_g": sgu_ln_g[l].reshape(1, -1), "sgu_ln_b": sgu_ln_b[l].reshape(1, -1),
            "mix_prompt": _prep_mixer_weights(pool_w[l], pool_scale[l], sgu_w[l], sgu_b[l], SGU_CHUNK),
            "mix_sample": _prep_mixer_weights(pool_w[l], pool_scale[l], sgu_w[l], sgu_b[l], t_new),
            "ln_g0": ln_g[l, 0].reshape(1, d), "ln_b0": ln_b[l, 0].reshape(1, d),
            "ln_g1": ln_g[l, 1].reshape(1, d), "ln_b1": ln_b[l, 1].reshape(1, d),
        })

    chan_major = lambda x: jnp.transpose(x, (0, 1, 3, 4, 5, 2)).reshape(x.shape[0], x.shape[1], KV_WIDTH, x.shape[2])
    pe_pages = jnp.broadcast_to(jnp.transpose(cmp_pe, (0, 1, 3, 2))[:, :, None, :, None, :],
                                (depth, 2, N_KV, HEAD_DIM, PAGE_SIZE // CMP_BLOCK, CMP_BLOCK))
    kvc_pages = _compress_pages(chan_major(cache_cmp_kv), pe_pages.reshape(depth, KV_WIDTH, PAGE_SIZE), w1bd, w2bd, 64)
    kvc_pages = kvc_pages.reshape(depth, n_phys, PAGE_SIZE // CMP_BLOCK, KV_WIDTH)
    sel_cache = chan_major(cache_sel_kv)
    win_state = chan_major(state_win_kv)

    xp = x_prompt.reshape(batch * seq, d)
    xs = x_sample.reshape(n_seq * t_new, d)
    outs_p, outs_s = [], []
    for l in range(depth):
        xp, st = _prompt_layer(xp, mod_p[l], layers[l], shared, l, batch, seq)
        outs_p.append(st)
        xs, st = _sample_layer(xs, mod_s[l], layers[l], shared, l, page_table, kvc_pages, sel_cache, win_state,
                               state_pool, n_seq, t_new)
        outs_s.append(st)

    kv5 = lambda x, b: x.reshape(b, -1, 2, N_KV, HEAD_DIM)
    kv5_t = lambda x: jnp.transpose(x.reshape(x.shape[0], 2, N_KV, HEAD_DIM, x.shape[2]), (0, 4, 1, 2, 3))
    stack = lambda xs_: jnp.stack(xs_)
    w_keep = min(WINDOW, seq)
    return (xp.reshape(batch, seq, d), xs.reshape(n_seq, t_new, d),
            stack([kv5_t(o[0]) for o in outs_p]), stack([kv5(o[0], n_seq) for o in outs_s]),
            stack([kv5_t(o[1]) for o in outs_p]), stack([kv5(o[1], n_seq) for o in outs_s]),
            stack([kv5_t(o[2][:, :, seq - w_keep:]) for o in outs_p]), stack([kv5_t(o[2]) for o in outs_s]),
            stack([o[3].reshape(batch, seq, POOL_WIDTH)[:, seq - POOL_BUF:] for o in outs_p]),
            stack([o[3] for o in outs_s]),
            stack([o[4].reshape(n_seq, t_new, SGU_WIDTH) for o in outs_s]))
```

```python
import functools
import math

import numpy as np
import jax
import jax.numpy as jnp
from jax import lax
from jax.experimental import pallas as pl
from jax.experimental.pallas import tpu as pltpu

F32 = jnp.float32
BF16 = jnp.bfloat16

D_MODEL = 1024
POOL_WIDTH = 256
SGU_WIDTH = 256
ATT_WIDTH = 512
POOL_WINDOWS = (2, 4, 8, 16)
POOL_GROUP_DIM = 64
POOL_BUF = 15
SGU_GROUPS = 4
SGU_CHUNK = 128
N_HEADS = 8
HEAD_DIM = 64
N_KV = 2
REP = 4
CMP_BLOCK = 32
CMP_HIDDEN = 128
SEL_BLOCK = 64
N_SEL = 16
WINDOW = 512
N_BUCKETS = 32
REL_MAX_DIST = 128
N_EXPERTS = 16
N_GROUPS = 4
EXPERTS_PER_GROUP = 4
D_EXPERT = 512
DEPTH = 2
ALPHA = (2 * DEPTH) ** 0.25
LN_EPS = 1e-5
FORCED_SCORE = 1e4
NEG = -1e30
PAGE_SIZE = 128

KV_WIDTH = 2 * N_KV * HEAD_DIM
LANES = 128
VMEM_LIMIT = 56 * 1024 * 1024

C_P, C_U, C_V, C_CMP, C_SEL, C_WIN, C_GATE, C_Q = 0, 256, 512, 768, 1024, 1280, 1536, 1664
W_PROJ = C_Q + N_HEADS * LANES

TQ = 256
TK = 256
NEAR_BLOCKS = 12
FAR_BUCKET_DIST = 113


def _cparams(sem):
    return pltpu.CompilerParams(dimension_semantics=sem, vmem_limit_bytes=VMEM_LIMIT)


def _layer_norm(x, g, b):
    mu = jnp.mean(x, axis=-1, keepdims=True)
    xc = x - mu
    var = jnp.mean(xc * xc, axis=-1, keepdims=True)
    return xc * lax.rsqrt(var + LN_EPS) * g + b


def _mod_kernel(c_ref, w_ref, b_ref, o_ref):
    c = c_ref[...]
    a = (c * jax.nn.sigmoid(c)).astype(BF16)
    o_ref[0] = jnp.dot(a, w_ref[0].astype(BF16), preferred_element_type=F32) + b_ref[0]


def _modulation(c_all, w_mod, b_mod):
    n, d = c_all.shape
    depth, _, w = w_mod.shape
    tn = 1536
    return pl.pallas_call(
        _mod_kernel,
        grid=(depth, w // tn),
        in_specs=[pl.BlockSpec((n, d), lambda l, j: (0, 0)),
                  pl.BlockSpec((1, d, tn), lambda l, j: (l, 0, j)),
                  pl.BlockSpec((1, 1, tn), lambda l, j: (l, 0, j))],
        out_specs=pl.BlockSpec((1, n, tn), lambda l, j: (l, 0, j)),
        out_shape=jax.ShapeDtypeStruct((depth, n, w), F32),
        compiler_params=_cparams(("arbitrary", "arbitrary")),
        name="adaln_mod",
    )(c_all, w_mod, b_mod.reshape(depth, 1, w))


def _mod_spec(mod, tm, tiles_per_batch):
    if mod.ndim == 4:
        return pl.BlockSpec((None, 6, 1, D_MODEL), lambda i, *_: (i // tiles_per_batch, 0, 0, 0))
    return pl.BlockSpec((6, tm, D_MODEL), lambda i, *_: (0, i, 0))


def _inproj_kernel(x_ref, mod_ref, w_ref, lng_ref, lnb_ref, *rest, channel_major, n_state):
    h = (x_ref[...] * (1.0 + mod_ref[1]) + mod_ref[0]).astype(BF16)

    def seg(a, b):
        return jnp.dot(h, w_ref[:, a:b], preferred_element_type=F32)

    if channel_major:
        wqt_ref = rest[0]
        p_ref, u_ref, vn_ref, qt_ref = rest[1 + n_state:5 + n_state]
        rest = rest[5 + n_state:]
        qt_ref[...] = lax.dot_general(wqt_ref[...], h, _NT, preferred_element_type=F32).astype(BF16)
    else:
        p_ref, u_ref, vn_ref, qq_ref = rest[:4]
        rest = rest[4:]
        for hd in range(N_HEADS):
            qq_ref[hd] = seg(C_Q + hd * LANES, C_Q + (hd + 1) * LANES).astype(BF16)
    p_ref[...] = seg(C_P, C_U)
    u_ref[...] = seg(C_U, C_V)
    vn_ref[...] = _layer_norm(seg(C_V, C_CMP), lng_ref[...], lnb_ref[...])
    cmp = seg(C_CMP, C_SEL)
    sel = seg(C_SEL, C_WIN)
    win = seg(C_WIN, C_GATE)
    gates = jax.nn.sigmoid(seg(C_GATE, C_Q))
    if not channel_major:
        cmp_ref, sel_ref, win_ref, gate_ref = rest
        cmp_ref[...] = cmp
        sel_ref[...] = sel
        win_ref[...] = win
        gate_ref[...] = gates
        return
    cmpk_ref, cmpv_ref, cmpt_ref, selt_ref, wint_ref, gatet_ref, ksel_ref, vselt_ref, kwin_ref, vwint_ref = rest
    cmpk_ref[...] = cmp[:, :LANES]
    cmpv_ref[...] = cmp[:, LANES:]
    cmpt_ref[...] = cmp.T
    sel_t = sel.T
    selt_ref[...] = sel_t
    ksel_ref[...] = sel[:, :LANES].astype(BF16)
    vselt_ref[...] = sel_t[LANES:, :].astype(BF16)
    win_t = win.T
    wint_ref[...] = win_t
    kwin_ref[...] = win[:, :LANES].astype(BF16)
    vwint_ref[...] = win_t[LANES:, :].astype(BF16)
    gatet_ref[...] = gates.T


def _in_proj(x, mod, w_proj, ln_g, ln_b, tm, tiles_per_batch, batch=None, wq_t=None, layer=0, depth=1, state=None):
    n = x.shape[0]
    row = lambda wd: pl.BlockSpec((tm, wd), lambda i: (i, 0))
    full = lambda a: pl.BlockSpec(a.shape, lambda i: (0,) * a.ndim)
    f32o = lambda wd: jax.ShapeDtypeStruct((n, wd), F32)
    out_specs = [row(256), row(256), row(256)]
    out_shape = [f32o(256), f32o(256), f32o(256)]
    ins, in_specs = [x, mod, w_proj, ln_g, ln_b], [row(D_MODEL), _mod_spec(mod, tm, tiles_per_batch), full(w_proj),
                                                    full(ln_g), full(ln_b)]
    aliases = {}
    if batch is None:
        out_specs += [pl.BlockSpec((N_HEADS, tm, LANES), lambda i: (0, i, 0)), row(256), row(256), row(256), row(LANES)]
        out_shape += [jax.ShapeDtypeStruct((N_HEADS, n, LANES), BF16), f32o(256), f32o(256), f32o(256), f32o(LANES)]
    else:
        ins.append(wq_t)
        in_specs.append(full(wq_t))
        out_specs.append(pl.BlockSpec((N_HEADS * LANES, tm), lambda i: (0, i)))
        out_shape.append(jax.ShapeDtypeStruct((N_HEADS * LANES, n), BF16))
        seq = n // batch
        chan = lambda c: pl.BlockSpec((None, c, tm), lambda i: (i // tiles_per_batch, 0, i % tiles_per_batch))
        chan_o = lambda c, dt: jax.ShapeDtypeStruct((batch, c, seq), dt)
        kv_state = pl.BlockSpec((None, None, KV_WIDTH, tm),
                                lambda i: (layer, i // tiles_per_batch, 0, i % tiles_per_batch))
        kv_state_o = jax.ShapeDtypeStruct((depth, batch, KV_WIDTH, seq), F32)
        first_state_out = len(out_specs) + 2
        out_specs += [row(LANES), row(LANES), kv_state, kv_state, kv_state, pl.BlockSpec((LANES, tm), lambda i: (0, i)),
                      row(LANES), chan(LANES), row(LANES), chan(LANES)]
        out_shape += [f32o(LANES), f32o(LANES),
                      kv_state_o, kv_state_o, kv_state_o, jax.ShapeDtypeStruct((LANES, n), F32),
                      jax.ShapeDtypeStruct((n, LANES), BF16), chan_o(LANES, BF16),
                      jax.ShapeDtypeStruct((n, LANES), BF16), chan_o(LANES, BF16)]
        if state is not None:
            aliases = {len(ins) + k: first_state_out + k for k in range(len(state))}
            ins += list(state)
            in_specs += [pl.BlockSpec(memory_space=pl.ANY)] * len(state)
    return pl.pallas_call(
        functools.partial(_inproj_kernel, channel_major=batch is not None, n_state=len(state or ())),
        grid=(n // tm,),
        in_specs=in_specs,
        out_specs=out_specs,
        out_shape=out_shape,
        input_output_aliases=aliases,
        compiler_params=_cparams(("arbitrary",)),
        name="in_proj",
    )(*ins)


def _prep_w_proj(w_in):
    d = w_in.shape[0]
    q = w_in[:, 768:1280].reshape(d, N_KV, REP, HEAD_DIM) * (HEAD_DIM ** -0.5)
    qq = jnp.zeros((d, N_KV, REP, N_KV, HEAD_DIM), w_in.dtype)
    for g in range(N_KV):
        qq = qq.at[:, g, :, g, :].set(q[:, g])
    gate = jnp.pad(w_in[:, 2048:2072], ((0, 0), (0, LANES - 24)))
    cols = [w_in[:, 0:768], w_in[:, 1280:2048], gate, qq.reshape(d, N_HEADS * LANES)]
    return jnp.concatenate(cols, axis=1).astype(BF16)


def _window_sums(shifted):
    acc = shifted(0)
    sums = {}
    for k in range(1, 16):
        acc = acc + shifted(k)
        if k + 1 in POOL_WINDOWS:
            sums[k + 1] = acc
    lane = lax.broadcasted_iota(jnp.int32, acc.shape, 1)
    return jnp.where(lane < 64, sums[2], jnp.where(lane < 128, sums[4], jnp.where(lane < 192, sums[8], sums[16])))


def _mixers_tail(sums, cnt, cur, u, vn, wpool_ref, pscale_ref, wcat_ref, sbias_ref, o_ref):
    diff = (sums / cnt - cur).astype(BF16)
    pool = jnp.dot(diff, wpool_ref[...], preferred_element_type=F32) * pscale_ref[...]
    lane = lax.broadcasted_iota(jnp.int32, vn.shape, 1)
    vb = vn.astype(BF16)
    zero = jnp.zeros_like(vb)
    stacked = jnp.concatenate([jnp.where((lane // 64) == g, vb, zero) for g in range(SGU_GROUPS)], axis=0)
    mixed = jnp.dot(wcat_ref[...], stacked, preferred_element_type=F32) + sbias_ref[...]
    o_ref[:, :POOL_WIDTH] = pool.astype(BF16)
    o_ref[:, POOL_WIDTH:] = (u * mixed).astype(BF16)


def _lane_window():
    lane = lax.broadcasted_iota(jnp.int32, (SGU_CHUNK, POOL_WIDTH), 1)
    return jnp.where(lane < 64, 2, jnp.where(lane < 128, 4, jnp.where(lane < 192, 8, 16)))


def _mix_prompt_kernel(p_ref, u_ref, vn_ref, wpool_ref, pscale_ref, wcat_ref, sbias_ref, o_ref, prev_ref,
                       *, tiles_per_batch):
    t = pl.program_id(0) % tiles_per_batch

    @pl.when(t == 0)
    def _():
        prev_ref[...] = jnp.zeros_like(prev_ref)

    cur = p_ref[...]
    prev = prev_ref[...]
    row = lax.broadcasted_iota(jnp.int32, cur.shape, 0)

    def shifted(k):
        if k == 0:
            return cur
        return jnp.where(row >= k, pltpu.roll(cur, k, 0), pltpu.roll(prev, k, 0))

    sums = _window_sums(shifted)
    prev_ref[...] = cur
    cnt = jnp.minimum(_lane_window(), t * SGU_CHUNK + row + 1).astype(F32)
    _mixers_tail(sums, cnt, cur, u_ref[...], vn_ref[...], wpool_ref, pscale_ref, wcat_ref, sbias_ref, o_ref)


def _mix_sample_kernel(pa_ref, pb_ref, pc_ref, u_ref, vn_ref, wpool_ref, pscale_ref, wcat_ref, sbias_ref, o_ref,
                       *, t_new):
    a, b, c = pa_ref[...], pb_ref[...], pc_ref[...]
    rows = c.shape[0]
    t = lax.broadcasted_iota(jnp.int32, c.shape, 0) % t_new

    def shifted(k):
        if k == 0:
            return c
        hi, lo = (c, b) if k < t_new else (b, a)
        kk = k % t_new
        if kk == 0:
            return hi
        return jnp.where(t >= kk, pltpu.roll(hi, kk, 0), pltpu.roll(lo, rows - t_new + kk, 0))

    sums = _window_sums(shifted)
    cnt = _lane_window().astype(F32)
    _mixers_tail(sums, cnt, c, u_ref[...], vn_ref[...], wpool_ref, pscale_ref, wcat_ref, sbias_ref, o_ref)


def _mixers(p_in, u, vn, wpool, pscale, wcat, sbias, tiles_per_batch=None, t_new=None):
    n = u.shape[0]
    tm = SGU_CHUNK
    row = lambda wd: pl.BlockSpec((tm, wd), lambda i: (i, 0))
    full = lambda a: pl.BlockSpec(a.shape, lambda i: (0,) * a.ndim)
    weights = [wpool, pscale, wcat, sbias]
    if t_new is None:
        kern = functools.partial(_mix_prompt_kernel, tiles_per_batch=tiles_per_batch)
        ins, scratch = [p_in], [pltpu.VMEM((tm, POOL_WIDTH), F32)]
    else:
        kern = functools.partial(_mix_sample_kernel, t_new=t_new)
        ins, scratch = list(p_in), []
    return pl.pallas_call(
        kern,
        grid=(n // tm,),
        in_specs=[row(256)] * (len(ins) + 2) + [full(a) for a in weights],
        out_specs=row(512),
        out_shape=jax.ShapeDtypeStruct((n, 512), BF16),
        scratch_shapes=scratch,
        compiler_params=_cparams(("arbitrary",)),
        name="mixers",
    )(*ins, u, vn, *weights)


def _prep_mixer_weights(pool_w, pool_scale, sgu_w, sgu_b, chunk):
    wpool = jax.scipy.linalg.block_diag(*[pool_w[g] for g in range(len(POOL_WINDOWS))]).astype(BF16)
    tri = jnp.tril(jnp.ones((chunk, chunk), bool))
    w = jnp.where(tri, sgu_w[:, :chunk, :chunk], 0.0)
    reps = SGU_CHUNK // chunk
    eye = jnp.eye(reps, dtype=w.dtype)
    wt = jnp.concatenate([jnp.kron(eye, w[g]) for g in range(SGU_GROUPS)], axis=1).astype(BF16)
    b = jnp.tile(sgu_b[:, :chunk], (1, reps))
    sbias = jnp.repeat(b.T, SGU_WIDTH // SGU_GROUPS, axis=1)
    return wpool, pool_scale.reshape(1, POOL_WIDTH), wt, sbias


def _compress_mlp(token_rows, w1_ref, w2_ref, o_ref, nblk):
    for s in range(2):
        acc = jnp.zeros((nblk, N_KV * CMP_HIDDEN), F32)
        for j in range(0, CMP_BLOCK, 2):
            pair = jnp.concatenate([token_rows(s, j), token_rows(s, j + 1)], axis=1)
            acc = acc + jnp.dot(pair, w1_ref[0, s, j // 2], preferred_element_type=F32)
        hdn = jax.nn.gelu(acc).astype(BF16)
        o_ref[0, :, s * LANES:(s + 1) * LANES] = jnp.dot(hdn, w2_ref[0, s], preferred_element_type=F32)


def _compress_kernel(k_ref, v_ref, pe_ref, w1_ref, w2_ref, o_ref, *, nblk):
    bufs = (k_ref, v_ref)
    rows = lambda s, j: (bufs[s][pl.ds(j, nblk, stride=CMP_BLOCK), :] + pe_ref[0, s, j:j + 1, :]).astype(BF16)
    _compress_mlp(rows, w1_ref, w2_ref, o_ref, nblk)


PAIR_TOKENS = 2 * PAGE_SIZE
PAIR_BLOCKS = PAIR_TOKENS // CMP_BLOCK


def _compress_pages_kernel(raw_ref, pe_ref, perm_ref, w1_ref, w2_ref, o_ref, k_ref, v_ref, *, nblk):
    bufs = (k_ref, v_ref)

    def body(q, carry):
        for s in range(2):
            ch = slice(s * LANES, (s + 1) * LANES)
            x = jnp.concatenate([raw_ref[0, 2 * q, ch, :] + pe_ref[0, ch, :],
                                 raw_ref[0, 2 * q + 1, ch, :] + pe_ref[0, ch, :]], axis=1).astype(BF16)
            t = lax.dot_general(perm_ref[...], x, _NT, preferred_element_type=F32)
            rows = pl.ds(pl.multiple_of(q * PAIR_BLOCKS, PAIR_BLOCKS), PAIR_BLOCKS)
            bufs[s][:, rows, :] = t.reshape(CMP_BLOCK, PAIR_BLOCKS, LANES)
        return carry

    lax.fori_loop(0, raw_ref.shape[1] // 2, body, 0, unroll=4)
    _compress_mlp(lambda s, j: bufs[s][j].astype(BF16), w1_ref, w2_ref, o_ref, nblk)


def _compress_pages(pages, pe_pages, w1bd, w2bd, pages_per_step):
    nl, n_pages, _, _ = pages.shape
    assert n_pages % pages_per_step == 0 and pages_per_step % 2 == 0
    nblk = pages_per_step * (PAGE_SIZE // CMP_BLOCK)
    tok = np.arange(PAIR_TOKENS)
    dest = (tok % CMP_BLOCK) * PAIR_BLOCKS + tok // CMP_BLOCK
    perm = jnp.asarray((np.arange(PAIR_TOKENS)[:, None] == dest[None, :]).astype(np.float32)).astype(BF16)
    wspec = lambda a: pl.BlockSpec((1,) + a.shape[1:], lambda l, i: (l,) + (0,) * (a.ndim - 1))
    return pl.pallas_call(
        functools.partial(_compress_pages_kernel, nblk=nblk),
        grid=(nl, n_pages // pages_per_step),
        in_specs=[pl.BlockSpec((1, pages_per_step, KV_WIDTH, PAGE_SIZE), lambda l, i: (l, i, 0, 0)),
                  wspec(pe_pages), pl.BlockSpec(perm.shape, lambda l, i: (0, 0)), wspec(w1bd), wspec(w2bd)],
        out_specs=pl.BlockSpec((1, nblk, KV_WIDTH), lambda l, i: (l, i, 0)),
        out_shape=jax.ShapeDtypeStruct((nl, n_pages * (PAGE_SIZE // CMP_BLOCK), KV_WIDTH), F32),
        scratch_shapes=[pltpu.VMEM((CMP_BLOCK, nblk, LANES), F32)] * 2,
        compiler_params=_cparams(("arbitrary", "arbitrary")),
        name="compress_pages",
    )(pages, pe_pages, perm, w1bd, w2bd)


def _compress(k_raw, v_raw, pe_rows, w1bd, w2bd, nblk, layer):
    r = k_raw.shape[0]
    assert r % (nblk * CMP_BLOCK) == 0
    wspec = lambda a: pl.BlockSpec((1,) + a.shape[1:], lambda i: (layer,) + (0,) * (a.ndim - 1))
    rows = pl.BlockSpec((nblk * CMP_BLOCK, LANES), lambda i: (i, 0))
    return pl.pallas_call(
        functools.partial(_compress_kernel, nblk=nblk),
        grid=(r // (nblk * CMP_BLOCK),),
        in_specs=[rows, rows, wspec(pe_rows), wspec(w1bd), wspec(w2bd)],
        out_specs=pl.BlockSpec((1, nblk, KV_WIDTH), lambda i: (0, i, 0)),
        out_shape=jax.ShapeDtypeStruct((1, r // CMP_BLOCK, KV_WIDTH), F32),
        compiler_params=_cparams(("arbitrary",)),
        name="compress",
    )(k_raw, v_raw, pe_rows, w1bd, w2bd)[0]


def _prep_compress_weights(cmp_pe, cmp_w1, cmp_w2):
    nl = cmp_pe.shape[0]
    pe_rows = jnp.concatenate([cmp_pe] * N_KV, axis=-1)
    w1 = cmp_w1.reshape(nl, 2, CMP_BLOCK, HEAD_DIM, CMP_HIDDEN)
    eye = jnp.eye(N_KV, dtype=F32)
    w1bd = w1[:, :, :, None, :, None, :] * eye[None, None, None, :, None, :, None]
    w2bd = cmp_w2[:, :, None, :, None, :] * eye[None, None, :, None, :, None]
    return (pe_rows, w1bd.reshape(nl, 2, CMP_BLOCK // 2, 2 * LANES, N_KV * CMP_HIDDEN).astype(BF16),
            w2bd.reshape(nl, 2, N_KV * CMP_HIDDEN, LANES).astype(BF16))


def _np_bucket(dist):
    n = np.maximum(dist, 0)
    nf = np.maximum(n, 1).astype(np.float32)
    large = 16 + (np.log(nf / np.float32(16)) / np.float32(math.log(REL_MAX_DIST / 16)) * np.float32(16)).astype(np.int32)
    return np.where(n < 16, n, np.minimum(large, N_BUCKETS - 1)).astype(np.int32)


def _bucket_values(tbt, dist):
    hit = jnp.asarray(_np_bucket(dist))[None, ..., None] == jnp.arange(N_BUCKETS, dtype=jnp.int32)
    return jnp.sum(jnp.where(hit, tbt.reshape((tbt.shape[0],) + (1,) * dist.ndim + (N_BUCKETS,)), 0.0), axis=-1)


def _bias_minus_far(rel_bias, dist):
    tbt = rel_bias.astype(F32).T
    val = _bucket_values(tbt, dist) - tbt[:, N_BUCKETS - 1].reshape((-1,) + (1,) * dist.ndim)
    return jnp.where(jnp.asarray(dist >= 0), val, NEG)


def _split3(x):
    hi = x.astype(BF16)
    r1 = x - hi.astype(F32)
    mid = r1.astype(BF16)
    lo = (r1 - mid.astype(F32)).astype(BF16)
    return hi, mid, lo


def _prompt_bias_tables(rel_bias):
    a = np.arange(TQ)[:, None]
    span = 2 * TK + TQ
    per_dist = _bias_minus_far(rel_bias, np.arange(span) - (TQ - 1))
    skew = jnp.broadcast_to(per_dist[:, None, :], (N_HEADS, 2 * TK, span)).reshape(N_HEADS, 2 * TK * span)
    skew = skew[:, :2 * TK * (span - 1)].reshape(N_HEADS, 2 * TK, span - 1)
    near = skew[:, :, 2 * TK - 1:2 * TK - 1 + TQ]
    dist_c = a + (4 * CMP_BLOCK - CMP_BLOCK + 1) - CMP_BLOCK * np.arange(NEAR_BLOCKS)[None, :]
    dc = _bias_minus_far(rel_bias, dist_c)
    hi, mid, lo = _split3(dc)
    cols = jnp.stack([hi, mid, lo], axis=-1).reshape(N_HEADS, TQ, 3 * NEAR_BLOCKS)
    future = jnp.full((N_HEADS, TQ, 1), NEG, F32).astype(BF16)
    pad = jnp.zeros((N_HEADS, TQ, LANES - 3 * NEAR_BLOCKS - 1), BF16)
    dtab = jnp.concatenate([cols, future, pad], axis=-1).reshape(N_HEADS * TQ, LANES).T
    lane = np.arange(LANES)
    lane_m = np.where(lane < 3 * NEAR_BLOCKS, lane // 3, -1000).astype(np.int32)
    return near, dtab, jnp.asarray(np.tile(lane_m[None, :], (8, 1)))


_NT = (((1,), (1,)), ((), ()))


def _select_blocks(imp_t, cur, unroll=False):
    blk = lax.broadcasted_iota(jnp.int32, imp_t.shape, 0)
    forced = (blk == 0) | (blk == cur) | (blk == cur - 1)
    vals = jnp.where(forced, FORCED_SCORE, jnp.where(blk <= cur, imp_t, -1.0))

    def body(_, carry):
        vals, neg = carry
        mx = jnp.max(vals, axis=0, keepdims=True)
        first = jnp.min(jnp.where(vals == mx, blk, 1 << 20), axis=0, keepdims=True)
        pick = blk == first
        return jnp.where(pick, -3e38, vals), jnp.where(pick, 0.0, neg)

    _, neg = lax.fori_loop(0, N_SEL, body, (vals, jnp.full(imp_t.shape, NEG, F32)), unroll=unroll)
    return neg


def _attn_kernel(qt_ref, kc_ref, vct_ref, dtabt_ref, lanem_ref, ksel_ref, vselt_ref, kwin_ref, vwint_ref,
                 xt_ref, near_ref, gatet_ref, o_ref, qs_ref, s_ref, s2_ref, p_ref, m_ref, l_ref, alpha_ref, acc_ref, out_ref,
                 *, tiles_per_batch):
    u = pl.program_id(0) % tiles_per_batch
    nc = kc_ref.shape[0]
    ns = nc // 2
    gcols = REP * TQ

    def gate(h, br):
        c = h * 3 + br
        return gatet_ref[c:c + 1, :]

    def gate_row(g, br):
        return jnp.concatenate([gate(g * REP + r, br) for r in range(REP)], axis=1)

    def group_rows(h):
        g = h // REP
        return slice(g * HEAD_DIM, (g + 1) * HEAD_DIM)

    def head_cols(h):
        return slice((h % REP) * TQ, (h % REP + 1) * TQ)

    c = lax.broadcasted_iota(jnp.int32, (nc, LANES), 0)
    lane = lax.broadcasted_iota(jnp.int32, (nc, LANES), 1)
    rel = jnp.where(c < ns, 2 * c, 2 * c - (nc - 1)) - (8 * u - 4)
    near_hit = jnp.where(rel == lanem_ref[0:1, :], 1.0, 0.0)
    future_hit = jnp.where(rel >= NEAR_BLOCKS, 1.0, 0.0)
    onehot = jnp.where(lane < 3 * NEAR_BLOCKS, near_hit, jnp.where(lane == 3 * NEAR_BLOCKS, future_hit, 0.0))
    kk_c = jnp.concatenate([kc_ref[...], onehot.astype(BF16)], axis=1)
    imp = [jnp.zeros((ns, TQ), F32) for _ in range(N_KV)]
    for h in range(N_HEADS):
        qh = jnp.concatenate([qt_ref[h * LANES:(h + 1) * LANES, :], dtabt_ref[:, h * TQ:(h + 1) * TQ]], axis=0)
        s = jnp.dot(kk_c, qh, preferred_element_type=F32)
        mx = jnp.max(s, axis=0, keepdims=True)
        p = jnp.where(s > 0.1 * NEG, jnp.exp(s - mx), 0.0)
        pn = p / jnp.maximum(jnp.sum(p, axis=0, keepdims=True), 1e-30)
        imp[h // REP] = imp[h // REP] + (pn[:ns, :] + pn[ns:, :])
        out_ref[h // REP, :, head_cols(h)] = gate(h, 0) * jnp.dot(vct_ref[group_rows(h), :], pn.astype(BF16),
                                                                  preferred_element_type=F32)

    imp_t = jnp.concatenate(imp, axis=1)
    a = lax.broadcasted_iota(jnp.int32, imp_t.shape, 1) & (TQ - 1)
    cur = (TQ // SEL_BLOCK) * u + (a >> 6)
    neg = _select_blocks(imp_t, cur).astype(BF16)
    for h in range(N_HEADS):
        g = h // REP
        qs_ref[:LANES, h * TQ:(h + 1) * TQ] = qt_ref[h * LANES:(h + 1) * LANES, :]
        qs_ref[LANES:, h * TQ:(h + 1) * TQ] = neg[:, g * TQ:(g + 1) * TQ]

    def reset():
        m_ref[...] = jnp.full(m_ref.shape, -1e38, F32)
        l_ref[...] = jnp.zeros(l_ref.shape, F32)
        acc_ref[...] = jnp.zeros(acc_ref.shape, F32)

    def finish(br):
        for g in range(N_KV):
            cols = slice(g * gcols, (g + 1) * gcols)
            out_ref[g] = out_ref[g] + gate_row(g, br) * (acc_ref[g] / l_ref[:, cols])

    def tile_keys(kt):
        return pl.ds(pl.multiple_of(kt * TK, TK), TK)

    def softmax_tile(buf, vt_ref, kt, bias):
        keys = tile_keys(kt)
        for g in range(N_KV):
            for c0 in range(g * gcols, (g + 1) * gcols, LANES):
                cols = slice(c0, c0 + LANES)
                s = bias(c0 // TQ, c0 % TQ, buf[:, cols])
                m_prev = m_ref[:, cols]
                m_new = jnp.maximum(m_prev, jnp.max(s, axis=0, keepdims=True))
                alpha = jnp.exp(m_prev - m_new)
                p = jnp.exp(s - m_new)
                l_ref[:, cols] = alpha * l_ref[:, cols] + jnp.sum(p, axis=0, keepdims=True)
                p_ref[:, cols] = p.astype(BF16)
                alpha_ref[:, cols] = alpha
                m_ref[:, cols] = m_new
            cols = slice(g * gcols, (g + 1) * gcols)
            pv = jnp.dot(vt_ref[g * HEAD_DIM:(g + 1) * HEAD_DIM, keys], p_ref[:, cols], preferred_element_type=F32)
            acc_ref[g] = alpha_ref[:, cols] * acc_ref[g] + pv

    def near_bias(row):
        if row is None:
            return lambda h, q0, s: s
        return lambda h, q0, s: s + near_ref[h, row:row + TK, q0:q0 + LANES]

    reset()

    def sel_scores(kt, buf):
        keys = tile_keys(kt)
        kk = jnp.concatenate([ksel_ref[keys, :], xt_ref[keys, :]], axis=1)
        buf[...] = jnp.dot(kk, qs_ref[...], preferred_element_type=F32)

    def stage(pred, cur_buf, cur_tile, bias, next_buf=None, next_tile=None):
        @pl.when(pred)
        def _():
            if next_buf is not None:
                sel_scores(next_tile, next_buf)
            softmax_tile(cur_buf, vselt_ref, cur_tile, bias)

    far, near, diag = near_bias(None), near_bias(0), near_bias(TK)
    sel_scores(0, s_ref)

    def far_pair(i, carry):
        in_range = 2 * i + 1 < u
        stage(in_range, s_ref, 2 * i, far, s2_ref, 2 * i + 1)
        stage(in_range, s2_ref, 2 * i + 1, far, s_ref, 2 * i + 2)
        return carry

    lax.fori_loop(0, (u - 1) // 2, far_pair, 0)

    stage(u == 0, s_ref, 0, diag)
    odd = u % 2 == 1
    stage(odd, s_ref, u - 1, near, s2_ref, u)
    stage(odd, s2_ref, u, diag)
    even = (u >= 2) & (u % 2 == 0)
    stage(even, s_ref, u - 2, far, s2_ref, u - 1)
    stage(even, s2_ref, u - 1, near, s_ref, u)
    stage(even, s_ref, u, diag)
    finish(1)

    reset()

    def later_keys_only(h, q0, s):
        kj = lax.broadcasted_iota(jnp.int32, s.shape, 0)
        qa = lax.broadcasted_iota(jnp.int32, s.shape, 1) + q0
        return jnp.where(kj > qa, s, NEG)

    def win_tile(kt, bias):
        s_ref[...] = jnp.dot(kwin_ref[tile_keys(kt), :], qs_ref[:LANES, :], preferred_element_type=F32)
        softmax_tile(s_ref, vwint_ref, kt, bias)

    @pl.when(u >= 2)
    def _():
        win_tile(u - 2, later_keys_only)

    @pl.when(u >= 1)
    def _():
        win_tile(u - 1, near)

    win_tile(u, diag)
    finish(2)

    for i in range(N_HEADS // 2):
        pair = jnp.concatenate([out_ref[(2 * i) // REP, :, head_cols(2 * i)],
                                out_ref[(2 * i + 1) // REP, :, head_cols(2 * i + 1)]], axis=0)
        o_ref[:, i * LANES:(i + 1) * LANES] = pair.T.astype(BF16)


def _attention_prompt(qt, kc, vct, ksel, vselt, kwin, vwint, gatet, tables, batch, seq):
    near, dtab, lane_m, xt = tables
    n = batch * seq
    tpb = seq // TQ
    nc = seq // CMP_BLOCK
    per_batch = lambda rows: pl.BlockSpec((rows, LANES), lambda i: (i // tpb, 0))
    chan = lambda cols: pl.BlockSpec((None, LANES, cols), lambda i: (i // tpb, 0, 0))
    full = lambda a: pl.BlockSpec(a.shape, lambda i: (0,) * a.ndim)
    return pl.pallas_call(
        functools.partial(_attn_kernel, tiles_per_batch=tpb),
        grid=(n // TQ,),
        in_specs=[pl.BlockSpec((N_HEADS * LANES, TQ), lambda i: (0, i)),
                  per_batch(nc), chan(nc), full(dtab), full(lane_m),
                  per_batch(seq), chan(seq), per_batch(seq), chan(seq),
                  full(xt), full(near), pl.BlockSpec((LANES, TQ), lambda i: (0, i))],
        out_specs=pl.BlockSpec((TQ, ATT_WIDTH), lambda i: (i, 0)),
        out_shape=jax.ShapeDtypeStruct((n, ATT_WIDTH), BF16),
        scratch_shapes=[pltpu.VMEM((2 * LANES, N_HEADS * TQ), BF16),
                        pltpu.VMEM((TK, N_HEADS * TQ), F32), pltpu.VMEM((TK, N_HEADS * TQ), F32),
                        pltpu.VMEM((TK, N_HEADS * TQ), BF16),
                        pltpu.VMEM((1, N_HEADS * TQ), F32), pltpu.VMEM((1, N_HEADS * TQ), F32),
                        pltpu.VMEM((1, N_HEADS * TQ), F32),
                        pltpu.VMEM((N_KV, HEAD_DIM, REP * TQ), F32), pltpu.VMEM((N_KV, HEAD_DIM, REP * TQ), F32)],
        compiler_params=_cparams(("arbitrary",)),
        name="nsa_prompt",
    )(qt, kc, vct, dtab, lane_m, ksel, vselt, kwin, vwint, xt, near, gatet)


def _block_onehot_rows(seq):
    j = np.arange(seq)[:, None] // SEL_BLOCK
    return jnp.asarray((j == np.arange(LANES)[None, :]).astype(np.float32)).astype(BF16)


def _even_odd(x, nc):
    x = x.reshape(-1, nc // 2, 2, x.shape[-1])
    return jnp.concatenate([x[:, :, 0], x[:, :, 1]], axis=1).reshape(-1, x.shape[-1])


def _sample_bias_tables(rel_bias, past_len, t_new, n_win):
    t = np.arange(t_new)[:, None]
    nc = past_len // CMP_BLOCK
    tb_full = lambda dist, ok: jnp.where(jnp.asarray(ok), _bucket_values(rel_bias.astype(F32).T, dist), NEG)
    dist_c = past_len + t - (CMP_BLOCK * np.arange(nc)[None, :] + CMP_BLOCK - 1)
    dist_s = past_len + t - np.arange(past_len)[None, :]
    jn = np.arange(LANES)[None, :]
    dist_n = t - jn
    dist_w = n_win + t - np.arange(n_win)[None, :]
    rows = lambda x: x.reshape(N_HEADS * t_new, x.shape[-1])
    return (rows(tb_full(dist_c, dist_c >= 0)), rows(tb_full(dist_s, dist_s >= 0)),
            rows(tb_full(dist_n, (dist_n >= 0) & (jn < t_new))),
            rows(tb_full(dist_w, (dist_w >= 0) & (dist_w < WINDOW))))


def _attn_sample_kernel(pt_ref, *refs, n_pages, t_new):
    del pt_ref
    kvc_refs = refs[:n_pages]
    sel_refs = refs[n_pages:2 * n_pages]
    (qq_ref, seln_ref, winn_ref, winb_ref, gate_ref, xt_ref, xtn_ref, bc_ref, bs_ref, bn_ref, bw_ref,
     rsum_ref, pair_ref, o_ref, qs_ref, out_ref, s_ref, p_ref) = refs[2 * n_pages:]

    def gate_col(br):
        return jnp.concatenate([gate_ref[:, h * 3 + br:h * 3 + br + 1] for h in range(N_HEADS)], axis=0)

    def softmax_tiles(q, tiles):
        off = 0
        for k, _, bias, channel_major, width in tiles:
            if channel_major:
                s = jnp.dot(q, k(), preferred_element_type=F32)
            else:
                s = lax.dot_general(q, k(), _NT, preferred_element_type=F32)
            s_ref[:, off:off + width] = s + bias()
            off += width
        s = s_ref[:, :off]
        p = jnp.exp(s - jnp.max(s, axis=1, keepdims=True))
        den = jnp.sum(p, axis=1, keepdims=True)
        p_ref[:, :off] = p.astype(BF16)
        acc, off = None, 0
        for _, v, _, channel_major, width in tiles:
            p_t = p_ref[:, off:off + width]
            if channel_major:
                pv = lax.dot_general(p_t, v(), _NT, preferred_element_type=F32)
            else:
                pv = jnp.dot(p_t, v(), preferred_element_type=F32)
            acc = pv if acc is None else acc + pv
            off += width
        return acc / den

    qs_ref[:, :LANES] = qq_ref[...]

    kvc = jnp.concatenate([r[...] for r in kvc_refs], axis=0)
    s = lax.dot_general(qq_ref[...], kvc[:, :LANES].astype(BF16), _NT, preferred_element_type=F32) + bc_ref[...]
    mx = jnp.max(s, axis=1, keepdims=True)
    p = jnp.where(s > 0.1 * NEG, jnp.exp(s - mx), 0.0)
    pn = p / jnp.maximum(jnp.sum(p, axis=1, keepdims=True), 1e-30)
    out_ref[...] = gate_col(0) * jnp.dot(pn.astype(BF16), kvc[:, LANES:].astype(BF16), preferred_element_type=F32)

    hp = lax.Precision.HIGHEST
    imp = jnp.dot(jnp.dot(rsum_ref[...], pn, precision=hp, preferred_element_type=F32), pair_ref[...],
                  precision=hp, preferred_element_type=F32)
    cur = (n_pages * PAGE_SIZE) // SEL_BLOCK
    neg = _select_blocks(imp.T, cur, unroll=True).T
    qs_ref[:, LANES:] = jnp.concatenate(
        [neg[(h // REP) * t_new:(h // REP + 1) * t_new, :] for h in range(N_HEADS)], axis=0).astype(BF16)

    def past_tile(ref, j):
        cols = slice(j * PAGE_SIZE, (j + 1) * PAGE_SIZE)
        return (lambda: jnp.concatenate([ref[:LANES, :].astype(BF16), xt_ref[:, cols]], axis=0),
                lambda: ref[LANES:, :].astype(BF16), lambda: bs_ref[:, cols], True, PAGE_SIZE)

    new_sel_tile = (lambda: jnp.concatenate([seln_ref[:, :LANES].astype(BF16), xtn_ref[...]], axis=1),
                    lambda: seln_ref[:, LANES:].astype(BF16), lambda: bn_ref[...], False, PAGE_SIZE)
    tiles = [past_tile(sel_refs[j], j) for j in range(n_pages)] + [new_sel_tile]
    out_ref[...] += gate_col(1) * softmax_tiles(qs_ref[...], tiles)

    tiles = [(lambda: winb_ref[:LANES, :].astype(BF16), lambda: winb_ref[LANES:, :].astype(BF16), lambda: bw_ref[...],
              True, winb_ref.shape[1]),
             (lambda: winn_ref[:, :LANES].astype(BF16), lambda: winn_ref[:, LANES:].astype(BF16), lambda: bn_ref[...],
              False, PAGE_SIZE)]
    out_ref[...] += gate_col(2) * softmax_tiles(qq_ref[...], tiles)

    lane_o = lax.broadcasted_iota(jnp.int32, (t_new, LANES), 1)
    for i in range(N_HEADS // 2):
        left = out_ref[2 * i * t_new:(2 * i + 1) * t_new, :]
        right = out_ref[(2 * i + 1) * t_new:(2 * i + 2) * t_new, :]
        if (2 * i) // REP == 0:
            right = pltpu.roll(right, HEAD_DIM, 1)
        else:
            left = pltpu.roll(left, HEAD_DIM, 1)
        o_ref[:, i * LANES:(i + 1) * LANES] = jnp.where(lane_o < HEAD_DIM, left, right)


def _attention_sample(page_table, kvc_pages, sel_cache, layer, qq, sel_new, win_new, win_buf, gates, tables, t_new):
    n_seq, n_pages = page_table.shape
    xt, xtn, bc, bs, bn, bw, rsum, pair = tables
    n_win = win_buf.shape[3]
    nq = N_HEADS * t_new
    page_spec = lambda shape, j: pl.BlockSpec((None, None) + shape, lambda b, pt, j=j: (layer, pt[b, j], 0, 0))
    full = lambda a: pl.BlockSpec(a.shape, lambda b, pt: (0,) * a.ndim)
    new = pl.BlockSpec((None, PAGE_SIZE, KV_WIDTH), lambda b, pt: (b, 0, 0))
    in_specs = ([page_spec((PAGE_SIZE // CMP_BLOCK, KV_WIDTH), j) for j in range(n_pages)]
                + [page_spec((KV_WIDTH, PAGE_SIZE), j) for j in range(n_pages)]
                + [pl.BlockSpec((None, nq, LANES), lambda b, pt: (b, 0, 0)), new, new,
                   pl.BlockSpec((None, None, KV_WIDTH, n_win), lambda b, pt: (layer, b, 0, 0)),
                   pl.BlockSpec((t_new, LANES), lambda b, pt: (b, 0))]
                + [full(a) for a in (xt, xtn, bc, bs, bn, bw, rsum, pair)])
    return pl.pallas_call(
        functools.partial(_attn_sample_kernel, n_pages=n_pages, t_new=t_new),
        grid_spec=pltpu.PrefetchScalarGridSpec(
            num_scalar_prefetch=1, grid=(n_seq,), in_specs=in_specs,
            out_specs=pl.BlockSpec((t_new, ATT_WIDTH), lambda b, pt: (b, 0)),
            scratch_shapes=[pltpu.VMEM((nq, 2 * LANES), BF16), pltpu.VMEM((nq, LANES), F32),
                            pltpu.VMEM((nq, (n_pages + 1) * PAGE_SIZE), F32),
                            pltpu.VMEM((nq, (n_pages + 1) * PAGE_SIZE), BF16)]),
        out_shape=jax.ShapeDtypeStruct((n_seq * t_new, ATT_WIDTH), F32),
        compiler_params=_cparams(("arbitrary",)),
        name="nsa_sample",
    )(page_table, *([kvc_pages] * n_pages), *([sel_cache] * n_pages), qq, sel_new, win_new, win_buf, gates,
      xt, xtn, bc, bs, bn, bw, rsum, pair)


def _sample_sum_matrices(t_new, nc):
    rsum = np.zeros((LANES, N_HEADS * t_new), np.float32)
    for h in range(N_HEADS):
        for t in range(t_new):
            rsum[(h // REP) * t_new + t, h * t_new + t] = 1.0
    pair = np.zeros((nc, LANES), np.float32)
    pair[np.arange(nc), np.arange(nc) // 2] = 1.0
    return jnp.asarray(rsum), jnp.asarray(pair)


def _rank_before(vals, k):
    r = jnp.zeros(vals[k].shape, jnp.int32)
    for j, vj in enumerate(vals):
        if j < k:
            r = r + jnp.where(vj >= vals[k], 1, 0)
        elif j > k:
            r = r + jnp.where(vj > vals[k], 1, 0)
    return r


def _route(s_rows, b_rows):
    scores = []
    for g in range(N_GROUPS):
        b0, b1, b2, b3 = b_rows[4 * g:4 * g + 4]
        hi01, lo01, hi23, lo23 = jnp.maximum(b0, b1), jnp.minimum(b0, b1), jnp.maximum(b2, b3), jnp.minimum(b2, b3)
        top1 = jnp.maximum(hi01, hi23)
        top2 = jnp.maximum(jnp.maximum(lo01, lo23), jnp.minimum(hi01, hi23))
        scores.append(top1 + top2)
    in_group = [_rank_before(scores, g) == 0 for g in range(N_GROUPS)]

    def pick(rows, k):
        out = rows[4 * (N_GROUPS - 1) + k]
        for g in range(N_GROUPS - 2, -1, -1):
            out = jnp.where(in_group[g], rows[4 * g + k], out)
        return out

    bv = [pick(b_rows, k) for k in range(EXPERTS_PER_GROUP)]
    sv = [pick(s_rows, k) for k in range(EXPERTS_PER_GROUP)]
    w = [jnp.where(_rank_before(bv, k) < 2, sv[k], 0.0) for k in range(EXPERTS_PER_GROUP)]
    den = (w[0] + w[1]) + (w[2] + w[3])
    return [jnp.where(in_group[e // 4], w[e % 4] / den, 0.0) for e in range(N_EXPERTS)]


def _outproj_kernel(ps_ref, att_ref, x_ref, mod_ref, wout_ref, lng_ref, lnb_ref, rwt_ref, rb_ref,
                    x1_ref, h2_ref, g_ref):
    half = ps_ref.shape[1]
    mix = (jnp.dot(ps_ref[...], wout_ref[:half, :], preferred_element_type=F32)
           + jnp.dot(att_ref[...], wout_ref[half:, :], preferred_element_type=F32))
    x1 = _layer_norm(ALPHA * x_ref[...] + (1.0 + mod_ref[2]) * mix, lng_ref[...], lnb_ref[...])
    x1_ref[...] = x1
    h2 = x1 * (1.0 + mod_ref[4]) + mod_ref[3]
    h2b = h2.astype(BF16)
    h2_ref[...] = h2b
    st = jax.nn.sigmoid(lax.dot_general(rwt_ref[...].astype(BF16), h2b, _NT, preferred_element_type=F32))
    s_rows = [st[e:e + 1, :] for e in range(N_EXPERTS)]
    b_rows = [s_rows[e] + rb_ref[e:e + 1, :] for e in range(N_EXPERTS)]
    gt = jnp.concatenate(_route(s_rows, b_rows) + [jnp.zeros((LANES - N_EXPERTS, st.shape[1]), F32)], axis=0)
    g_ref[...] = gt.T


def _out_proj(ps, att, x, mod, wout, ln_g, ln_b, rwt, rb, tm, tiles_per_batch):
    n = x.shape[0]
    row = lambda wd: pl.BlockSpec((tm, wd), lambda i: (i, 0))
    full = lambda a: pl.BlockSpec(a.shape, lambda i: (0,) * a.ndim)
    return pl.pallas_call(
        _outproj_kernel,
        grid=(n // tm,),
        in_specs=[row(512), row(512), row(D_MODEL), _mod_spec(mod, tm, tiles_per_batch),
                  full(wout), full(ln_g), full(ln_b), full(rwt), full(rb)],
        out_specs=[row(D_MODEL), row(D_MODEL), row(LANES)],
        out_shape=[jax.ShapeDtypeStruct((n, D_MODEL), F32), jax.ShapeDtypeStruct((n, D_MODEL), BF16),
                   jax.ShapeDtypeStruct((n, LANES), F32)],
        compiler_params=_cparams(("arbitrary",)),
        name="out_proj",
    )(ps, att, x, mod, wout, ln_g, ln_b, rwt, rb)


def _moe_kernel(h2_ref, g_ref, wg_ref, wu_ref, wd_ref, x1_ref, mod_ref, lng_ref, lnb_ref, o_ref, acc_ref):
    e = pl.program_id(1)

    @pl.when(e == 0)
    def _():
        acc_ref[...] = jnp.zeros_like(acc_ref)

    x = h2_ref[...]
    gate = jnp.dot(x, wg_ref[0, 0], preferred_element_type=F32)
    up = jnp.dot(x, wu_ref[0, 0], preferred_element_type=F32)
    act = (gate * jax.nn.sigmoid(gate) * up).astype(BF16)
    y = jnp.dot(act, wd_ref[0, 0], preferred_element_type=F32)
    lane = lax.broadcasted_iota(jnp.int32, g_ref.shape, 1)
    w = jnp.sum(jnp.where(lane == e, g_ref[...], 0.0), axis=1, keepdims=True)
    acc_ref[...] += y * w

    @pl.when(e == pl.num_programs(1) - 1)
    def _():
        o_ref[...] = _layer_norm(ALPHA * x1_ref[...] + (1.0 + mod_ref[5]) * acc_ref[...], lng_ref[...], lnb_ref[...])


def _moe(h2, gates, wg, wu, wd, layer, x1, mod, ln_g, ln_b, tm, tiles_per_batch):
    n = h2.shape[0]
    row = lambda wd_: pl.BlockSpec((tm, wd_), lambda i, e: (i, 0))
    full = lambda a: pl.BlockSpec(a.shape, lambda i, e: (0,) * a.ndim)
    wspec = lambda a: pl.BlockSpec((1, 1) + a.shape[2:], lambda i, e: (layer, e, 0, 0))
    return pl.pallas_call(
        _moe_kernel,
        grid=(n // tm, N_EXPERTS),
        in_specs=[row(D_MODEL), row(LANES), wspec(wg), wspec(wu), wspec(wd), row(D_MODEL),
                  _mod_spec(mod, tm, tiles_per_batch), full(ln_g), full(ln_b)],
        out_specs=row(D_MODEL),
        out_shape=jax.ShapeDtypeStruct((n, D_MODEL), F32),
        scratch_shapes=[pltpu.VMEM((tm, D_MODEL), F32)],
        compiler_params=_cparams(("arbitrary", "arbitrary")),
        name="moe",
    )(h2, gates, wg, wu, wd, x1, mod, ln_g, ln_b)


TM_PROJ = 256
TM_MOE = 512


def _channel_mix(ps, att, x, mod, lw, sw, layer, tiles_per_seq):
    x1, h2, gts = _out_proj(ps, att, x, mod, lw["wout"], lw["ln_g0"], lw["ln_b0"], sw["rwt"], sw["rb"],
                            TM_PROJ, tiles_per_seq(TM_PROJ))
    return _moe(h2, gts, sw["wg"], sw["wu"], sw["wd"], layer, x1, mod, lw["ln_g1"], lw["ln_b1"],
                TM_MOE, tiles_per_seq(TM_MOE))


def _prompt_layer(x, mod, lw, sw, layer, depth, kv_state, batch, seq):
    tiles = lambda tm: seq // tm
    (p, u, vn, q_t, cmp_k, cmp_v, cmp_t, sel_t, win_t, gate_t, ksel, vsel_t, kwin, vwin_t) = _in_proj(
        x, mod, lw["w_proj"], lw["sgu_ln_g"], lw["sgu_ln_b"], TM_PROJ, tiles(TM_PROJ), batch=batch, wq_t=lw["wq_t"],
        layer=layer, depth=depth, state=kv_state)
    ps = _mixers(p, u, vn, *lw["mix_prompt"], tiles_per_batch=tiles(SGU_CHUNK))
    nc = seq // CMP_BLOCK
    kvc = _compress(cmp_k, cmp_v, sw["pe_rows"], sw["w1bd"], sw["w2bd"], nc, layer)
    kc = _even_odd(kvc[:, :LANES].astype(BF16), nc)
    vc_t = jnp.transpose(_even_odd(kvc[:, LANES:].astype(BF16), nc).reshape(batch, nc, LANES), (0, 2, 1))
    att = _attention_prompt(q_t, kc, vc_t, ksel, vsel_t, kwin, vwin_t, gate_t, sw["prompt_tables"], batch, seq)
    x2 = _channel_mix(ps, att, x, mod, lw, sw, layer, tiles)
    return x2, (cmp_t, sel_t, win_t), p


def _sample_layer(x, mod, lw, sw, layer, page_table, kvc_pages, sel_cache, win_state, pool_state, n_seq, t_new):
    tiles = lambda tm: 1
    (p, u, vn, qq, cmp_raw, sel_raw, win_raw, gates) = _in_proj(
        x, mod, lw["w_proj"], lw["sgu_ln_g"], lw["sgu_ln_b"], TM_PROJ, 1)
    p_ext = jnp.concatenate([pool_state[layer], p.reshape(n_seq, t_new, POOL_WIDTH)], axis=1)
    hist = jnp.pad(p_ext, ((0, 0), (3 * t_new - p_ext.shape[1], 0), (0, 0)))
    chunks = [hist[:, k * t_new:(k + 1) * t_new].reshape(n_seq * t_new, POOL_WIDTH) for k in range(3)]
    ps = _mixers(chunks, u, vn, *lw["mix_sample"], t_new=t_new)
    q_rows = jnp.transpose(qq.reshape(N_HEADS, n_seq, t_new, LANES), (1, 0, 2, 3)).reshape(n_seq, N_HEADS * t_new, LANES)
    new_page = lambda raw: jnp.pad(raw.reshape(n_seq, t_new, KV_WIDTH), ((0, 0), (0, PAGE_SIZE - t_new), (0, 0)))
    att = _attention_sample(page_table, kvc_pages, sel_cache, layer, q_rows, new_page(sel_raw), new_page(win_raw),
                            win_state, gates, sw["sample_tables"], t_new)
    x2 = _channel_mix(ps, att.astype(BF16), x, mod, lw, sw, layer, tiles)
    win_new_t = jnp.transpose(win_raw.reshape(n_seq, t_new, KV_WIDTH), (0, 2, 1))
    return x2, (cmp_raw, sel_raw, win_new_t, p_ext[:, p_ext.shape[1] - POOL_BUF:], vn)


def kernel(x_prompt, x_sample, cache_cmp_kv, cache_sel_kv, state_win_kv, state_pool, page_table, c_prompt, c_sample,
           w_in, w_out, pool_w, pool_scale, sgu_ln_g, sgu_ln_b, sgu_w, sgu_b, cmp_pe, cmp_w1, cmp_w2, rel_bias,
           w_mod, b_mod, ln_g, ln_b, router_w, router_b, moe_w_gate, moe_w_up, moe_w_down):
    batch, seq, d = x_prompt.shape
    n_seq, t_new, _ = x_sample.shape
    depth = w_in.shape[0]
    n_pages = page_table.shape[1]
    past_len = n_pages * PAGE_SIZE
    n_phys = cache_cmp_kv.shape[1]
    n_win = state_win_kv.shape[2]
    assert seq // SEL_BLOCK == LANES and seq % TM_MOE == 0 and (n_seq * t_new) % TM_MOE == 0
    assert POOL_BUF + 1 == 2 * t_new and past_len % TK == 0

    n_c = batch + n_seq
    c_all = jnp.pad(jnp.concatenate([c_prompt, c_sample], axis=0), ((0, -n_c % 8), (0, 0)))
    m_all = _modulation(c_all, w_mod, b_mod)
    mod_p = m_all[:, :batch].reshape(depth, batch, 6, 1, d)
    mod_s = jnp.transpose(jnp.repeat(m_all[:, batch:n_c].reshape(depth, n_seq, 6, d), t_new, axis=1), (0, 2, 1, 3))

    pe_rows, w1bd, w2bd = _prep_compress_weights(cmp_pe, cmp_w1, cmp_w2)
    near, dtab, lane_m = _prompt_bias_tables(rel_bias)
    nc_past = past_len // CMP_BLOCK
    shared = {
        "pe_rows": pe_rows, "w1bd": w1bd, "w2bd": w2bd,
        "prompt_tables": (near, dtab, lane_m, _block_onehot_rows(seq)),
        "sample_tables": (_block_onehot_rows(past_len + PAGE_SIZE).T, _block_onehot_rows(past_len + PAGE_SIZE)[past_len:],
                          *_sample_bias_tables(rel_bias, past_len, t_new, n_win),
                          *_sample_sum_matrices(t_new, nc_past)),
        "rwt": router_w.T, "rb": router_b.reshape(N_EXPERTS, 1),
        "wg": moe_w_gate.astype(BF16), "wu": moe_w_up.astype(BF16), "wd": moe_w_down.astype(BF16),
    }
    layers = []
    for l in range(depth):
        layers.append({
            "w_proj": _prep_w_proj(w_in[l]), "wq_t": _prep_w_proj(w_in[l])[:, C_Q:].T, "wout": w_out[l].astype(BF16),
            "sgu_ln_g": sgu_ln_g[l].reshape(1, -1), "sgu_ln_b": sgu_ln_b[l].reshape(1, -1),
            "mix_prompt": _prep_mixer_weights(pool_w[l], pool_scale[l], sgu_w[l], sgu_b[l], SGU_CHUNK),
            "mix_sample": _prep_mixer_weights(pool_w[l], pool_scale[l], sgu_w[l], sgu_b[l], t_new),
            "ln_g0": ln_g[l, 0].reshape(1, d), "ln_b0": ln_b[l, 0].reshape(1, d),
            "ln_g1": ln_g[l, 1].reshape(1, d), "ln_b1": ln_b[l, 1].reshape(1, d),
        })

    chan_major = lambda x: jnp.transpose(x, (0, 1, 3, 4, 5, 2)).reshape(x.shape[0], x.shape[1], KV_WIDTH, x.shape[2])
    pe_pages = jnp.broadcast_to(jnp.transpose(cmp_pe, (0, 1, 3, 2))[:, :, None, :, None, :],
                                (depth, 2, N_KV, HEAD_DIM, PAGE_SIZE // CMP_BLOCK, CMP_BLOCK))
    kvc_pages = _compress_pages(chan_major(cache_cmp_kv), pe_pages.reshape(depth, KV_WIDTH, PAGE_SIZE), w1bd, w2bd, 64)
    kvc_pages = kvc_pages.reshape(depth, n_phys, PAGE_SIZE // CMP_BLOCK, KV_WIDTH)
    sel_cache = chan_major(cache_sel_kv)
    win_state = chan_major(state_win_kv)

    xp = x_prompt.reshape(batch * seq, d)
    xs = x_sample.reshape(n_seq * t_new, d)
    pool_p, outs_s, kv_state = [], [], None
    for l in range(depth):
        xp, kv_state, p_l = _prompt_layer(xp, mod_p[l], layers[l], shared, l, depth, kv_state, batch, seq)
        pool_p.append(p_l)
        xs, st = _sample_layer(xs, mod_s[l], layers[l], shared, l, page_table, kvc_pages, sel_cache, win_state,
                               state_pool, n_seq, t_new)
        outs_s.append(st)

    kv5 = lambda x, b: x.reshape(b, -1, 2, N_KV, HEAD_DIM)
    kv6_t = lambda x: jnp.transpose(x.reshape(x.shape[:2] + (2, N_KV, HEAD_DIM, x.shape[3])), (0, 1, 5, 2, 3, 4))
    stack = lambda xs_: jnp.stack(xs_)
    w_keep = min(WINDOW, seq)
    cmp_t, sel_t, win_t = kv_state
    new_win_t = jnp.concatenate([win_state[:, :, :, t_new:], stack([o[2] for o in outs_s])], axis=3)
    return (xp.reshape(batch, seq, d), xs.reshape(n_seq, t_new, d),
            kv6_t(cmp_t), stack([kv5(o[0], n_seq) for o in outs_s]),
            kv6_t(sel_t), stack([kv5(o[1], n_seq) for o in outs_s]),
            kv6_t(win_t[:, :, :, seq - w_keep:]), kv6_t(new_win_t),
            stack([p_l.reshape(batch, seq, POOL_WIDTH)[:, seq - POOL_BUF:] for p_l in pool_p]),
            stack([o[3] for o in outs_s]),
            stack([o[4].reshape(n_seq, t_new, SGU_WIDTH) for o in outs_s]))
```

```python
import functools
import math

import numpy as np
import jax
import jax.numpy as jnp
from jax import lax
from jax.experimental import pallas as pl
from jax.experimental.pallas import tpu as pltpu

F32 = jnp.float32
BF16 = jnp.bfloat16

D_MODEL = 1024
POOL_WIDTH = 256
SGU_WIDTH = 256
ATT_WIDTH = 512
POOL_WINDOWS = (2, 4, 8, 16)
POOL_GROUP_DIM = 64
POOL_BUF = 15
SGU_GROUPS = 4
SGU_CHUNK = 128
N_HEADS = 8
HEAD_DIM = 64
N_KV = 2
REP = 4
CMP_BLOCK = 32
CMP_HIDDEN = 128
SEL_BLOCK = 64
N_SEL = 16
WINDOW = 512
N_BUCKETS = 32
REL_MAX_DIST = 128
N_EXPERTS = 16
N_GROUPS = 4
EXPERTS_PER_GROUP = 4
D_EXPERT = 512
DEPTH = 2
ALPHA = (2 * DEPTH) ** 0.25
LN_EPS = 1e-5
FORCED_SCORE = 1e4
NEG = -1e30
LOG2E = math.log2(math.e)
PAGE_SIZE = 128

KV_WIDTH = 2 * N_KV * HEAD_DIM
LANES = 128
VMEM_LIMIT = 56 * 1024 * 1024

C_P, C_U, C_V, C_CMP, C_SEL, C_WIN, C_GATE, C_Q = 0, 256, 512, 768, 1024, 1280, 1536, 1664
W_PROJ = C_Q + N_HEADS * LANES

TQ = 256
TK = 256
NEAR_BLOCKS = 12
FAR_BUCKET_DIST = 113


def _cparams(sem):
    return pltpu.CompilerParams(dimension_semantics=sem, vmem_limit_bytes=VMEM_LIMIT)


def _layer_norm(x, g, b):
    mu = jnp.mean(x, axis=-1, keepdims=True)
    xc = x - mu
    var = jnp.mean(xc * xc, axis=-1, keepdims=True)
    return xc * lax.rsqrt(var + LN_EPS) * g + b


def _mod_kernel(c_ref, w_ref, b_ref, o_ref):
    c = c_ref[...]
    a = (c * jax.nn.sigmoid(c)).astype(BF16)
    o_ref[0] = jnp.dot(a, w_ref[0].astype(BF16), preferred_element_type=F32) + b_ref[0]


def _modulation(c_all, w_mod, b_mod):
    n, d = c_all.shape
    depth, _, w = w_mod.shape
    tn = 1536
    return pl.pallas_call(
        _mod_kernel,
        grid=(depth, w // tn),
        in_specs=[pl.BlockSpec((n, d), lambda l, j: (0, 0)),
                  pl.BlockSpec((1, d, tn), lambda l, j: (l, 0, j)),
                  pl.BlockSpec((1, 1, tn), lambda l, j: (l, 0, j))],
        out_specs=pl.BlockSpec((1, n, tn), lambda l, j: (l, 0, j)),
        out_shape=jax.ShapeDtypeStruct((depth, n, w), F32),
        compiler_params=_cparams(("arbitrary", "arbitrary")),
        name="adaln_mod",
    )(c_all, w_mod, b_mod.reshape(depth, 1, w))


def _mod_spec(mod, tm, tiles_per_batch):
    if mod.ndim == 4:
        return pl.BlockSpec((None, 6, 1, D_MODEL), lambda i, *_: (i // tiles_per_batch, 0, 0, 0))
    return pl.BlockSpec((6, tm, D_MODEL), lambda i, *_: (0, i, 0))


def _inproj_kernel(x_ref, mod_ref, w_ref, lng_ref, lnb_ref, *rest, channel_major, n_state):
    h = (x_ref[...] * (1.0 + mod_ref[1]) + mod_ref[0]).astype(BF16)

    def seg(a, b):
        return jnp.dot(h, w_ref[:, a:b], preferred_element_type=F32)

    if channel_major:
        wqt_ref = rest[0]
        p_ref, u_ref, vn_ref, qt_ref = rest[1 + n_state:5 + n_state]
        rest = rest[5 + n_state:]
        qt_ref[...] = lax.dot_general(wqt_ref[...], h, _NT, preferred_element_type=F32).astype(BF16)
    else:
        p_ref, u_ref, vn_ref, qq_ref = rest[:4]
        rest = rest[4:]
        for hd in range(N_HEADS):
            qq_ref[hd] = seg(C_Q + hd * LANES, C_Q + (hd + 1) * LANES).astype(BF16)
    p_ref[...] = seg(C_P, C_U)
    u_ref[...] = seg(C_U, C_V)
    vn_ref[...] = _layer_norm(seg(C_V, C_CMP), lng_ref[...], lnb_ref[...])
    cmp = seg(C_CMP, C_SEL)
    sel = seg(C_SEL, C_WIN)
    win = seg(C_WIN, C_GATE)
    gates = jax.nn.sigmoid(seg(C_GATE, C_Q))
    if not channel_major:
        cmp_ref, sel_ref, win_ref, gate_ref = rest
        cmp_ref[...] = cmp
        sel_ref[...] = sel
        win_ref[...] = win
        gate_ref[...] = gates
        return
    cmpk_ref, cmpv_ref, cmpt_ref, selt_ref, wint_ref, gatet_ref, ksel_ref, vselt_ref, kwin_ref, vwint_ref = rest
    cmpk_ref[...] = cmp[:, :LANES]
    cmpv_ref[...] = cmp[:, LANES:]
    cmpt_ref[...] = cmp.T
    sel_t = sel.T
    selt_ref[...] = sel_t
    ksel_ref[...] = sel[:, :LANES].astype(BF16)
    vselt_ref[...] = sel_t[LANES:, :].astype(BF16)
    win_t = win.T
    wint_ref[...] = win_t
    kwin_ref[...] = win[:, :LANES].astype(BF16)
    vwint_ref[...] = win_t[LANES:, :].astype(BF16)
    gatet_ref[...] = gates.T


def _in_proj(x, mod, w_proj, ln_g, ln_b, tm, tiles_per_batch, batch=None, wq_t=None, layer=0, depth=1, state=None):
    n = x.shape[0]
    row = lambda wd: pl.BlockSpec((tm, wd), lambda i: (i, 0))
    full = lambda a: pl.BlockSpec(a.shape, lambda i: (0,) * a.ndim)
    f32o = lambda wd: jax.ShapeDtypeStruct((n, wd), F32)
    out_specs = [row(256), row(256), row(256)]
    out_shape = [f32o(256), f32o(256), f32o(256)]
    ins, in_specs = [x, mod, w_proj, ln_g, ln_b], [row(D_MODEL), _mod_spec(mod, tm, tiles_per_batch), full(w_proj),
                                                    full(ln_g), full(ln_b)]
    aliases = {}
    if batch is None:
        out_specs += [pl.BlockSpec((N_HEADS, tm, LANES), lambda i: (0, i, 0)), row(256), row(256), row(256), row(LANES)]
        out_shape += [jax.ShapeDtypeStruct((N_HEADS, n, LANES), BF16), f32o(256), f32o(256), f32o(256), f32o(LANES)]
    else:
        ins.append(wq_t)
        in_specs.append(full(wq_t))
        out_specs.append(pl.BlockSpec((N_HEADS * LANES, tm), lambda i: (0, i)))
        out_shape.append(jax.ShapeDtypeStruct((N_HEADS * LANES, n), BF16))
        seq = n // batch
        chan = lambda c: pl.BlockSpec((None, c, tm), lambda i: (i // tiles_per_batch, 0, i % tiles_per_batch))
        chan_o = lambda c, dt: jax.ShapeDtypeStruct((batch, c, seq), dt)
        kv_state = pl.BlockSpec((None, None, KV_WIDTH, tm),
                                lambda i: (layer, i // tiles_per_batch, 0, i % tiles_per_batch))
        kv_state_o = jax.ShapeDtypeStruct((depth, batch, KV_WIDTH, seq), F32)
        first_state_out = len(out_specs) + 2
        out_specs += [row(LANES), row(LANES), kv_state, kv_state, kv_state, pl.BlockSpec((LANES, tm), lambda i: (0, i)),
                      row(LANES), chan(LANES), row(LANES), chan(LANES)]
        out_shape += [f32o(LANES), f32o(LANES),
                      kv_state_o, kv_state_o, kv_state_o, jax.ShapeDtypeStruct((LANES, n), F32),
                      jax.ShapeDtypeStruct((n, LANES), BF16), chan_o(LANES, BF16),
                      jax.ShapeDtypeStruct((n, LANES), BF16), chan_o(LANES, BF16)]
        if state is not None:
            aliases = {len(ins) + k: first_state_out + k for k in range(len(state))}
            ins += list(state)
            in_specs += [pl.BlockSpec(memory_space=pl.ANY)] * len(state)
    return pl.pallas_call(
        functools.partial(_inproj_kernel, channel_major=batch is not None, n_state=len(state or ())),
        grid=(n // tm,),
        in_specs=in_specs,
        out_specs=out_specs,
        out_shape=out_shape,
        input_output_aliases=aliases,
        compiler_params=_cparams(("arbitrary",)),
        name="in_proj",
    )(*ins)


def _prep_w_proj(w_in):
    d = w_in.shape[0]
    q = w_in[:, 768:1280].reshape(d, N_KV, REP, HEAD_DIM) * (HEAD_DIM ** -0.5)
    qq = jnp.zeros((d, N_KV, REP, N_KV, HEAD_DIM), w_in.dtype)
    for g in range(N_KV):
        qq = qq.at[:, g, :, g, :].set(q[:, g])
    gate = jnp.pad(w_in[:, 2048:2072], ((0, 0), (0, LANES - 24)))
    cols = [w_in[:, 0:768], w_in[:, 1280:2048], gate, qq.reshape(d, N_HEADS * LANES)]
    return jnp.concatenate(cols, axis=1).astype(BF16)


def _window_sums(shifted):
    acc = shifted(0)
    sums = {}
    for k in range(1, 16):
        acc = acc + shifted(k)
        if k + 1 in POOL_WINDOWS:
            sums[k + 1] = acc
    lane = lax.broadcasted_iota(jnp.int32, acc.shape, 1)
    return jnp.where(lane < 64, sums[2], jnp.where(lane < 128, sums[4], jnp.where(lane < 192, sums[8], sums[16])))


def _mixers_tail(sums, cnt, cur, u, vn, wpool_ref, pscale_ref, wcat_ref, sbias_ref, o_ref):
    diff = (sums / cnt - cur).astype(BF16)
    pool = jnp.dot(diff, wpool_ref[...], preferred_element_type=F32) * pscale_ref[...]
    lane = lax.broadcasted_iota(jnp.int32, vn.shape, 1)
    vb = vn.astype(BF16)
    zero = jnp.zeros_like(vb)
    stacked = jnp.concatenate([jnp.where((lane // 64) == g, vb, zero) for g in range(SGU_GROUPS)], axis=0)
    mixed = jnp.dot(wcat_ref[...], stacked, preferred_element_type=F32) + sbias_ref[...]
    o_ref[:, :POOL_WIDTH] = pool.astype(BF16)
    o_ref[:, POOL_WIDTH:] = (u * mixed).astype(BF16)


def _lane_window():
    lane = lax.broadcasted_iota(jnp.int32, (SGU_CHUNK, POOL_WIDTH), 1)
    return jnp.where(lane < 64, 2, jnp.where(lane < 128, 4, jnp.where(lane < 192, 8, 16)))


def _mix_prompt_kernel(p_ref, u_ref, vn_ref, wpool_ref, pscale_ref, wcat_ref, sbias_ref, o_ref, prev_ref,
                       *, tiles_per_batch):
    t = pl.program_id(0) % tiles_per_batch

    @pl.when(t == 0)
    def _():
        prev_ref[...] = jnp.zeros_like(prev_ref)

    cur = p_ref[...]
    prev = prev_ref[...]
    row = lax.broadcasted_iota(jnp.int32, cur.shape, 0)

    def shifted(k):
        if k == 0:
            return cur
        return jnp.where(row >= k, pltpu.roll(cur, k, 0), pltpu.roll(prev, k, 0))

    sums = _window_sums(shifted)
    prev_ref[...] = cur
    cnt = jnp.minimum(_lane_window(), t * SGU_CHUNK + row + 1).astype(F32)
    _mixers_tail(sums, cnt, cur, u_ref[...], vn_ref[...], wpool_ref, pscale_ref, wcat_ref, sbias_ref, o_ref)


def _mix_sample_kernel(pa_ref, pb_ref, pc_ref, u_ref, vn_ref, wpool_ref, pscale_ref, wcat_ref, sbias_ref, o_ref,
                       *, t_new):
    a, b, c = pa_ref[...], pb_ref[...], pc_ref[...]
    rows = c.shape[0]
    t = lax.broadcasted_iota(jnp.int32, c.shape, 0) % t_new

    def shifted(k):
        if k == 0:
            return c
        hi, lo = (c, b) if k < t_new else (b, a)
        kk = k % t_new
        if kk == 0:
            return hi
        return jnp.where(t >= kk, pltpu.roll(hi, kk, 0), pltpu.roll(lo, rows - t_new + kk, 0))

    sums = _window_sums(shifted)
    cnt = _lane_window().astype(F32)
    _mixers_tail(sums, cnt, c, u_ref[...], vn_ref[...], wpool_ref, pscale_ref, wcat_ref, sbias_ref, o_ref)


def _mixers(p_in, u, vn, wpool, pscale, wcat, sbias, tiles_per_batch=None, t_new=None):
    n = u.shape[0]
    tm = SGU_CHUNK
    row = lambda wd: pl.BlockSpec((tm, wd), lambda i: (i, 0))
    full = lambda a: pl.BlockSpec(a.shape, lambda i: (0,) * a.ndim)
    weights = [wpool, pscale, wcat, sbias]
    if t_new is None:
        kern = functools.partial(_mix_prompt_kernel, tiles_per_batch=tiles_per_batch)
        ins, scratch = [p_in], [pltpu.VMEM((tm, POOL_WIDTH), F32)]
    else:
        kern = functools.partial(_mix_sample_kernel, t_new=t_new)
        ins, scratch = list(p_in), []
    return pl.pallas_call(
        kern,
        grid=(n // tm,),
        in_specs=[row(256)] * (len(ins) + 2) + [full(a) for a in weights],
        out_specs=row(512),
        out_shape=jax.ShapeDtypeStruct((n, 512), BF16),
        scratch_shapes=scratch,
        compiler_params=_cparams(("arbitrary",)),
        name="mixers",
    )(*ins, u, vn, *weights)


def _prep_mixer_weights(pool_w, pool_scale, sgu_w, sgu_b, chunk):
    wpool = jax.scipy.linalg.block_diag(*[pool_w[g] for g in range(len(POOL_WINDOWS))]).astype(BF16)
    tri = jnp.tril(jnp.ones((chunk, chunk), bool))
    w = jnp.where(tri, sgu_w[:, :chunk, :chunk], 0.0)
    reps = SGU_CHUNK // chunk
    eye = jnp.eye(reps, dtype=w.dtype)
    wt = jnp.concatenate([jnp.kron(eye, w[g]) for g in range(SGU_GROUPS)], axis=1).astype(BF16)
    b = jnp.tile(sgu_b[:, :chunk], (1, reps))
    sbias = jnp.repeat(b.T, SGU_WIDTH // SGU_GROUPS, axis=1)
    return wpool, pool_scale.reshape(1, POOL_WIDTH), wt, sbias


def _compress_mlp(token_rows, w1_ref, w2_ref, o_ref, nblk):
    for s in range(2):
        acc = jnp.zeros((nblk, N_KV * CMP_HIDDEN), F32)
        for j in range(0, CMP_BLOCK, 2):
            pair = jnp.concatenate([token_rows(s, j), token_rows(s, j + 1)], axis=1)
            acc = acc + jnp.dot(pair, w1_ref[0, s, j // 2], preferred_element_type=F32)
        hdn = jax.nn.gelu(acc).astype(BF16)
        o_ref[0, :, s * LANES:(s + 1) * LANES] = jnp.dot(hdn, w2_ref[0, s], preferred_element_type=F32)


def _compress_kernel(k_ref, v_ref, pe_ref, w1_ref, w2_ref, o_ref, *, nblk):
    bufs = (k_ref, v_ref)
    rows = lambda s, j: (bufs[s][pl.ds(j, nblk, stride=CMP_BLOCK), :] + pe_ref[0, s, j:j + 1, :]).astype(BF16)
    _compress_mlp(rows, w1_ref, w2_ref, o_ref, nblk)


PAIR_TOKENS = 2 * PAGE_SIZE
PAIR_BLOCKS = PAIR_TOKENS // CMP_BLOCK


def _compress_pages_kernel(raw_ref, pe_ref, perm_ref, w1_ref, w2_ref, o_ref, k_ref, v_ref, *, nblk):
    bufs = (k_ref, v_ref)

    def body(q, carry):
        for s in range(2):
            ch = slice(s * LANES, (s + 1) * LANES)
            x = jnp.concatenate([raw_ref[0, 2 * q, ch, :] + pe_ref[0, ch, :],
                                 raw_ref[0, 2 * q + 1, ch, :] + pe_ref[0, ch, :]], axis=1).astype(BF16)
            t = lax.dot_general(perm_ref[...], x, _NT, preferred_element_type=F32)
            rows = pl.ds(pl.multiple_of(q * PAIR_BLOCKS, PAIR_BLOCKS), PAIR_BLOCKS)
            bufs[s][:, rows, :] = t.reshape(CMP_BLOCK, PAIR_BLOCKS, LANES)
        return carry

    lax.fori_loop(0, raw_ref.shape[1] // 2, body, 0, unroll=4)
    _compress_mlp(lambda s, j: bufs[s][j].astype(BF16), w1_ref, w2_ref, o_ref, nblk)


def _compress_pages(pages, pe_pages, w1bd, w2bd, pages_per_step):
    nl, n_pages, _, _ = pages.shape
    assert n_pages % pages_per_step == 0 and pages_per_step % 2 == 0
    nblk = pages_per_step * (PAGE_SIZE // CMP_BLOCK)
    tok = np.arange(PAIR_TOKENS)
    dest = (tok % CMP_BLOCK) * PAIR_BLOCKS + tok // CMP_BLOCK
    perm = jnp.asarray((np.arange(PAIR_TOKENS)[:, None] == dest[None, :]).astype(np.float32)).astype(BF16)
    wspec = lambda a: pl.BlockSpec((1,) + a.shape[1:], lambda l, i: (l,) + (0,) * (a.ndim - 1))
    return pl.pallas_call(
        functools.partial(_compress_pages_kernel, nblk=nblk),
        grid=(nl, n_pages // pages_per_step),
        in_specs=[pl.BlockSpec((1, pages_per_step, KV_WIDTH, PAGE_SIZE), lambda l, i: (l, i, 0, 0)),
                  wspec(pe_pages), pl.BlockSpec(perm.shape, lambda l, i: (0, 0)), wspec(w1bd), wspec(w2bd)],
        out_specs=pl.BlockSpec((1, nblk, KV_WIDTH), lambda l, i: (l, i, 0)),
        out_shape=jax.ShapeDtypeStruct((nl, n_pages * (PAGE_SIZE // CMP_BLOCK), KV_WIDTH), F32),
        scratch_shapes=[pltpu.VMEM((CMP_BLOCK, nblk, LANES), F32)] * 2,
        compiler_params=_cparams(("arbitrary", "arbitrary")),
        name="compress_pages",
    )(pages, pe_pages, perm, w1bd, w2bd)


def _compress(k_raw, v_raw, pe_rows, w1bd, w2bd, nblk, layer):
    r = k_raw.shape[0]
    assert r % (nblk * CMP_BLOCK) == 0
    wspec = lambda a: pl.BlockSpec((1,) + a.shape[1:], lambda i: (layer,) + (0,) * (a.ndim - 1))
    rows = pl.BlockSpec((nblk * CMP_BLOCK, LANES), lambda i: (i, 0))
    return pl.pallas_call(
        functools.partial(_compress_kernel, nblk=nblk),
        grid=(r // (nblk * CMP_BLOCK),),
        in_specs=[rows, rows, wspec(pe_rows), wspec(w1bd), wspec(w2bd)],
        out_specs=pl.BlockSpec((1, nblk, KV_WIDTH), lambda i: (0, i, 0)),
        out_shape=jax.ShapeDtypeStruct((1, r // CMP_BLOCK, KV_WIDTH), F32),
        compiler_params=_cparams(("arbitrary",)),
        name="compress",
    )(k_raw, v_raw, pe_rows, w1bd, w2bd)[0]


def _prep_compress_weights(cmp_pe, cmp_w1, cmp_w2):
    nl = cmp_pe.shape[0]
    pe_rows = jnp.concatenate([cmp_pe] * N_KV, axis=-1)
    w1 = cmp_w1.reshape(nl, 2, CMP_BLOCK, HEAD_DIM, CMP_HIDDEN)
    eye = jnp.eye(N_KV, dtype=F32)
    w1bd = w1[:, :, :, None, :, None, :] * eye[None, None, None, :, None, :, None]
    w2bd = cmp_w2[:, :, None, :, None, :] * eye[None, None, :, None, :, None]
    return (pe_rows, w1bd.reshape(nl, 2, CMP_BLOCK // 2, 2 * LANES, N_KV * CMP_HIDDEN).astype(BF16),
            w2bd.reshape(nl, 2, N_KV * CMP_HIDDEN, LANES).astype(BF16))


def _np_bucket(dist):
    n = np.maximum(dist, 0)
    nf = np.maximum(n, 1).astype(np.float32)
    large = 16 + (np.log(nf / np.float32(16)) / np.float32(math.log(REL_MAX_DIST / 16)) * np.float32(16)).astype(np.int32)
    return np.where(n < 16, n, np.minimum(large, N_BUCKETS - 1)).astype(np.int32)


def _bucket_values(tbt, dist):
    hit = jnp.asarray(_np_bucket(dist))[None, ..., None] == jnp.arange(N_BUCKETS, dtype=jnp.int32)
    return jnp.sum(jnp.where(hit, tbt.reshape((tbt.shape[0],) + (1,) * dist.ndim + (N_BUCKETS,)), 0.0), axis=-1)


def _bias_minus_far(rel_bias, dist):
    tbt = rel_bias.astype(F32).T
    val = _bucket_values(tbt, dist) - tbt[:, N_BUCKETS - 1].reshape((-1,) + (1,) * dist.ndim)
    return jnp.where(jnp.asarray(dist >= 0), val, NEG)


def _split3(x):
    hi = x.astype(BF16)
    r1 = x - hi.astype(F32)
    mid = r1.astype(BF16)
    lo = (r1 - mid.astype(F32)).astype(BF16)
    return hi, mid, lo


def _prompt_bias_tables(rel_bias):
    a = np.arange(TQ)[:, None]
    span = 2 * TK + TQ
    per_dist = _bias_minus_far(rel_bias, np.arange(span) - (TQ - 1))
    skew = jnp.broadcast_to(per_dist[:, None, :], (N_HEADS, 2 * TK, span)).reshape(N_HEADS, 2 * TK * span)
    skew = skew[:, :2 * TK * (span - 1)].reshape(N_HEADS, 2 * TK, span - 1)
    near = skew[:, :, 2 * TK - 1:2 * TK - 1 + TQ]
    dist_c = a + (4 * CMP_BLOCK - CMP_BLOCK + 1) - CMP_BLOCK * np.arange(NEAR_BLOCKS)[None, :]
    dc = _bias_minus_far(rel_bias, dist_c)
    hi, mid, lo = _split3(dc)
    cols = jnp.stack([hi, mid, lo], axis=-1).reshape(N_HEADS, TQ, 3 * NEAR_BLOCKS)
    future = jnp.full((N_HEADS, TQ, 1), NEG, F32).astype(BF16)
    pad = jnp.zeros((N_HEADS, TQ, LANES - 3 * NEAR_BLOCKS - 1), BF16)
    dtab = jnp.concatenate([cols, future, pad], axis=-1).reshape(N_HEADS * TQ, LANES).T
    lane = np.arange(LANES)
    lane_m = np.where(lane < 3 * NEAR_BLOCKS, lane // 3, -1000).astype(np.int32)
    return near, dtab, jnp.asarray(np.tile(lane_m[None, :], (8, 1)))


_NT = (((1,), (1,)), ((), ()))


def _select_blocks(imp_t, cur, unroll=False):
    blk = lax.broadcasted_iota(jnp.int32, imp_t.shape, 0)
    forced = (blk == 0) | (blk == cur) | (blk == cur - 1)
    vals = jnp.where(forced, FORCED_SCORE, jnp.where(blk <= cur, imp_t, -1.0))

    def body(_, carry):
        vals, neg = carry
        mx = jnp.max(vals, axis=0, keepdims=True)
        first = jnp.min(jnp.where(vals == mx, blk, 1 << 20), axis=0, keepdims=True)
        pick = blk == first
        return jnp.where(pick, -3e38, vals), jnp.where(pick, 0.0, neg)

    _, neg = lax.fori_loop(0, N_SEL, body, (vals, jnp.full(imp_t.shape, NEG, F32)), unroll=unroll)
    return neg


def _attn_kernel(qt_ref, kc_ref, vct_ref, dtabt_ref, lanem_ref, ksel_ref, vselt_ref, kwin_ref, vwint_ref,
                 xt_ref, near_ref, gatet_ref, o_ref, qs_ref, s_ref, s2_ref, p_ref, m_ref, l_ref, alpha_ref, acc_ref, out_ref,
                 *, tiles_per_batch):
    u = pl.program_id(0) % tiles_per_batch
    nc = kc_ref.shape[0]
    ns = nc // 2
    gcols = REP * TQ

    def gate(h, br):
        c = h * 3 + br
        return gatet_ref[c:c + 1, :]

    def gate_row(g, br):
        return jnp.concatenate([gate(g * REP + r, br) for r in range(REP)], axis=1)

    def group_rows(h):
        g = h // REP
        return slice(g * HEAD_DIM, (g + 1) * HEAD_DIM)

    def head_cols(h):
        return slice((h % REP) * TQ, (h % REP + 1) * TQ)

    c = lax.broadcasted_iota(jnp.int32, (nc, LANES), 0)
    lane = lax.broadcasted_iota(jnp.int32, (nc, LANES), 1)
    rel = jnp.where(c < ns, 2 * c, 2 * c - (nc - 1)) - (8 * u - 4)
    near_hit = jnp.where(rel == lanem_ref[0:1, :], 1.0, 0.0)
    future_hit = jnp.where(rel >= NEAR_BLOCKS, 1.0, 0.0)
    onehot = jnp.where(lane < 3 * NEAR_BLOCKS, near_hit, jnp.where(lane == 3 * NEAR_BLOCKS, future_hit, 0.0))
    kk_c = jnp.concatenate([kc_ref[...], onehot.astype(BF16)], axis=1)
    imp = [jnp.zeros((ns, TQ), F32) for _ in range(N_KV)]
    for h in range(N_HEADS):
        qh = jnp.concatenate([qt_ref[h * LANES:(h + 1) * LANES, :], dtabt_ref[:, h * TQ:(h + 1) * TQ]], axis=0)
        s = jnp.dot(kk_c, qh, preferred_element_type=F32)
        mx = jnp.max(s, axis=0, keepdims=True)
        p = jnp.where(s > 0.1 * NEG, jnp.exp(s - mx), 0.0)
        pn = p / jnp.maximum(jnp.sum(p, axis=0, keepdims=True), 1e-30)
        imp[h // REP] = imp[h // REP] + (pn[:ns, :] + pn[ns:, :])
        out_ref[h // REP, :, head_cols(h)] = gate(h, 0) * jnp.dot(vct_ref[group_rows(h), :], pn.astype(BF16),
                                                                  preferred_element_type=F32)

    imp_t = jnp.concatenate(imp, axis=1)
    a = lax.broadcasted_iota(jnp.int32, imp_t.shape, 1) & (TQ - 1)
    cur = (TQ // SEL_BLOCK) * u + (a >> 6)
    neg = _select_blocks(imp_t, cur).astype(BF16)
    for h in range(N_HEADS):
        g = h // REP
        qs_ref[:LANES, h * TQ:(h + 1) * TQ] = (qt_ref[h * LANES:(h + 1) * LANES, :].astype(F32) * LOG2E).astype(BF16)
        qs_ref[LANES:, h * TQ:(h + 1) * TQ] = neg[:, g * TQ:(g + 1) * TQ]

    def reset():
        m_ref[...] = jnp.full(m_ref.shape, -1e38, F32)
        l_ref[...] = jnp.zeros(l_ref.shape, F32)
        acc_ref[...] = jnp.zeros(acc_ref.shape, F32)

    def finish(br):
        for g in range(N_KV):
            cols = slice(g * gcols, (g + 1) * gcols)
            out_ref[g] = out_ref[g] + gate_row(g, br) * (acc_ref[g] / l_ref[:, cols])

    def tile_keys(kt):
        return pl.ds(pl.multiple_of(kt * TK, TK), TK)

    def softmax_tile(buf, vt_ref, kt, bias):
        keys = tile_keys(kt)
        for g in range(N_KV):
            for c0 in range(g * gcols, (g + 1) * gcols, LANES):
                cols = slice(c0, c0 + LANES)
                s = bias(c0 // TQ, c0 % TQ, buf[:, cols])
                m_prev = m_ref[:, cols]
                m_new = jnp.maximum(m_prev, jnp.max(s, axis=0, keepdims=True))
                alpha = jnp.exp2(m_prev - m_new)
                p = jnp.exp2(s - m_new)
                l_ref[:, cols] = alpha * l_ref[:, cols] + jnp.sum(p, axis=0, keepdims=True)
                p_ref[:, cols] = p.astype(BF16)
                alpha_ref[:, cols] = alpha
                m_ref[:, cols] = m_new
            cols = slice(g * gcols, (g + 1) * gcols)
            pv = jnp.dot(vt_ref[g * HEAD_DIM:(g + 1) * HEAD_DIM, keys], p_ref[:, cols], preferred_element_type=F32)
            acc_ref[g] = alpha_ref[:, cols] * acc_ref[g] + pv

    def near_bias(row):
        if row is None:
            return lambda h, q0, s: s
        return lambda h, q0, s: s + near_ref[h, row:row + TK, q0:q0 + LANES]

    reset()

    def sel_scores(kt, buf):
        keys = tile_keys(kt)
        kk = jnp.concatenate([ksel_ref[keys, :], xt_ref[keys, :]], axis=1)
        buf[...] = jnp.dot(kk, qs_ref[...], preferred_element_type=F32)

    def stage(pred, cur_buf, cur_tile, bias, next_buf=None, next_tile=None):
        @pl.when(pred)
        def _():
            if next_buf is not None:
                sel_scores(next_tile, next_buf)
            softmax_tile(cur_buf, vselt_ref, cur_tile, bias)

    far, near, diag = near_bias(None), near_bias(0), near_bias(TK)
    sel_scores(0, s_ref)

    def far_pair(i, carry):
        in_range = 2 * i + 1 < u
        stage(in_range, s_ref, 2 * i, far, s2_ref, 2 * i + 1)
        stage(in_range, s2_ref, 2 * i + 1, far, s_ref, 2 * i + 2)
        return carry

    lax.fori_loop(0, (u - 1) // 2, far_pair, 0)

    stage(u == 0, s_ref, 0, diag)
    odd = u % 2 == 1
    stage(odd, s_ref, u - 1, near, s2_ref, u)
    stage(odd, s2_ref, u, diag)
    even = (u >= 2) & (u % 2 == 0)
    stage(even, s_ref, u - 2, far, s2_ref, u - 1)
    stage(even, s2_ref, u - 1, near, s_ref, u)
    stage(even, s_ref, u, diag)
    finish(1)

    reset()

    def later_keys_only(h, q0, s):
        kj = lax.broadcasted_iota(jnp.int32, s.shape, 0)
        qa = lax.broadcasted_iota(jnp.int32, s.shape, 1) + q0
        return jnp.where(kj > qa, s, NEG)

    def win_tile(kt, bias):
        s_ref[...] = jnp.dot(kwin_ref[tile_keys(kt), :], qs_ref[:LANES, :], preferred_element_type=F32)
        softmax_tile(s_ref, vwint_ref, kt, bias)

    @pl.when(u >= 2)
    def _():
        win_tile(u - 2, later_keys_only)

    @pl.when(u >= 1)
    def _():
        win_tile(u - 1, near)

    win_tile(u, diag)
    finish(2)

    for i in range(N_HEADS // 2):
        pair = jnp.concatenate([out_ref[(2 * i) // REP, :, head_cols(2 * i)],
                                out_ref[(2 * i + 1) // REP, :, head_cols(2 * i + 1)]], axis=0)
        o_ref[:, i * LANES:(i + 1) * LANES] = pair.T.astype(BF16)


def _attention_prompt(qt, kc, vct, ksel, vselt, kwin, vwint, gatet, tables, batch, seq):
    near, dtab, lane_m, xt = tables
    n = batch * seq
    tpb = seq // TQ
    nc = seq // CMP_BLOCK
    per_batch = lambda rows: pl.BlockSpec((rows, LANES), lambda i: (i // tpb, 0))
    chan = lambda cols: pl.BlockSpec((None, LANES, cols), lambda i: (i // tpb, 0, 0))
    full = lambda a: pl.BlockSpec(a.shape, lambda i: (0,) * a.ndim)
    return pl.pallas_call(
        functools.partial(_attn_kernel, tiles_per_batch=tpb),
        grid=(n // TQ,),
        in_specs=[pl.BlockSpec((N_HEADS * LANES, TQ), lambda i: (0, i)),
                  per_batch(nc), chan(nc), full(dtab), full(lane_m),
                  per_batch(seq), chan(seq), per_batch(seq), chan(seq),
                  full(xt), full(near), pl.BlockSpec((LANES, TQ), lambda i: (0, i))],
        out_specs=pl.BlockSpec((TQ, ATT_WIDTH), lambda i: (i, 0)),
        out_shape=jax.ShapeDtypeStruct((n, ATT_WIDTH), BF16),
        scratch_shapes=[pltpu.VMEM((2 * LANES, N_HEADS * TQ), BF16),
                        pltpu.VMEM((TK, N_HEADS * TQ), F32), pltpu.VMEM((TK, N_HEADS * TQ), F32),
                        pltpu.VMEM((TK, N_HEADS * TQ), BF16),
                        pltpu.VMEM((1, N_HEADS * TQ), F32), pltpu.VMEM((1, N_HEADS * TQ), F32),
                        pltpu.VMEM((1, N_HEADS * TQ), F32),
                        pltpu.VMEM((N_KV, HEAD_DIM, REP * TQ), F32), pltpu.VMEM((N_KV, HEAD_DIM, REP * TQ), F32)],
        compiler_params=_cparams(("arbitrary",)),
        name="nsa_prompt",
    )(qt, kc, vct, dtab, lane_m, ksel, vselt, kwin, vwint, xt, near, gatet)


def _block_onehot_rows(seq):
    j = np.arange(seq)[:, None] // SEL_BLOCK
    return jnp.asarray((j == np.arange(LANES)[None, :]).astype(np.float32)).astype(BF16)


def _even_odd(x, nc):
    x = x.reshape(-1, nc // 2, 2, x.shape[-1])
    return jnp.concatenate([x[:, :, 0], x[:, :, 1]], axis=1).reshape(-1, x.shape[-1])


def _sample_bias_tables(rel_bias, past_len, t_new, n_win):
    t = np.arange(t_new)[:, None]
    nc = past_len // CMP_BLOCK
    tb_full = lambda dist, ok: jnp.where(jnp.asarray(ok), _bucket_values(rel_bias.astype(F32).T, dist), NEG)
    dist_c = past_len + t - (CMP_BLOCK * np.arange(nc)[None, :] + CMP_BLOCK - 1)
    dist_s = past_len + t - np.arange(past_len)[None, :]
    jn = np.arange(LANES)[None, :]
    dist_n = t - jn
    dist_w = n_win + t - np.arange(n_win)[None, :]
    rows = lambda x: x.reshape(N_HEADS * t_new, x.shape[-1])
    return (rows(tb_full(dist_c, dist_c >= 0)), rows(tb_full(dist_s, dist_s >= 0)),
            rows(tb_full(dist_n, (dist_n >= 0) & (jn < t_new))),
            rows(tb_full(dist_w, (dist_w >= 0) & (dist_w < WINDOW))))


def _attn_sample_kernel(pt_ref, *refs, n_pages, t_new):
    del pt_ref
    kvc_refs = refs[:n_pages]
    sel_refs = refs[n_pages:2 * n_pages]
    (qq_ref, seln_ref, winn_ref, winb_ref, gate_ref, xt_ref, xtn_ref, bc_ref, bs_ref, bn_ref, bw_ref,
     rsum_ref, pair_ref, o_ref, qs_ref, out_ref, s_ref, p_ref) = refs[2 * n_pages:]

    def gate_col(br):
        return jnp.concatenate([gate_ref[:, h * 3 + br:h * 3 + br + 1] for h in range(N_HEADS)], axis=0)

    def softmax_tiles(q, tiles):
        off = 0
        for k, _, bias, channel_major, width in tiles:
            if channel_major:
                s = jnp.dot(q, k(), preferred_element_type=F32)
            else:
                s = lax.dot_general(q, k(), _NT, preferred_element_type=F32)
            s_ref[:, off:off + width] = s + bias()
            off += width
        s = s_ref[:, :off]
        p = jnp.exp(s - jnp.max(s, axis=1, keepdims=True))
        den = jnp.sum(p, axis=1, keepdims=True)
        p_ref[:, :off] = p.astype(BF16)
        acc, off = None, 0
        for _, v, _, channel_major, width in tiles:
            p_t = p_ref[:, off:off + width]
            if channel_major:
                pv = lax.dot_general(p_t, v(), _NT, preferred_element_type=F32)
            else:
                pv = jnp.dot(p_t, v(), preferred_element_type=F32)
            acc = pv if acc is None else acc + pv
            off += width
        return acc / den

    qs_ref[:, :LANES] = qq_ref[...]

    kvc = jnp.concatenate([r[...] for r in kvc_refs], axis=0)
    s = lax.dot_general(qq_ref[...], kvc[:, :LANES].astype(BF16), _NT, preferred_element_type=F32) + bc_ref[...]
    mx = jnp.max(s, axis=1, keepdims=True)
    p = jnp.where(s > 0.1 * NEG, jnp.exp(s - mx), 0.0)
    pn = p / jnp.maximum(jnp.sum(p, axis=1, keepdims=True), 1e-30)
    out_ref[...] = gate_col(0) * jnp.dot(pn.astype(BF16), kvc[:, LANES:].astype(BF16), preferred_element_type=F32)

    hp = lax.Precision.HIGHEST
    imp = jnp.dot(jnp.dot(rsum_ref[...], pn, precision=hp, preferred_element_type=F32), pair_ref[...],
                  precision=hp, preferred_element_type=F32)
    cur = (n_pages * PAGE_SIZE) // SEL_BLOCK
    neg = _select_blocks(imp.T, cur, unroll=True).T
    qs_ref[:, LANES:] = jnp.concatenate(
        [neg[(h // REP) * t_new:(h // REP + 1) * t_new, :] for h in range(N_HEADS)], axis=0).astype(BF16)

    def past_tile(ref, j):
        cols = slice(j * PAGE_SIZE, (j + 1) * PAGE_SIZE)
        return (lambda: jnp.concatenate([ref[:LANES, :].astype(BF16), xt_ref[:, cols]], axis=0),
                lambda: ref[LANES:, :].astype(BF16), lambda: bs_ref[:, cols], True, PAGE_SIZE)

    new_sel_tile = (lambda: jnp.concatenate([seln_ref[:, :LANES].astype(BF16), xtn_ref[...]], axis=1),
                    lambda: seln_ref[:, LANES:].astype(BF16), lambda: bn_ref[...], False, PAGE_SIZE)
    tiles = [past_tile(sel_refs[j], j) for j in range(n_pages)] + [new_sel_tile]
    out_ref[...] += gate_col(1) * softmax_tiles(qs_ref[...], tiles)

    tiles = [(lambda: winb_ref[:LANES, :].astype(BF16), lambda: winb_ref[LANES:, :].astype(BF16), lambda: bw_ref[...],
              True, winb_ref.shape[1]),
             (lambda: winn_ref[:, :LANES].astype(BF16), lambda: winn_ref[:, LANES:].astype(BF16), lambda: bn_ref[...],
              False, PAGE_SIZE)]
    out_ref[...] += gate_col(2) * softmax_tiles(qq_ref[...], tiles)

    lane_o = lax.broadcasted_iota(jnp.int32, (t_new, LANES), 1)
    for i in range(N_HEADS // 2):
        left = out_ref[2 * i * t_new:(2 * i + 1) * t_new, :]
        right = out_ref[(2 * i + 1) * t_new:(2 * i + 2) * t_new, :]
        if (2 * i) // REP == 0:
            right = pltpu.roll(right, HEAD_DIM, 1)
        else:
            left = pltpu.roll(left, HEAD_DIM, 1)
        o_ref[:, i * LANES:(i + 1) * LANES] = jnp.where(lane_o < HEAD_DIM, left, right)


def _attention_sample(page_table, kvc_pages, sel_cache, layer, qq, sel_new, win_new, win_buf, gates, tables, t_new):
    n_seq, n_pages = page_table.shape
    xt, xtn, bc, bs, bn, bw, rsum, pair = tables
    n_win = win_buf.shape[3]
    nq = N_HEADS * t_new
    page_spec = lambda shape, j: pl.BlockSpec((None, None) + shape, lambda b, pt, j=j: (layer, pt[b, j], 0, 0))
    full = lambda a: pl.BlockSpec(a.shape, lambda b, pt: (0,) * a.ndim)
    new = pl.BlockSpec((None, PAGE_SIZE, KV_WIDTH), lambda b, pt: (b, 0, 0))
    in_specs = ([page_spec((PAGE_SIZE // CMP_BLOCK, KV_WIDTH), j) for j in range(n_pages)]
                + [page_spec((KV_WIDTH, PAGE_SIZE), j) for j in range(n_pages)]
                + [pl.BlockSpec((None, nq, LANES), lambda b, pt: (b, 0, 0)), new, new,
                   pl.BlockSpec((None, None, KV_WIDTH, n_win), lambda b, pt: (layer, b, 0, 0)),
                   pl.BlockSpec((t_new, LANES), lambda b, pt: (b, 0))]
                + [full(a) for a in (xt, xtn, bc, bs, bn, bw, rsum, pair)])
    return pl.pallas_call(
        functools.partial(_attn_sample_kernel, n_pages=n_pages, t_new=t_new),
        grid_spec=pltpu.PrefetchScalarGridSpec(
            num_scalar_prefetch=1, grid=(n_seq,), in_specs=in_specs,
            out_specs=pl.BlockSpec((t_new, ATT_WIDTH), lambda b, pt: (b, 0)),
            scratch_shapes=[pltpu.VMEM((nq, 2 * LANES), BF16), pltpu.VMEM((nq, LANES), F32),
                            pltpu.VMEM((nq, (n_pages + 1) * PAGE_SIZE), F32),
                            pltpu.VMEM((nq, (n_pages + 1) * PAGE_SIZE), BF16)]),
        out_shape=jax.ShapeDtypeStruct((n_seq * t_new, ATT_WIDTH), F32),
        compiler_params=_cparams(("arbitrary",)),
        name="nsa_sample",
    )(page_table, *([kvc_pages] * n_pages), *([sel_cache] * n_pages), qq, sel_new, win_new, win_buf, gates,
      xt, xtn, bc, bs, bn, bw, rsum, pair)


def _sample_sum_matrices(t_new, nc):
    rsum = np.zeros((LANES, N_HEADS * t_new), np.float32)
    for h in range(N_HEADS):
        for t in range(t_new):
            rsum[(h // REP) * t_new + t, h * t_new + t] = 1.0
    pair = np.zeros((nc, LANES), np.float32)
    pair[np.arange(nc), np.arange(nc) // 2] = 1.0
    return jnp.asarray(rsum), jnp.asarray(pair)


def _rank_before(vals, k):
    r = jnp.zeros(vals[k].shape, jnp.int32)
    for j, vj in enumerate(vals):
        if j < k:
            r = r + jnp.where(vj >= vals[k], 1, 0)
        elif j > k:
            r = r + jnp.where(vj > vals[k], 1, 0)
    return r


def _route(s_rows, b_rows):
    scores = []
    for g in range(N_GROUPS):
        b0, b1, b2, b3 = b_rows[4 * g:4 * g + 4]
        hi01, lo01, hi23, lo23 = jnp.maximum(b0, b1), jnp.minimum(b0, b1), jnp.maximum(b2, b3), jnp.minimum(b2, b3)
        top1 = jnp.maximum(hi01, hi23)
        top2 = jnp.maximum(jnp.maximum(lo01, lo23), jnp.minimum(hi01, hi23))
        scores.append(top1 + top2)
    in_group = [_rank_before(scores, g) == 0 for g in range(N_GROUPS)]

    def pick(rows, k):
        out = rows[4 * (N_GROUPS - 1) + k]
        for g in range(N_GROUPS - 2, -1, -1):
            out = jnp.where(in_group[g], rows[4 * g + k], out)
        return out

    bv = [pick(b_rows, k) for k in range(EXPERTS_PER_GROUP)]
    sv = [pick(s_rows, k) for k in range(EXPERTS_PER_GROUP)]
    w = [jnp.where(_rank_before(bv, k) < 2, sv[k], 0.0) for k in range(EXPERTS_PER_GROUP)]
    den = (w[0] + w[1]) + (w[2] + w[3])
    return [jnp.where(in_group[e // 4], w[e % 4] / den, 0.0) for e in range(N_EXPERTS)]


def _outproj_kernel(ps_ref, att_ref, x_ref, mod_ref, wout_ref, lng_ref, lnb_ref, rwt_ref, rb_ref,
                    x1_ref, h2_ref, g_ref):
    half = ps_ref.shape[1]
    mix = (jnp.dot(ps_ref[...], wout_ref[:half, :], preferred_element_type=F32)
           + jnp.dot(att_ref[...], wout_ref[half:, :], preferred_element_type=F32))
    x1 = _layer_norm(ALPHA * x_ref[...] + (1.0 + mod_ref[2]) * mix, lng_ref[...], lnb_ref[...])
    x1_ref[...] = x1
    h2 = x1 * (1.0 + mod_ref[4]) + mod_ref[3]
    h2b = h2.astype(BF16)
    h2_ref[...] = h2b
    st = jax.nn.sigmoid(lax.dot_general(rwt_ref[...].astype(BF16), h2b, _NT, preferred_element_type=F32))
    s_rows = [st[e:e + 1, :] for e in range(N_EXPERTS)]
    b_rows = [s_rows[e] + rb_ref[e:e + 1, :] for e in range(N_EXPERTS)]
    gt = jnp.concatenate(_route(s_rows, b_rows) + [jnp.zeros((LANES - N_EXPERTS, st.shape[1]), F32)], axis=0)
    g_ref[...] = gt.T


def _out_proj(ps, att, x, mod, wout, ln_g, ln_b, rwt, rb, tm, tiles_per_batch):
    n = x.shape[0]
    row = lambda wd: pl.BlockSpec((tm, wd), lambda i: (i, 0))
    full = lambda a: pl.BlockSpec(a.shape, lambda i: (0,) * a.ndim)
    return pl.pallas_call(
        _outproj_kernel,
        grid=(n // tm,),
        in_specs=[row(512), row(512), row(D_MODEL), _mod_spec(mod, tm, tiles_per_batch),
                  full(wout), full(ln_g), full(ln_b), full(rwt), full(rb)],
        out_specs=[row(D_MODEL), row(D_MODEL), row(LANES)],
        out_shape=[jax.ShapeDtypeStruct((n, D_MODEL), F32), jax.ShapeDtypeStruct((n, D_MODEL), BF16),
                   jax.ShapeDtypeStruct((n, LANES), F32)],
        compiler_params=_cparams(("arbitrary",)),
        name="out_proj",
    )(ps, att, x, mod, wout, ln_g, ln_b, rwt, rb)


def _moe_kernel(h2_ref, g_ref, wg_ref, wu_ref, wd_ref, x1_ref, mod_ref, lng_ref, lnb_ref, o_ref, acc_ref):
    e = pl.program_id(1)

    @pl.when(e == 0)
    def _():
        acc_ref[...] = jnp.zeros_like(acc_ref)

    x = h2_ref[...]
    gate = jnp.dot(x, wg_ref[0, 0], preferred_element_type=F32)
    up = jnp.dot(x, wu_ref[0, 0], preferred_element_type=F32)
    act = (gate * jax.nn.sigmoid(gate) * up).astype(BF16)
    y = jnp.dot(act, wd_ref[0, 0], preferred_element_type=F32)
    lane = lax.broadcasted_iota(jnp.int32, g_ref.shape, 1)
    w = jnp.sum(jnp.where(lane == e, g_ref[...], 0.0), axis=1, keepdims=True)
    acc_ref[...] += y * w

    @pl.when(e == pl.num_programs(1) - 1)
    def _():
        o_ref[...] = _layer_norm(ALPHA * x1_ref[...] + (1.0 + mod_ref[5]) * acc_ref[...], lng_ref[...], lnb_ref[...])


def _moe(h2, gates, wg, wu, wd, layer, x1, mod, ln_g, ln_b, tm, tiles_per_batch):
    n = h2.shape[0]
    row = lambda wd_: pl.BlockSpec((tm, wd_), lambda i, e: (i, 0))
    full = lambda a: pl.BlockSpec(a.shape, lambda i, e: (0,) * a.ndim)
    wspec = lambda a: pl.BlockSpec((1, 1) + a.shape[2:], lambda i, e: (layer, e, 0, 0))
    return pl.pallas_call(
        _moe_kernel,
        grid=(n // tm, N_EXPERTS),
        in_specs=[row(D_MODEL), row(LANES), wspec(wg), wspec(wu), wspec(wd), row(D_MODEL),
                  _mod_spec(mod, tm, tiles_per_batch), full(ln_g), full(ln_b)],
        out_specs=row(D_MODEL),
        out_shape=jax.ShapeDtypeStruct((n, D_MODEL), F32),
        scratch_shapes=[pltpu.VMEM((tm, D_MODEL), F32)],
        compiler_params=_cparams(("arbitrary", "arbitrary")),
        name="moe",
    )(h2, gates, wg, wu, wd, x1, mod, ln_g, ln_b)


SHIFT_ROWS = 2048


def _shift_window_kernel(state_ref, new_ref, o_ref, *, t_new):
    x = state_ref[...]
    w = x.shape[1]
    shifted = pltpu.roll(x, w - t_new, 1)
    o_ref[:, :w - LANES] = shifted[:, :w - LANES]
    lane = lax.broadcasted_iota(jnp.int32, (x.shape[0], LANES), 1)
    o_ref[:, w - LANES:] = jnp.where(lane < LANES - t_new, shifted[:, w - LANES:], new_ref[...])


def _shift_window(state, new):
    w, t_new = state.shape[-1], new.shape[-1]
    rows = state.size // w
    new_rows = jnp.pad(new.reshape(rows, t_new), ((0, 0), (LANES - t_new, 0)))
    out = pl.pallas_call(
        functools.partial(_shift_window_kernel, t_new=t_new),
        grid=(rows // SHIFT_ROWS,),
        in_specs=[pl.BlockSpec((SHIFT_ROWS, w), lambda i: (i, 0)), pl.BlockSpec((SHIFT_ROWS, LANES), lambda i: (i, 0))],
        out_specs=pl.BlockSpec((SHIFT_ROWS, w), lambda i: (i, 0)),
        out_shape=jax.ShapeDtypeStruct((rows, w), F32),
        compiler_params=_cparams(("arbitrary",)),
        name="shift_window",
    )(state.reshape(rows, w), new_rows)
    return out.reshape(state.shape)


TM_PROJ = 256
TM_MOE = 512


def _channel_mix(ps, att, x, mod, lw, sw, layer, tiles_per_seq):
    x1, h2, gts = _out_proj(ps, att, x, mod, lw["wout"], lw["ln_g0"], lw["ln_b0"], sw["rwt"], sw["rb"],
                            TM_PROJ, tiles_per_seq(TM_PROJ))
    return _moe(h2, gts, sw["wg"], sw["wu"], sw["wd"], layer, x1, mod, lw["ln_g1"], lw["ln_b1"],
                TM_MOE, tiles_per_seq(TM_MOE))


def _prompt_layer(x, mod, lw, sw, layer, depth, kv_state, batch, seq):
    tiles = lambda tm: seq // tm
    (p, u, vn, q_t, cmp_k, cmp_v, cmp_t, sel_t, win_t, gate_t, ksel, vsel_t, kwin, vwin_t) = _in_proj(
        x, mod, lw["w_proj"], lw["sgu_ln_g"], lw["sgu_ln_b"], TM_PROJ, tiles(TM_PROJ), batch=batch, wq_t=lw["wq_t"],
        layer=layer, depth=depth, state=kv_state)
    ps = _mixers(p, u, vn, *lw["mix_prompt"], tiles_per_batch=tiles(SGU_CHUNK))
    nc = seq // CMP_BLOCK
    kvc = _compress(cmp_k, cmp_v, sw["pe_rows"], sw["w1bd"], sw["w2bd"], nc, layer)
    kc = _even_odd(kvc[:, :LANES].astype(BF16), nc)
    vc_t = jnp.transpose(_even_odd(kvc[:, LANES:].astype(BF16), nc).reshape(batch, nc, LANES), (0, 2, 1))
    att = _attention_prompt(q_t, kc, vc_t, ksel, vsel_t, kwin, vwin_t, gate_t, sw["prompt_tables"], batch, seq)
    x2 = _channel_mix(ps, att, x, mod, lw, sw, layer, tiles)
    return x2, (cmp_t, sel_t, win_t), p


def _sample_layer(x, mod, lw, sw, layer, page_table, kvc_pages, sel_cache, win_state, pool_state, n_seq, t_new):
    tiles = lambda tm: 1
    (p, u, vn, qq, cmp_raw, sel_raw, win_raw, gates) = _in_proj(
        x, mod, lw["w_proj"], lw["sgu_ln_g"], lw["sgu_ln_b"], TM_PROJ, 1)
    p_ext = jnp.concatenate([pool_state[layer], p.reshape(n_seq, t_new, POOL_WIDTH)], axis=1)
    hist = jnp.pad(p_ext, ((0, 0), (3 * t_new - p_ext.shape[1], 0), (0, 0)))
    chunks = [hist[:, k * t_new:(k + 1) * t_new].reshape(n_seq * t_new, POOL_WIDTH) for k in range(3)]
    ps = _mixers(chunks, u, vn, *lw["mix_sample"], t_new=t_new)
    q_rows = jnp.transpose(qq.reshape(N_HEADS, n_seq, t_new, LANES), (1, 0, 2, 3)).reshape(n_seq, N_HEADS * t_new, LANES)
    new_page = lambda raw: jnp.pad(raw.reshape(n_seq, t_new, KV_WIDTH), ((0, 0), (0, PAGE_SIZE - t_new), (0, 0)))
    att = _attention_sample(page_table, kvc_pages, sel_cache, layer, q_rows, new_page(sel_raw), new_page(win_raw),
                            win_state, gates, sw["sample_tables"], t_new)
    x2 = _channel_mix(ps, att.astype(BF16), x, mod, lw, sw, layer, tiles)
    win_new_t = jnp.transpose(win_raw.reshape(n_seq, t_new, KV_WIDTH), (0, 2, 1))
    return x2, (cmp_raw, sel_raw, win_new_t, p_ext[:, p_ext.shape[1] - POOL_BUF:], vn)


def kernel(x_prompt, x_sample, cache_cmp_kv, cache_sel_kv, state_win_kv, state_pool, page_table, c_prompt, c_sample,
           w_in, w_out, pool_w, pool_scale, sgu_ln_g, sgu_ln_b, sgu_w, sgu_b, cmp_pe, cmp_w1, cmp_w2, rel_bias,
           w_mod, b_mod, ln_g, ln_b, router_w, router_b, moe_w_gate, moe_w_up, moe_w_down):
    batch, seq, d = x_prompt.shape
    n_seq, t_new, _ = x_sample.shape
    depth = w_in.shape[0]
    n_pages = page_table.shape[1]
    past_len = n_pages * PAGE_SIZE
    n_phys = cache_cmp_kv.shape[1]
    n_win = state_win_kv.shape[2]
    assert seq // SEL_BLOCK == LANES and seq % TM_MOE == 0 and (n_seq * t_new) % TM_MOE == 0
    assert POOL_BUF + 1 == 2 * t_new and past_len % TK == 0

    n_c = batch + n_seq
    c_all = jnp.pad(jnp.concatenate([c_prompt, c_sample], axis=0), ((0, -n_c % 8), (0, 0)))
    m_all = _modulation(c_all, w_mod, b_mod)
    mod_p = m_all[:, :batch].reshape(depth, batch, 6, 1, d)
    mod_s = jnp.transpose(jnp.repeat(m_all[:, batch:n_c].reshape(depth, n_seq, 6, d), t_new, axis=1), (0, 2, 1, 3))

    pe_rows, w1bd, w2bd = _prep_compress_weights(cmp_pe, cmp_w1, cmp_w2)
    near, dtab, lane_m = _prompt_bias_tables(rel_bias)
    near = near * LOG2E
    nc_past = past_len // CMP_BLOCK
    shared = {
        "pe_rows": pe_rows, "w1bd": w1bd, "w2bd": w2bd,
        "prompt_tables": (near, dtab, lane_m, _block_onehot_rows(seq)),
        "sample_tables": (_block_onehot_rows(past_len + PAGE_SIZE).T, _block_onehot_rows(past_len + PAGE_SIZE)[past_len:],
                          *_sample_bias_tables(rel_bias, past_len, t_new, n_win),
                          *_sample_sum_matrices(t_new, nc_past)),
        "rwt": router_w.T, "rb": router_b.reshape(N_EXPERTS, 1),
        "wg": moe_w_gate.astype(BF16), "wu": moe_w_up.astype(BF16), "wd": moe_w_down.astype(BF16),
    }
    layers = []
    for l in range(depth):
        layers.append({
            "w_proj": _prep_w_proj(w_in[l]), "wq_t": _prep_w_proj(w_in[l])[:, C_Q:].T, "wout": w_out[l].astype(BF16),
            "sgu_ln_g": sgu_ln_g[l].reshape(1, -1), "sgu_ln_b": sgu_ln_b[l].reshape(1, -1),
            "mix_prompt": _prep_mixer_weights(pool_w[l], pool_scale[l], sgu_w[l], sgu_b[l], SGU_CHUNK),
            "mix_sample": _prep_mixer_weights(pool_w[l], pool_scale[l], sgu_w[l], sgu_b[l], t_new),
            "ln_g0": ln_g[l, 0].reshape(1, d), "ln_b0": ln_b[l, 0].reshape(1, d),
            "ln_g1": ln_g[l, 1].reshape(1, d), "ln_b1": ln_b[l, 1].reshape(1, d),
        })

    chan_major = lambda x: jnp.transpose(x, (0, 1, 3, 4, 5, 2)).reshape(x.shape[0], x.shape[1], KV_WIDTH, x.shape[2])
    pe_pages = jnp.broadcast_to(jnp.transpose(cmp_pe, (0, 1, 3, 2))[:, :, None, :, None, :],
                                (depth, 2, N_KV, HEAD_DIM, PAGE_SIZE // CMP_BLOCK, CMP_BLOCK))
    kvc_pages = _compress_pages(chan_major(cache_cmp_kv), pe_pages.reshape(depth, KV_WIDTH, PAGE_SIZE), w1bd, w2bd, 64)
    kvc_pages = kvc_pages.reshape(depth, n_phys, PAGE_SIZE // CMP_BLOCK, KV_WIDTH)
    sel_cache = chan_major(cache_sel_kv)
    win_state = chan_major(state_win_kv)

    xp = x_prompt.reshape(batch * seq, d)
    xs = x_sample.reshape(n_seq * t_new, d)
    pool_p, outs_s, kv_state = [], [], None
    for l in range(depth):
        xp, kv_state, p_l = _prompt_layer(xp, mod_p[l], layers[l], shared, l, depth, kv_state, batch, seq)
        pool_p.append(p_l)
        xs, st = _sample_layer(xs, mod_s[l], layers[l], shared, l, page_table, kvc_pages, sel_cache, win_state,
                               state_pool, n_seq, t_new)
        outs_s.append(st)

    kv5 = lambda x, b: x.reshape(b, -1, 2, N_KV, HEAD_DIM)
    kv6_t = lambda x: jnp.transpose(x.reshape(x.shape[:2] + (2, N_KV, HEAD_DIM, x.shape[3])), (0, 1, 5, 2, 3, 4))
    stack = lambda xs_: jnp.stack(xs_)
    w_keep = min(WINDOW, seq)
    cmp_t, sel_t, win_t = kv_state
    new_win_t = _shift_window(win_state, stack([o[2] for o in outs_s]))
    return (xp.reshape(batch, seq, d), xs.reshape(n_seq, t_new, d),
            kv6_t(cmp_t), stack([kv5(o[0], n_seq) for o in outs_s]),
            kv6_t(sel_t), stack([kv5(o[1], n_seq) for o in outs_s]),
            kv6_t(win_t[:, :, :, seq - w_keep:]), kv6_t(new_win_t),
            stack([p_l.reshape(batch, seq, POOL_WIDTH)[:, seq - POOL_BUF:] for p_l in pool_p]),
            stack([o[3] for o in outs_s]),
            stack([o[4].reshape(n_seq, t_new, SGU_WIDTH) for o in outs_s]))
```

```python
import functools
import math

import numpy as np
import jax
import jax.numpy as jnp
from jax import lax
from jax.experimental import pallas as pl
from jax.experimental.pallas import tpu as pltpu

F32 = jnp.float32
BF16 = jnp.bfloat16

D_MODEL = 1024
POOL_WIDTH = 256
SGU_WIDTH = 256
ATT_WIDTH = 512
POOL_WINDOWS = (2, 4, 8, 16)
POOL_GROUP_DIM = 64
POOL_BUF = 15
SGU_GROUPS = 4
SGU_CHUNK = 128
N_HEADS = 8
HEAD_DIM = 64
N_KV = 2
REP = 4
CMP_BLOCK = 32
CMP_HIDDEN = 128
SEL_BLOCK = 64
N_SEL = 16
WINDOW = 512
N_BUCKETS = 32
REL_MAX_DIST = 128
N_EXPERTS = 16
N_GROUPS = 4
EXPERTS_PER_GROUP = 4
D_EXPERT = 512
DEPTH = 2
ALPHA = (2 * DEPTH) ** 0.25
LN_EPS = 1e-5
FORCED_SCORE = 1e4
NEG = -1e30
LOG2E = math.log2(math.e)
PAGE_SIZE = 128

KV_WIDTH = 2 * N_KV * HEAD_DIM
LANES = 128
VMEM_LIMIT = 56 * 1024 * 1024

C_P, C_U, C_V, C_CMP, C_SEL, C_WIN, C_GATE, C_Q = 0, 256, 512, 768, 1024, 1280, 1536, 1664
W_PROJ = C_Q + N_HEADS * LANES

TQ = 256
TK = 256
NEAR_BLOCKS = 12
FAR_BUCKET_DIST = 113


def _cparams(sem):
    return pltpu.CompilerParams(dimension_semantics=sem, vmem_limit_bytes=VMEM_LIMIT)


def _layer_norm(x, g, b):
    mu = jnp.mean(x, axis=-1, keepdims=True)
    xc = x - mu
    var = jnp.mean(xc * xc, axis=-1, keepdims=True)
    return xc * lax.rsqrt(var + LN_EPS) * g + b


def _mod_kernel(c_ref, w_ref, b_ref, o_ref):
    c = c_ref[...]
    a = (c * jax.nn.sigmoid(c)).astype(BF16)
    o_ref[0] = jnp.dot(a, w_ref[0].astype(BF16), preferred_element_type=F32) + b_ref[0]


def _modulation(c_all, w_mod, b_mod):
    n, d = c_all.shape
    depth, _, w = w_mod.shape
    tn = 1536
    return pl.pallas_call(
        _mod_kernel,
        grid=(depth, w // tn),
        in_specs=[pl.BlockSpec((n, d), lambda l, j: (0, 0)),
                  pl.BlockSpec((1, d, tn), lambda l, j: (l, 0, j)),
                  pl.BlockSpec((1, 1, tn), lambda l, j: (l, 0, j))],
        out_specs=pl.BlockSpec((1, n, tn), lambda l, j: (l, 0, j)),
        out_shape=jax.ShapeDtypeStruct((depth, n, w), F32),
        compiler_params=_cparams(("arbitrary", "arbitrary")),
        name="adaln_mod",
    )(c_all, w_mod, b_mod.reshape(depth, 1, w))


def _mod_spec(mod, tm, tiles_per_batch):
    if mod.ndim == 4:
        return pl.BlockSpec((None, 6, 1, D_MODEL), lambda i, *_: (i // tiles_per_batch, 0, 0, 0))
    return pl.BlockSpec((6, tm, D_MODEL), lambda i, *_: (0, i, 0))


def _inproj_kernel(x_ref, mod_ref, w_ref, lng_ref, lnb_ref, *rest, channel_major, n_state):
    h = (x_ref[...] * (1.0 + mod_ref[1]) + mod_ref[0]).astype(BF16)

    def seg(a, b):
        return jnp.dot(h, w_ref[:, a:b], preferred_element_type=F32)

    if channel_major:
        wqt_ref = rest[0]
        p_ref, u_ref, vn_ref, qt_ref = rest[1 + n_state:5 + n_state]
        rest = rest[5 + n_state:]
        qt_ref[...] = lax.dot_general(wqt_ref[...], h, _NT, preferred_element_type=F32).astype(BF16)
    else:
        p_ref, u_ref, vn_ref, qq_ref = rest[:4]
        rest = rest[4:]
        for hd in range(N_HEADS):
            qq_ref[hd] = seg(C_Q + hd * LANES, C_Q + (hd + 1) * LANES).astype(BF16)
    p_ref[...] = seg(C_P, C_U)
    u_ref[...] = seg(C_U, C_V)
    vn_ref[...] = _layer_norm(seg(C_V, C_CMP), lng_ref[...], lnb_ref[...])
    cmp = seg(C_CMP, C_SEL)
    sel = seg(C_SEL, C_WIN)
    win = seg(C_WIN, C_GATE)
    gates = jax.nn.sigmoid(seg(C_GATE, C_Q))
    if not channel_major:
        cmp_ref, sel_ref, win_ref, gate_ref = rest
        cmp_ref[...] = cmp
        sel_ref[...] = sel
        win_ref[...] = win
        gate_ref[...] = gates
        return
    cmpk_ref, cmpv_ref, cmpt_ref, selt_ref, wint_ref, gatet_ref, ksel_ref, vselt_ref, kwin_ref, vwint_ref = rest
    cmpk_ref[...] = cmp[:, :LANES]
    cmpv_ref[...] = cmp[:, LANES:]
    cmpt_ref[...] = cmp.T
    sel_t = sel.T
    selt_ref[...] = sel_t
    ksel_ref[...] = sel[:, :LANES].astype(BF16)
    vselt_ref[...] = sel_t[LANES:, :].astype(BF16)
    win_t = win.T
    wint_ref[...] = win_t
    kwin_ref[...] = win[:, :LANES].astype(BF16)
    vwint_ref[...] = win_t[LANES:, :].astype(BF16)
    gatet_ref[...] = gates.T


def _in_proj(x, mod, w_proj, ln_g, ln_b, tm, tiles_per_batch, batch=None, wq_t=None, layer=0, depth=1, state=None):
    n = x.shape[0]
    row = lambda wd: pl.BlockSpec((tm, wd), lambda i: (i, 0))
    full = lambda a: pl.BlockSpec(a.shape, lambda i: (0,) * a.ndim)
    f32o = lambda wd: jax.ShapeDtypeStruct((n, wd), F32)
    out_specs = [row(256), row(256), row(256)]
    out_shape = [f32o(256), f32o(256), f32o(256)]
    ins, in_specs = [x, mod, w_proj, ln_g, ln_b], [row(D_MODEL), _mod_spec(mod, tm, tiles_per_batch), full(w_proj),
                                                    full(ln_g), full(ln_b)]
    aliases = {}
    if batch is None:
        out_specs += [pl.BlockSpec((N_HEADS, tm, LANES), lambda i: (0, i, 0)), row(256), row(256), row(256), row(LANES)]
        out_shape += [jax.ShapeDtypeStruct((N_HEADS, n, LANES), BF16), f32o(256), f32o(256), f32o(256), f32o(LANES)]
    else:
        ins.append(wq_t)
        in_specs.append(full(wq_t))
        out_specs.append(pl.BlockSpec((N_HEADS * LANES, tm), lambda i: (0, i)))
        out_shape.append(jax.ShapeDtypeStruct((N_HEADS * LANES, n), BF16))
        seq = n // batch
        chan = lambda c: pl.BlockSpec((None, c, tm), lambda i: (i // tiles_per_batch, 0, i % tiles_per_batch))
        chan_o = lambda c, dt: jax.ShapeDtypeStruct((batch, c, seq), dt)
        kv_state = pl.BlockSpec((None, None, KV_WIDTH, tm),
                                lambda i: (layer, i // tiles_per_batch, 0, i % tiles_per_batch))
        kv_state_o = jax.ShapeDtypeStruct((depth, batch, KV_WIDTH, seq), F32)
        first_state_out = len(out_specs) + 2
        out_specs += [row(LANES), row(LANES), kv_state, kv_state, kv_state, pl.BlockSpec((LANES, tm), lambda i: (0, i)),
                      row(LANES), chan(LANES), row(LANES), chan(LANES)]
        out_shape += [f32o(LANES), f32o(LANES),
                      kv_state_o, kv_state_o, kv_state_o, jax.ShapeDtypeStruct((LANES, n), F32),
                      jax.ShapeDtypeStruct((n, LANES), BF16), chan_o(LANES, BF16),
                      jax.ShapeDtypeStruct((n, LANES), BF16), chan_o(LANES, BF16)]
        if state is not None:
            aliases = {len(ins) + k: first_state_out + k for k in range(len(state))}
            ins += list(state)
            in_specs += [pl.BlockSpec(memory_space=pl.ANY)] * len(state)
    return pl.pallas_call(
        functools.partial(_inproj_kernel, channel_major=batch is not None, n_state=len(state or ())),
        grid=(n // tm,),
        in_specs=in_specs,
        out_specs=out_specs,
        out_shape=out_shape,
        input_output_aliases=aliases,
        compiler_params=_cparams(("arbitrary",)),
        name="in_proj",
    )(*ins)


def _prep_w_proj(w_in):
    d = w_in.shape[0]
    q = w_in[:, 768:1280].reshape(d, N_KV, REP, HEAD_DIM) * (HEAD_DIM ** -0.5)
    qq = jnp.zeros((d, N_KV, REP, N_KV, HEAD_DIM), w_in.dtype)
    for g in range(N_KV):
        qq = qq.at[:, g, :, g, :].set(q[:, g])
    gate = jnp.pad(w_in[:, 2048:2072], ((0, 0), (0, LANES - 24)))
    cols = [w_in[:, 0:768], w_in[:, 1280:2048], gate, qq.reshape(d, N_HEADS * LANES)]
    return jnp.concatenate(cols, axis=1).astype(BF16)


def _window_sums(shifted):
    acc = shifted(0)
    sums = {}
    for k in range(1, 16):
        acc = acc + shifted(k)
        if k + 1 in POOL_WINDOWS:
            sums[k + 1] = acc
    lane = lax.broadcasted_iota(jnp.int32, acc.shape, 1)
    return jnp.where(lane < 64, sums[2], jnp.where(lane < 128, sums[4], jnp.where(lane < 192, sums[8], sums[16])))


def _mixers_tail(sums, cnt, cur, u, vn, wpool_ref, pscale_ref, wcat_ref, sbias_ref, o_ref):
    diff = (sums / cnt - cur).astype(BF16)
    pool = jnp.dot(diff, wpool_ref[...], preferred_element_type=F32) * pscale_ref[...]
    lane = lax.broadcasted_iota(jnp.int32, vn.shape, 1)
    vb = vn.astype(BF16)
    zero = jnp.zeros_like(vb)
    stacked = jnp.concatenate([jnp.where((lane // 64) == g, vb, zero) for g in range(SGU_GROUPS)], axis=0)
    mixed = jnp.dot(wcat_ref[...], stacked, preferred_element_type=F32) + sbias_ref[...]
    o_ref[:, :POOL_WIDTH] = pool.astype(BF16)
    o_ref[:, POOL_WIDTH:] = (u * mixed).astype(BF16)


def _lane_window():
    lane = lax.broadcasted_iota(jnp.int32, (SGU_CHUNK, POOL_WIDTH), 1)
    return jnp.where(lane < 64, 2, jnp.where(lane < 128, 4, jnp.where(lane < 192, 8, 16)))


def _mix_prompt_kernel(p_ref, u_ref, vn_ref, wpool_ref, pscale_ref, wcat_ref, sbias_ref, o_ref, prev_ref,
                       *, tiles_per_batch):
    t = pl.program_id(0) % tiles_per_batch

    @pl.when(t == 0)
    def _():
        prev_ref[...] = jnp.zeros_like(prev_ref)

    cur = p_ref[...]
    prev = prev_ref[...]
    row = lax.broadcasted_iota(jnp.int32, cur.shape, 0)

    def shifted(k):
        if k == 0:
            return cur
        return jnp.where(row >= k, pltpu.roll(cur, k, 0), pltpu.roll(prev, k, 0))

    sums = _window_sums(shifted)
    prev_ref[...] = cur
    cnt = jnp.minimum(_lane_window(), t * SGU_CHUNK + row + 1).astype(F32)
    _mixers_tail(sums, cnt, cur, u_ref[...], vn_ref[...], wpool_ref, pscale_ref, wcat_ref, sbias_ref, o_ref)


def _mix_sample_kernel(pa_ref, pb_ref, pc_ref, u_ref, vn_ref, wpool_ref, pscale_ref, wcat_ref, sbias_ref, o_ref,
                       *, t_new):
    a, b, c = pa_ref[...], pb_ref[...], pc_ref[...]
    rows = c.shape[0]
    t = lax.broadcasted_iota(jnp.int32, c.shape, 0) % t_new

    def shifted(k):
        if k == 0:
            return c
        hi, lo = (c, b) if k < t_new else (b, a)
        kk = k % t_new
        if kk == 0:
            return hi
        return jnp.where(t >= kk, pltpu.roll(hi, kk, 0), pltpu.roll(lo, rows - t_new + kk, 0))

    sums = _window_sums(shifted)
    cnt = _lane_window().astype(F32)
    _mixers_tail(sums, cnt, c, u_ref[...], vn_ref[...], wpool_ref, pscale_ref, wcat_ref, sbias_ref, o_ref)


def _mixers(p_in, u, vn, wpool, pscale, wcat, sbias, tiles_per_batch=None, t_new=None):
    n = u.shape[0]
    tm = SGU_CHUNK
    row = lambda wd: pl.BlockSpec((tm, wd), lambda i: (i, 0))
    full = lambda a: pl.BlockSpec(a.shape, lambda i: (0,) * a.ndim)
    weights = [wpool, pscale, wcat, sbias]
    if t_new is None:
        kern = functools.partial(_mix_prompt_kernel, tiles_per_batch=tiles_per_batch)
        ins, scratch = [p_in], [pltpu.VMEM((tm, POOL_WIDTH), F32)]
    else:
        kern = functools.partial(_mix_sample_kernel, t_new=t_new)
        ins, scratch = list(p_in), []
    return pl.pallas_call(
        kern,
        grid=(n // tm,),
        in_specs=[row(256)] * (len(ins) + 2) + [full(a) for a in weights],
        out_specs=row(512),
        out_shape=jax.ShapeDtypeStruct((n, 512), BF16),
        scratch_shapes=scratch,
        compiler_params=_cparams(("arbitrary",)),
        name="mixers",
    )(*ins, u, vn, *weights)


def _prep_mixer_weights(pool_w, pool_scale, sgu_w, sgu_b, chunk):
    wpool = jax.scipy.linalg.block_diag(*[pool_w[g] for g in range(len(POOL_WINDOWS))]).astype(BF16)
    tri = jnp.tril(jnp.ones((chunk, chunk), bool))
    w = jnp.where(tri, sgu_w[:, :chunk, :chunk], 0.0)
    reps = SGU_CHUNK // chunk
    eye = jnp.eye(reps, dtype=w.dtype)
    wt = jnp.concatenate([jnp.kron(eye, w[g]) for g in range(SGU_GROUPS)], axis=1).astype(BF16)
    b = jnp.tile(sgu_b[:, :chunk], (1, reps))
    sbias = jnp.repeat(b.T, SGU_WIDTH // SGU_GROUPS, axis=1)
    return wpool, pool_scale.reshape(1, POOL_WIDTH), wt, sbias


def _compress_mlp(token_rows, w1_ref, w2_ref, o_ref, nblk):
    for s in range(2):
        acc = jnp.zeros((nblk, N_KV * CMP_HIDDEN), F32)
        for j in range(0, CMP_BLOCK, 2):
            pair = jnp.concatenate([token_rows(s, j), token_rows(s, j + 1)], axis=1)
            acc = acc + jnp.dot(pair, w1_ref[0, s, j // 2], preferred_element_type=F32)
        hdn = jax.nn.gelu(acc).astype(BF16)
        o_ref[0, :, s * LANES:(s + 1) * LANES] = jnp.dot(hdn, w2_ref[0, s], preferred_element_type=F32)


def _compress_kernel(k_ref, v_ref, pe_ref, w1_ref, w2_ref, o_ref, *, nblk):
    bufs = (k_ref, v_ref)
    rows = lambda s, j: (bufs[s][pl.ds(j, nblk, stride=CMP_BLOCK), :] + pe_ref[0, s, j:j + 1, :]).astype(BF16)
    _compress_mlp(rows, w1_ref, w2_ref, o_ref, nblk)


PAIR_TOKENS = 2 * PAGE_SIZE
PAIR_BLOCKS = PAIR_TOKENS // CMP_BLOCK


def _compress_pages_kernel(raw_ref, pe_ref, perm_ref, w1_ref, w2_ref, o_ref, k_ref, v_ref, *, nblk):
    bufs = (k_ref, v_ref)

    def body(q, carry):
        for s in range(2):
            ch = slice(s * LANES, (s + 1) * LANES)
            x = jnp.concatenate([raw_ref[0, 2 * q, ch, :] + pe_ref[0, ch, :],
                                 raw_ref[0, 2 * q + 1, ch, :] + pe_ref[0, ch, :]], axis=1).astype(BF16)
            t = lax.dot_general(perm_ref[...], x, _NT, preferred_element_type=F32)
            rows = pl.ds(pl.multiple_of(q * PAIR_BLOCKS, PAIR_BLOCKS), PAIR_BLOCKS)
            bufs[s][:, rows, :] = t.reshape(CMP_BLOCK, PAIR_BLOCKS, LANES)
        return carry

    lax.fori_loop(0, raw_ref.shape[1] // 2, body, 0, unroll=4)
    _compress_mlp(lambda s, j: bufs[s][j].astype(BF16), w1_ref, w2_ref, o_ref, nblk)


def _compress_pages(pages, pe_pages, w1bd, w2bd, pages_per_step):
    nl, n_pages, _, _ = pages.shape
    assert n_pages % pages_per_step == 0 and pages_per_step % 2 == 0
    nblk = pages_per_step * (PAGE_SIZE // CMP_BLOCK)
    tok = np.arange(PAIR_TOKENS)
    dest = (tok % CMP_BLOCK) * PAIR_BLOCKS + tok // CMP_BLOCK
    perm = jnp.asarray((np.arange(PAIR_TOKENS)[:, None] == dest[None, :]).astype(np.float32)).astype(BF16)
    wspec = lambda a: pl.BlockSpec((1,) + a.shape[1:], lambda l, i: (l,) + (0,) * (a.ndim - 1))
    return pl.pallas_call(
        functools.partial(_compress_pages_kernel, nblk=nblk),
        grid=(nl, n_pages // pages_per_step),
        in_specs=[pl.BlockSpec((1, pages_per_step, KV_WIDTH, PAGE_SIZE), lambda l, i: (l, i, 0, 0)),
                  wspec(pe_pages), pl.BlockSpec(perm.shape, lambda l, i: (0, 0)), wspec(w1bd), wspec(w2bd)],
        out_specs=pl.BlockSpec((1, nblk, KV_WIDTH), lambda l, i: (l, i, 0)),
        out_shape=jax.ShapeDtypeStruct((nl, n_pages * (PAGE_SIZE // CMP_BLOCK), KV_WIDTH), F32),
        scratch_shapes=[pltpu.VMEM((CMP_BLOCK, nblk, LANES), F32)] * 2,
        compiler_params=_cparams(("arbitrary", "arbitrary")),
        name="compress_pages",
    )(pages, pe_pages, perm, w1bd, w2bd)


def _compress(k_raw, v_raw, pe_rows, w1bd, w2bd, nblk, layer):
    r = k_raw.shape[0]
    assert r % (nblk * CMP_BLOCK) == 0
    wspec = lambda a: pl.BlockSpec((1,) + a.shape[1:], lambda i: (layer,) + (0,) * (a.ndim - 1))
    rows = pl.BlockSpec((nblk * CMP_BLOCK, LANES), lambda i: (i, 0))
    return pl.pallas_call(
        functools.partial(_compress_kernel, nblk=nblk),
        grid=(r // (nblk * CMP_BLOCK),),
        in_specs=[rows, rows, wspec(pe_rows), wspec(w1bd), wspec(w2bd)],
        out_specs=pl.BlockSpec((1, nblk, KV_WIDTH), lambda i: (0, i, 0)),
        out_shape=jax.ShapeDtypeStruct((1, r // CMP_BLOCK, KV_WIDTH), F32),
        compiler_params=_cparams(("arbitrary",)),
        name="compress",
    )(k_raw, v_raw, pe_rows, w1bd, w2bd)[0]


def _prep_compress_weights(cmp_pe, cmp_w1, cmp_w2):
    nl = cmp_pe.shape[0]
    pe_rows = jnp.concatenate([cmp_pe] * N_KV, axis=-1)
    w1 = cmp_w1.reshape(nl, 2, CMP_BLOCK, HEAD_DIM, CMP_HIDDEN)
    eye = jnp.eye(N_KV, dtype=F32)
    w1bd = w1[:, :, :, None, :, None, :] * eye[None, None, None, :, None, :, None]
    w2bd = cmp_w2[:, :, None, :, None, :] * eye[None, None, :, None, :, None]
    return (pe_rows, w1bd.reshape(nl, 2, CMP_BLOCK // 2, 2 * LANES, N_KV * CMP_HIDDEN).astype(BF16),
            w2bd.reshape(nl, 2, N_KV * CMP_HIDDEN, LANES).astype(BF16))


def _np_bucket(dist):
    n = np.maximum(dist, 0)
    nf = np.maximum(n, 1).astype(np.float32)
    large = 16 + (np.log(nf / np.float32(16)) / np.float32(math.log(REL_MAX_DIST / 16)) * np.float32(16)).astype(np.int32)
    return np.where(n < 16, n, np.minimum(large, N_BUCKETS - 1)).astype(np.int32)


def _bucket_values(tbt, dist):
    hit = jnp.asarray(_np_bucket(dist))[None, ..., None] == jnp.arange(N_BUCKETS, dtype=jnp.int32)
    return jnp.sum(jnp.where(hit, tbt.reshape((tbt.shape[0],) + (1,) * dist.ndim + (N_BUCKETS,)), 0.0), axis=-1)


def _bias_minus_far(rel_bias, dist):
    tbt = rel_bias.astype(F32).T
    val = _bucket_values(tbt, dist) - tbt[:, N_BUCKETS - 1].reshape((-1,) + (1,) * dist.ndim)
    return jnp.where(jnp.asarray(dist >= 0), val, NEG)


def _split3(x):
    hi = x.astype(BF16)
    r1 = x - hi.astype(F32)
    mid = r1.astype(BF16)
    lo = (r1 - mid.astype(F32)).astype(BF16)
    return hi, mid, lo


def _prompt_bias_tables(rel_bias):
    a = np.arange(TQ)[:, None]
    span = 2 * TK + TQ
    per_dist = _bias_minus_far(rel_bias, np.arange(span) - (TQ - 1))
    skew = jnp.broadcast_to(per_dist[:, None, :], (N_HEADS, 2 * TK, span)).reshape(N_HEADS, 2 * TK * span)
    skew = skew[:, :2 * TK * (span - 1)].reshape(N_HEADS, 2 * TK, span - 1)
    near = skew[:, :, 2 * TK - 1:2 * TK - 1 + TQ]
    dist_c = a + (4 * CMP_BLOCK - CMP_BLOCK + 1) - CMP_BLOCK * np.arange(NEAR_BLOCKS)[None, :]
    dc = _bias_minus_far(rel_bias, dist_c)
    hi, mid, lo = _split3(dc)
    cols = jnp.stack([hi, mid, lo], axis=-1).reshape(N_HEADS, TQ, 3 * NEAR_BLOCKS)
    future = jnp.full((N_HEADS, TQ, 1), NEG, F32).astype(BF16)
    pad = jnp.zeros((N_HEADS, TQ, LANES - 3 * NEAR_BLOCKS - 1), BF16)
    dtab = jnp.concatenate([cols, future, pad], axis=-1).reshape(N_HEADS * TQ, LANES).T
    lane = np.arange(LANES)
    lane_m = np.where(lane < 3 * NEAR_BLOCKS, lane // 3, -1000).astype(np.int32)
    return near, dtab, jnp.asarray(np.tile(lane_m[None, :], (8, 1)))


_NT = (((1,), (1,)), ((), ()))


def _select_start(imp_t, cur):
    blk = lax.broadcasted_iota(jnp.int32, imp_t.shape, 0)
    forced = (blk == 0) | (blk == cur) | (blk == cur - 1)
    vals = jnp.where(forced, FORCED_SCORE, jnp.where(blk <= cur, imp_t, -1.0))
    return vals, jnp.full(imp_t.shape, NEG, F32)


def _select_round(vals, neg):
    blk = lax.broadcasted_iota(jnp.int32, vals.shape, 0)
    mx = jnp.max(vals, axis=0, keepdims=True)
    first = jnp.min(jnp.where(vals == mx, blk, 1 << 20), axis=0, keepdims=True)
    pick = blk == first
    return jnp.where(pick, -3e38, vals), jnp.where(pick, 0.0, neg)


def _select_blocks(imp_t, cur):
    _, neg = lax.fori_loop(0, N_SEL, lambda _, c: _select_round(*c), _select_start(imp_t, cur))
    return neg


def _attn_kernel(qt_ref, kc_ref, vct_ref, dtabt_ref, lanem_ref, ksel_ref, vselt_ref, kwin_ref, vwint_ref,
                 xt_ref, near_ref, gatet_ref, o_ref, qs_ref, s_ref, s2_ref, p_ref, m_ref, l_ref, alpha_ref, acc_ref, out_ref,
                 *, tiles_per_batch):
    u = pl.program_id(0) % tiles_per_batch
    nc = kc_ref.shape[0]
    ns = nc // 2
    gcols = REP * TQ

    def gate(h, br):
        c = h * 3 + br
        return gatet_ref[c:c + 1, :]

    def gate_row(g, br):
        return jnp.concatenate([gate(g * REP + r, br) for r in range(REP)], axis=1)

    def group_rows(h):
        g = h // REP
        return slice(g * HEAD_DIM, (g + 1) * HEAD_DIM)

    def head_cols(h):
        return slice((h % REP) * TQ, (h % REP + 1) * TQ)

    c = lax.broadcasted_iota(jnp.int32, (nc, LANES), 0)
    lane = lax.broadcasted_iota(jnp.int32, (nc, LANES), 1)
    rel = jnp.where(c < ns, 2 * c, 2 * c - (nc - 1)) - (8 * u - 4)
    near_hit = jnp.where(rel == lanem_ref[0:1, :], 1.0, 0.0)
    future_hit = jnp.where(rel >= NEAR_BLOCKS, 1.0, 0.0)
    onehot = jnp.where(lane < 3 * NEAR_BLOCKS, near_hit, jnp.where(lane == 3 * NEAR_BLOCKS, future_hit, 0.0))
    kk_c = jnp.concatenate([kc_ref[...], onehot.astype(BF16)], axis=1)
    imp = [jnp.zeros((ns, TQ), F32) for _ in range(N_KV)]
    for h in range(N_HEADS):
        qh = jnp.concatenate([qt_ref[h * LANES:(h + 1) * LANES, :], dtabt_ref[:, h * TQ:(h + 1) * TQ]], axis=0)
        s = jnp.dot(kk_c, qh, preferred_element_type=F32)
        mx = jnp.max(s, axis=0, keepdims=True)
        p = jnp.where(s > 0.1 * NEG, jnp.exp(s - mx), 0.0)
        pn = p / jnp.maximum(jnp.sum(p, axis=0, keepdims=True), 1e-30)
        imp[h // REP] = imp[h // REP] + (pn[:ns, :] + pn[ns:, :])
        out_ref[h // REP, :, head_cols(h)] = gate(h, 0) * jnp.dot(vct_ref[group_rows(h), :], pn.astype(BF16),
                                                                  preferred_element_type=F32)

    imp_t = jnp.concatenate(imp, axis=1)
    a = lax.broadcasted_iota(jnp.int32, imp_t.shape, 1) & (TQ - 1)
    cur = (TQ // SEL_BLOCK) * u + (a >> 6)
    neg = _select_blocks(imp_t, cur).astype(BF16)
    for h in range(N_HEADS):
        g = h // REP
        qs_ref[:LANES, h * TQ:(h + 1) * TQ] = (qt_ref[h * LANES:(h + 1) * LANES, :].astype(F32) * LOG2E).astype(BF16)
        qs_ref[LANES:, h * TQ:(h + 1) * TQ] = neg[:, g * TQ:(g + 1) * TQ]

    def reset():
        m_ref[...] = jnp.full(m_ref.shape, -1e38, F32)
        l_ref[...] = jnp.zeros(l_ref.shape, F32)
        acc_ref[...] = jnp.zeros(acc_ref.shape, F32)

    def finish(br):
        for g in range(N_KV):
            cols = slice(g * gcols, (g + 1) * gcols)
            out_ref[g] = out_ref[g] + gate_row(g, br) * (acc_ref[g] / l_ref[:, cols])

    def tile_keys(kt):
        return pl.ds(pl.multiple_of(kt * TK, TK), TK)

    def softmax_tile(buf, vt_ref, kt, bias):
        keys = tile_keys(kt)
        for g in range(N_KV):
            for c0 in range(g * gcols, (g + 1) * gcols, LANES):
                cols = slice(c0, c0 + LANES)
                s = bias(c0 // TQ, c0 % TQ, buf[:, cols])
                m_prev = m_ref[:, cols]
                m_new = jnp.maximum(m_prev, jnp.max(s, axis=0, keepdims=True))
                alpha = jnp.exp2(m_prev - m_new)
                p = jnp.exp2(s - m_new)
                l_ref[:, cols] = alpha * l_ref[:, cols] + jnp.sum(p, axis=0, keepdims=True)
                p_ref[:, cols] = p.astype(BF16)
                alpha_ref[:, cols] = alpha
                m_ref[:, cols] = m_new
            cols = slice(g * gcols, (g + 1) * gcols)
            pv = jnp.dot(vt_ref[g * HEAD_DIM:(g + 1) * HEAD_DIM, keys], p_ref[:, cols], preferred_element_type=F32)
            acc_ref[g] = alpha_ref[:, cols] * acc_ref[g] + pv

    def near_bias(row):
        if row is None:
            return lambda h, q0, s: s
        return lambda h, q0, s: s + near_ref[h, row:row + TK, q0:q0 + LANES]

    reset()

    def sel_scores(kt, buf):
        keys = tile_keys(kt)
        kk = jnp.concatenate([ksel_ref[keys, :], xt_ref[keys, :]], axis=1)
        buf[...] = jnp.dot(kk, qs_ref[...], preferred_element_type=F32)

    def stage(pred, cur_buf, cur_tile, bias, next_buf=None, next_tile=None):
        @pl.when(pred)
        def _():
            if next_buf is not None:
                sel_scores(next_tile, next_buf)
            softmax_tile(cur_buf, vselt_ref, cur_tile, bias)

    far, near, diag = near_bias(None), near_bias(0), near_bias(TK)
    sel_scores(0, s_ref)

    def far_pair(i, carry):
        in_range = 2 * i + 1 < u
        stage(in_range, s_ref, 2 * i, far, s2_ref, 2 * i + 1)
        stage(in_range, s2_ref, 2 * i + 1, far, s_ref, 2 * i + 2)
        return carry

    lax.fori_loop(0, (u - 1) // 2, far_pair, 0)

    stage(u == 0, s_ref, 0, diag)
    odd = u % 2 == 1
    stage(odd, s_ref, u - 1, near, s2_ref, u)
    stage(odd, s2_ref, u, diag)
    even = (u >= 2) & (u % 2 == 0)
    stage(even, s_ref, u - 2, far, s2_ref, u - 1)
    stage(even, s2_ref, u - 1, near, s_ref, u)
    stage(even, s_ref, u, diag)
    finish(1)

    reset()

    def later_keys_only(h, q0, s):
        kj = lax.broadcasted_iota(jnp.int32, s.shape, 0)
        qa = lax.broadcasted_iota(jnp.int32, s.shape, 1) + q0
        return jnp.where(kj > qa, s, NEG)

    def win_tile(kt, bias):
        s_ref[...] = jnp.dot(kwin_ref[tile_keys(kt), :], qs_ref[:LANES, :], preferred_element_type=F32)
        softmax_tile(s_ref, vwint_ref, kt, bias)

    @pl.when(u >= 2)
    def _():
        win_tile(u - 2, later_keys_only)

    @pl.when(u >= 1)
    def _():
        win_tile(u - 1, near)

    win_tile(u, diag)
    finish(2)

    for i in range(N_HEADS // 2):
        pair = jnp.concatenate([out_ref[(2 * i) // REP, :, head_cols(2 * i)],
                                out_ref[(2 * i + 1) // REP, :, head_cols(2 * i + 1)]], axis=0)
        o_ref[:, i * LANES:(i + 1) * LANES] = pair.T.astype(BF16)


def _attention_prompt(qt, kc, vct, ksel, vselt, kwin, vwint, gatet, tables, batch, seq):
    near, dtab, lane_m, xt = tables
    n = batch * seq
    tpb = seq // TQ
    nc = seq // CMP_BLOCK
    per_batch = lambda rows: pl.BlockSpec((rows, LANES), lambda i: (i // tpb, 0))
    chan = lambda cols: pl.BlockSpec((None, LANES, cols), lambda i: (i // tpb, 0, 0))
    full = lambda a: pl.BlockSpec(a.shape, lambda i: (0,) * a.ndim)
    return pl.pallas_call(
        functools.partial(_attn_kernel, tiles_per_batch=tpb),
        grid=(n // TQ,),
        in_specs=[pl.BlockSpec((N_HEADS * LANES, TQ), lambda i: (0, i)),
                  per_batch(nc), chan(nc), full(dtab), full(lane_m),
                  per_batch(seq), chan(seq), per_batch(seq), chan(seq),
                  full(xt), full(near), pl.BlockSpec((LANES, TQ), lambda i: (0, i))],
        out_specs=pl.BlockSpec((TQ, ATT_WIDTH), lambda i: (i, 0)),
        out_shape=jax.ShapeDtypeStruct((n, ATT_WIDTH), BF16),
        scratch_shapes=[pltpu.VMEM((2 * LANES, N_HEADS * TQ), BF16),
                        pltpu.VMEM((TK, N_HEADS * TQ), F32), pltpu.VMEM((TK, N_HEADS * TQ), F32),
                        pltpu.VMEM((TK, N_HEADS * TQ), BF16),
                        pltpu.VMEM((1, N_HEADS * TQ), F32), pltpu.VMEM((1, N_HEADS * TQ), F32),
                        pltpu.VMEM((1, N_HEADS * TQ), F32),
                        pltpu.VMEM((N_KV, HEAD_DIM, REP * TQ), F32), pltpu.VMEM((N_KV, HEAD_DIM, REP * TQ), F32)],
        compiler_params=_cparams(("arbitrary",)),
        name="nsa_prompt",
    )(qt, kc, vct, dtab, lane_m, ksel, vselt, kwin, vwint, xt, near, gatet)


def _block_onehot_rows(seq):
    j = np.arange(seq)[:, None] // SEL_BLOCK
    return jnp.asarray((j == np.arange(LANES)[None, :]).astype(np.float32)).astype(BF16)


def _even_odd(x, nc):
    x = x.reshape(-1, nc // 2, 2, x.shape[-1])
    return jnp.concatenate([x[:, :, 0], x[:, :, 1]], axis=1).reshape(-1, x.shape[-1])


def _sample_bias_tables(rel_bias, past_len, t_new, n_win):
    t = np.arange(t_new)[:, None]
    nc = past_len // CMP_BLOCK
    tb_full = lambda dist, ok: jnp.where(jnp.asarray(ok), _bucket_values(rel_bias.astype(F32).T, dist), NEG)
    dist_c = past_len + t - (CMP_BLOCK * np.arange(nc)[None, :] + CMP_BLOCK - 1)
    dist_s = past_len + t - np.arange(past_len)[None, :]
    jn = np.arange(LANES)[None, :]
    dist_n = t - jn
    dist_w = n_win + t - np.arange(n_win)[None, :]
    rows = lambda x: x.reshape(N_HEADS * t_new, x.shape[-1])
    return (rows(tb_full(dist_c, dist_c >= 0)), rows(tb_full(dist_s, dist_s >= 0)),
            rows(tb_full(dist_n, (dist_n >= 0) & (jn < t_new))),
            rows(tb_full(dist_w, (dist_w >= 0) & (dist_w < WINDOW))))


SEQ_PER_STEP = 4


def _attn_sample_kernel(pt_ref, *refs, n_pages, t_new):
    del pt_ref
    n_par = SEQ_PER_STEP
    kvc_refs = refs[:n_par * n_pages]
    sel_refs = refs[n_par * n_pages:2 * n_par * n_pages]
    (qq_ref, seln_ref, winn_ref, winb_ref, gate_ref, xt_ref, xtn_ref, bc_ref, bs_ref, bn_ref, bw_ref,
     rsum_ref, pair_ref, o_ref, qs_ref, out_ref, s_ref, p_ref) = refs[2 * n_par * n_pages:]
    tables = (xt_ref, xtn_ref, bc_ref, bs_ref, bn_ref, bw_ref, rsum_ref, pair_ref)
    chains = []
    for s in range(n_par):
        rows = pl.ds(s * t_new, t_new)
        chains.append(_sample_sequence(
            kvc_refs[s * n_pages:(s + 1) * n_pages], sel_refs[s * n_pages:(s + 1) * n_pages],
            qq_ref.at[s], seln_ref.at[s], winn_ref.at[s], winb_ref.at[s], gate_ref.at[rows], tables,
            o_ref.at[rows], qs_ref.at[s], out_ref.at[s], s_ref.at[s], p_ref.at[s], n_pages, t_new))
    while chains:
        chains = [c for c in chains if next(c, "done") != "done"]


def _sample_sequence(kvc_refs, sel_refs, qq_ref, seln_ref, winn_ref, winb_ref, gate_ref, tables, o_ref,
                     qs_ref, out_ref, s_ref, p_ref, n_pages, t_new):
    xt_ref, xtn_ref, bc_ref, bs_ref, bn_ref, bw_ref, rsum_ref, pair_ref = tables

    def gate_col(br):
        return jnp.concatenate([gate_ref[:, h * 3 + br:h * 3 + br + 1] for h in range(N_HEADS)], axis=0)

    def softmax_tiles(q, tiles):
        off = 0
        for k, _, bias, channel_major, width in tiles:
            if channel_major:
                s = jnp.dot(q, k(), preferred_element_type=F32)
            else:
                s = lax.dot_general(q, k(), _NT, preferred_element_type=F32)
            s_ref[:, off:off + width] = s + bias()
            off += width
            yield
        s = s_ref[:, :off]
        p = jnp.exp(s - jnp.max(s, axis=1, keepdims=True))
        den = jnp.sum(p, axis=1, keepdims=True)
        p_ref[:, :off] = p.astype(BF16)
        acc, off = None, 0
        for _, v, _, channel_major, width in tiles:
            p_t = p_ref[:, off:off + width]
            if channel_major:
                pv = lax.dot_general(p_t, v(), _NT, preferred_element_type=F32)
            else:
                pv = jnp.dot(p_t, v(), preferred_element_type=F32)
            acc = pv if acc is None else acc + pv
            off += width
            yield
        return acc / den

    qs_ref[:, :LANES] = qq_ref[...]

    kvc = jnp.concatenate([r[...] for r in kvc_refs], axis=0)
    s = lax.dot_general(qq_ref[...], kvc[:, :LANES].astype(BF16), _NT, preferred_element_type=F32) + bc_ref[...]
    mx = jnp.max(s, axis=1, keepdims=True)
    p = jnp.where(s > 0.1 * NEG, jnp.exp(s - mx), 0.0)
    pn = p / jnp.maximum(jnp.sum(p, axis=1, keepdims=True), 1e-30)
    out_ref[...] = gate_col(0) * jnp.dot(pn.astype(BF16), kvc[:, LANES:].astype(BF16), preferred_element_type=F32)
    yield

    hp = lax.Precision.HIGHEST
    imp = jnp.dot(jnp.dot(rsum_ref[...], pn, precision=hp, preferred_element_type=F32), pair_ref[...],
                  precision=hp, preferred_element_type=F32)
    cur = (n_pages * PAGE_SIZE) // SEL_BLOCK
    yield
    vals, neg = _select_start(imp.T, cur)
    for _ in range(N_SEL):
        vals, neg = _select_round(vals, neg)
        yield
    neg = neg.T
    qs_ref[:, LANES:] = jnp.concatenate(
        [neg[(h // REP) * t_new:(h // REP + 1) * t_new, :] for h in range(N_HEADS)], axis=0).astype(BF16)

    def past_tile(ref, j):
        cols = slice(j * PAGE_SIZE, (j + 1) * PAGE_SIZE)
        return (lambda: jnp.concatenate([ref[:LANES, :].astype(BF16), xt_ref[:, cols]], axis=0),
                lambda: ref[LANES:, :].astype(BF16), lambda: bs_ref[:, cols], True, PAGE_SIZE)

    new_sel_tile = (lambda: jnp.concatenate([seln_ref[:, :LANES].astype(BF16), xtn_ref[...]], axis=1),
                    lambda: seln_ref[:, LANES:].astype(BF16), lambda: bn_ref[...], False, PAGE_SIZE)
    tiles = [past_tile(sel_refs[j], j) for j in range(n_pages)] + [new_sel_tile]
    out_ref[...] += gate_col(1) * (yield from softmax_tiles(qs_ref[...], tiles))

    tiles = [(lambda: winb_ref[:LANES, :].astype(BF16), lambda: winb_ref[LANES:, :].astype(BF16), lambda: bw_ref[...],
              True, winb_ref.shape[1]),
             (lambda: winn_ref[:, :LANES].astype(BF16), lambda: winn_ref[:, LANES:].astype(BF16), lambda: bn_ref[...],
              False, PAGE_SIZE)]
    out_ref[...] += gate_col(2) * (yield from softmax_tiles(qq_ref[...], tiles))

    lane_o = lax.broadcasted_iota(jnp.int32, (t_new, LANES), 1)
    for i in range(N_HEADS // 2):
        left = out_ref[2 * i * t_new:(2 * i + 1) * t_new, :]
        right = out_ref[(2 * i + 1) * t_new:(2 * i + 2) * t_new, :]
        if (2 * i) // REP == 0:
            right = pltpu.roll(right, HEAD_DIM, 1)
        else:
            left = pltpu.roll(left, HEAD_DIM, 1)
        o_ref[:, i * LANES:(i + 1) * LANES] = jnp.where(lane_o < HEAD_DIM, left, right)


def _attention_sample(page_table, kvc_pages, sel_cache, layer, qq, sel_new, win_new, win_buf, gates, tables, t_new):
    n_seq, n_pages = page_table.shape
    xt, xtn, bc, bs, bn, bw, rsum, pair = tables
    n_win = win_buf.shape[3]
    nq = N_HEADS * t_new
    par = SEQ_PER_STEP
    assert n_seq % par == 0
    page_spec = lambda shape, s, j: pl.BlockSpec((None, None) + shape,
                                                 lambda b, pt, s=s, j=j: (layer, pt[par * b + s, j], 0, 0))
    full = lambda a: pl.BlockSpec(a.shape, lambda b, pt: (0,) * a.ndim)
    new = pl.BlockSpec((par, PAGE_SIZE, KV_WIDTH), lambda b, pt: (b, 0, 0))
    in_specs = ([page_spec((PAGE_SIZE // CMP_BLOCK, KV_WIDTH), s, j) for s in range(par) for j in range(n_pages)]
                + [page_spec((KV_WIDTH, PAGE_SIZE), s, j) for s in range(par) for j in range(n_pages)]
                + [pl.BlockSpec((par, nq, LANES), lambda b, pt: (b, 0, 0)), new, new,
                   pl.BlockSpec((None, par, KV_WIDTH, n_win), lambda b, pt: (layer, b, 0, 0)),
                   pl.BlockSpec((par * t_new, LANES), lambda b, pt: (b, 0))]
                + [full(a) for a in (xt, xtn, bc, bs, bn, bw, rsum, pair)])
    return pl.pallas_call(
        functools.partial(_attn_sample_kernel, n_pages=n_pages, t_new=t_new),
        grid_spec=pltpu.PrefetchScalarGridSpec(
            num_scalar_prefetch=1, grid=(n_seq // par,), in_specs=in_specs,
            out_specs=pl.BlockSpec((par * t_new, ATT_WIDTH), lambda b, pt: (b, 0)),
            scratch_shapes=[pltpu.VMEM((par, nq, 2 * LANES), BF16), pltpu.VMEM((par, nq, LANES), F32),
                            pltpu.VMEM((par, nq, (n_pages + 1) * PAGE_SIZE), F32),
                            pltpu.VMEM((par, nq, (n_pages + 1) * PAGE_SIZE), BF16)]),
        out_shape=jax.ShapeDtypeStruct((n_seq * t_new, ATT_WIDTH), F32),
        compiler_params=_cparams(("arbitrary",)),
        name="nsa_sample",
    )(page_table, *([kvc_pages] * (par * n_pages)), *([sel_cache] * (par * n_pages)), qq, sel_new, win_new, win_buf,
      gates, xt, xtn, bc, bs, bn, bw, rsum, pair)


def _sample_sum_matrices(t_new, nc):
    rsum = np.zeros((LANES, N_HEADS * t_new), np.float32)
    for h in range(N_HEADS):
        for t in range(t_new):
            rsum[(h // REP) * t_new + t, h * t_new + t] = 1.0
    pair = np.zeros((nc, LANES), np.float32)
    pair[np.arange(nc), np.arange(nc) // 2] = 1.0
    return jnp.asarray(rsum), jnp.asarray(pair)


def _rank_before(vals, k):
    r = jnp.zeros(vals[k].shape, jnp.int32)
    for j, vj in enumerate(vals):
        if j < k:
            r = r + jnp.where(vj >= vals[k], 1, 0)
        elif j > k:
            r = r + jnp.where(vj > vals[k], 1, 0)
    return r


def _route(s_rows, b_rows):
    scores = []
    for g in range(N_GROUPS):
        b0, b1, b2, b3 = b_rows[4 * g:4 * g + 4]
        hi01, lo01, hi23, lo23 = jnp.maximum(b0, b1), jnp.minimum(b0, b1), jnp.maximum(b2, b3), jnp.minimum(b2, b3)
        top1 = jnp.maximum(hi01, hi23)
        top2 = jnp.maximum(jnp.maximum(lo01, lo23), jnp.minimum(hi01, hi23))
        scores.append(top1 + top2)
    in_group = [_rank_before(scores, g) == 0 for g in range(N_GROUPS)]

    def pick(rows, k):
        out = rows[4 * (N_GROUPS - 1) + k]
        for g in range(N_GROUPS - 2, -1, -1):
            out = jnp.where(in_group[g], rows[4 * g + k], out)
        return out

    bv = [pick(b_rows, k) for k in range(EXPERTS_PER_GROUP)]
    sv = [pick(s_rows, k) for k in range(EXPERTS_PER_GROUP)]
    w = [jnp.where(_rank_before(bv, k) < 2, sv[k], 0.0) for k in range(EXPERTS_PER_GROUP)]
    den = (w[0] + w[1]) + (w[2] + w[3])
    return [jnp.where(in_group[e // 4], w[e % 4] / den, 0.0) for e in range(N_EXPERTS)]


def _outproj_kernel(ps_ref, att_ref, x_ref, mod_ref, wout_ref, lng_ref, lnb_ref, rwt_ref, rb_ref,
                    x1_ref, h2_ref, g_ref):
    half = ps_ref.shape[1]
    mix = (jnp.dot(ps_ref[...], wout_ref[:half, :], preferred_element_type=F32)
           + jnp.dot(att_ref[...], wout_ref[half:, :], preferred_element_type=F32))
    x1 = _layer_norm(ALPHA * x_ref[...] + (1.0 + mod_ref[2]) * mix, lng_ref[...], lnb_ref[...])
    x1_ref[...] = x1
    h2 = x1 * (1.0 + mod_ref[4]) + mod_ref[3]
    h2b = h2.astype(BF16)
    h2_ref[...] = h2b
    st = jax.nn.sigmoid(lax.dot_general(rwt_ref[...].astype(BF16), h2b, _NT, preferred_element_type=F32))
    s_rows = [st[e:e + 1, :] for e in range(N_EXPERTS)]
    b_rows = [s_rows[e] + rb_ref[e:e + 1, :] for e in range(N_EXPERTS)]
    gt = jnp.concatenate(_route(s_rows, b_rows) + [jnp.zeros((LANES - N_EXPERTS, st.shape[1]), F32)], axis=0)
    g_ref[...] = gt.T


def _out_proj(ps, att, x, mod, wout, ln_g, ln_b, rwt, rb, tm, tiles_per_batch):
    n = x.shape[0]
    row = lambda wd: pl.BlockSpec((tm, wd), lambda i: (i, 0))
    full = lambda a: pl.BlockSpec(a.shape, lambda i: (0,) * a.ndim)
    return pl.pallas_call(
        _outproj_kernel,
        grid=(n // tm,),
        in_specs=[row(512), row(512), row(D_MODEL), _mod_spec(mod, tm, tiles_per_batch),
                  full(wout), full(ln_g), full(ln_b), full(rwt), full(rb)],
        out_specs=[row(D_MODEL), row(D_MODEL), row(LANES)],
        out_shape=[jax.ShapeDtypeStruct((n, D_MODEL), F32), jax.ShapeDtypeStruct((n, D_MODEL), BF16),
                   jax.ShapeDtypeStruct((n, LANES), F32)],
        compiler_params=_cparams(("arbitrary",)),
        name="out_proj",
    )(ps, att, x, mod, wout, ln_g, ln_b, rwt, rb)


def _moe_kernel(h2_ref, g_ref, wg_ref, wu_ref, wd_ref, x1_ref, mod_ref, lng_ref, lnb_ref, o_ref, acc_ref):
    e = pl.program_id(1)

    @pl.when(e == 0)
    def _():
        acc_ref[...] = jnp.zeros_like(acc_ref)

    x = h2_ref[...]
    gate = jnp.dot(x, wg_ref[0, 0], preferred_element_type=F32)
    up = jnp.dot(x, wu_ref[0, 0], preferred_element_type=F32)
    act = (gate * jax.nn.sigmoid(gate) * up).astype(BF16)
    y = jnp.dot(act, wd_ref[0, 0], preferred_element_type=F32)
    lane = lax.broadcasted_iota(jnp.int32, g_ref.shape, 1)
    w = jnp.sum(jnp.where(lane == e, g_ref[...], 0.0), axis=1, keepdims=True)
    acc_ref[...] += y * w

    @pl.when(e == pl.num_programs(1) - 1)
    def _():
        o_ref[...] = _layer_norm(ALPHA * x1_ref[...] + (1.0 + mod_ref[5]) * acc_ref[...], lng_ref[...], lnb_ref[...])


def _moe(h2, gates, wg, wu, wd, layer, x1, mod, ln_g, ln_b, tm, tiles_per_batch):
    n = h2.shape[0]
    row = lambda wd_: pl.BlockSpec((tm, wd_), lambda i, e: (i, 0))
    full = lambda a: pl.BlockSpec(a.shape, lambda i, e: (0,) * a.ndim)
    wspec = lambda a: pl.BlockSpec((1, 1) + a.shape[2:], lambda i, e: (layer, e, 0, 0))
    return pl.pallas_call(
        _moe_kernel,
        grid=(n // tm, N_EXPERTS),
        in_specs=[row(D_MODEL), row(LANES), wspec(wg), wspec(wu), wspec(wd), row(D_MODEL),
                  _mod_spec(mod, tm, tiles_per_batch), full(ln_g), full(ln_b)],
        out_specs=row(D_MODEL),
        out_shape=jax.ShapeDtypeStruct((n, D_MODEL), F32),
        scratch_shapes=[pltpu.VMEM((tm, D_MODEL), F32)],
        compiler_params=_cparams(("arbitrary", "arbitrary")),
        name="moe",
    )(h2, gates, wg, wu, wd, x1, mod, ln_g, ln_b)


SHIFT_ROWS = 2048


def _shift_window_kernel(state_ref, new_ref, o_ref, *, t_new):
    x = state_ref[...]
    w = x.shape[1]
    shifted = pltpu.roll(x, w - t_new, 1)
    o_ref[:, :w - LANES] = shifted[:, :w - LANES]
    lane = lax.broadcasted_iota(jnp.int32, (x.shape[0], LANES), 1)
    o_ref[:, w - LANES:] = jnp.where(lane < LANES - t_new, shifted[:, w - LANES:], new_ref[...])


def _shift_window(state, new):
    w, t_new = state.shape[-1], new.shape[-1]
    rows = state.size // w
    new_rows = jnp.pad(new.reshape(rows, t_new), ((0, 0), (LANES - t_new, 0)))
    out = pl.pallas_call(
        functools.partial(_shift_window_kernel, t_new=t_new),
        grid=(rows // SHIFT_ROWS,),
        in_specs=[pl.BlockSpec((SHIFT_ROWS, w), lambda i: (i, 0)), pl.BlockSpec((SHIFT_ROWS, LANES), lambda i: (i, 0))],
        out_specs=pl.BlockSpec((SHIFT_ROWS, w), lambda i: (i, 0)),
        out_shape=jax.ShapeDtypeStruct((rows, w), F32),
        compiler_params=_cparams(("arbitrary",)),
        name="shift_window",
    )(state.reshape(rows, w), new_rows)
    return out.reshape(state.shape)


TM_PROJ = 256
TM_MOE = 512


def _channel_mix(ps, att, x, mod, lw, sw, layer, tiles_per_seq):
    x1, h2, gts = _out_proj(ps, att, x, mod, lw["wout"], lw["ln_g0"], lw["ln_b0"], sw["rwt"], sw["rb"],
                            TM_PROJ, tiles_per_seq(TM_PROJ))
    return _moe(h2, gts, sw["wg"], sw["wu"], sw["wd"], layer, x1, mod, lw["ln_g1"], lw["ln_b1"],
                TM_MOE, tiles_per_seq(TM_MOE))


def _prompt_layer(x, mod, lw, sw, layer, depth, kv_state, batch, seq):
    tiles = lambda tm: seq // tm
    (p, u, vn, q_t, cmp_k, cmp_v, cmp_t, sel_t, win_t, gate_t, ksel, vsel_t, kwin, vwin_t) = _in_proj(
        x, mod, lw["w_proj"], lw["sgu_ln_g"], lw["sgu_ln_b"], TM_PROJ, tiles(TM_PROJ), batch=batch, wq_t=lw["wq_t"],
        layer=layer, depth=depth, state=kv_state)
    ps = _mixers(p, u, vn, *lw["mix_prompt"], tiles_per_batch=tiles(SGU_CHUNK))
    nc = seq // CMP_BLOCK
    kvc = _compress(cmp_k, cmp_v, sw["pe_rows"], sw["w1bd"], sw["w2bd"], nc, layer)
    kc = _even_odd(kvc[:, :LANES].astype(BF16), nc)
    vc_t = jnp.transpose(_even_odd(kvc[:, LANES:].astype(BF16), nc).reshape(batch, nc, LANES), (0, 2, 1))
    att = _attention_prompt(q_t, kc, vc_t, ksel, vsel_t, kwin, vwin_t, gate_t, sw["prompt_tables"], batch, seq)
    x2 = _channel_mix(ps, att, x, mod, lw, sw, layer, tiles)
    return x2, (cmp_t, sel_t, win_t), p


def _sample_layer(x, mod, lw, sw, layer, page_table, kvc_pages, sel_cache, win_state, pool_state, n_seq, t_new):
    tiles = lambda tm: 1
    (p, u, vn, qq, cmp_raw, sel_raw, win_raw, gates) = _in_proj(
        x, mod, lw["w_proj"], lw["sgu_ln_g"], lw["sgu_ln_b"], TM_PROJ, 1)
    p_ext = jnp.concatenate([pool_state[layer], p.reshape(n_seq, t_new, POOL_WIDTH)], axis=1)
    hist = jnp.pad(p_ext, ((0, 0), (3 * t_new - p_ext.shape[1], 0), (0, 0)))
    chunks = [hist[:, k * t_new:(k + 1) * t_new].reshape(n_seq * t_new, POOL_WIDTH) for k in range(3)]
    ps = _mixers(chunks, u, vn, *lw["mix_sample"], t_new=t_new)
    q_rows = jnp.transpose(qq.reshape(N_HEADS, n_seq, t_new, LANES), (1, 0, 2, 3)).reshape(n_seq, N_HEADS * t_new, LANES)
    new_page = lambda raw: jnp.pad(raw.reshape(n_seq, t_new, KV_WIDTH), ((0, 0), (0, PAGE_SIZE - t_new), (0, 0)))
    att = _attention_sample(page_table, kvc_pages, sel_cache, layer, q_rows, new_page(sel_raw), new_page(win_raw),
                            win_state, gates, sw["sample_tables"], t_new)
    x2 = _channel_mix(ps, att.astype(BF16), x, mod, lw, sw, layer, tiles)
    win_new_t = jnp.transpose(win_raw.reshape(n_seq, t_new, KV_WIDTH), (0, 2, 1))
    return x2, (cmp_raw, sel_raw, win_new_t, p_ext[:, p_ext.shape[1] - POOL_BUF:], vn)


def kernel(x_prompt, x_sample, cache_cmp_kv, cache_sel_kv, state_win_kv, state_pool, page_table, c_prompt, c_sample,
           w_in, w_out, pool_w, pool_scale, sgu_ln_g, sgu_ln_b, sgu_w, sgu_b, cmp_pe, cmp_w1, cmp_w2, rel_bias,
           w_mod, b_mod, ln_g, ln_b, router_w, router_b, moe_w_gate, moe_w_up, moe_w_down):
    batch, seq, d = x_prompt.shape
    n_seq, t_new, _ = x_sample.shape
    depth = w_in.shape[0]
    n_pages = page_table.shape[1]
    past_len = n_pages * PAGE_SIZE
    n_phys = cache_cmp_kv.shape[1]
    n_win = state_win_kv.shape[2]
    assert seq // SEL_BLOCK == LANES and seq % TM_MOE == 0 and (n_seq * t_new) % TM_MOE == 0
    assert POOL_BUF + 1 == 2 * t_new and past_len % TK == 0

    n_c = batch + n_seq
    c_all = jnp.pad(jnp.concatenate([c_prompt, c_sample], axis=0), ((0, -n_c % 8), (0, 0)))
    m_all = _modulation(c_all, w_mod, b_mod)
    mod_p = m_all[:, :batch].reshape(depth, batch, 6, 1, d)
    mod_s = jnp.transpose(jnp.repeat(m_all[:, batch:n_c].reshape(depth, n_seq, 6, d), t_new, axis=1), (0, 2, 1, 3))

    pe_rows, w1bd, w2bd = _prep_compress_weights(cmp_pe, cmp_w1, cmp_w2)
    near, dtab, lane_m = _prompt_bias_tables(rel_bias)
    near = near * LOG2E
    nc_past = past_len // CMP_BLOCK
    shared = {
        "pe_rows": pe_rows, "w1bd": w1bd, "w2bd": w2bd,
        "prompt_tables": (near, dtab, lane_m, _block_onehot_rows(seq)),
        "sample_tables": (_block_onehot_rows(past_len + PAGE_SIZE).T, _block_onehot_rows(past_len + PAGE_SIZE)[past_len:],
                          *_sample_bias_tables(rel_bias, past_len, t_new, n_win),
                          *_sample_sum_matrices(t_new, nc_past)),
        "rwt": router_w.T, "rb": router_b.reshape(N_EXPERTS, 1),
        "wg": moe_w_gate.astype(BF16), "wu": moe_w_up.astype(BF16), "wd": moe_w_down.astype(BF16),
    }
    layers = []
    for l in range(depth):
        layers.append({
            "w_proj": _prep_w_proj(w_in[l]), "wq_t": _prep_w_proj(w_in[l])[:, C_Q:].T, "wout": w_out[l].astype(BF16),
            "sgu_ln_g": sgu_ln_g[l].reshape(1, -1), "sgu_ln_b": sgu_ln_b[l].reshape(1, -1),
            "mix_prompt": _prep_mixer_weights(pool_w[l], pool_scale[l], sgu_w[l], sgu_b[l], SGU_CHUNK),
            "mix_sample": _prep_mixer_weights(pool_w[l], pool_scale[l], sgu_w[l], sgu_b[l], t_new),
            "ln_g0": ln_g[l, 0].reshape(1, d), "ln_b0": ln_b[l, 0].reshape(1, d),
            "ln_g1": ln_g[l, 1].reshape(1, d), "ln_b1": ln_b[l, 1].reshape(1, d),
        })

    chan_major = lambda x: jnp.transpose(x, (0, 1, 3, 4, 5, 2)).reshape(x.shape[0], x.shape[1], KV_WIDTH, x.shape[2])
    pe_pages = jnp.broadcast_to(jnp.transpose(cmp_pe, (0, 1, 3, 2))[:, :, None, :, None, :],
                                (depth, 2, N_KV, HEAD_DIM, PAGE_SIZE // CMP_BLOCK, CMP_BLOCK))
    kvc_pages = _compress_pages(chan_major(cache_cmp_kv), pe_pages.reshape(depth, KV_WIDTH, PAGE_SIZE), w1bd, w2bd, 64)
    kvc_pages = kvc_pages.reshape(depth, n_phys, PAGE_SIZE // CMP_BLOCK, KV_WIDTH)
    sel_cache = chan_major(cache_sel_kv)
    win_state = chan_major(state_win_kv)

    xp = x_prompt.reshape(batch * seq, d)
    xs = x_sample.reshape(n_seq * t_new, d)
    kv_state = tuple(jnp.zeros((depth, batch, KV_WIDTH, seq), F32) for _ in range(3))
    pool_p, outs_s = [], []
    for l in range(depth):
        xp, kv_state, p_l = _prompt_layer(xp, mod_p[l], layers[l], shared, l, depth, kv_state, batch, seq)
        pool_p.append(p_l)
        xs, st = _sample_layer(xs, mod_s[l], layers[l], shared, l, page_table, kvc_pages, sel_cache, win_state,
                               state_pool, n_seq, t_new)
        outs_s.append(st)

    kv5 = lambda x, b: x.reshape(b, -1, 2, N_KV, HEAD_DIM)
    kv6_t = lambda x: jnp.transpose(x.reshape(x.shape[:2] + (2, N_KV, HEAD_DIM, x.shape[3])), (0, 1, 5, 2, 3, 4))
    stack = lambda xs_: jnp.stack(xs_)
    w_keep = min(WINDOW, seq)
    cmp_t, sel_t, win_t = kv_state
    new_win_t = _shift_window(win_state, stack([o[2] for o in outs_s]))
    return (xp.reshape(batch, seq, d), xs.reshape(n_seq, t_new, d),
            kv6_t(cmp_t), stack([kv5(o[0], n_seq) for o in outs_s]),
            kv6_t(sel_t), stack([kv5(o[1], n_seq) for o in outs_s]),
            kv6_t(win_t[:, :, :, seq - w_keep:]), kv6_t(new_win_t),
            stack([p_l.reshape(batch, seq, POOL_WIDTH)[:, seq - POOL_BUF:] for p_l in pool_p]),
            stack([o[3] for o in outs_s]),
            stack([o[4].reshape(n_seq, t_new, SGU_WIDTH) for o in outs_s]))
```

```python
import functools
import math

import numpy as np
import jax
import jax.numpy as jnp
from jax import lax
from jax.experimental import pallas as pl
from jax.experimental.pallas import tpu as pltpu

F32 = jnp.float32
BF16 = jnp.bfloat16

D_MODEL = 1024
POOL_WIDTH = 256
SGU_WIDTH = 256
ATT_WIDTH = 512
POOL_WINDOWS = (2, 4, 8, 16)
POOL_GROUP_DIM = 64
POOL_BUF = 15
SGU_GROUPS = 4
SGU_CHUNK = 128
N_HEADS = 8
HEAD_DIM = 64
N_KV = 2
REP = 4
CMP_BLOCK = 32
CMP_HIDDEN = 128
SEL_BLOCK = 64
N_SEL = 16
WINDOW = 512
N_BUCKETS = 32
REL_MAX_DIST = 128
N_EXPERTS = 16
N_GROUPS = 4
EXPERTS_PER_GROUP = 4
D_EXPERT = 512
DEPTH = 2
ALPHA = (2 * DEPTH) ** 0.25
LN_EPS = 1e-5
FORCED_SCORE = 1e4
NEG = -1e30
LOG2E = math.log2(math.e)
PAGE_SIZE = 128

KV_WIDTH = 2 * N_KV * HEAD_DIM
LANES = 128
VMEM_LIMIT = 56 * 1024 * 1024

C_P, C_U, C_V, C_CMP, C_SEL, C_WIN, C_GATE, C_Q = 0, 256, 512, 768, 1024, 1280, 1536, 1664
W_PROJ = C_Q + N_HEADS * LANES

TQ = 256
TK = 256
NEAR_BLOCKS = 12
FAR_BUCKET_DIST = 113


def _cparams(sem):
    return pltpu.CompilerParams(dimension_semantics=sem, vmem_limit_bytes=VMEM_LIMIT)


def _layer_norm(x, g, b):
    mu = jnp.mean(x, axis=-1, keepdims=True)
    xc = x - mu
    var = jnp.mean(xc * xc, axis=-1, keepdims=True)
    return xc * lax.rsqrt(var + LN_EPS) * g + b


def _mod_kernel(c_ref, w_ref, b_ref, o_ref):
    c = c_ref[...]
    a = (c * jax.nn.sigmoid(c)).astype(BF16)
    o_ref[0] = jnp.dot(a, w_ref[0].astype(BF16), preferred_element_type=F32) + b_ref[0]


def _modulation(c_all, w_mod, b_mod):
    n, d = c_all.shape
    depth, _, w = w_mod.shape
    tn = 1536
    return pl.pallas_call(
        _mod_kernel,
        grid=(depth, w // tn),
        in_specs=[pl.BlockSpec((n, d), lambda l, j: (0, 0)),
                  pl.BlockSpec((1, d, tn), lambda l, j: (l, 0, j)),
                  pl.BlockSpec((1, 1, tn), lambda l, j: (l, 0, j))],
        out_specs=pl.BlockSpec((1, n, tn), lambda l, j: (l, 0, j)),
        out_shape=jax.ShapeDtypeStruct((depth, n, w), F32),
        compiler_params=_cparams(("arbitrary", "arbitrary")),
        name="adaln_mod",
    )(c_all, w_mod, b_mod.reshape(depth, 1, w))


def _mod_spec(mod, tm, tiles_per_batch):
    if mod.ndim == 4:
        return pl.BlockSpec((None, 6, 1, D_MODEL), lambda i, *_: (i // tiles_per_batch, 0, 0, 0))
    return pl.BlockSpec((6, tm, D_MODEL), lambda i, *_: (0, i, 0))


def _inproj_kernel(x_ref, mod_ref, w_ref, lng_ref, lnb_ref, *rest, channel_major, n_state):
    h = (x_ref[...] * (1.0 + mod_ref[1]) + mod_ref[0]).astype(BF16)

    def seg(a, b):
        return jnp.dot(h, w_ref[:, a:b], preferred_element_type=F32)

    if channel_major:
        wqt_ref = rest[0]
        p_ref, u_ref, vn_ref, qt_ref = rest[1 + n_state:5 + n_state]
        rest = rest[5 + n_state:]
        qt_ref[...] = lax.dot_general(wqt_ref[...], h, _NT, preferred_element_type=F32).astype(BF16)
    else:
        p_ref, u_ref, vn_ref, qq_ref = rest[:4]
        rest = rest[4:]
        for hd in range(N_HEADS):
            qq_ref[hd] = seg(C_Q + hd * LANES, C_Q + (hd + 1) * LANES).astype(BF16)
    p_ref[...] = seg(C_P, C_U)
    u_ref[...] = seg(C_U, C_V)
    vn_ref[...] = _layer_norm(seg(C_V, C_CMP), lng_ref[...], lnb_ref[...])
    cmp = seg(C_CMP, C_SEL)
    sel = seg(C_SEL, C_WIN)
    win = seg(C_WIN, C_GATE)
    gates = jax.nn.sigmoid(seg(C_GATE, C_Q))
    if not channel_major:
        cmp_ref, sel_ref, win_ref, gate_ref = rest
        cmp_ref[...] = cmp
        sel_ref[...] = sel
        win_ref[...] = win
        gate_ref[...] = gates
        return
    cmpk_ref, cmpv_ref, cmpt_ref, selt_ref, wint_ref, gatet_ref, ksel_ref, vselt_ref, kwin_ref, vwint_ref = rest
    cmpk_ref[...] = cmp[:, :LANES]
    cmpv_ref[...] = cmp[:, LANES:]
    cmpt_ref[...] = cmp.T
    sel_t = sel.T
    selt_ref[...] = sel_t
    ksel_ref[...] = sel[:, :LANES].astype(BF16)
    vselt_ref[...] = sel_t[LANES:, :].astype(BF16)
    win_t = win.T
    wint_ref[...] = win_t
    kwin_ref[...] = win[:, :LANES].astype(BF16)
    vwint_ref[...] = win_t[LANES:, :].astype(BF16)
    gatet_ref[...] = gates.T


def _in_proj(x, mod, w_proj, ln_g, ln_b, tm, tiles_per_batch, batch=None, wq_t=None, layer=0, depth=1, state=None):
    n = x.shape[0]
    row = lambda wd: pl.BlockSpec((tm, wd), lambda i: (i, 0))
    full = lambda a: pl.BlockSpec(a.shape, lambda i: (0,) * a.ndim)
    f32o = lambda wd: jax.ShapeDtypeStruct((n, wd), F32)
    out_specs = [row(256), row(256), row(256)]
    out_shape = [f32o(256), f32o(256), f32o(256)]
    ins, in_specs = [x, mod, w_proj, ln_g, ln_b], [row(D_MODEL), _mod_spec(mod, tm, tiles_per_batch), full(w_proj),
                                                    full(ln_g), full(ln_b)]
    aliases = {}
    if batch is None:
        out_specs += [pl.BlockSpec((N_HEADS, tm, LANES), lambda i: (0, i, 0)), row(256), row(256), row(256), row(LANES)]
        out_shape += [jax.ShapeDtypeStruct((N_HEADS, n, LANES), BF16), f32o(256), f32o(256), f32o(256), f32o(LANES)]
    else:
        ins.append(wq_t)
        in_specs.append(full(wq_t))
        out_specs.append(pl.BlockSpec((N_HEADS * LANES, tm), lambda i: (0, i)))
        out_shape.append(jax.ShapeDtypeStruct((N_HEADS * LANES, n), BF16))
        seq = n // batch
        chan = lambda c: pl.BlockSpec((None, c, tm), lambda i: (i // tiles_per_batch, 0, i % tiles_per_batch))
        chan_o = lambda c, dt: jax.ShapeDtypeStruct((batch, c, seq), dt)
        kv_state = pl.BlockSpec((None, None, KV_WIDTH, tm),
                                lambda i: (layer, i // tiles_per_batch, 0, i % tiles_per_batch))
        kv_state_o = jax.ShapeDtypeStruct((depth, batch, KV_WIDTH, seq), F32)
        first_state_out = len(out_specs) + 2
        out_specs += [row(LANES), row(LANES), kv_state, kv_state, kv_state, pl.BlockSpec((LANES, tm), lambda i: (0, i)),
                      row(LANES), chan(LANES), row(LANES), chan(LANES)]
        out_shape += [f32o(LANES), f32o(LANES),
                      kv_state_o, kv_state_o, kv_state_o, jax.ShapeDtypeStruct((LANES, n), F32),
                      jax.ShapeDtypeStruct((n, LANES), BF16), chan_o(LANES, BF16),
                      jax.ShapeDtypeStruct((n, LANES), BF16), chan_o(LANES, BF16)]
        if state is not None:
            aliases = {len(ins) + k: first_state_out + k for k in range(len(state))}
            ins += list(state)
            in_specs += [pl.BlockSpec(memory_space=pl.ANY)] * len(state)
    return pl.pallas_call(
        functools.partial(_inproj_kernel, channel_major=batch is not None, n_state=len(state or ())),
        grid=(n // tm,),
        in_specs=in_specs,
        out_specs=out_specs,
        out_shape=out_shape,
        input_output_aliases=aliases,
        compiler_params=_cparams(("arbitrary",)),
        name="in_proj",
    )(*ins)


def _prep_w_proj(w_in):
    d = w_in.shape[0]
    q = w_in[:, 768:1280].reshape(d, N_KV, REP, HEAD_DIM) * (HEAD_DIM ** -0.5)
    qq = jnp.zeros((d, N_KV, REP, N_KV, HEAD_DIM), w_in.dtype)
    for g in range(N_KV):
        qq = qq.at[:, g, :, g, :].set(q[:, g])
    gate = jnp.pad(w_in[:, 2048:2072], ((0, 0), (0, LANES - 24)))
    cols = [w_in[:, 0:768], w_in[:, 1280:2048], gate, qq.reshape(d, N_HEADS * LANES)]
    return jnp.concatenate(cols, axis=1).astype(BF16)


def _window_sums(shifted):
    acc = shifted(0)
    sums = {}
    for k in range(1, 16):
        acc = acc + shifted(k)
        if k + 1 in POOL_WINDOWS:
            sums[k + 1] = acc
    lane = lax.broadcasted_iota(jnp.int32, acc.shape, 1)
    return jnp.where(lane < 64, sums[2], jnp.where(lane < 128, sums[4], jnp.where(lane < 192, sums[8], sums[16])))


def _mixers_tail(sums, cnt, cur, u, vn, wpool_ref, pscale_ref, wcat_ref, sbias_ref, o_ref):
    diff = (sums / cnt - cur).astype(BF16)
    pool = jnp.dot(diff, wpool_ref[...], preferred_element_type=F32) * pscale_ref[...]
    lane = lax.broadcasted_iota(jnp.int32, vn.shape, 1)
    vb = vn.astype(BF16)
    zero = jnp.zeros_like(vb)
    stacked = jnp.concatenate([jnp.where((lane // 64) == g, vb, zero) for g in range(SGU_GROUPS)], axis=0)
    mixed = jnp.dot(wcat_ref[...], stacked, preferred_element_type=F32) + sbias_ref[...]
    o_ref[:, :POOL_WIDTH] = pool.astype(BF16)
    o_ref[:, POOL_WIDTH:] = (u * mixed).astype(BF16)


def _lane_window():
    lane = lax.broadcasted_iota(jnp.int32, (SGU_CHUNK, POOL_WIDTH), 1)
    return jnp.where(lane < 64, 2, jnp.where(lane < 128, 4, jnp.where(lane < 192, 8, 16)))


def _mix_prompt_kernel(p_ref, u_ref, vn_ref, wpool_ref, pscale_ref, wcat_ref, sbias_ref, o_ref, prev_ref,
                       *, tiles_per_batch):
    t = pl.program_id(0) % tiles_per_batch

    @pl.when(t == 0)
    def _():
        prev_ref[...] = jnp.zeros_like(prev_ref)

    cur = p_ref[...]
    prev = prev_ref[...]
    row = lax.broadcasted_iota(jnp.int32, cur.shape, 0)

    def shifted(k):
        if k == 0:
            return cur
        return jnp.where(row >= k, pltpu.roll(cur, k, 0), pltpu.roll(prev, k, 0))

    sums = _window_sums(shifted)
    prev_ref[...] = cur
    cnt = jnp.minimum(_lane_window(), t * SGU_CHUNK + row + 1).astype(F32)
    _mixers_tail(sums, cnt, cur, u_ref[...], vn_ref[...], wpool_ref, pscale_ref, wcat_ref, sbias_ref, o_ref)


def _mix_sample_kernel(pa_ref, pb_ref, pc_ref, u_ref, vn_ref, wpool_ref, pscale_ref, wcat_ref, sbias_ref, o_ref,
                       *, t_new):
    a, b, c = pa_ref[...], pb_ref[...], pc_ref[...]
    rows = c.shape[0]
    t = lax.broadcasted_iota(jnp.int32, c.shape, 0) % t_new

    def shifted(k):
        if k == 0:
            return c
        hi, lo = (c, b) if k < t_new else (b, a)
        kk = k % t_new
        if kk == 0:
            return hi
        return jnp.where(t >= kk, pltpu.roll(hi, kk, 0), pltpu.roll(lo, rows - t_new + kk, 0))

    sums = _window_sums(shifted)
    cnt = _lane_window().astype(F32)
    _mixers_tail(sums, cnt, c, u_ref[...], vn_ref[...], wpool_ref, pscale_ref, wcat_ref, sbias_ref, o_ref)


def _mixers(p_in, u, vn, wpool, pscale, wcat, sbias, tiles_per_batch=None, t_new=None):
    n = u.shape[0]
    tm = SGU_CHUNK
    row = lambda wd: pl.BlockSpec((tm, wd), lambda i: (i, 0))
    full = lambda a: pl.BlockSpec(a.shape, lambda i: (0,) * a.ndim)
    weights = [wpool, pscale, wcat, sbias]
    if t_new is None:
        kern = functools.partial(_mix_prompt_kernel, tiles_per_batch=tiles_per_batch)
        ins, scratch = [p_in], [pltpu.VMEM((tm, POOL_WIDTH), F32)]
    else:
        kern = functools.partial(_mix_sample_kernel, t_new=t_new)
        ins, scratch = list(p_in), []
    return pl.pallas_call(
        kern,
        grid=(n // tm,),
        in_specs=[row(256)] * (len(ins) + 2) + [full(a) for a in weights],
        out_specs=row(512),
        out_shape=jax.ShapeDtypeStruct((n, 512), BF16),
        scratch_shapes=scratch,
        compiler_params=_cparams(("arbitrary",)),
        name="mixers",
    )(*ins, u, vn, *weights)


def _prep_mixer_weights(pool_w, pool_scale, sgu_w, sgu_b, chunk):
    wpool = jax.scipy.linalg.block_diag(*[pool_w[g] for g in range(len(POOL_WINDOWS))]).astype(BF16)
    tri = jnp.tril(jnp.ones((chunk, chunk), bool))
    w = jnp.where(tri, sgu_w[:, :chunk, :chunk], 0.0)
    reps = SGU_CHUNK // chunk
    eye = jnp.eye(reps, dtype=w.dtype)
    wt = jnp.concatenate([jnp.kron(eye, w[g]) for g in range(SGU_GROUPS)], axis=1).astype(BF16)
    b = jnp.tile(sgu_b[:, :chunk], (1, reps))
    sbias = jnp.repeat(b.T, SGU_WIDTH // SGU_GROUPS, axis=1)
    return wpool, pool_scale.reshape(1, POOL_WIDTH), wt, sbias


def _compress_mlp(token_rows, w1_ref, w2_ref, o_ref, nblk):
    for s in range(2):
        acc = jnp.zeros((nblk, N_KV * CMP_HIDDEN), F32)
        for j in range(0, CMP_BLOCK, 2):
            pair = jnp.concatenate([token_rows(s, j), token_rows(s, j + 1)], axis=1)
            acc = acc + jnp.dot(pair, w1_ref[0, s, j // 2], preferred_element_type=F32)
        hdn = jax.nn.gelu(acc).astype(BF16)
        o_ref[0, :, s * LANES:(s + 1) * LANES] = jnp.dot(hdn, w2_ref[0, s], preferred_element_type=F32)


def _compress_kernel(k_ref, v_ref, pe_ref, w1_ref, w2_ref, o_ref, *, nblk):
    bufs = (k_ref, v_ref)
    rows = lambda s, j: (bufs[s][pl.ds(j, nblk, stride=CMP_BLOCK), :] + pe_ref[0, s, j:j + 1, :]).astype(BF16)
    _compress_mlp(rows, w1_ref, w2_ref, o_ref, nblk)


PAIR_TOKENS = 2 * PAGE_SIZE
PAIR_BLOCKS = PAIR_TOKENS // CMP_BLOCK


def _compress_pages_kernel(raw_ref, pe_ref, perm_ref, w1_ref, w2_ref, o_ref, k_ref, v_ref, *, nblk):
    bufs = (k_ref, v_ref)

    def body(q, carry):
        for s in range(2):
            ch = slice(s * LANES, (s + 1) * LANES)
            x = jnp.concatenate([raw_ref[0, 2 * q, ch, :] + pe_ref[0, ch, :],
                                 raw_ref[0, 2 * q + 1, ch, :] + pe_ref[0, ch, :]], axis=1).astype(BF16)
            t = lax.dot_general(perm_ref[...], x, _NT, preferred_element_type=F32)
            rows = pl.ds(pl.multiple_of(q * PAIR_BLOCKS, PAIR_BLOCKS), PAIR_BLOCKS)
            bufs[s][:, rows, :] = t.reshape(CMP_BLOCK, PAIR_BLOCKS, LANES)
        return carry

    lax.fori_loop(0, raw_ref.shape[1] // 2, body, 0, unroll=4)
    _compress_mlp(lambda s, j: bufs[s][j].astype(BF16), w1_ref, w2_ref, o_ref, nblk)


def _compress_pages(pages, pe_pages, w1bd, w2bd, pages_per_step):
    nl, n_pages, _, _ = pages.shape
    assert n_pages % pages_per_step == 0 and pages_per_step % 2 == 0
    nblk = pages_per_step * (PAGE_SIZE // CMP_BLOCK)
    tok = np.arange(PAIR_TOKENS)
    dest = (tok % CMP_BLOCK) * PAIR_BLOCKS + tok // CMP_BLOCK
    perm = jnp.asarray((np.arange(PAIR_TOKENS)[:, None] == dest[None, :]).astype(np.float32)).astype(BF16)
    wspec = lambda a: pl.BlockSpec((1,) + a.shape[1:], lambda l, i: (l,) + (0,) * (a.ndim - 1))
    return pl.pallas_call(
        functools.partial(_compress_pages_kernel, nblk=nblk),
        grid=(nl, n_pages // pages_per_step),
        in_specs=[pl.BlockSpec((1, pages_per_step, KV_WIDTH, PAGE_SIZE), lambda l, i: (l, i, 0, 0)),
                  wspec(pe_pages), pl.BlockSpec(perm.shape, lambda l, i: (0, 0)), wspec(w1bd), wspec(w2bd)],
        out_specs=pl.BlockSpec((1, nblk, KV_WIDTH), lambda l, i: (l, i, 0)),
        out_shape=jax.ShapeDtypeStruct((nl, n_pages * (PAGE_SIZE // CMP_BLOCK), KV_WIDTH), F32),
        scratch_shapes=[pltpu.VMEM((CMP_BLOCK, nblk, LANES), F32)] * 2,
        compiler_params=_cparams(("arbitrary", "arbitrary")),
        name="compress_pages",
    )(pages, pe_pages, perm, w1bd, w2bd)


def _compress(k_raw, v_raw, pe_rows, w1bd, w2bd, nblk, layer):
    r = k_raw.shape[0]
    assert r % (nblk * CMP_BLOCK) == 0
    wspec = lambda a: pl.BlockSpec((1,) + a.shape[1:], lambda i: (layer,) + (0,) * (a.ndim - 1))
    rows = pl.BlockSpec((nblk * CMP_BLOCK, LANES), lambda i: (i, 0))
    return pl.pallas_call(
        functools.partial(_compress_kernel, nblk=nblk),
        grid=(r // (nblk * CMP_BLOCK),),
        in_specs=[rows, rows, wspec(pe_rows), wspec(w1bd), wspec(w2bd)],
        out_specs=pl.BlockSpec((1, nblk, KV_WIDTH), lambda i: (0, i, 0)),
        out_shape=jax.ShapeDtypeStruct((1, r // CMP_BLOCK, KV_WIDTH), F32),
        compiler_params=_cparams(("arbitrary",)),
        name="compress",
    )(k_raw, v_raw, pe_rows, w1bd, w2bd)[0]


def _prep_compress_weights(cmp_pe, cmp_w1, cmp_w2):
    nl = cmp_pe.shape[0]
    pe_rows = jnp.concatenate([cmp_pe] * N_KV, axis=-1)
    w1 = cmp_w1.reshape(nl, 2, CMP_BLOCK, HEAD_DIM, CMP_HIDDEN)
    eye = jnp.eye(N_KV, dtype=F32)
    w1bd = w1[:, :, :, None, :, None, :] * eye[None, None, None, :, None, :, None]
    w2bd = cmp_w2[:, :, None, :, None, :] * eye[None, None, :, None, :, None]
    return (pe_rows, w1bd.reshape(nl, 2, CMP_BLOCK // 2, 2 * LANES, N_KV * CMP_HIDDEN).astype(BF16),
            w2bd.reshape(nl, 2, N_KV * CMP_HIDDEN, LANES).astype(BF16))


def _np_bucket(dist):
    n = np.maximum(dist, 0)
    nf = np.maximum(n, 1).astype(np.float32)
    large = 16 + (np.log(nf / np.float32(16)) / np.float32(math.log(REL_MAX_DIST / 16)) * np.float32(16)).astype(np.int32)
    return np.where(n < 16, n, np.minimum(large, N_BUCKETS - 1)).astype(np.int32)


def _bucket_values(tbt, dist):
    hit = jnp.asarray(_np_bucket(dist))[None, ..., None] == jnp.arange(N_BUCKETS, dtype=jnp.int32)
    return jnp.sum(jnp.where(hit, tbt.reshape((tbt.shape[0],) + (1,) * dist.ndim + (N_BUCKETS,)), 0.0), axis=-1)


def _bias_minus_far(rel_bias, dist):
    tbt = rel_bias.astype(F32).T
    val = _bucket_values(tbt, dist) - tbt[:, N_BUCKETS - 1].reshape((-1,) + (1,) * dist.ndim)
    return jnp.where(jnp.asarray(dist >= 0), val, NEG)


def _split3(x):
    hi = x.astype(BF16)
    r1 = x - hi.astype(F32)
    mid = r1.astype(BF16)
    lo = (r1 - mid.astype(F32)).astype(BF16)
    return hi, mid, lo


def _prompt_bias_tables(rel_bias):
    a = np.arange(TQ)[:, None]
    span = 2 * TK + TQ
    per_dist = _bias_minus_far(rel_bias, np.arange(span) - (TQ - 1))
    skew = jnp.broadcast_to(per_dist[:, None, :], (N_HEADS, 2 * TK, span)).reshape(N_HEADS, 2 * TK * span)
    skew = skew[:, :2 * TK * (span - 1)].reshape(N_HEADS, 2 * TK, span - 1)
    near = skew[:, :, 2 * TK - 1:2 * TK - 1 + TQ]
    dist_c = a + (4 * CMP_BLOCK - CMP_BLOCK + 1) - CMP_BLOCK * np.arange(NEAR_BLOCKS)[None, :]
    dc = _bias_minus_far(rel_bias, dist_c)
    hi, mid, lo = _split3(dc)
    cols = jnp.stack([hi, mid, lo], axis=-1).reshape(N_HEADS, TQ, 3 * NEAR_BLOCKS)
    future = jnp.full((N_HEADS, TQ, 1), NEG, F32).astype(BF16)
    pad = jnp.zeros((N_HEADS, TQ, LANES - 3 * NEAR_BLOCKS - 1), BF16)
    dtab = jnp.concatenate([cols, future, pad], axis=-1).reshape(N_HEADS * TQ, LANES).T
    lane = np.arange(LANES)
    lane_m = np.where(lane < 3 * NEAR_BLOCKS, lane // 3, -1000).astype(np.int32)
    return near, dtab, jnp.asarray(np.tile(lane_m[None, :], (8, 1)))


_NT = (((1,), (1,)), ((), ()))


def _select_start(imp_t, cur):
    blk = lax.broadcasted_iota(jnp.int32, imp_t.shape, 0)
    forced = (blk == 0) | (blk == cur) | (blk == cur - 1)
    vals = jnp.where(forced, FORCED_SCORE, jnp.where(blk <= cur, imp_t, -1.0))
    return vals, jnp.full(imp_t.shape, NEG, F32)


def _select_round(vals, neg):
    blk = lax.broadcasted_iota(jnp.int32, vals.shape, 0)
    mx = jnp.max(vals, axis=0, keepdims=True)
    first = jnp.min(jnp.where(vals == mx, blk, 1 << 20), axis=0, keepdims=True)
    pick = blk == first
    return jnp.where(pick, -3e38, vals), jnp.where(pick, 0.0, neg)


def _select_blocks(imp_t, cur):
    _, neg = lax.fori_loop(0, N_SEL, lambda _, c: _select_round(*c), _select_start(imp_t, cur))
    return neg


def _attn_kernel(qt_ref, kc_ref, vct_ref, dtabt_ref, lanem_ref, ksel_ref, vselt_ref, kwin_ref, vwint_ref,
                 xt_ref, near_ref, gatet_ref, o_ref, qs_ref, s_ref, s2_ref, p_ref, m_ref, l_ref, alpha_ref, acc_ref, out_ref,
                 *, tiles_per_batch):
    u = pl.program_id(0) % tiles_per_batch
    nc = kc_ref.shape[0]
    ns = nc // 2
    gcols = REP * TQ

    def gate(h, br):
        c = h * 3 + br
        return gatet_ref[c:c + 1, :]

    def gate_row(g, br):
        return jnp.concatenate([gate(g * REP + r, br) for r in range(REP)], axis=1)

    def group_rows(h):
        g = h // REP
        return slice(g * HEAD_DIM, (g + 1) * HEAD_DIM)

    def head_cols(h):
        return slice((h % REP) * TQ, (h % REP + 1) * TQ)

    c = lax.broadcasted_iota(jnp.int32, (nc, LANES), 0)
    lane = lax.broadcasted_iota(jnp.int32, (nc, LANES), 1)
    rel = jnp.where(c < ns, 2 * c, 2 * c - (nc - 1)) - (8 * u - 4)
    near_hit = jnp.where(rel == lanem_ref[0:1, :], 1.0, 0.0)
    future_hit = jnp.where(rel >= NEAR_BLOCKS, 1.0, 0.0)
    onehot = jnp.where(lane < 3 * NEAR_BLOCKS, near_hit, jnp.where(lane == 3 * NEAR_BLOCKS, future_hit, 0.0))
    kk_c = jnp.concatenate([kc_ref[...], onehot.astype(BF16)], axis=1)
    for h in range(N_HEADS):
        qh = jnp.concatenate([qt_ref[h * LANES:(h + 1) * LANES, :], dtabt_ref[:, h * TQ:(h + 1) * TQ]], axis=0)
        s_ref[:, h * TQ:(h + 1) * TQ] = jnp.dot(kk_c, qh, preferred_element_type=F32)
    imp = [[jnp.zeros((ns, LANES), F32) for _ in range(TQ // LANES)] for _ in range(N_KV)]
    for c0 in range(0, N_HEADS * TQ, LANES):
        s = s_ref[:, c0:c0 + LANES]
        mx = jnp.max(s, axis=0, keepdims=True)
        p = jnp.where(s > 0.1 * NEG, jnp.exp(s - mx), 0.0)
        pn = p / jnp.maximum(jnp.sum(p, axis=0, keepdims=True), 1e-30)
        g, part = c0 // gcols, (c0 % TQ) // LANES
        imp[g][part] = imp[g][part] + (pn[:ns, :] + pn[ns:, :])
        p_ref[:, c0:c0 + LANES] = pn.astype(BF16)
    imp = [jnp.concatenate(parts, axis=1) for parts in imp]
    for h in range(N_HEADS):
        out_ref[h // REP, :, head_cols(h)] = gate(h, 0) * jnp.dot(
            vct_ref[group_rows(h), :], p_ref[:, h * TQ:(h + 1) * TQ], preferred_element_type=F32)

    imp_t = jnp.concatenate(imp, axis=1)
    a = lax.broadcasted_iota(jnp.int32, imp_t.shape, 1) & (TQ - 1)
    cur = (TQ // SEL_BLOCK) * u + (a >> 6)
    neg = _select_blocks(imp_t, cur).astype(BF16)
    for h in range(N_HEADS):
        g = h // REP
        qs_ref[:LANES, h * TQ:(h + 1) * TQ] = (qt_ref[h * LANES:(h + 1) * LANES, :].astype(F32) * LOG2E).astype(BF16)
        qs_ref[LANES:, h * TQ:(h + 1) * TQ] = neg[:, g * TQ:(g + 1) * TQ]

    def reset():
        m_ref[...] = jnp.full(m_ref.shape, -1e38, F32)
        l_ref[...] = jnp.zeros(l_ref.shape, F32)
        acc_ref[...] = jnp.zeros(acc_ref.shape, F32)

    def finish(br):
        for g in range(N_KV):
            cols = slice(g * gcols, (g + 1) * gcols)
            out_ref[g] = out_ref[g] + gate_row(g, br) * (acc_ref[g] / l_ref[:, cols])

    def tile_keys(kt):
        return pl.ds(pl.multiple_of(kt * TK, TK), TK)

    def softmax_tile(buf, vt_ref, kt, bias):
        keys = tile_keys(kt)
        for g in range(N_KV):
            for c0 in range(g * gcols, (g + 1) * gcols, LANES):
                cols = slice(c0, c0 + LANES)
                s = bias(c0 // TQ, c0 % TQ, buf[:, cols])
                m_prev = m_ref[:, cols]
                m_new = jnp.maximum(m_prev, jnp.max(s, axis=0, keepdims=True))
                alpha = jnp.exp2(m_prev - m_new)
                p = jnp.exp2(s - m_new)
                l_ref[:, cols] = alpha * l_ref[:, cols] + jnp.sum(p, axis=0, keepdims=True)
                p_ref[:, cols] = p.astype(BF16)
                alpha_ref[:, cols] = alpha
                m_ref[:, cols] = m_new
            cols = slice(g * gcols, (g + 1) * gcols)
            pv = jnp.dot(vt_ref[g * HEAD_DIM:(g + 1) * HEAD_DIM, keys], p_ref[:, cols], preferred_element_type=F32)
            acc_ref[g] = alpha_ref[:, cols] * acc_ref[g] + pv

    def near_bias(row):
        if row is None:
            return lambda h, q0, s: s
        return lambda h, q0, s: s + near_ref[h, row:row + TK, q0:q0 + LANES]

    reset()

    def sel_scores(kt, buf):
        keys = tile_keys(kt)
        kk = jnp.concatenate([ksel_ref[keys, :], xt_ref[keys, :]], axis=1)
        buf[...] = jnp.dot(kk, qs_ref[...], preferred_element_type=F32)

    def stage(pred, cur_buf, cur_tile, bias, next_buf=None, next_tile=None):
        @pl.when(pred)
        def _():
            if next_buf is not None:
                sel_scores(next_tile, next_buf)
            softmax_tile(cur_buf, vselt_ref, cur_tile, bias)

    far, near, diag = near_bias(None), near_bias(0), near_bias(TK)
    sel_scores(0, s_ref)

    def far_pair(i, carry):
        in_range = 2 * i + 1 < u
        stage(in_range, s_ref, 2 * i, far, s2_ref, 2 * i + 1)
        stage(in_range, s2_ref, 2 * i + 1, far, s_ref, 2 * i + 2)
        return carry

    lax.fori_loop(0, (u - 1) // 2, far_pair, 0)

    stage(u == 0, s_ref, 0, diag)
    odd = u % 2 == 1
    stage(odd, s_ref, u - 1, near, s2_ref, u)
    stage(odd, s2_ref, u, diag)
    even = (u >= 2) & (u % 2 == 0)
    stage(even, s_ref, u - 2, far, s2_ref, u - 1)
    stage(even, s2_ref, u - 1, near, s_ref, u)
    stage(even, s_ref, u, diag)
    finish(1)

    reset()

    def later_keys_only(h, q0, s):
        kj = lax.broadcasted_iota(jnp.int32, s.shape, 0)
        qa = lax.broadcasted_iota(jnp.int32, s.shape, 1) + q0
        return jnp.where(kj > qa, s, NEG)

    def win_tile(kt, bias):
        s_ref[...] = jnp.dot(kwin_ref[tile_keys(kt), :], qs_ref[:LANES, :], preferred_element_type=F32)
        softmax_tile(s_ref, vwint_ref, kt, bias)

    @pl.when(u >= 2)
    def _():
        win_tile(u - 2, later_keys_only)

    @pl.when(u >= 1)
    def _():
        win_tile(u - 1, near)

    win_tile(u, diag)
    finish(2)

    for i in range(N_HEADS // 2):
        pair = jnp.concatenate([out_ref[(2 * i) // REP, :, head_cols(2 * i)],
                                out_ref[(2 * i + 1) // REP, :, head_cols(2 * i + 1)]], axis=0)
        o_ref[:, i * LANES:(i + 1) * LANES] = pair.T.astype(BF16)


def _attention_prompt(qt, kc, vct, ksel, vselt, kwin, vwint, gatet, tables, batch, seq):
    near, dtab, lane_m, xt = tables
    n = batch * seq
    tpb = seq // TQ
    nc = seq // CMP_BLOCK
    assert nc == TK
    per_batch = lambda rows: pl.BlockSpec((rows, LANES), lambda i: (i // tpb, 0))
    chan = lambda cols: pl.BlockSpec((None, LANES, cols), lambda i: (i // tpb, 0, 0))
    full = lambda a: pl.BlockSpec(a.shape, lambda i: (0,) * a.ndim)
    return pl.pallas_call(
        functools.partial(_attn_kernel, tiles_per_batch=tpb),
        grid=(n // TQ,),
        in_specs=[pl.BlockSpec((N_HEADS * LANES, TQ), lambda i: (0, i)),
                  per_batch(nc), chan(nc), full(dtab), full(lane_m),
                  per_batch(seq), chan(seq), per_batch(seq), chan(seq),
                  full(xt), full(near), pl.BlockSpec((LANES, TQ), lambda i: (0, i))],
        out_specs=pl.BlockSpec((TQ, ATT_WIDTH), lambda i: (i, 0)),
        out_shape=jax.ShapeDtypeStruct((n, ATT_WIDTH), BF16),
        scratch_shapes=[pltpu.VMEM((2 * LANES, N_HEADS * TQ), BF16),
                        pltpu.VMEM((TK, N_HEADS * TQ), F32), pltpu.VMEM((TK, N_HEADS * TQ), F32),
                        pltpu.VMEM((TK, N_HEADS * TQ), BF16),
                        pltpu.VMEM((1, N_HEADS * TQ), F32), pltpu.VMEM((1, N_HEADS * TQ), F32),
                        pltpu.VMEM((1, N_HEADS * TQ), F32),
                        pltpu.VMEM((N_KV, HEAD_DIM, REP * TQ), F32), pltpu.VMEM((N_KV, HEAD_DIM, REP * TQ), F32)],
        compiler_params=_cparams(("arbitrary",)),
        name="nsa_prompt",
    )(qt, kc, vct, dtab, lane_m, ksel, vselt, kwin, vwint, xt, near, gatet)


def _block_onehot_rows(seq):
    j = np.arange(seq)[:, None] // SEL_BLOCK
    return jnp.asarray((j == np.arange(LANES)[None, :]).astype(np.float32)).astype(BF16)


def _even_odd(x, nc):
    x = x.reshape(-1, nc // 2, 2, x.shape[-1])
    return jnp.concatenate([x[:, :, 0], x[:, :, 1]], axis=1).reshape(-1, x.shape[-1])


def _sample_bias_tables(rel_bias, past_len, t_new, n_win):
    t = np.arange(t_new)[:, None]
    nc = past_len // CMP_BLOCK
    tb_full = lambda dist, ok: jnp.where(jnp.asarray(ok), _bucket_values(rel_bias.astype(F32).T, dist), NEG)
    dist_c = past_len + t - (CMP_BLOCK * np.arange(nc)[None, :] + CMP_BLOCK - 1)
    dist_s = past_len + t - np.arange(past_len)[None, :]
    jn = np.arange(LANES)[None, :]
    dist_n = t - jn
    dist_w = n_win + t - np.arange(n_win)[None, :]
    rows = lambda x: x.reshape(N_HEADS * t_new, x.shape[-1])
    return (rows(tb_full(dist_c, dist_c >= 0)), rows(tb_full(dist_s, dist_s >= 0)),
            rows(tb_full(dist_n, (dist_n >= 0) & (jn < t_new))),
            rows(tb_full(dist_w, (dist_w >= 0) & (dist_w < WINDOW))))


SEQ_PER_STEP = 4


def _attn_sample_kernel(pt_ref, *refs, n_pages, t_new):
    del pt_ref
    n_par = SEQ_PER_STEP
    kvc_refs = refs[:n_par * n_pages]
    sel_refs = refs[n_par * n_pages:2 * n_par * n_pages]
    (qq_ref, seln_ref, winn_ref, winb_ref, gate_ref, xt_ref, xtn_ref, bc_ref, bs_ref, bn_ref, bw_ref,
     rsum_ref, pair_ref, o_ref, qs_ref, out_ref, s_ref, p_ref) = refs[2 * n_par * n_pages:]
    tables = (xt_ref, xtn_ref, bc_ref, bs_ref, bn_ref, bw_ref, rsum_ref, pair_ref)
    chains = []
    for s in range(n_par):
        rows = pl.ds(s * t_new, t_new)
        chains.append(_sample_sequence(
            kvc_refs[s * n_pages:(s + 1) * n_pages], sel_refs[s * n_pages:(s + 1) * n_pages],
            qq_ref.at[s], seln_ref.at[s], winn_ref.at[s], winb_ref.at[s], gate_ref.at[rows], tables,
            o_ref.at[rows], qs_ref.at[s], out_ref.at[s], s_ref.at[s], p_ref.at[s], n_pages, t_new))
    while chains:
        chains = [c for c in chains if next(c, "done") != "done"]


def _sample_sequence(kvc_refs, sel_refs, qq_ref, seln_ref, winn_ref, winb_ref, gate_ref, tables, o_ref,
                     qs_ref, out_ref, s_ref, p_ref, n_pages, t_new):
    xt_ref, xtn_ref, bc_ref, bs_ref, bn_ref, bw_ref, rsum_ref, pair_ref = tables

    def gate_col(br):
        return jnp.concatenate([gate_ref[:, h * 3 + br:h * 3 + br + 1] for h in range(N_HEADS)], axis=0)

    def softmax_tiles(q, tiles):
        off = 0
        for k, _, bias, channel_major, width in tiles:
            if channel_major:
                s = jnp.dot(q, k(), preferred_element_type=F32)
            else:
                s = lax.dot_general(q, k(), _NT, preferred_element_type=F32)
            s_ref[:, off:off + width] = s + bias()
            off += width
            yield
        s = s_ref[:, :off]
        p = jnp.exp(s - jnp.max(s, axis=1, keepdims=True))
        den = jnp.sum(p, axis=1, keepdims=True)
        p_ref[:, :off] = p.astype(BF16)
        acc, off = None, 0
        for _, v, _, channel_major, width in tiles:
            p_t = p_ref[:, off:off + width]
            if channel_major:
                pv = lax.dot_general(p_t, v(), _NT, preferred_element_type=F32)
            else:
                pv = jnp.dot(p_t, v(), preferred_element_type=F32)
            acc = pv if acc is None else acc + pv
            off += width
            yield
        return acc / den

    qs_ref[:, :LANES] = qq_ref[...]

    kvc = jnp.concatenate([r[...] for r in kvc_refs], axis=0)
    s = lax.dot_general(qq_ref[...], kvc[:, :LANES].astype(BF16), _NT, preferred_element_type=F32) + bc_ref[...]
    mx = jnp.max(s, axis=1, keepdims=True)
    p = jnp.where(s > 0.1 * NEG, jnp.exp(s - mx), 0.0)
    pn = p / jnp.maximum(jnp.sum(p, axis=1, keepdims=True), 1e-30)
    out_ref[...] = gate_col(0) * jnp.dot(pn.astype(BF16), kvc[:, LANES:].astype(BF16), preferred_element_type=F32)
    yield

    hp = lax.Precision.HIGHEST
    imp = jnp.dot(jnp.dot(rsum_ref[...], pn, precision=hp, preferred_element_type=F32), pair_ref[...],
                  precision=hp, preferred_element_type=F32)
    cur = (n_pages * PAGE_SIZE) // SEL_BLOCK
    yield
    vals, neg = _select_start(imp.T, cur)
    for _ in range(N_SEL):
        vals, neg = _select_round(vals, neg)
        yield
    neg = neg.T
    qs_ref[:, LANES:] = jnp.concatenate(
        [neg[(h // REP) * t_new:(h // REP + 1) * t_new, :] for h in range(N_HEADS)], axis=0).astype(BF16)

    def past_tile(ref, j):
        cols = slice(j * PAGE_SIZE, (j + 1) * PAGE_SIZE)
        return (lambda: jnp.concatenate([ref[:LANES, :].astype(BF16), xt_ref[:, cols]], axis=0),
                lambda: ref[LANES:, :].astype(BF16), lambda: bs_ref[:, cols], True, PAGE_SIZE)

    new_sel_tile = (lambda: jnp.concatenate([seln_ref[:, :LANES].astype(BF16), xtn_ref[...]], axis=1),
                    lambda: seln_ref[:, LANES:].astype(BF16), lambda: bn_ref[...], False, PAGE_SIZE)
    tiles = [past_tile(sel_refs[j], j) for j in range(n_pages)] + [new_sel_tile]
    out_ref[...] += gate_col(1) * (yield from softmax_tiles(qs_ref[...], tiles))

    tiles = [(lambda: winb_ref[:LANES, :].astype(BF16), lambda: winb_ref[LANES:, :].astype(BF16), lambda: bw_ref[...],
              True, winb_ref.shape[1]),
             (lambda: winn_ref[:, :LANES].astype(BF16), lambda: winn_ref[:, LANES:].astype(BF16), lambda: bn_ref[...],
              False, PAGE_SIZE)]
    out_ref[...] += gate_col(2) * (yield from softmax_tiles(qq_ref[...], tiles))

    lane_o = lax.broadcasted_iota(jnp.int32, (t_new, LANES), 1)
    for i in range(N_HEADS // 2):
        left = out_ref[2 * i * t_new:(2 * i + 1) * t_new, :]
        right = out_ref[(2 * i + 1) * t_new:(2 * i + 2) * t_new, :]
        if (2 * i) // REP == 0:
            right = pltpu.roll(right, HEAD_DIM, 1)
        else:
            left = pltpu.roll(left, HEAD_DIM, 1)
        o_ref[:, i * LANES:(i + 1) * LANES] = jnp.where(lane_o < HEAD_DIM, left, right)


def _attention_sample(page_table, kvc_pages, sel_cache, layer, qq, sel_new, win_new, win_buf, gates, tables, t_new):
    n_seq, n_pages = page_table.shape
    xt, xtn, bc, bs, bn, bw, rsum, pair = tables
    n_win = win_buf.shape[3]
    nq = N_HEADS * t_new
    par = SEQ_PER_STEP
    assert n_seq % par == 0
    page_spec = lambda shape, s, j: pl.BlockSpec((None, None) + shape,
                                                 lambda b, pt, s=s, j=j: (layer, pt[par * b + s, j], 0, 0))
    full = lambda a: pl.BlockSpec(a.shape, lambda b, pt: (0,) * a.ndim)
    new = pl.BlockSpec((par, PAGE_SIZE, KV_WIDTH), lambda b, pt: (b, 0, 0))
    in_specs = ([page_spec((PAGE_SIZE // CMP_BLOCK, KV_WIDTH), s, j) for s in range(par) for j in range(n_pages)]
                + [page_spec((KV_WIDTH, PAGE_SIZE), s, j) for s in range(par) for j in range(n_pages)]
                + [pl.BlockSpec((par, nq, LANES), lambda b, pt: (b, 0, 0)), new, new,
                   pl.BlockSpec((None, par, KV_WIDTH, n_win), lambda b, pt: (layer, b, 0, 0)),
                   pl.BlockSpec((par * t_new, LANES), lambda b, pt: (b, 0))]
                + [full(a) for a in (xt, xtn, bc, bs, bn, bw, rsum, pair)])
    return pl.pallas_call(
        functools.partial(_attn_sample_kernel, n_pages=n_pages, t_new=t_new),
        grid_spec=pltpu.PrefetchScalarGridSpec(
            num_scalar_prefetch=1, grid=(n_seq // par,), in_specs=in_specs,
            out_specs=pl.BlockSpec((par * t_new, ATT_WIDTH), lambda b, pt: (b, 0)),
            scratch_shapes=[pltpu.VMEM((par, nq, 2 * LANES), BF16), pltpu.VMEM((par, nq, LANES), F32),
                            pltpu.VMEM((par, nq, (n_pages + 1) * PAGE_SIZE), F32),
                            pltpu.VMEM((par, nq, (n_pages + 1) * PAGE_SIZE), BF16)]),
        out_shape=jax.ShapeDtypeStruct((n_seq * t_new, ATT_WIDTH), F32),
        compiler_params=_cparams(("arbitrary",)),
        name="nsa_sample",
    )(page_table, *([kvc_pages] * (par * n_pages)), *([sel_cache] * (par * n_pages)), qq, sel_new, win_new, win_buf,
      gates, xt, xtn, bc, bs, bn, bw, rsum, pair)


def _sample_sum_matrices(t_new, nc):
    rsum = np.zeros((LANES, N_HEADS * t_new), np.float32)
    for h in range(N_HEADS):
        for t in range(t_new):
            rsum[(h // REP) * t_new + t, h * t_new + t] = 1.0
    pair = np.zeros((nc, LANES), np.float32)
    pair[np.arange(nc), np.arange(nc) // 2] = 1.0
    return jnp.asarray(rsum), jnp.asarray(pair)


def _rank_before(vals, k):
    r = jnp.zeros(vals[k].shape, jnp.int32)
    for j, vj in enumerate(vals):
        if j < k:
            r = r + jnp.where(vj >= vals[k], 1, 0)
        elif j > k:
            r = r + jnp.where(vj > vals[k], 1, 0)
    return r


def _route(s_rows, b_rows):
    scores = []
    for g in range(N_GROUPS):
        b0, b1, b2, b3 = b_rows[4 * g:4 * g + 4]
        hi01, lo01, hi23, lo23 = jnp.maximum(b0, b1), jnp.minimum(b0, b1), jnp.maximum(b2, b3), jnp.minimum(b2, b3)
        top1 = jnp.maximum(hi01, hi23)
        top2 = jnp.maximum(jnp.maximum(lo01, lo23), jnp.minimum(hi01, hi23))
        scores.append(top1 + top2)
    in_group = [_rank_before(scores, g) == 0 for g in range(N_GROUPS)]

    def pick(rows, k):
        out = rows[4 * (N_GROUPS - 1) + k]
        for g in range(N_GROUPS - 2, -1, -1):
            out = jnp.where(in_group[g], rows[4 * g + k], out)
        return out

    bv = [pick(b_rows, k) for k in range(EXPERTS_PER_GROUP)]
    sv = [pick(s_rows, k) for k in range(EXPERTS_PER_GROUP)]
    w = [jnp.where(_rank_before(bv, k) < 2, sv[k], 0.0) for k in range(EXPERTS_PER_GROUP)]
    den = (w[0] + w[1]) + (w[2] + w[3])
    return [jnp.where(in_group[e // 4], w[e % 4] / den, 0.0) for e in range(N_EXPERTS)]


def _outproj_kernel(ps_ref, att_ref, x_ref, mod_ref, wout_ref, lng_ref, lnb_ref, rwt_ref, rb_ref,
                    x1_ref, h2_ref, g_ref):
    half = ps_ref.shape[1]
    mix = (jnp.dot(ps_ref[...], wout_ref[:half, :], preferred_element_type=F32)
           + jnp.dot(att_ref[...], wout_ref[half:, :], preferred_element_type=F32))
    x1 = _layer_norm(ALPHA * x_ref[...] + (1.0 + mod_ref[2]) * mix, lng_ref[...], lnb_ref[...])
    x1_ref[...] = x1
    h2 = x1 * (1.0 + mod_ref[4]) + mod_ref[3]
    h2b = h2.astype(BF16)
    h2_ref[...] = h2b
    st = jax.nn.sigmoid(lax.dot_general(rwt_ref[...].astype(BF16), h2b, _NT, preferred_element_type=F32))
    s_rows = [st[e:e + 1, :] for e in range(N_EXPERTS)]
    b_rows = [s_rows[e] + rb_ref[e:e + 1, :] for e in range(N_EXPERTS)]
    gt = jnp.concatenate(_route(s_rows, b_rows) + [jnp.zeros((LANES - N_EXPERTS, st.shape[1]), F32)], axis=0)
    g_ref[...] = gt.T


def _out_proj(ps, att, x, mod, wout, ln_g, ln_b, rwt, rb, tm, tiles_per_batch):
    n = x.shape[0]
    row = lambda wd: pl.BlockSpec((tm, wd), lambda i: (i, 0))
    full = lambda a: pl.BlockSpec(a.shape, lambda i: (0,) * a.ndim)
    return pl.pallas_call(
        _outproj_kernel,
        grid=(n // tm,),
        in_specs=[row(512), row(512), row(D_MODEL), _mod_spec(mod, tm, tiles_per_batch),
                  full(wout), full(ln_g), full(ln_b), full(rwt), full(rb)],
        out_specs=[row(D_MODEL), row(D_MODEL), row(LANES)],
        out_shape=[jax.ShapeDtypeStruct((n, D_MODEL), F32), jax.ShapeDtypeStruct((n, D_MODEL), BF16),
                   jax.ShapeDtypeStruct((n, LANES), F32)],
        compiler_params=_cparams(("arbitrary",)),
        name="out_proj",
    )(ps, att, x, mod, wout, ln_g, ln_b, rwt, rb)


def _moe_kernel(h2_ref, g_ref, wg_ref, wu_ref, wd_ref, x1_ref, mod_ref, lng_ref, lnb_ref, o_ref, acc_ref):
    e = pl.program_id(1)

    @pl.when(e == 0)
    def _():
        acc_ref[...] = jnp.zeros_like(acc_ref)

    x = h2_ref[...]
    gate = jnp.dot(x, wg_ref[0, 0], preferred_element_type=F32)
    up = jnp.dot(x, wu_ref[0, 0], preferred_element_type=F32)
    act = (gate * jax.nn.sigmoid(gate) * up).astype(BF16)
    y = jnp.dot(act, wd_ref[0, 0], preferred_element_type=F32)
    lane = lax.broadcasted_iota(jnp.int32, g_ref.shape, 1)
    w = jnp.sum(jnp.where(lane == e, g_ref[...], 0.0), axis=1, keepdims=True)
    acc_ref[...] += y * w

    @pl.when(e == pl.num_programs(1) - 1)
    def _():
        o_ref[...] = _layer_norm(ALPHA * x1_ref[...] + (1.0 + mod_ref[5]) * acc_ref[...], lng_ref[...], lnb_ref[...])


def _moe(h2, gates, wg, wu, wd, layer, x1, mod, ln_g, ln_b, tm, tiles_per_batch):
    n = h2.shape[0]
    row = lambda wd_: pl.BlockSpec((tm, wd_), lambda i, e: (i, 0))
    full = lambda a: pl.BlockSpec(a.shape, lambda i, e: (0,) * a.ndim)
    wspec = lambda a: pl.BlockSpec((1, 1) + a.shape[2:], lambda i, e: (layer, e, 0, 0))
    return pl.pallas_call(
        _moe_kernel,
        grid=(n // tm, N_EXPERTS),
        in_specs=[row(D_MODEL), row(LANES), wspec(wg), wspec(wu), wspec(wd), row(D_MODEL),
                  _mod_spec(mod, tm, tiles_per_batch), full(ln_g), full(ln_b)],
        out_specs=row(D_MODEL),
        out_shape=jax.ShapeDtypeStruct((n, D_MODEL), F32),
        scratch_shapes=[pltpu.VMEM((tm, D_MODEL), F32)],
        compiler_params=_cparams(("arbitrary", "arbitrary")),
        name="moe",
    )(h2, gates, wg, wu, wd, x1, mod, ln_g, ln_b)


SHIFT_ROWS = 2048


def _shift_window_kernel(state_ref, new_ref, o_ref, *, t_new):
    x = state_ref[...]
    w = x.shape[1]
    shifted = pltpu.roll(x, w - t_new, 1)
    o_ref[:, :w - LANES] = shifted[:, :w - LANES]
    lane = lax.broadcasted_iota(jnp.int32, (x.shape[0], LANES), 1)
    o_ref[:, w - LANES:] = jnp.where(lane < LANES - t_new, shifted[:, w - LANES:], new_ref[...])


def _shift_window(state, new):
    w, t_new = state.shape[-1], new.shape[-1]
    rows = state.size // w
    new_rows = jnp.pad(new.reshape(rows, t_new), ((0, 0), (LANES - t_new, 0)))
    out = pl.pallas_call(
        functools.partial(_shift_window_kernel, t_new=t_new),
        grid=(rows // SHIFT_ROWS,),
        in_specs=[pl.BlockSpec((SHIFT_ROWS, w), lambda i: (i, 0)), pl.BlockSpec((SHIFT_ROWS, LANES), lambda i: (i, 0))],
        out_specs=pl.BlockSpec((SHIFT_ROWS, w), lambda i: (i, 0)),
        out_shape=jax.ShapeDtypeStruct((rows, w), F32),
        compiler_params=_cparams(("arbitrary",)),
        name="shift_window",
    )(state.reshape(rows, w), new_rows)
    return out.reshape(state.shape)


TM_PROJ = 256
TM_MOE = 512


def _channel_mix(ps, att, x, mod, lw, sw, layer, tiles_per_seq):
    x1, h2, gts = _out_proj(ps, att, x, mod, lw["wout"], lw["ln_g0"], lw["ln_b0"], sw["rwt"], sw["rb"],
                            TM_PROJ, tiles_per_seq(TM_PROJ))
    return _moe(h2, gts, sw["wg"], sw["wu"], sw["wd"], layer, x1, mod, lw["ln_g1"], lw["ln_b1"],
                TM_MOE, tiles_per_seq(TM_MOE))


def _prompt_layer(x, mod, lw, sw, layer, depth, kv_state, batch, seq):
    tiles = lambda tm: seq // tm
    (p, u, vn, q_t, cmp_k, cmp_v, cmp_t, sel_t, win_t, gate_t, ksel, vsel_t, kwin, vwin_t) = _in_proj(
        x, mod, lw["w_proj"], lw["sgu_ln_g"], lw["sgu_ln_b"], TM_PROJ, tiles(TM_PROJ), batch=batch, wq_t=lw["wq_t"],
        layer=layer, depth=depth, state=kv_state)
    ps = _mixers(p, u, vn, *lw["mix_prompt"], tiles_per_batch=tiles(SGU_CHUNK))
    nc = seq // CMP_BLOCK
    kvc = _compress(cmp_k, cmp_v, sw["pe_rows"], sw["w1bd"], sw["w2bd"], nc, layer)
    kc = _even_odd(kvc[:, :LANES].astype(BF16), nc)
    vc_t = jnp.transpose(_even_odd(kvc[:, LANES:].astype(BF16), nc).reshape(batch, nc, LANES), (0, 2, 1))
    att = _attention_prompt(q_t, kc, vc_t, ksel, vsel_t, kwin, vwin_t, gate_t, sw["prompt_tables"], batch, seq)
    x2 = _channel_mix(ps, att, x, mod, lw, sw, layer, tiles)
    return x2, (cmp_t, sel_t, win_t), p


def _sample_layer(x, mod, lw, sw, layer, page_table, kvc_pages, sel_cache, win_state, pool_state, n_seq, t_new):
    tiles = lambda tm: 1
    (p, u, vn, qq, cmp_raw, sel_raw, win_raw, gates) = _in_proj(
        x, mod, lw["w_proj"], lw["sgu_ln_g"], lw["sgu_ln_b"], TM_PROJ, 1)
    p_ext = jnp.concatenate([pool_state[layer], p.reshape(n_seq, t_new, POOL_WIDTH)], axis=1)
    hist = jnp.pad(p_ext, ((0, 0), (3 * t_new - p_ext.shape[1], 0), (0, 0)))
    chunks = [hist[:, k * t_new:(k + 1) * t_new].reshape(n_seq * t_new, POOL_WIDTH) for k in range(3)]
    ps = _mixers(chunks, u, vn, *lw["mix_sample"], t_new=t_new)
    q_rows = jnp.transpose(qq.reshape(N_HEADS, n_seq, t_new, LANES), (1, 0, 2, 3)).reshape(n_seq, N_HEADS * t_new, LANES)
    new_page = lambda raw: jnp.pad(raw.reshape(n_seq, t_new, KV_WIDTH), ((0, 0), (0, PAGE_SIZE - t_new), (0, 0)))
    att = _attention_sample(page_table, kvc_pages, sel_cache, layer, q_rows, new_page(sel_raw), new_page(win_raw),
                            win_state, gates, sw["sample_tables"], t_new)
    x2 = _channel_mix(ps, att.astype(BF16), x, mod, lw, sw, layer, tiles)
    win_new_t = jnp.transpose(win_raw.reshape(n_seq, t_new, KV_WIDTH), (0, 2, 1))
    return x2, (cmp_raw, sel_raw, win_new_t, p_ext[:, p_ext.shape[1] - POOL_BUF:], vn)


def kernel(x_prompt, x_sample, cache_cmp_kv, cache_sel_kv, state_win_kv, state_pool, page_table, c_prompt, c_sample,
           w_in, w_out, pool_w, pool_scale, sgu_ln_g, sgu_ln_b, sgu_w, sgu_b, cmp_pe, cmp_w1, cmp_w2, rel_bias,
           w_mod, b_mod, ln_g, ln_b, router_w, router_b, moe_w_gate, moe_w_up, moe_w_down):
    batch, seq, d = x_prompt.shape
    n_seq, t_new, _ = x_sample.shape
    depth = w_in.shape[0]
    n_pages = page_table.shape[1]
    past_len = n_pages * PAGE_SIZE
    n_phys = cache_cmp_kv.shape[1]
    n_win = state_win_kv.shape[2]
    assert seq // SEL_BLOCK == LANES and seq % TM_MOE == 0 and (n_seq * t_new) % TM_MOE == 0
    assert POOL_BUF + 1 == 2 * t_new and past_len % TK == 0

    n_c = batch + n_seq
    c_all = jnp.pad(jnp.concatenate([c_prompt, c_sample], axis=0), ((0, -n_c % 8), (0, 0)))
    m_all = _modulation(c_all, w_mod, b_mod)
    mod_p = m_all[:, :batch].reshape(depth, batch, 6, 1, d)
    mod_s = jnp.transpose(jnp.repeat(m_all[:, batch:n_c].reshape(depth, n_seq, 6, d), t_new, axis=1), (0, 2, 1, 3))

    pe_rows, w1bd, w2bd = _prep_compress_weights(cmp_pe, cmp_w1, cmp_w2)
    near, dtab, lane_m = _prompt_bias_tables(rel_bias)
    near = near * LOG2E
    nc_past = past_len // CMP_BLOCK
    shared = {
        "pe_rows": pe_rows, "w1bd": w1bd, "w2bd": w2bd,
        "prompt_tables": (near, dtab, lane_m, _block_onehot_rows(seq)),
        "sample_tables": (_block_onehot_rows(past_len + PAGE_SIZE).T, _block_onehot_rows(past_len + PAGE_SIZE)[past_len:],
                          *_sample_bias_tables(rel_bias, past_len, t_new, n_win),
                          *_sample_sum_matrices(t_new, nc_past)),
        "rwt": router_w.T, "rb": router_b.reshape(N_EXPERTS, 1),
        "wg": moe_w_gate.astype(BF16), "wu": moe_w_up.astype(BF16), "wd": moe_w_down.astype(BF16),
    }
    layers = []
    for l in range(depth):
        layers.append({
            "w_proj": _prep_w_proj(w_in[l]), "wq_t": _prep_w_proj(w_in[l])[:, C_Q:].T, "wout": w_out[l].astype(BF16),
            "sgu_ln_g": sgu_ln_g[l].reshape(1, -1), "sgu_ln_b": sgu_ln_b[l].reshape(1, -1),
            "mix_prompt": _prep_mixer_weights(pool_w[l], pool_scale[l], sgu_w[l], sgu_b[l], SGU_CHUNK),
            "mix_sample": _prep_mixer_weights(pool_w[l], pool_scale[l], sgu_w[l], sgu_b[l], t_new),
            "ln_g0": ln_g[l, 0].reshape(1, d), "ln_b0": ln_b[l, 0].reshape(1, d),
            "ln_g1": ln_g[l, 1].reshape(1, d), "ln_b1": ln_b[l, 1].reshape(1, d),
        })

    chan_major = lambda x: jnp.transpose(x, (0, 1, 3, 4, 5, 2)).reshape(x.shape[0], x.shape[1], KV_WIDTH, x.shape[2])
    pe_pages = jnp.broadcast_to(jnp.transpose(cmp_pe, (0, 1, 3, 2))[:, :, None, :, None, :],
                                (depth, 2, N_KV, HEAD_DIM, PAGE_SIZE // CMP_BLOCK, CMP_BLOCK))
    kvc_pages = _compress_pages(chan_major(cache_cmp_kv), pe_pages.reshape(depth, KV_WIDTH, PAGE_SIZE), w1bd, w2bd, 64)
    kvc_pages = kvc_pages.reshape(depth, n_phys, PAGE_SIZE // CMP_BLOCK, KV_WIDTH)
    sel_cache = chan_major(cache_sel_kv)
    win_state = chan_major(state_win_kv)

    xp = x_prompt.reshape(batch * seq, d)
    xs = x_sample.reshape(n_seq * t_new, d)
    kv_state = tuple(jnp.zeros((depth, batch, KV_WIDTH, seq), F32) for _ in range(3))
    pool_p, outs_s = [], []
    for l in range(depth):
        xp, kv_state, p_l = _prompt_layer(xp, mod_p[l], layers[l], shared, l, depth, kv_state, batch, seq)
        pool_p.append(p_l)
        xs, st = _sample_layer(xs, mod_s[l], layers[l], shared, l, page_table, kvc_pages, sel_cache, win_state,
                               state_pool, n_seq, t_new)
        outs_s.append(st)

    kv5 = lambda x, b: x.reshape(b, -1, 2, N_KV, HEAD_DIM)
    kv6_t = lambda x: jnp.transpose(x.reshape(x.shape[:2] + (2, N_KV, HEAD_DIM, x.shape[3])), (0, 1, 5, 2, 3, 4))
    stack = lambda xs_: jnp.stack(xs_)
    w_keep = min(WINDOW, seq)
    cmp_t, sel_t, win_t = kv_state
    new_win_t = _shift_window(win_state, stack([o[2] for o in outs_s]))
    return (xp.reshape(batch, seq, d), xs.reshape(n_seq, t_new, d),
            kv6_t(cmp_t), stack([kv5(o[0], n_seq) for o in outs_s]),
            kv6_t(sel_t), stack([kv5(o[1], n_seq) for o in outs_s]),
            kv6_t(win_t[:, :, :, seq - w_keep:]), kv6_t(new_win_t),
            stack([p_l.reshape(batch, seq, POOL_WIDTH)[:, seq - POOL_BUF:] for p_l in pool_p]),
            stack([o[3] for o in outs_s]),
            stack([o[4].reshape(n_seq, t_new, SGU_WIDTH) for o in outs_s]))
```

```python
import functools
import math

import numpy as np
import jax
import jax.numpy as jnp
from jax import lax
from jax.experimental import pallas as pl
from jax.experimental.pallas import tpu as pltpu

F32 = jnp.float32
BF16 = jnp.bfloat16

D_MODEL = 1024
POOL_WIDTH = 256
SGU_WIDTH = 256
ATT_WIDTH = 512
POOL_WINDOWS = (2, 4, 8, 16)
POOL_GROUP_DIM = 64
POOL_BUF = 15
SGU_GROUPS = 4
SGU_CHUNK = 128
N_HEADS = 8
HEAD_DIM = 64
N_KV = 2
REP = 4
CMP_BLOCK = 32
CMP_HIDDEN = 128
SEL_BLOCK = 64
N_SEL = 16
WINDOW = 512
N_BUCKETS = 32
REL_MAX_DIST = 128
N_EXPERTS = 16
N_GROUPS = 4
EXPERTS_PER_GROUP = 4
D_EXPERT = 512
DEPTH = 2
ALPHA = (2 * DEPTH) ** 0.25
LN_EPS = 1e-5
FORCED_SCORE = 1e4
NEG = -1e30
LOG2E = math.log2(math.e)
PAGE_SIZE = 128

KV_WIDTH = 2 * N_KV * HEAD_DIM
LANES = 128
VMEM_LIMIT = 56 * 1024 * 1024

C_P, C_U, C_V, C_CMP, C_SEL, C_WIN, C_GATE, C_Q = 0, 256, 512, 768, 1024, 1280, 1536, 1664
W_PROJ = C_Q + N_HEADS * LANES

TQ = 256
TK = 256
NEAR_BLOCKS = 12
FAR_BUCKET_DIST = 113


def _cparams(sem):
    return pltpu.CompilerParams(dimension_semantics=sem, vmem_limit_bytes=VMEM_LIMIT)


def _layer_norm(x, g, b):
    mu = jnp.mean(x, axis=-1, keepdims=True)
    xc = x - mu
    var = jnp.mean(xc * xc, axis=-1, keepdims=True)
    return xc * lax.rsqrt(var + LN_EPS) * g + b


def _mod_kernel(c_ref, w_ref, b_ref, o_ref):
    c = c_ref[...]
    a = (c * jax.nn.sigmoid(c)).astype(BF16)
    o_ref[0] = jnp.dot(a, w_ref[0].astype(BF16), preferred_element_type=F32) + b_ref[0]


def _modulation(c_all, w_mod, b_mod):
    n, d = c_all.shape
    depth, _, w = w_mod.shape
    tn = 1536
    return pl.pallas_call(
        _mod_kernel,
        grid=(depth, w // tn),
        in_specs=[pl.BlockSpec((n, d), lambda l, j: (0, 0)),
                  pl.BlockSpec((1, d, tn), lambda l, j: (l, 0, j)),
                  pl.BlockSpec((1, 1, tn), lambda l, j: (l, 0, j))],
        out_specs=pl.BlockSpec((1, n, tn), lambda l, j: (l, 0, j)),
        out_shape=jax.ShapeDtypeStruct((depth, n, w), F32),
        compiler_params=_cparams(("arbitrary", "arbitrary")),
        name="adaln_mod",
    )(c_all, w_mod, b_mod.reshape(depth, 1, w))


def _mod_spec(mod, tm, tiles_per_batch):
    if mod.ndim == 4:
        return pl.BlockSpec((None, 6, 1, D_MODEL), lambda i, *_: (i // tiles_per_batch, 0, 0, 0))
    return pl.BlockSpec((6, tm, D_MODEL), lambda i, *_: (0, i, 0))


def _inproj_kernel(x_ref, mod_ref, w_ref, lng_ref, lnb_ref, *rest, channel_major, n_state):
    h = (x_ref[...] * (1.0 + mod_ref[1]) + mod_ref[0]).astype(BF16)

    def seg(a, b):
        return jnp.dot(h, w_ref[:, a:b], preferred_element_type=F32)

    if channel_major:
        wqt_ref = rest[0]
        p_ref, u_ref, vn_ref, qt_ref = rest[1 + n_state:5 + n_state]
        rest = rest[5 + n_state:]
        qt_ref[...] = lax.dot_general(wqt_ref[...], h, _NT, preferred_element_type=F32).astype(BF16)
    else:
        p_ref, u_ref, vn_ref, qq_ref = rest[:4]
        rest = rest[4:]
        for hd in range(N_HEADS):
            qq_ref[hd] = seg(C_Q + hd * LANES, C_Q + (hd + 1) * LANES).astype(BF16)
    p_ref[...] = seg(C_P, C_U)
    u_ref[...] = seg(C_U, C_V)
    vn_ref[...] = _layer_norm(seg(C_V, C_CMP), lng_ref[...], lnb_ref[...])
    cmp = seg(C_CMP, C_SEL)
    sel = seg(C_SEL, C_WIN)
    win = seg(C_WIN, C_GATE)
    gates = jax.nn.sigmoid(seg(C_GATE, C_Q))
    if not channel_major:
        cmp_ref, sel_ref, win_ref, gate_ref = rest
        cmp_ref[...] = cmp
        sel_ref[...] = sel
        win_ref[...] = win
        gate_ref[...] = gates
        return
    cmpk_ref, cmpv_ref, cmpt_ref, selt_ref, wint_ref, gatet_ref, ksel_ref, vselt_ref, kwin_ref, vwint_ref = rest
    cmpk_ref[...] = cmp[:, :LANES]
    cmpv_ref[...] = cmp[:, LANES:]
    cmpt_ref[...] = cmp.T
    sel_t = sel.T
    selt_ref[...] = sel_t
    ksel_ref[...] = sel[:, :LANES].astype(BF16)
    vselt_ref[...] = sel_t[LANES:, :].astype(BF16)
    win_t = win.T
    wint_ref[...] = win_t
    kwin_ref[...] = win[:, :LANES].astype(BF16)
    vwint_ref[...] = win_t[LANES:, :].astype(BF16)
    gatet_ref[...] = gates.T


def _in_proj(x, mod, w_proj, ln_g, ln_b, tm, tiles_per_batch, batch=None, wq_t=None, layer=0, depth=1, state=None):
    n = x.shape[0]
    row = lambda wd: pl.BlockSpec((tm, wd), lambda i: (i, 0))
    full = lambda a: pl.BlockSpec(a.shape, lambda i: (0,) * a.ndim)
    f32o = lambda wd: jax.ShapeDtypeStruct((n, wd), F32)
    out_specs = [row(256), row(256), row(256)]
    out_shape = [f32o(256), f32o(256), f32o(256)]
    ins, in_specs = [x, mod, w_proj, ln_g, ln_b], [row(D_MODEL), _mod_spec(mod, tm, tiles_per_batch), full(w_proj),
                                                    full(ln_g), full(ln_b)]
    aliases = {}
    if batch is None:
        out_specs += [pl.BlockSpec((N_HEADS, tm, LANES), lambda i: (0, i, 0)), row(256), row(256), row(256), row(LANES)]
        out_shape += [jax.ShapeDtypeStruct((N_HEADS, n, LANES), BF16), f32o(256), f32o(256), f32o(256), f32o(LANES)]
    else:
        ins.append(wq_t)
        in_specs.append(full(wq_t))
        out_specs.append(pl.BlockSpec((N_HEADS * LANES, tm), lambda i: (0, i)))
        out_shape.append(jax.ShapeDtypeStruct((N_HEADS * LANES, n), BF16))
        seq = n // batch
        chan = lambda c: pl.BlockSpec((None, c, tm), lambda i: (i // tiles_per_batch, 0, i % tiles_per_batch))
        chan_o = lambda c, dt: jax.ShapeDtypeStruct((batch, c, seq), dt)
        kv_state = pl.BlockSpec((None, None, KV_WIDTH, tm),
                                lambda i: (layer, i // tiles_per_batch, 0, i % tiles_per_batch))
        kv_state_o = jax.ShapeDtypeStruct((depth, batch, KV_WIDTH, seq), F32)
        first_state_out = len(out_specs) + 2
        out_specs += [row(LANES), row(LANES), kv_state, kv_state, kv_state, pl.BlockSpec((LANES, tm), lambda i: (0, i)),
                      row(LANES), chan(LANES), row(LANES), chan(LANES)]
        out_shape += [f32o(LANES), f32o(LANES),
                      kv_state_o, kv_state_o, kv_state_o, jax.ShapeDtypeStruct((LANES, n), F32),
                      jax.ShapeDtypeStruct((n, LANES), BF16), chan_o(LANES, BF16),
                      jax.ShapeDtypeStruct((n, LANES), BF16), chan_o(LANES, BF16)]
        if state is not None:
            aliases = {len(ins) + k: first_state_out + k for k in range(len(state))}
            ins += list(state)
            in_specs += [pl.BlockSpec(memory_space=pl.ANY)] * len(state)
    return pl.pallas_call(
        functools.partial(_inproj_kernel, channel_major=batch is not None, n_state=len(state or ())),
        grid=(n // tm,),
        in_specs=in_specs,
        out_specs=out_specs,
        out_shape=out_shape,
        input_output_aliases=aliases,
        compiler_params=_cparams(("arbitrary",)),
        name="in_proj",
    )(*ins)


def _prep_w_proj(w_in):
    d = w_in.shape[0]
    q = w_in[:, 768:1280].reshape(d, N_KV, REP, HEAD_DIM) * (HEAD_DIM ** -0.5)
    qq = jnp.zeros((d, N_KV, REP, N_KV, HEAD_DIM), w_in.dtype)
    for g in range(N_KV):
        qq = qq.at[:, g, :, g, :].set(q[:, g])
    gate = jnp.pad(w_in[:, 2048:2072], ((0, 0), (0, LANES - 24)))
    cols = [w_in[:, 0:768], w_in[:, 1280:2048], gate, qq.reshape(d, N_HEADS * LANES)]
    return jnp.concatenate(cols, axis=1).astype(BF16)


def _window_sums(shifted):
    acc = shifted(0)
    sums = {}
    for k in range(1, 16):
        acc = acc + shifted(k)
        if k + 1 in POOL_WINDOWS:
            sums[k + 1] = acc
    lane = lax.broadcasted_iota(jnp.int32, acc.shape, 1)
    return jnp.where(lane < 64, sums[2], jnp.where(lane < 128, sums[4], jnp.where(lane < 192, sums[8], sums[16])))


def _mixers_tail(sums, cnt, cur, u, vn, wpool_ref, pscale_ref, wcat_ref, sbias_ref, o_ref):
    diff = (sums / cnt - cur).astype(BF16)
    pool = jnp.dot(diff, wpool_ref[...], preferred_element_type=F32) * pscale_ref[...]
    lane = lax.broadcasted_iota(jnp.int32, vn.shape, 1)
    vb = vn.astype(BF16)
    zero = jnp.zeros_like(vb)
    stacked = jnp.concatenate([jnp.where((lane // 64) == g, vb, zero) for g in range(SGU_GROUPS)], axis=0)
    mixed = jnp.dot(wcat_ref[...], stacked, preferred_element_type=F32) + sbias_ref[...]
    o_ref[:, :POOL_WIDTH] = pool.astype(BF16)
    o_ref[:, POOL_WIDTH:] = (u * mixed).astype(BF16)


def _lane_window():
    lane = lax.broadcasted_iota(jnp.int32, (SGU_CHUNK, POOL_WIDTH), 1)
    return jnp.where(lane < 64, 2, jnp.where(lane < 128, 4, jnp.where(lane < 192, 8, 16)))


def _mix_prompt_kernel(p_ref, u_ref, vn_ref, wpool_ref, pscale_ref, wcat_ref, sbias_ref, o_ref, prev_ref,
                       *, tiles_per_batch):
    t = pl.program_id(0) % tiles_per_batch

    @pl.when(t == 0)
    def _():
        prev_ref[...] = jnp.zeros_like(prev_ref)

    cur = p_ref[...]
    prev = prev_ref[...]
    row = lax.broadcasted_iota(jnp.int32, cur.shape, 0)

    def shifted(k):
        if k == 0:
            return cur
        return jnp.where(row >= k, pltpu.roll(cur, k, 0), pltpu.roll(prev, k, 0))

    sums = _window_sums(shifted)
    prev_ref[...] = cur
    cnt = jnp.minimum(_lane_window(), t * SGU_CHUNK + row + 1).astype(F32)
    _mixers_tail(sums, cnt, cur, u_ref[...], vn_ref[...], wpool_ref, pscale_ref, wcat_ref, sbias_ref, o_ref)


def _mix_sample_kernel(pa_ref, pb_ref, pc_ref, u_ref, vn_ref, wpool_ref, pscale_ref, wcat_ref, sbias_ref, o_ref,
                       *, t_new):
    a, b, c = pa_ref[...], pb_ref[...], pc_ref[...]
    rows = c.shape[0]
    t = lax.broadcasted_iota(jnp.int32, c.shape, 0) % t_new

    def shifted(k):
        if k == 0:
            return c
        hi, lo = (c, b) if k < t_new else (b, a)
        kk = k % t_new
        if kk == 0:
            return hi
        return jnp.where(t >= kk, pltpu.roll(hi, kk, 0), pltpu.roll(lo, rows - t_new + kk, 0))

    sums = _window_sums(shifted)
    cnt = _lane_window().astype(F32)
    _mixers_tail(sums, cnt, c, u_ref[...], vn_ref[...], wpool_ref, pscale_ref, wcat_ref, sbias_ref, o_ref)


def _mixers(p_in, u, vn, wpool, pscale, wcat, sbias, tiles_per_batch=None, t_new=None):
    n = u.shape[0]
    tm = SGU_CHUNK
    row = lambda wd: pl.BlockSpec((tm, wd), lambda i: (i, 0))
    full = lambda a: pl.BlockSpec(a.shape, lambda i: (0,) * a.ndim)
    weights = [wpool, pscale, wcat, sbias]
    if t_new is None:
        kern = functools.partial(_mix_prompt_kernel, tiles_per_batch=tiles_per_batch)
        ins, scratch = [p_in], [pltpu.VMEM((tm, POOL_WIDTH), F32)]
    else:
        kern = functools.partial(_mix_sample_kernel, t_new=t_new)
        ins, scratch = list(p_in), []
    return pl.pallas_call(
        kern,
        grid=(n // tm,),
        in_specs=[row(256)] * (len(ins) + 2) + [full(a) for a in weights],
        out_specs=row(512),
        out_shape=jax.ShapeDtypeStruct((n, 512), BF16),
        scratch_shapes=scratch,
        compiler_params=_cparams(("arbitrary",)),
        name="mixers",
    )(*ins, u, vn, *weights)


def _prep_mixer_weights(pool_w, pool_scale, sgu_w, sgu_b, chunk):
    wpool = jax.scipy.linalg.block_diag(*[pool_w[g] for g in range(len(POOL_WINDOWS))]).astype(BF16)
    tri = jnp.tril(jnp.ones((chunk, chunk), bool))
    w = jnp.where(tri, sgu_w[:, :chunk, :chunk], 0.0)
    reps = SGU_CHUNK // chunk
    eye = jnp.eye(reps, dtype=w.dtype)
    wt = jnp.concatenate([jnp.kron(eye, w[g]) for g in range(SGU_GROUPS)], axis=1).astype(BF16)
    b = jnp.tile(sgu_b[:, :chunk], (1, reps))
    sbias = jnp.repeat(b.T, SGU_WIDTH // SGU_GROUPS, axis=1)
    return wpool, pool_scale.reshape(1, POOL_WIDTH), wt, sbias


def _compress_mlp(token_rows, w1_ref, w2_ref, o_ref, nblk):
    for s in range(2):
        acc = jnp.zeros((nblk, N_KV * CMP_HIDDEN), F32)
        for j in range(0, CMP_BLOCK, 2):
            pair = jnp.concatenate([token_rows(s, j), token_rows(s, j + 1)], axis=1)
            acc = acc + jnp.dot(pair, w1_ref[0, s, j // 2], preferred_element_type=F32)
        hdn = jax.nn.gelu(acc).astype(BF16)
        o_ref[0, :, s * LANES:(s + 1) * LANES] = jnp.dot(hdn, w2_ref[0, s], preferred_element_type=F32)


def _compress_kernel(k_ref, v_ref, pe_ref, w1_ref, w2_ref, o_ref, *, nblk):
    bufs = (k_ref, v_ref)
    rows = lambda s, j: (bufs[s][pl.ds(j, nblk, stride=CMP_BLOCK), :] + pe_ref[0, s, j:j + 1, :]).astype(BF16)
    _compress_mlp(rows, w1_ref, w2_ref, o_ref, nblk)


PAIR_TOKENS = 2 * PAGE_SIZE
PAIR_BLOCKS = PAIR_TOKENS // CMP_BLOCK


def _compress_pages_kernel(raw_ref, pe_ref, perm_ref, w1_ref, w2_ref, o_ref, k_ref, v_ref, *, nblk):
    bufs = (k_ref, v_ref)

    def body(q, carry):
        for s in range(2):
            ch = slice(s * LANES, (s + 1) * LANES)
            x = jnp.concatenate([raw_ref[0, 2 * q, ch, :] + pe_ref[0, ch, :],
                                 raw_ref[0, 2 * q + 1, ch, :] + pe_ref[0, ch, :]], axis=1).astype(BF16)
            t = lax.dot_general(perm_ref[...], x, _NT, preferred_element_type=F32)
            rows = pl.ds(pl.multiple_of(q * PAIR_BLOCKS, PAIR_BLOCKS), PAIR_BLOCKS)
            bufs[s][:, rows, :] = t.reshape(CMP_BLOCK, PAIR_BLOCKS, LANES)
        return carry

    lax.fori_loop(0, raw_ref.shape[1] // 2, body, 0, unroll=4)
    _compress_mlp(lambda s, j: bufs[s][j].astype(BF16), w1_ref, w2_ref, o_ref, nblk)


def _compress_pages(pages, pe_pages, w1bd, w2bd, pages_per_step):
    nl, n_pages, _, _ = pages.shape
    assert n_pages % pages_per_step == 0 and pages_per_step % 2 == 0
    nblk = pages_per_step * (PAGE_SIZE // CMP_BLOCK)
    tok = np.arange(PAIR_TOKENS)
    dest = (tok % CMP_BLOCK) * PAIR_BLOCKS + tok // CMP_BLOCK
    perm = jnp.asarray((np.arange(PAIR_TOKENS)[:, None] == dest[None, :]).astype(np.float32)).astype(BF16)
    wspec = lambda a: pl.BlockSpec((1,) + a.shape[1:], lambda l, i: (l,) + (0,) * (a.ndim - 1))
    return pl.pallas_call(
        functools.partial(_compress_pages_kernel, nblk=nblk),
        grid=(nl, n_pages // pages_per_step),
        in_specs=[pl.BlockSpec((1, pages_per_step, KV_WIDTH, PAGE_SIZE), lambda l, i: (l, i, 0, 0)),
                  wspec(pe_pages), pl.BlockSpec(perm.shape, lambda l, i: (0, 0)), wspec(w1bd), wspec(w2bd)],
        out_specs=pl.BlockSpec((1, nblk, KV_WIDTH), lambda l, i: (l, i, 0)),
        out_shape=jax.ShapeDtypeStruct((nl, n_pages * (PAGE_SIZE // CMP_BLOCK), KV_WIDTH), F32),
        scratch_shapes=[pltpu.VMEM((CMP_BLOCK, nblk, LANES), F32)] * 2,
        compiler_params=_cparams(("arbitrary", "arbitrary")),
        name="compress_pages",
    )(pages, pe_pages, perm, w1bd, w2bd)


def _compress(k_raw, v_raw, pe_rows, w1bd, w2bd, nblk, layer):
    r = k_raw.shape[0]
    assert r % (nblk * CMP_BLOCK) == 0
    wspec = lambda a: pl.BlockSpec((1,) + a.shape[1:], lambda i: (layer,) + (0,) * (a.ndim - 1))
    rows = pl.BlockSpec((nblk * CMP_BLOCK, LANES), lambda i: (i, 0))
    return pl.pallas_call(
        functools.partial(_compress_kernel, nblk=nblk),
        grid=(r // (nblk * CMP_BLOCK),),
        in_specs=[rows, rows, wspec(pe_rows), wspec(w1bd), wspec(w2bd)],
        out_specs=pl.BlockSpec((1, nblk, KV_WIDTH), lambda i: (0, i, 0)),
        out_shape=jax.ShapeDtypeStruct((1, r // CMP_BLOCK, KV_WIDTH), F32),
        compiler_params=_cparams(("arbitrary",)),
        name="compress",
    )(k_raw, v_raw, pe_rows, w1bd, w2bd)[0]


def _prep_compress_weights(cmp_pe, cmp_w1, cmp_w2):
    nl = cmp_pe.shape[0]
    pe_rows = jnp.concatenate([cmp_pe] * N_KV, axis=-1)
    w1 = cmp_w1.reshape(nl, 2, CMP_BLOCK, HEAD_DIM, CMP_HIDDEN)
    eye = jnp.eye(N_KV, dtype=F32)
    w1bd = w1[:, :, :, None, :, None, :] * eye[None, None, None, :, None, :, None]
    w2bd = cmp_w2[:, :, None, :, None, :] * eye[None, None, :, None, :, None]
    return (pe_rows, w1bd.reshape(nl, 2, CMP_BLOCK // 2, 2 * LANES, N_KV * CMP_HIDDEN).astype(BF16),
            w2bd.reshape(nl, 2, N_KV * CMP_HIDDEN, LANES).astype(BF16))


def _np_bucket(dist):
    n = np.maximum(dist, 0)
    nf = np.maximum(n, 1).astype(np.float32)
    large = 16 + (np.log(nf / np.float32(16)) / np.float32(math.log(REL_MAX_DIST / 16)) * np.float32(16)).astype(np.int32)
    return np.where(n < 16, n, np.minimum(large, N_BUCKETS - 1)).astype(np.int32)


def _bucket_values(tbt, dist):
    hit = jnp.asarray(_np_bucket(dist))[None, ..., None] == jnp.arange(N_BUCKETS, dtype=jnp.int32)
    return jnp.sum(jnp.where(hit, tbt.reshape((tbt.shape[0],) + (1,) * dist.ndim + (N_BUCKETS,)), 0.0), axis=-1)


def _bias_minus_far(rel_bias, dist):
    tbt = rel_bias.astype(F32).T
    val = _bucket_values(tbt, dist) - tbt[:, N_BUCKETS - 1].reshape((-1,) + (1,) * dist.ndim)
    return jnp.where(jnp.asarray(dist >= 0), val, NEG)


def _split3(x):
    hi = x.astype(BF16)
    r1 = x - hi.astype(F32)
    mid = r1.astype(BF16)
    lo = (r1 - mid.astype(F32)).astype(BF16)
    return hi, mid, lo


def _prompt_bias_tables(rel_bias):
    a = np.arange(TQ)[:, None]
    span = 2 * TK + TQ
    per_dist = _bias_minus_far(rel_bias, np.arange(span) - (TQ - 1))
    skew = jnp.broadcast_to(per_dist[:, None, :], (N_HEADS, 2 * TK, span)).reshape(N_HEADS, 2 * TK * span)
    skew = skew[:, :2 * TK * (span - 1)].reshape(N_HEADS, 2 * TK, span - 1)
    near = skew[:, :, 2 * TK - 1:2 * TK - 1 + TQ]
    dist_c = a + (4 * CMP_BLOCK - CMP_BLOCK + 1) - CMP_BLOCK * np.arange(NEAR_BLOCKS)[None, :]
    dc = _bias_minus_far(rel_bias, dist_c)
    hi, mid, lo = _split3(dc)
    cols = jnp.stack([hi, mid, lo], axis=-1).reshape(N_HEADS, TQ, 3 * NEAR_BLOCKS)
    future = jnp.full((N_HEADS, TQ, 1), NEG, F32).astype(BF16)
    pad = jnp.zeros((N_HEADS, TQ, LANES - 3 * NEAR_BLOCKS - 1), BF16)
    dtab = jnp.concatenate([cols, future, pad], axis=-1).reshape(N_HEADS * TQ, LANES).T
    lane = np.arange(LANES)
    lane_m = np.where(lane < 3 * NEAR_BLOCKS, lane // 3, -1000).astype(np.int32)
    return near, dtab, jnp.asarray(np.tile(lane_m[None, :], (8, 1)))


_NT = (((1,), (1,)), ((), ()))


PICKED = -3e38


def _select_start(imp_t, cur):
    blk = lax.broadcasted_iota(jnp.int32, imp_t.shape, 0)
    forced = (blk == 0) | (blk == cur) | (blk == cur - 1)
    return jnp.where(forced, FORCED_SCORE, jnp.where(blk <= cur, imp_t, -1.0))


def _select_round(vals):
    blk = lax.broadcasted_iota(jnp.int32, vals.shape, 0)
    mx = jnp.max(vals, axis=0, keepdims=True)
    first = jnp.min(jnp.where(vals == mx, blk, 1 << 20), axis=0, keepdims=True)
    return jnp.where(blk == first, PICKED, vals)


def _select_mask(vals):
    return jnp.where(vals == PICKED, 0.0, NEG)


def _select_blocks(imp_t, cur):
    return _select_mask(lax.fori_loop(0, N_SEL, lambda _, v: _select_round(v), _select_start(imp_t, cur)))


def _attn_kernel(qt_ref, kc_ref, vct_ref, dtabt_ref, lanem_ref, ksel_ref, vselt_ref, kwin_ref, vwint_ref,
                 xt_ref, near_ref, gatet_ref, o_ref, qs_ref, s_ref, s2_ref, p_ref, m_ref, l_ref, alpha_ref, acc_ref, out_ref,
                 *, tiles_per_batch):
    u = pl.program_id(0) % tiles_per_batch
    nc = kc_ref.shape[0]
    ns = nc // 2
    gcols = REP * TQ

    def gate(h, br):
        c = h * 3 + br
        return gatet_ref[c:c + 1, :]

    def gate_row(g, br):
        return jnp.concatenate([gate(g * REP + r, br) for r in range(REP)], axis=1)

    def group_rows(h):
        g = h // REP
        return slice(g * HEAD_DIM, (g + 1) * HEAD_DIM)

    def head_cols(h):
        return slice((h % REP) * TQ, (h % REP + 1) * TQ)

    c = lax.broadcasted_iota(jnp.int32, (nc, LANES), 0)
    lane = lax.broadcasted_iota(jnp.int32, (nc, LANES), 1)
    rel = jnp.where(c < ns, 2 * c, 2 * c - (nc - 1)) - (8 * u - 4)
    near_hit = jnp.where(rel == lanem_ref[0:1, :], 1.0, 0.0)
    future_hit = jnp.where(rel >= NEAR_BLOCKS, 1.0, 0.0)
    onehot = jnp.where(lane < 3 * NEAR_BLOCKS, near_hit, jnp.where(lane == 3 * NEAR_BLOCKS, future_hit, 0.0))
    kk_c = jnp.concatenate([kc_ref[...], onehot.astype(BF16)], axis=1)
    for h in range(N_HEADS):
        qh = jnp.concatenate([qt_ref[h * LANES:(h + 1) * LANES, :], dtabt_ref[:, h * TQ:(h + 1) * TQ]], axis=0)
        s_ref[:, h * TQ:(h + 1) * TQ] = jnp.dot(kk_c, qh, preferred_element_type=F32)
    imp = [[jnp.zeros((ns, LANES), F32) for _ in range(TQ // LANES)] for _ in range(N_KV)]
    for c0 in range(0, N_HEADS * TQ, LANES):
        s = s_ref[:, c0:c0 + LANES]
        mx = jnp.max(s, axis=0, keepdims=True)
        p = jnp.where(s > 0.1 * NEG, jnp.exp(s - mx), 0.0)
        pn = p / jnp.maximum(jnp.sum(p, axis=0, keepdims=True), 1e-30)
        g, part = c0 // gcols, (c0 % TQ) // LANES
        imp[g][part] = imp[g][part] + (pn[:ns, :] + pn[ns:, :])
        p_ref[:, c0:c0 + LANES] = pn.astype(BF16)
    imp = [jnp.concatenate(parts, axis=1) for parts in imp]
    for h in range(N_HEADS):
        out_ref[h // REP, :, head_cols(h)] = gate(h, 0) * jnp.dot(
            vct_ref[group_rows(h), :], p_ref[:, h * TQ:(h + 1) * TQ], preferred_element_type=F32)

    imp_t = jnp.concatenate(imp, axis=1)
    a = lax.broadcasted_iota(jnp.int32, imp_t.shape, 1) & (TQ - 1)
    cur = (TQ // SEL_BLOCK) * u + (a >> 6)
    neg = _select_blocks(imp_t, cur).astype(BF16)
    for h in range(N_HEADS):
        g = h // REP
        qs_ref[:LANES, h * TQ:(h + 1) * TQ] = (qt_ref[h * LANES:(h + 1) * LANES, :].astype(F32) * LOG2E).astype(BF16)
        qs_ref[LANES:, h * TQ:(h + 1) * TQ] = neg[:, g * TQ:(g + 1) * TQ]

    def reset():
        m_ref[...] = jnp.full(m_ref.shape, -1e38, F32)
        l_ref[...] = jnp.zeros(l_ref.shape, F32)
        acc_ref[...] = jnp.zeros(acc_ref.shape, F32)

    def finish(br):
        for g in range(N_KV):
            cols = slice(g * gcols, (g + 1) * gcols)
            out_ref[g] = out_ref[g] + gate_row(g, br) * (acc_ref[g] / l_ref[:, cols])

    def tile_keys(kt):
        return pl.ds(pl.multiple_of(kt * TK, TK), TK)

    def softmax_tile(buf, vt_ref, kt, bias):
        keys = tile_keys(kt)
        for g in range(N_KV):
            for c0 in range(g * gcols, (g + 1) * gcols, LANES):
                cols = slice(c0, c0 + LANES)
                s = bias(c0 // TQ, c0 % TQ, buf[:, cols])
                m_prev = m_ref[:, cols]
                m_new = jnp.maximum(m_prev, jnp.max(s, axis=0, keepdims=True))
                alpha = jnp.exp2(m_prev - m_new)
                p = jnp.exp2(s - m_new)
                l_ref[:, cols] = alpha * l_ref[:, cols] + jnp.sum(p, axis=0, keepdims=True)
                p_ref[:, cols] = p.astype(BF16)
                alpha_ref[:, cols] = alpha
                m_ref[:, cols] = m_new
            cols = slice(g * gcols, (g + 1) * gcols)
            pv = jnp.dot(vt_ref[g * HEAD_DIM:(g + 1) * HEAD_DIM, keys], p_ref[:, cols], preferred_element_type=F32)
            acc_ref[g] = alpha_ref[:, cols] * acc_ref[g] + pv

    def near_bias(row):
        if row is None:
            return lambda h, q0, s: s
        return lambda h, q0, s: s + near_ref[h, row:row + TK, q0:q0 + LANES]

    reset()

    def sel_scores(kt, buf):
        keys = tile_keys(kt)
        kk = jnp.concatenate([ksel_ref[keys, :], xt_ref[keys, :]], axis=1)
        buf[...] = jnp.dot(kk, qs_ref[...], preferred_element_type=F32)

    def stage(pred, cur_buf, cur_tile, bias, next_buf=None, next_tile=None):
        @pl.when(pred)
        def _():
            if next_buf is not None:
                sel_scores(next_tile, next_buf)
            softmax_tile(cur_buf, vselt_ref, cur_tile, bias)

    far, near, diag = near_bias(None), near_bias(0), near_bias(TK)
    sel_scores(0, s_ref)

    def far_pair(i, carry):
        in_range = 2 * i + 1 < u
        stage(in_range, s_ref, 2 * i, far, s2_ref, 2 * i + 1)
        stage(in_range, s2_ref, 2 * i + 1, far, s_ref, 2 * i + 2)
        return carry

    lax.fori_loop(0, (u - 1) // 2, far_pair, 0)

    stage(u == 0, s_ref, 0, diag)
    odd = u % 2 == 1
    stage(odd, s_ref, u - 1, near, s2_ref, u)
    stage(odd, s2_ref, u, diag)
    even = (u >= 2) & (u % 2 == 0)
    stage(even, s_ref, u - 2, far, s2_ref, u - 1)
    stage(even, s2_ref, u - 1, near, s_ref, u)
    stage(even, s_ref, u, diag)
    finish(1)

    reset()

    def later_keys_only(h, q0, s):
        kj = lax.broadcasted_iota(jnp.int32, s.shape, 0)
        qa = lax.broadcasted_iota(jnp.int32, s.shape, 1) + q0
        return jnp.where(kj > qa, s, NEG)

    def win_tile(kt, bias):
        s_ref[...] = jnp.dot(kwin_ref[tile_keys(kt), :], qs_ref[:LANES, :], preferred_element_type=F32)
        softmax_tile(s_ref, vwint_ref, kt, bias)

    @pl.when(u >= 2)
    def _():
        win_tile(u - 2, later_keys_only)

    @pl.when(u >= 1)
    def _():
        win_tile(u - 1, near)

    win_tile(u, diag)
    finish(2)

    for i in range(N_HEADS // 2):
        pair = jnp.concatenate([out_ref[(2 * i) // REP, :, head_cols(2 * i)],
                                out_ref[(2 * i + 1) // REP, :, head_cols(2 * i + 1)]], axis=0)
        o_ref[:, i * LANES:(i + 1) * LANES] = pair.T.astype(BF16)


def _attention_prompt(qt, kc, vct, ksel, vselt, kwin, vwint, gatet, tables, batch, seq):
    near, dtab, lane_m, xt = tables
    n = batch * seq
    tpb = seq // TQ
    nc = seq // CMP_BLOCK
    assert nc == TK
    per_batch = lambda rows: pl.BlockSpec((rows, LANES), lambda i: (i // tpb, 0))
    chan = lambda cols: pl.BlockSpec((None, LANES, cols), lambda i: (i // tpb, 0, 0))
    full = lambda a: pl.BlockSpec(a.shape, lambda i: (0,) * a.ndim)
    return pl.pallas_call(
        functools.partial(_attn_kernel, tiles_per_batch=tpb),
        grid=(n // TQ,),
        in_specs=[pl.BlockSpec((N_HEADS * LANES, TQ), lambda i: (0, i)),
                  per_batch(nc), chan(nc), full(dtab), full(lane_m),
                  per_batch(seq), chan(seq), per_batch(seq), chan(seq),
                  full(xt), full(near), pl.BlockSpec((LANES, TQ), lambda i: (0, i))],
        out_specs=pl.BlockSpec((TQ, ATT_WIDTH), lambda i: (i, 0)),
        out_shape=jax.ShapeDtypeStruct((n, ATT_WIDTH), BF16),
        scratch_shapes=[pltpu.VMEM((2 * LANES, N_HEADS * TQ), BF16),
                        pltpu.VMEM((TK, N_HEADS * TQ), F32), pltpu.VMEM((TK, N_HEADS * TQ), F32),
                        pltpu.VMEM((TK, N_HEADS * TQ), BF16),
                        pltpu.VMEM((1, N_HEADS * TQ), F32), pltpu.VMEM((1, N_HEADS * TQ), F32),
                        pltpu.VMEM((1, N_HEADS * TQ), F32),
                        pltpu.VMEM((N_KV, HEAD_DIM, REP * TQ), F32), pltpu.VMEM((N_KV, HEAD_DIM, REP * TQ), F32)],
        compiler_params=_cparams(("arbitrary",)),
        name="nsa_prompt",
    )(qt, kc, vct, dtab, lane_m, ksel, vselt, kwin, vwint, xt, near, gatet)


def _block_onehot_rows(seq):
    j = np.arange(seq)[:, None] // SEL_BLOCK
    return jnp.asarray((j == np.arange(LANES)[None, :]).astype(np.float32)).astype(BF16)


def _even_odd(x, nc):
    x = x.reshape(-1, nc // 2, 2, x.shape[-1])
    return jnp.concatenate([x[:, :, 0], x[:, :, 1]], axis=1).reshape(-1, x.shape[-1])


def _sample_bias_tables(rel_bias, past_len, t_new, n_win):
    t = np.arange(t_new)[:, None]
    nc = past_len // CMP_BLOCK
    tb_full = lambda dist, ok: jnp.where(jnp.asarray(ok), _bucket_values(rel_bias.astype(F32).T, dist), NEG)
    dist_c = past_len + t - (CMP_BLOCK * np.arange(nc)[None, :] + CMP_BLOCK - 1)
    dist_s = past_len + t - np.arange(past_len)[None, :]
    jn = np.arange(LANES)[None, :]
    dist_n = t - jn
    dist_w = n_win + t - np.arange(n_win)[None, :]
    rows = lambda x: x.reshape(N_HEADS * t_new, x.shape[-1])
    return (rows(tb_full(dist_c, dist_c >= 0)), rows(tb_full(dist_s, dist_s >= 0)),
            rows(tb_full(dist_n, (dist_n >= 0) & (jn < t_new))),
            rows(tb_full(dist_w, (dist_w >= 0) & (dist_w < WINDOW))))


SEQ_PER_STEP = 4


def _attn_sample_kernel(pt_ref, *refs, n_pages, t_new):
    del pt_ref
    n_par = SEQ_PER_STEP
    kvc_refs = refs[:n_par * n_pages]
    sel_refs = refs[n_par * n_pages:2 * n_par * n_pages]
    (qq_ref, seln_ref, winn_ref, winb_ref, gate_ref, xt_ref, xtn_ref, bc_ref, bs_ref, bn_ref, bw_ref,
     rsum_ref, pair_ref, o_ref, qs_ref, out_ref, s_ref, p_ref) = refs[2 * n_par * n_pages:]
    tables = (xt_ref, xtn_ref, bc_ref, bs_ref, bn_ref, bw_ref, rsum_ref, pair_ref)
    chains = []
    for s in range(n_par):
        rows = pl.ds(s * t_new, t_new)
        chains.append(_sample_sequence(
            kvc_refs[s * n_pages:(s + 1) * n_pages], sel_refs[s * n_pages:(s + 1) * n_pages],
            qq_ref.at[s], seln_ref.at[s], winn_ref.at[s], winb_ref.at[s], gate_ref.at[rows], tables,
            o_ref.at[rows], qs_ref.at[s], out_ref.at[s], s_ref.at[s], p_ref.at[s], n_pages, t_new))
    while chains:
        chains = [c for c in chains if next(c, "done") != "done"]


def _sample_sequence(kvc_refs, sel_refs, qq_ref, seln_ref, winn_ref, winb_ref, gate_ref, tables, o_ref,
                     qs_ref, out_ref, s_ref, p_ref, n_pages, t_new):
    xt_ref, xtn_ref, bc_ref, bs_ref, bn_ref, bw_ref, rsum_ref, pair_ref = tables

    def gate_col(br):
        return jnp.concatenate([gate_ref[:, h * 3 + br:h * 3 + br + 1] for h in range(N_HEADS)], axis=0)

    def softmax_tiles(q, tiles):
        off = 0
        for k, _, bias, channel_major, width in tiles:
            if channel_major:
                s = jnp.dot(q, k(), preferred_element_type=F32)
            else:
                s = lax.dot_general(q, k(), _NT, preferred_element_type=F32)
            s_ref[:, off:off + width] = s + bias()
            off += width
            yield
        s = s_ref[:, :off]
        p = jnp.exp(s - jnp.max(s, axis=1, keepdims=True))
        den = jnp.sum(p, axis=1, keepdims=True)
        p_ref[:, :off] = p.astype(BF16)
        acc, off = None, 0
        for _, v, _, channel_major, width in tiles:
            p_t = p_ref[:, off:off + width]
            if channel_major:
                pv = lax.dot_general(p_t, v(), _NT, preferred_element_type=F32)
            else:
                pv = jnp.dot(p_t, v(), preferred_element_type=F32)
            acc = pv if acc is None else acc + pv
            off += width
            yield
        return acc / den

    qs_ref[:, :LANES] = qq_ref[...]

    kvc = jnp.concatenate([r[...] for r in kvc_refs], axis=0)
    s = lax.dot_general(qq_ref[...], kvc[:, :LANES].astype(BF16), _NT, preferred_element_type=F32) + bc_ref[...]
    mx = jnp.max(s, axis=1, keepdims=True)
    p = jnp.where(s > 0.1 * NEG, jnp.exp(s - mx), 0.0)
    pn = p / jnp.maximum(jnp.sum(p, axis=1, keepdims=True), 1e-30)
    out_ref[...] = gate_col(0) * jnp.dot(pn.astype(BF16), kvc[:, LANES:].astype(BF16), preferred_element_type=F32)
    yield

    hp = lax.Precision.HIGHEST
    imp = jnp.dot(jnp.dot(rsum_ref[...], pn, precision=hp, preferred_element_type=F32), pair_ref[...],
                  precision=hp, preferred_element_type=F32)
    cur = (n_pages * PAGE_SIZE) // SEL_BLOCK
    yield
    vals = _select_start(imp.T, cur)
    for _ in range(N_SEL):
        vals = _select_round(vals)
        yield
    neg = _select_mask(vals).T
    qs_ref[:, LANES:] = jnp.concatenate(
        [neg[(h // REP) * t_new:(h // REP + 1) * t_new, :] for h in range(N_HEADS)], axis=0).astype(BF16)

    def past_tile(ref, j):
        cols = slice(j * PAGE_SIZE, (j + 1) * PAGE_SIZE)
        return (lambda: jnp.concatenate([ref[:LANES, :].astype(BF16), xt_ref[:, cols]], axis=0),
                lambda: ref[LANES:, :].astype(BF16), lambda: bs_ref[:, cols], True, PAGE_SIZE)

    new_sel_tile = (lambda: jnp.concatenate([seln_ref[:, :LANES].astype(BF16), xtn_ref[...]], axis=1),
                    lambda: seln_ref[:, LANES:].astype(BF16), lambda: bn_ref[...], False, PAGE_SIZE)
    tiles = [past_tile(sel_refs[j], j) for j in range(n_pages)] + [new_sel_tile]
    out_ref[...] += gate_col(1) * (yield from softmax_tiles(qs_ref[...], tiles))

    tiles = [(lambda: winb_ref[:LANES, :].astype(BF16), lambda: winb_ref[LANES:, :].astype(BF16), lambda: bw_ref[...],
              True, winb_ref.shape[1]),
             (lambda: winn_ref[:, :LANES].astype(BF16), lambda: winn_ref[:, LANES:].astype(BF16), lambda: bn_ref[...],
              False, PAGE_SIZE)]
    out_ref[...] += gate_col(2) * (yield from softmax_tiles(qq_ref[...], tiles))

    lane_o = lax.broadcasted_iota(jnp.int32, (t_new, LANES), 1)
    for i in range(N_HEADS // 2):
        left = out_ref[2 * i * t_new:(2 * i + 1) * t_new, :]
        right = out_ref[(2 * i + 1) * t_new:(2 * i + 2) * t_new, :]
        if (2 * i) // REP == 0:
            right = pltpu.roll(right, HEAD_DIM, 1)
        else:
            left = pltpu.roll(left, HEAD_DIM, 1)
        o_ref[:, i * LANES:(i + 1) * LANES] = jnp.where(lane_o < HEAD_DIM, left, right)


def _attention_sample(page_table, kvc_pages, sel_cache, layer, qq, sel_new, win_new, win_buf, gates, tables, t_new):
    n_seq, n_pages = page_table.shape
    xt, xtn, bc, bs, bn, bw, rsum, pair = tables
    n_win = win_buf.shape[3]
    nq = N_HEADS * t_new
    par = SEQ_PER_STEP
    assert n_seq % par == 0
    page_spec = lambda shape, s, j: pl.BlockSpec((None, None) + shape,
                                                 lambda b, pt, s=s, j=j: (layer, pt[par * b + s, j], 0, 0))
    full = lambda a: pl.BlockSpec(a.shape, lambda b, pt: (0,) * a.ndim)
    new = pl.BlockSpec((par, PAGE_SIZE, KV_WIDTH), lambda b, pt: (b, 0, 0))
    in_specs = ([page_spec((PAGE_SIZE // CMP_BLOCK, KV_WIDTH), s, j) for s in range(par) for j in range(n_pages)]
                + [page_spec((KV_WIDTH, PAGE_SIZE), s, j) for s in range(par) for j in range(n_pages)]
                + [pl.BlockSpec((par, nq, LANES), lambda b, pt: (b, 0, 0)), new, new,
                   pl.BlockSpec((None, par, KV_WIDTH, n_win), lambda b, pt: (layer, b, 0, 0)),
                   pl.BlockSpec((par * t_new, LANES), lambda b, pt: (b, 0))]
                + [full(a) for a in (xt, xtn, bc, bs, bn, bw, rsum, pair)])
    return pl.pallas_call(
        functools.partial(_attn_sample_kernel, n_pages=n_pages, t_new=t_new),
        grid_spec=pltpu.PrefetchScalarGridSpec(
            num_scalar_prefetch=1, grid=(n_seq // par,), in_specs=in_specs,
            out_specs=pl.BlockSpec((par * t_new, ATT_WIDTH), lambda b, pt: (b, 0)),
            scratch_shapes=[pltpu.VMEM((par, nq, 2 * LANES), BF16), pltpu.VMEM((par, nq, LANES), F32),
                            pltpu.VMEM((par, nq, (n_pages + 1) * PAGE_SIZE), F32),
                            pltpu.VMEM((par, nq, (n_pages + 1) * PAGE_SIZE), BF16)]),
        out_shape=jax.ShapeDtypeStruct((n_seq * t_new, ATT_WIDTH), F32),
        compiler_params=_cparams(("arbitrary",)),
        name="nsa_sample",
    )(page_table, *([kvc_pages] * (par * n_pages)), *([sel_cache] * (par * n_pages)), qq, sel_new, win_new, win_buf,
      gates, xt, xtn, bc, bs, bn, bw, rsum, pair)


def _sample_sum_matrices(t_new, nc):
    rsum = np.zeros((LANES, N_HEADS * t_new), np.float32)
    for h in range(N_HEADS):
        for t in range(t_new):
            rsum[(h // REP) * t_new + t, h * t_new + t] = 1.0
    pair = np.zeros((nc, LANES), np.float32)
    pair[np.arange(nc), np.arange(nc) // 2] = 1.0
    return jnp.asarray(rsum), jnp.asarray(pair)


def _rank_before(vals, k):
    r = jnp.zeros(vals[k].shape, jnp.int32)
    for j, vj in enumerate(vals):
        if j < k:
            r = r + jnp.where(vj >= vals[k], 1, 0)
        elif j > k:
            r = r + jnp.where(vj > vals[k], 1, 0)
    return r


def _route(s_rows, b_rows):
    scores = []
    for g in range(N_GROUPS):
        b0, b1, b2, b3 = b_rows[4 * g:4 * g + 4]
        hi01, lo01, hi23, lo23 = jnp.maximum(b0, b1), jnp.minimum(b0, b1), jnp.maximum(b2, b3), jnp.minimum(b2, b3)
        top1 = jnp.maximum(hi01, hi23)
        top2 = jnp.maximum(jnp.maximum(lo01, lo23), jnp.minimum(hi01, hi23))
        scores.append(top1 + top2)
    in_group = [_rank_before(scores, g) == 0 for g in range(N_GROUPS)]

    def pick(rows, k):
        out = rows[4 * (N_GROUPS - 1) + k]
        for g in range(N_GROUPS - 2, -1, -1):
            out = jnp.where(in_group[g], rows[4 * g + k], out)
        return out

    bv = [pick(b_rows, k) for k in range(EXPERTS_PER_GROUP)]
    sv = [pick(s_rows, k) for k in range(EXPERTS_PER_GROUP)]
    w = [jnp.where(_rank_before(bv, k) < 2, sv[k], 0.0) for k in range(EXPERTS_PER_GROUP)]
    den = (w[0] + w[1]) + (w[2] + w[3])
    return [jnp.where(in_group[e // 4], w[e % 4] / den, 0.0) for e in range(N_EXPERTS)]


def _outproj_kernel(ps_ref, att_ref, x_ref, mod_ref, wout_ref, lng_ref, lnb_ref, rwt_ref, rb_ref,
                    x1_ref, h2_ref, g_ref):
    half = ps_ref.shape[1]
    mix = (jnp.dot(ps_ref[...], wout_ref[:half, :], preferred_element_type=F32)
           + jnp.dot(att_ref[...], wout_ref[half:, :], preferred_element_type=F32))
    x1 = _layer_norm(ALPHA * x_ref[...] + (1.0 + mod_ref[2]) * mix, lng_ref[...], lnb_ref[...])
    x1_ref[...] = x1
    h2 = x1 * (1.0 + mod_ref[4]) + mod_ref[3]
    h2b = h2.astype(BF16)
    h2_ref[...] = h2b
    st = jax.nn.sigmoid(lax.dot_general(rwt_ref[...].astype(BF16), h2b, _NT, preferred_element_type=F32))
    s_rows = [st[e:e + 1, :] for e in range(N_EXPERTS)]
    b_rows = [s_rows[e] + rb_ref[e:e + 1, :] for e in range(N_EXPERTS)]
    gt = jnp.concatenate(_route(s_rows, b_rows) + [jnp.zeros((LANES - N_EXPERTS, st.shape[1]), F32)], axis=0)
    g_ref[...] = gt.T


def _out_proj(ps, att, x, mod, wout, ln_g, ln_b, rwt, rb, tm, tiles_per_batch):
    n = x.shape[0]
    row = lambda wd: pl.BlockSpec((tm, wd), lambda i: (i, 0))
    full = lambda a: pl.BlockSpec(a.shape, lambda i: (0,) * a.ndim)
    return pl.pallas_call(
        _outproj_kernel,
        grid=(n // tm,),
        in_specs=[row(512), row(512), row(D_MODEL), _mod_spec(mod, tm, tiles_per_batch),
                  full(wout), full(ln_g), full(ln_b), full(rwt), full(rb)],
        out_specs=[row(D_MODEL), row(D_MODEL), row(LANES)],
        out_shape=[jax.ShapeDtypeStruct((n, D_MODEL), F32), jax.ShapeDtypeStruct((n, D_MODEL), BF16),
                   jax.ShapeDtypeStruct((n, LANES), F32)],
        compiler_params=_cparams(("arbitrary",)),
        name="out_proj",
    )(ps, att, x, mod, wout, ln_g, ln_b, rwt, rb)


def _moe_kernel(h2_ref, g_ref, wg_ref, wu_ref, wd_ref, x1_ref, mod_ref, lng_ref, lnb_ref, o_ref, acc_ref):
    e = pl.program_id(1)

    @pl.when(e == 0)
    def _():
        acc_ref[...] = jnp.zeros_like(acc_ref)

    x = h2_ref[...]
    gate = jnp.dot(x, wg_ref[0, 0], preferred_element_type=F32)
    up = jnp.dot(x, wu_ref[0, 0], preferred_element_type=F32)
    act = (gate * jax.nn.sigmoid(gate) * up).astype(BF16)
    y = jnp.dot(act, wd_ref[0, 0], preferred_element_type=F32)
    lane = lax.broadcasted_iota(jnp.int32, g_ref.shape, 1)
    w = jnp.sum(jnp.where(lane == e, g_ref[...], 0.0), axis=1, keepdims=True)
    acc_ref[...] += y * w

    @pl.when(e == pl.num_programs(1) - 1)
    def _():
        o_ref[...] = _layer_norm(ALPHA * x1_ref[...] + (1.0 + mod_ref[5]) * acc_ref[...], lng_ref[...], lnb_ref[...])


def _moe(h2, gates, wg, wu, wd, layer, x1, mod, ln_g, ln_b, tm, tiles_per_batch):
    n = h2.shape[0]
    row = lambda wd_: pl.BlockSpec((tm, wd_), lambda i, e: (i, 0))
    full = lambda a: pl.BlockSpec(a.shape, lambda i, e: (0,) * a.ndim)
    wspec = lambda a: pl.BlockSpec((1, 1) + a.shape[2:], lambda i, e: (layer, e, 0, 0))
    return pl.pallas_call(
        _moe_kernel,
        grid=(n // tm, N_EXPERTS),
        in_specs=[row(D_MODEL), row(LANES), wspec(wg), wspec(wu), wspec(wd), row(D_MODEL),
                  _mod_spec(mod, tm, tiles_per_batch), full(ln_g), full(ln_b)],
        out_specs=row(D_MODEL),
        out_shape=jax.ShapeDtypeStruct((n, D_MODEL), F32),
        scratch_shapes=[pltpu.VMEM((tm, D_MODEL), F32)],
        compiler_params=_cparams(("arbitrary", "arbitrary")),
        name="moe",
    )(h2, gates, wg, wu, wd, x1, mod, ln_g, ln_b)


SHIFT_ROWS = 2048


def _shift_window_kernel(state_ref, new_ref, o_ref, *, t_new):
    x = state_ref[...]
    w = x.shape[1]
    shifted = pltpu.roll(x, w - t_new, 1)
    o_ref[:, :w - LANES] = shifted[:, :w - LANES]
    lane = lax.broadcasted_iota(jnp.int32, (x.shape[0], LANES), 1)
    o_ref[:, w - LANES:] = jnp.where(lane < LANES - t_new, shifted[:, w - LANES:], new_ref[...])


def _shift_window(state, new):
    w, t_new = state.shape[-1], new.shape[-1]
    rows = state.size // w
    new_rows = jnp.pad(new.reshape(rows, t_new), ((0, 0), (LANES - t_new, 0)))
    out = pl.pallas_call(
        functools.partial(_shift_window_kernel, t_new=t_new),
        grid=(rows // SHIFT_ROWS,),
        in_specs=[pl.BlockSpec((SHIFT_ROWS, w), lambda i: (i, 0)), pl.BlockSpec((SHIFT_ROWS, LANES), lambda i: (i, 0))],
        out_specs=pl.BlockSpec((SHIFT_ROWS, w), lambda i: (i, 0)),
        out_shape=jax.ShapeDtypeStruct((rows, w), F32),
        compiler_params=_cparams(("arbitrary",)),
        name="shift_window",
    )(state.reshape(rows, w), new_rows)
    return out.reshape(state.shape)


TM_PROJ = 256
TM_MOE = 512


def _channel_mix(ps, att, x, mod, lw, sw, layer, tiles_per_seq):
    x1, h2, gts = _out_proj(ps, att, x, mod, lw["wout"], lw["ln_g0"], lw["ln_b0"], sw["rwt"], sw["rb"],
                            TM_PROJ, tiles_per_seq(TM_PROJ))
    return _moe(h2, gts, sw["wg"], sw["wu"], sw["wd"], layer, x1, mod, lw["ln_g1"], lw["ln_b1"],
                TM_MOE, tiles_per_seq(TM_MOE))


def _prompt_layer(x, mod, lw, sw, layer, depth, kv_state, batch, seq):
    tiles = lambda tm: seq // tm
    (p, u, vn, q_t, cmp_k, cmp_v, cmp_t, sel_t, win_t, gate_t, ksel, vsel_t, kwin, vwin_t) = _in_proj(
        x, mod, lw["w_proj"], lw["sgu_ln_g"], lw["sgu_ln_b"], TM_PROJ, tiles(TM_PROJ), batch=batch, wq_t=lw["wq_t"],
        layer=layer, depth=depth, state=kv_state)
    ps = _mixers(p, u, vn, *lw["mix_prompt"], tiles_per_batch=tiles(SGU_CHUNK))
    nc = seq // CMP_BLOCK
    kvc = _compress(cmp_k, cmp_v, sw["pe_rows"], sw["w1bd"], sw["w2bd"], nc, layer)
    kc = _even_odd(kvc[:, :LANES].astype(BF16), nc)
    vc_t = jnp.transpose(_even_odd(kvc[:, LANES:].astype(BF16), nc).reshape(batch, nc, LANES), (0, 2, 1))
    att = _attention_prompt(q_t, kc, vc_t, ksel, vsel_t, kwin, vwin_t, gate_t, sw["prompt_tables"], batch, seq)
    x2 = _channel_mix(ps, att, x, mod, lw, sw, layer, tiles)
    return x2, (cmp_t, sel_t, win_t), p


def _sample_layer(x, mod, lw, sw, layer, page_table, kvc_pages, sel_cache, win_state, pool_state, n_seq, t_new):
    tiles = lambda tm: 1
    (p, u, vn, qq, cmp_raw, sel_raw, win_raw, gates) = _in_proj(
        x, mod, lw["w_proj"], lw["sgu_ln_g"], lw["sgu_ln_b"], TM_PROJ, 1)
    p_ext = jnp.concatenate([pool_state[layer], p.reshape(n_seq, t_new, POOL_WIDTH)], axis=1)
    hist = jnp.pad(p_ext, ((0, 0), (3 * t_new - p_ext.shape[1], 0), (0, 0)))
    chunks = [hist[:, k * t_new:(k + 1) * t_new].reshape(n_seq * t_new, POOL_WIDTH) for k in range(3)]
    ps = _mixers(chunks, u, vn, *lw["mix_sample"], t_new=t_new)
    q_rows = jnp.transpose(qq.reshape(N_HEADS, n_seq, t_new, LANES), (1, 0, 2, 3)).reshape(n_seq, N_HEADS * t_new, LANES)
    new_page = lambda raw: jnp.pad(raw.reshape(n_seq, t_new, KV_WIDTH), ((0, 0), (0, PAGE_SIZE - t_new), (0, 0)))
    att = _attention_sample(page_table, kvc_pages, sel_cache, layer, q_rows, new_page(sel_raw), new_page(win_raw),
                            win_state, gates, sw["sample_tables"], t_new)
    x2 = _channel_mix(ps, att.astype(BF16), x, mod, lw, sw, layer, tiles)
    win_new_t = jnp.transpose(win_raw.reshape(n_seq, t_new, KV_WIDTH), (0, 2, 1))
    return x2, (cmp_raw, sel_raw, win_new_t, p_ext[:, p_ext.shape[1] - POOL_BUF:], vn)


def kernel(x_prompt, x_sample, cache_cmp_kv, cache_sel_kv, state_win_kv, state_pool, page_table, c_prompt, c_sample,
           w_in, w_out, pool_w, pool_scale, sgu_ln_g, sgu_ln_b, sgu_w, sgu_b, cmp_pe, cmp_w1, cmp_w2, rel_bias,
           w_mod, b_mod, ln_g, ln_b, router_w, router_b, moe_w_gate, moe_w_up, moe_w_down):
    batch, seq, d = x_prompt.shape
    n_seq, t_new, _ = x_sample.shape
    depth = w_in.shape[0]
    n_pages = page_table.shape[1]
    past_len = n_pages * PAGE_SIZE
    n_phys = cache_cmp_kv.shape[1]
    n_win = state_win_kv.shape[2]
    assert seq // SEL_BLOCK == LANES and seq % TM_MOE == 0 and (n_seq * t_new) % TM_MOE == 0
    assert POOL_BUF + 1 == 2 * t_new and past_len % TK == 0

    n_c = batch + n_seq
    c_all = jnp.pad(jnp.concatenate([c_prompt, c_sample], axis=0), ((0, -n_c % 8), (0, 0)))
    m_all = _modulation(c_all, w_mod, b_mod)
    mod_p = m_all[:, :batch].reshape(depth, batch, 6, 1, d)
    mod_s = jnp.transpose(jnp.repeat(m_all[:, batch:n_c].reshape(depth, n_seq, 6, d), t_new, axis=1), (0, 2, 1, 3))

    pe_rows, w1bd, w2bd = _prep_compress_weights(cmp_pe, cmp_w1, cmp_w2)
    near, dtab, lane_m = _prompt_bias_tables(rel_bias)
    near = near * LOG2E
    nc_past = past_len // CMP_BLOCK
    shared = {
        "pe_rows": pe_rows, "w1bd": w1bd, "w2bd": w2bd,
        "prompt_tables": (near, dtab, lane_m, _block_onehot_rows(seq)),
        "sample_tables": (_block_onehot_rows(past_len + PAGE_SIZE).T, _block_onehot_rows(past_len + PAGE_SIZE)[past_len:],
                          *_sample_bias_tables(rel_bias, past_len, t_new, n_win),
                          *_sample_sum_matrices(t_new, nc_past)),
        "rwt": router_w.T, "rb": router_b.reshape(N_EXPERTS, 1),
        "wg": moe_w_gate.astype(BF16), "wu": moe_w_up.astype(BF16), "wd": moe_w_down.astype(BF16),
    }
    layers = []
    for l in range(depth):
        layers.append({
            "w_proj": _prep_w_proj(w_in[l]), "wq_t": _prep_w_proj(w_in[l])[:, C_Q:].T, "wout": w_out[l].astype(BF16),
            "sgu_ln_g": sgu_ln_g[l].reshape(1, -1), "sgu_ln_b": sgu_ln_b[l].reshape(1, -1),
            "mix_prompt": _prep_mixer_weights(pool_w[l], pool_scale[l], sgu_w[l], sgu_b[l], SGU_CHUNK),
            "mix_sample": _prep_mixer_weights(pool_w[l], pool_scale[l], sgu_w[l], sgu_b[l], t_new),
            "ln_g0": ln_g[l, 0].reshape(1, d), "ln_b0": ln_b[l, 0].reshape(1, d),
            "ln_g1": ln_g[l, 1].reshape(1, d), "ln_b1": ln_b[l, 1].reshape(1, d),
        })

    chan_major = lambda x: jnp.transpose(x, (0, 1, 3, 4, 5, 2)).reshape(x.shape[0], x.shape[1], KV_WIDTH, x.shape[2])
    pe_pages = jnp.broadcast_to(jnp.transpose(cmp_pe, (0, 1, 3, 2))[:, :, None, :, None, :],
                                (depth, 2, N_KV, HEAD_DIM, PAGE_SIZE // CMP_BLOCK, CMP_BLOCK))
    kvc_pages = _compress_pages(chan_major(cache_cmp_kv), pe_pages.reshape(depth, KV_WIDTH, PAGE_SIZE), w1bd, w2bd, 64)
    kvc_pages = kvc_pages.reshape(depth, n_phys, PAGE_SIZE // CMP_BLOCK, KV_WIDTH)
    sel_cache = chan_major(cache_sel_kv)
    win_state = chan_major(state_win_kv)

    xp = x_prompt.reshape(batch * seq, d)
    xs = x_sample.reshape(n_seq * t_new, d)
    kv_state = tuple(jnp.zeros((depth, batch, KV_WIDTH, seq), F32) for _ in range(3))
    pool_p, outs_s = [], []
    for l in range(depth):
        xp, kv_state, p_l = _prompt_layer(xp, mod_p[l], layers[l], shared, l, depth, kv_state, batch, seq)
        pool_p.append(p_l)
        xs, st = _sample_layer(xs, mod_s[l], layers[l], shared, l, page_table, kvc_pages, sel_cache, win_state,
                               state_pool, n_seq, t_new)
        outs_s.append(st)

    kv5 = lambda x, b: x.reshape(b, -1, 2, N_KV, HEAD_DIM)
    kv6_t = lambda x: jnp.transpose(x.reshape(x.shape[:2] + (2, N_KV, HEAD_DIM, x.shape[3])), (0, 1, 5, 2, 3, 4))
    stack = lambda xs_: jnp.stack(xs_)
    w_keep = min(WINDOW, seq)
    cmp_t, sel_t, win_t = kv_state
    new_win_t = _shift_window(win_state, stack([o[2] for o in outs_s]))
    return (xp.reshape(batch, seq, d), xs.reshape(n_seq, t_new, d),
            kv6_t(cmp_t), stack([kv5(o[0], n_seq) for o in outs_s]),
            kv6_t(sel_t), stack([kv5(o[1], n_seq) for o in outs_s]),
            kv6_t(win_t[:, :, :, seq - w_keep:]), kv6_t(new_win_t),
            stack([p_l.reshape(batch, seq, POOL_WIDTH)[:, seq - POOL_BUF:] for p_l in pool_p]),
            stack([o[3] for o in outs_s]),
            stack([o[4].reshape(n_seq, t_new, SGU_WIDTH) for o in outs_s]))
```
